```python
import jax, jax.numpy as jnp
from jax import lax
import numpy as np

D_MODEL = 1024
BATCH = 4
SEQ = 8192
DEPTH = 4

CHUNK = 64
N_MIXERS = 2
N_LAYERS_A = (DEPTH + N_MIXERS - 1) // N_MIXERS
N_LAYERS_B = DEPTH // N_MIXERS
CONV_A_WIDTH = 31
CONV_B_WIDTH = 3
N_GROUPS = 4
EXPERTS_PER_GROUP = 8
N_EXPERTS = N_GROUPS * EXPERTS_PER_GROUP
TOP_K = 2
D_EXPERT = D_MODEL // 2
ROUTE_BLOCK = 128
ALPHA = (2.0 * DEPTH) ** 0.25
BETA = (8.0 * DEPTH) ** -0.25
LN_EPS = 1e-5

kernel_name = "hybrid_conv_hmoe_deepnorm_adaln"


def layer_norm(x, g, b):
    xf = x.astype(jnp.float32)
    mu = jnp.mean(xf, axis=-1, keepdims=True)
    var = jnp.mean(jnp.square(xf - mu), axis=-1, keepdims=True)
    y = (xf - mu) * lax.rsqrt(var + LN_EPS)
    return (y * g.astype(jnp.float32) + b.astype(jnp.float32)).astype(x.dtype)


def causal_depthwise_conv(u, w):
    k = w.shape[0]
    return lax.conv_general_dilated(
        u, w[:, None, :].astype(u.dtype), window_strides=(1,), padding=[(k - 1, 0)],
        dimension_numbers=("NWC", "WIO", "NWC"), feature_group_count=u.shape[-1])


def conformer_conv(h, w_in, b_in, w_dw, b_dw, ln_g, ln_b, w_out, b_out):
    a, g = jnp.split(h @ w_in + b_in, 2, axis=-1)
    u = a * jax.nn.sigmoid(g)
    u = causal_depthwise_conv(u, w_dw) + b_dw
    u = jax.nn.silu(layer_norm(u, ln_g, ln_b))
    return u @ w_out + b_out


def short_gated_conv(h, w_in, w_dw, w_out):
    gb, gc, v = jnp.split(h @ w_in, 3, axis=-1)
    u = causal_depthwise_conv(gc * v, w_dw)
    return (gb * u) @ w_out


def hierarchical_moe(h, w_group, b_group, w_router, b_router, w_gate, w_up, w_down):
    bsz, seq, d = h.shape
    t = bsz * seq
    xf = h.reshape(t, d)
    g_logits = (xf @ w_group + b_group).astype(jnp.float32)
    g_probs = jax.nn.softmax(g_logits, axis=-1)
    g_idx = jnp.argmax(g_logits, axis=-1).astype(jnp.int32)
    g_w = jnp.take_along_axis(g_probs, g_idx[:, None], axis=-1)
    e_logits = (xf @ w_router + b_router).astype(jnp.float32).reshape(t, N_GROUPS, EXPERTS_PER_GROUP)
    e_logits = jnp.take_along_axis(e_logits, g_idx[:, None, None], axis=1)[:, 0]
    e_probs = jax.nn.softmax(e_logits, axis=-1)
    top_p, top_e = lax.top_k(e_probs, TOP_K)
    top_p = top_p / jnp.sum(top_p, axis=-1, keepdims=True)
    weights = (g_w * top_p).reshape(-1)
    flat_e = (g_idx[:, None] * EXPERTS_PER_GROUP + top_e.astype(jnp.int32)).reshape(-1)
    n_assign = t * TOP_K
    order = jnp.argsort(flat_e)
    sorted_e = flat_e[order]
    sorted_tok = (order // TOP_K).astype(jnp.int32)
    counts = jnp.zeros((N_EXPERTS,), jnp.int32).at[flat_e].add(1)
    starts = jnp.cumsum(counts) - counts
    padded = (counts + ROUTE_BLOCK - 1) // ROUTE_BLOCK * ROUTE_BLOCK
    pad_ends = jnp.cumsum(padded)
    pad_starts = pad_ends - padded
    dest = pad_starts[sorted_e] + (jnp.arange(n_assign, dtype=jnp.int32) - starts[sorted_e])
    n_slots = n_assign + N_EXPERTS * ROUTE_BLOCK
    n_blocks = n_slots // ROUTE_BLOCK
    slot_tok = jnp.full((n_slots,), t, jnp.int32).at[dest].set(sorted_tok)
    block_start = jnp.arange(n_blocks, dtype=jnp.int32) * ROUTE_BLOCK
    block_e = jnp.minimum(jnp.sum(pad_ends[None, :] <= block_start[:, None], axis=-1), N_EXPERTS - 1)
    x_pad = jnp.concatenate([xf, jnp.zeros((1, d), xf.dtype)], axis=0)
    xb = x_pad[slot_tok].reshape(n_blocks, ROUTE_BLOCK, d)

    def expert_block(args):
        xblk, e = args
        hid = jax.nn.silu(xblk @ w_gate[e]) * (xblk @ w_up[e])
        return hid @ w_down[e]

    yb = lax.map(expert_block, (xb, block_e)).reshape(n_slots, d)
    contrib = weights[order][:, None].astype(xf.dtype) * yb[dest]
    y = jnp.zeros((t, d), xf.dtype).at[sorted_tok].add(contrib)
    return y.reshape(bsz, seq, d)


def setup_inputs(seed: int = 0) -> dict:
    key = jax.random.key(seed)
    ks = jax.random.split(key, 26)
    D, F = D_MODEL, D_EXPERT
    nrm = lambda k, shape, s: jax.random.normal(k, shape, jnp.float32) * s
    return {
        "x": nrm(ks[0], (BATCH, SEQ, D), 1.0),
        "c": nrm(ks[1], (BATCH, D), 1.0),
        "ada_w": nrm(ks[2], (DEPTH, D, 6 * D), 0.5 * D ** -0.5),
        "ada_b": nrm(ks[3], (DEPTH, 6 * D), 0.01),
        "a_w_in": nrm(ks[4], (N_LAYERS_A, D, 2 * D), D ** -0.5),
        "a_b_in": nrm(ks[5], (N_LAYERS_A, 2 * D), 0.01),
        "a_w_dw": nrm(ks[6], (N_LAYERS_A, CONV_A_WIDTH, D), CONV_A_WIDTH ** -0.5),
        "a_b_dw": nrm(ks[7], (N_LAYERS_A, D), 0.01),
        "a_ln_g": 1.0 + nrm(ks[8], (N_LAYERS_A, D), 0.01),
        "a_ln_b": nrm(ks[9], (N_LAYERS_A, D), 0.01),
        "a_w_out": nrm(ks[10], (N_LAYERS_A, D, D), BETA * D ** -0.5),
        "a_b_out": nrm(ks[11], (N_LAYERS_A, D), 0.01),
        "b_w_in": nrm(ks[12], (N_LAYERS_B, D, 3 * D), D ** -0.5),
        "b_w_dw": nrm(ks[13], (N_LAYERS_B, CONV_B_WIDTH, D), CONV_B_WIDTH ** -0.5),
        "b_w_out": nrm(ks[14], (N_LAYERS_B, D, D), BETA * D ** -0.5),
        "mix_ln_g": 1.0 + nrm(ks[15], (DEPTH, D), 0.01),
        "mix_ln_b": nrm(ks[16], (DEPTH, D), 0.01),
        "ffn_ln_g": 1.0 + nrm(ks[17], (DEPTH, D), 0.01),
        "ffn_ln_b": nrm(ks[18], (DEPTH, D), 0.01),
        "r_w_group": nrm(ks[19], (DEPTH, D, N_GROUPS), D ** -0.5),
        "r_b_group": nrm(ks[20], (DEPTH, N_GROUPS), 0.01),
        "r_w_expert": nrm(ks[21], (DEPTH, D, N_EXPERTS), D ** -0.5),
        "r_b_expert": nrm(ks[22], (DEPTH, N_EXPERTS), 0.01),
        "e_w_gate": nrm(ks[23], (DEPTH, N_EXPERTS, D, F), D ** -0.5),
        "e_w_up": nrm(ks[24], (DEPTH, N_EXPERTS, D, F), D ** -0.5),
        "e_w_down": nrm(ks[25], (DEPTH, N_EXPERTS, F, D), BETA * F ** -0.5),
    }


def reference(x, c, ada_w, ada_b, a_w_in, a_b_in, a_w_dw, a_b_dw, a_ln_g, a_ln_b, a_w_out, a_b_out,
              b_w_in, b_w_dw, b_w_out, mix_ln_g, mix_ln_b, ffn_ln_g, ffn_ln_b,
              r_w_group, r_b_group, r_w_expert, r_b_expert, e_w_gate, e_w_up, e_w_down):
    c_act = jax.nn.silu(c)
    for i in range(DEPTH):
        mods = (c_act @ ada_w[i] + ada_b[i])[:, None, :]
        sh_m, sc_m, g_m, sh_f, sc_f, g_f = jnp.split(mods, 6, axis=-1)
        h = x * (1.0 + sc_m) + sh_m
        j = i // N_MIXERS
        if i % N_MIXERS == 0:
            y = conformer_conv(h, a_w_in[j], a_b_in[j], a_w_dw[j], a_b_dw[j],
                               a_ln_g[j], a_ln_b[j], a_w_out[j], a_b_out[j])
        else:
            y = short_gated_conv(h, b_w_in[j], b_w_dw[j], b_w_out[j])
        x = layer_norm(ALPHA * x + (1.0 + g_m) * y, mix_ln_g[i], mix_ln_b[i])
        h = x * (1.0 + sc_f) + sh_f
        y = hierarchical_moe(h, r_w_group[i], r_b_group[i], r_w_expert[i], r_b_expert[i],
                             e_w_gate[i], e_w_up[i], e_w_down[i])
        x = layer_norm(ALPHA * x + (1.0 + g_f) * y, ffn_ln_g[i], ffn_ln_b[i])
    return x
```

```python
import functools

import jax
import jax.numpy as jnp
from jax import lax
from jax.experimental import pallas as pl
from jax.experimental.pallas import tpu as pltpu

F32 = jnp.float32
BF16 = jnp.bfloat16
I32 = jnp.int32

D = 1024
BATCH = 4
SEQ = 8192
T = BATCH * SEQ
DEPTH = 4
N_GROUPS = 4
EPG = 8
NE = N_GROUPS * EPG
TOP_K = 2
F = D // 2
CONV_A = 31
CONV_B = 3
ALPHA = (2.0 * DEPTH) ** 0.25
LN_EPS = 1e-5

LANES = 128
SUBLANES = 8
VMEM_LIMIT = 56 * 1024 * 1024

TS = 512
NS = SEQ // TS
HALO_A = 32
HALO_B = 8
BM = 256
NSLOT = T * TOP_K + NE * BM
NB = NSLOT // BM
NR = 48
ADA_TN = 1536
NEG = -1e30


def _sigmoid(x):
    return 1.0 / (1.0 + jnp.exp(-x))


def _layer_norm(x, g, b):
    mu = jnp.mean(x, axis=-1, keepdims=True)
    xc = x - mu
    var = jnp.mean(xc * xc, axis=-1, keepdims=True)
    return xc * lax.rsqrt(var + LN_EPS) * g + b


def _ada_kernel(c_ref, w_ref, b_ref, o_ref):
    c = c_ref[...]
    ca = (c * _sigmoid(c)).astype(BF16)
    w = w_ref[0].astype(BF16)
    o_ref[0] = jnp.dot(ca, w, preferred_element_type=F32) + b_ref[0]


def _ada_mods(c, ada_w, ada_b):
    out = pl.pallas_call(
        _ada_kernel,
        grid=(DEPTH, 6 * D // ADA_TN),
        in_specs=[
            pl.BlockSpec((BATCH, D), lambda i, j: (0, 0)),
            pl.BlockSpec((1, D, ADA_TN), lambda i, j: (i, 0, j)),
            pl.BlockSpec((1, 1, ADA_TN), lambda i, j: (i, 0, j)),
        ],
        out_specs=pl.BlockSpec((1, BATCH, ADA_TN), lambda i, j: (i, 0, j)),
        out_shape=jax.ShapeDtypeStruct((DEPTH, BATCH, 6 * D), F32),
        compiler_params=pltpu.CompilerParams(
            dimension_semantics=("arbitrary", "arbitrary"), vmem_limit_bytes=VMEM_LIMIT),
        name="ada_mods",
    )(c, ada_w, ada_b.reshape(DEPTH, 1, 6 * D))
    return out.reshape(DEPTH, BATCH, 6, D)


def _combine(x1, y0_ref, y1_ref, wc_ref, g_f, ln_g, ln_b):
    wc = wc_ref[0]
    w0 = pltpu.repeat(wc[:, :LANES], D // LANES, axis=1)
    w1 = pltpu.repeat(wc[:, LANES:], D // LANES, axis=1)
    y = w0 * y0_ref[0].astype(F32) + w1 * y1_ref[0].astype(F32)
    return _layer_norm(ALPHA * x1 + (1.0 + g_f) * y, ln_g, ln_b)


def _route(h2, wr_ref, rb_ref, tri_ref, cnt_ref, ri_ref, wc_ref, cnto_ref):
    lt = lax.dot_general(wr_ref[...], h2, (((1,), (1,)), ((), ())),
                         preferred_element_type=F32) + rb_ref[...]
    iota8 = lax.broadcasted_iota(I32, (SUBLANES, TS), 0).astype(F32)
    gl = lt[0:SUBLANES]
    gmax = jnp.max(gl, axis=0, keepdims=True)
    gidx = jnp.min(jnp.where(gl == gmax, iota8, float(SUBLANES)), axis=0, keepdims=True)
    gw = 1.0 / jnp.sum(jnp.exp(gl - gmax), axis=0, keepdims=True)
    el = lt[SUBLANES:2 * SUBLANES]
    for g in range(1, N_GROUPS):
        el = jnp.where(gidx == float(g), lt[SUBLANES * (g + 1):SUBLANES * (g + 2)], el)
    m1 = jnp.max(el, axis=0, keepdims=True)
    i1 = jnp.min(jnp.where(el == m1, iota8, float(SUBLANES)), axis=0, keepdims=True)
    el2 = jnp.where(iota8 == i1, -jnp.inf, el)
    m2 = jnp.max(el2, axis=0, keepdims=True)
    i2 = jnp.min(jnp.where(el2 == m2, iota8, float(SUBLANES)), axis=0, keepdims=True)
    r = jnp.exp(m2 - m1)
    w_a = gw / (1.0 + r)
    w_b = gw * r / (1.0 + r)
    e1 = gidx * float(EPG) + i1
    e2 = gidx * float(EPG) + i2

    iota_e = lax.broadcasted_iota(I32, (NE, TS), 0).astype(F32)
    oh1 = iota_e == e1
    oh2 = iota_e == e2
    oh = jnp.concatenate([jnp.where(oh1, 1.0, 0.0), jnp.where(oh2, 1.0, 0.0)], axis=0)
    before = jnp.dot(oh.astype(BF16), tri_ref[...], preferred_element_type=F32)
    tot = jnp.sum(oh, axis=1, keepdims=True)
    cnt = cnt_ref[...]
    base = pltpu.repeat(cnt, TS // LANES, axis=1)
    tot1 = tot[:NE]
    tot2 = tot[NE:]
    rank1 = jnp.sum(jnp.where(oh1, base + before[:NE], 0.0), axis=0, keepdims=True)
    rank2 = jnp.sum(jnp.where(oh2, base + tot1 + before[NE:], 0.0), axis=0, keepdims=True)
    new_cnt = cnt + tot1 + tot2
    cnt_ref[...] = new_cnt
    cnto_ref[...] = new_cnt.astype(I32)

    ri_ref[0:1, :] = e1.astype(I32)
    ri_ref[1:2, :] = e2.astype(I32)
    ri_ref[2:3, :] = rank1.astype(I32)
    ri_ref[3:4, :] = rank2.astype(I32)
    ri_ref[4:8, :] = jnp.zeros((4, TS), I32)
    wc_ref[0, :, :LANES] = jnp.broadcast_to(w_a, (LANES, TS)).T
    wc_ref[0, :, LANES:] = jnp.broadcast_to(w_b, (LANES, TS)).T


def _conv_taps(uext_ref, u, w_dw, halo, width):
    s = pl.program_id(1)

    @pl.when(s == 0)
    def _():
        uext_ref[0:halo, :] = jnp.zeros((halo, D), F32)

    uext_ref[halo:halo + TS, :] = u
    acc = None
    for k in range(width):
        off = halo - (width - 1) + k
        term = w_dw[k:k + 1, :] * uext_ref[off:off + TS, :]
        acc = term if acc is None else acc + term
    uext_ref[0:halo, :] = uext_ref[TS:TS + halo, :]
    return acc


def _mixer_kernel(*refs, kind, has_prev):
    it = iter(refs)
    xin_ref = next(it)
    if has_prev:
        y0_ref, y1_ref, wcin_ref, pmods_ref, pln_ref = (next(it) for _ in range(5))
    mods_ref, mln_ref, win_ref = next(it), next(it), next(it)
    if kind == "a":
        bin_ref, wdw_ref, vec_ref = next(it), next(it), next(it)
    else:
        wdw_ref = next(it)
    wout_ref, wr_ref, rb_ref, tri_ref = (next(it) for _ in range(4))
    x1_ref, h2_ref, ri_ref, wc_ref, cnto_ref = (next(it) for _ in range(5))
    uext_ref, cnt_ref = next(it), next(it)

    first = (pl.program_id(0) == 0) & (pl.program_id(1) == 0)

    @pl.when(first)
    def _():
        cnt_ref[...] = jnp.zeros((NE, LANES), F32)

    m = mods_ref[0, 0]
    x = xin_ref[0]
    if has_prev:
        pln = pln_ref[...]
        x = _combine(x, y0_ref, y1_ref, wcin_ref, pmods_ref[0, 0][5:6, :], pln[0:1, :], pln[1:2, :])

    h = (x * (1.0 + m[1:2, :]) + m[0:1, :]).astype(BF16)
    p = jnp.dot(h, win_ref[...], preferred_element_type=F32)
    if kind == "a":
        p = p + bin_ref[...]
        vec = vec_ref[...]
        u = p[:, :D] * _sigmoid(p[:, D:])
        u = _conv_taps(uext_ref, u, wdw_ref[...], HALO_A, CONV_A) + vec[0:1, :]
        u = _layer_norm(u, vec[1:2, :], vec[2:3, :])
        u = u * _sigmoid(u)
        y = jnp.dot(u.astype(BF16), wout_ref[...], preferred_element_type=F32) + vec[3:4, :]
    else:
        gb = p[:, :D]
        q = p[:, D:2 * D] * p[:, 2 * D:]
        u = _conv_taps(uext_ref, q, wdw_ref[...], HALO_B, CONV_B)
        y = jnp.dot((gb * u).astype(BF16), wout_ref[...], preferred_element_type=F32)

    mln = mln_ref[...]
    x1 = _layer_norm(ALPHA * x + (1.0 + m[2:3, :]) * y, mln[0:1, :], mln[1:2, :])
    x1_ref[0] = x1
    h2 = (x1 * (1.0 + m[4:5, :]) + m[3:4, :]).astype(BF16)
    h2_ref[0] = h2
    _route(h2, wr_ref, rb_ref, tri_ref, cnt_ref, ri_ref, wc_ref, cnto_ref)


def _tile_spec(width):
    return pl.BlockSpec((1, TS, width), lambda b, s: (b, s, 0))


def _const_spec(shape):
    nd = len(shape)
    return pl.BlockSpec(shape, lambda b, s: (0,) * nd)


def _mods_spec():
    return pl.BlockSpec((1, 1, 6, D), lambda b, s: (0, b, 0, 0))


def _mixer(kind, xin, prev, mods_l, mix_ln, weights, wr, rb, tri, name):
    has_prev = prev is not None
    args = [xin]
    specs = [_tile_spec(D)]
    if has_prev:
        y0, y1, wcin, pmods, pln = prev
        args += [y0, y1, wcin, pmods, pln]
        specs += [_tile_spec(D), _tile_spec(D), _tile_spec(2 * LANES), _mods_spec(), _const_spec((2, D))]
    args += [mods_l, mix_ln]
    specs += [_mods_spec(), _const_spec((2, D))]
    for w in weights:
        args.append(w)
        specs.append(_const_spec(w.shape))
    args += [wr, rb, tri]
    specs += [_const_spec(wr.shape), _const_spec(rb.shape), _const_spec(tri.shape)]
    halo = HALO_A if kind == "a" else HALO_B
    out_shape = (
        jax.ShapeDtypeStruct((BATCH, SEQ, D), F32),
        jax.ShapeDtypeStruct((BATCH, SEQ, D), BF16),
        jax.ShapeDtypeStruct((SUBLANES, T), I32),
        jax.ShapeDtypeStruct((BATCH, SEQ, 2 * LANES), F32),
        jax.ShapeDtypeStruct((NE, LANES), I32),
    )
    out_specs = (
        _tile_spec(D), _tile_spec(D),
        pl.BlockSpec((SUBLANES, TS), lambda b, s: (0, b * NS + s)),
        _tile_spec(2 * LANES),
        pl.BlockSpec((NE, LANES), lambda b, s: (0, 0)),
    )
    return pl.pallas_call(
        functools.partial(_mixer_kernel, kind=kind, has_prev=has_prev),
        grid=(BATCH, NS),
        in_specs=specs,
        out_specs=out_specs,
        out_shape=out_shape,
        scratch_shapes=[pltpu.VMEM((TS + halo, D), F32), pltpu.VMEM((NE, LANES), F32)],
        compiler_params=pltpu.CompilerParams(
            dimension_semantics=("arbitrary", "arbitrary"), vmem_limit_bytes=VMEM_LIMIT),
        name=name,
    )(*args)


def _dest_kernel(ps_ref, ri_ref, o_ref):
    ri = ri_ref[...]
    e = ri[0:2, :]
    start = jnp.zeros_like(e)
    for k in range(NE):
        start = jnp.where(e == k, ps_ref[k], start)
    o_ref[0:2, :] = start + ri[2:4, :]
    o_ref[2:8, :] = jnp.zeros((6, ri.shape[1]), I32)


def _dest_slots(pad_starts, ri):
    tn = 4096
    return pl.pallas_call(
        _dest_kernel,
        grid_spec=pltpu.PrefetchScalarGridSpec(
            num_scalar_prefetch=1,
            grid=(T // tn,),
            in_specs=[pl.BlockSpec((SUBLANES, tn), lambda i, ps: (0, i))],
            out_specs=pl.BlockSpec((SUBLANES, tn), lambda i, ps: (0, i)),
        ),
        out_shape=jax.ShapeDtypeStruct((SUBLANES, T), I32),
        compiler_params=pltpu.CompilerParams(dimension_semantics=("arbitrary",)),
        name="dest_slots",
    )(pad_starts, ri)


def _expert_kernel(be_ref, nu_ref, x_ref, wg_ref, wu_ref, wd_ref, o_ref, wgu_s, wd_s):
    j = pl.program_id(0)

    @pl.when(j < nu_ref[0])
    def _():
        prev_e = be_ref[jnp.maximum(j - 1, 0)]

        @pl.when((j == 0) | (be_ref[j] != prev_e))
        def _():
            wgu_s[:, :F] = wg_ref[0].astype(BF16)
            wgu_s[:, F:] = wu_ref[0].astype(BF16)
            wd_s[...] = wd_ref[0].astype(BF16)

        gu = jnp.dot(x_ref[...], wgu_s[...], preferred_element_type=F32)
        g = gu[:, :F]
        hid = (g * _sigmoid(g) * gu[:, F:]).astype(BF16)
        o_ref[...] = jnp.dot(hid, wd_s[...], preferred_element_type=F32).astype(BF16)


def _experts(block_e, n_used, xs, w_gate, w_up, w_down, name):
    def row_map(j, be, nu):
        return (jnp.minimum(j, nu[0] - 1), 0)

    def w_map(j, be, nu):
        return (be[j], 0, 0)

    return pl.pallas_call(
        _expert_kernel,
        grid_spec=pltpu.PrefetchScalarGridSpec(
            num_scalar_prefetch=2,
            grid=(NB,),
            in_specs=[
                pl.BlockSpec((BM, D), row_map),
                pl.BlockSpec((1, D, F), w_map),
                pl.BlockSpec((1, D, F), w_map),
                pl.BlockSpec((1, F, D), w_map),
            ],
            out_specs=pl.BlockSpec((BM, D), row_map),
            scratch_shapes=[pltpu.VMEM((D, 2 * F), BF16), pltpu.VMEM((F, D), BF16)],
        ),
        out_shape=jax.ShapeDtypeStruct((NSLOT, D), BF16),
        compiler_params=pltpu.CompilerParams(
            dimension_semantics=("arbitrary",), vmem_limit_bytes=VMEM_LIMIT),
        name=name,
    )(block_e, n_used, xs, w_gate, w_up, w_down)


def _final_kernel(x1_ref, y0_ref, y1_ref, wc_ref, mods_ref, ln_ref, o_ref):
    ln = ln_ref[...]
    o_ref[0] = _combine(x1_ref[0], y0_ref, y1_ref, wc_ref, mods_ref[0, 0][5:6, :], ln[0:1, :], ln[1:2, :])


def _final(x1, y0, y1, wc, mods_l, ln):
    return pl.pallas_call(
        _final_kernel,
        grid=(BATCH, NS),
        in_specs=[_tile_spec(D), _tile_spec(D), _tile_spec(D), _tile_spec(2 * LANES),
                  _mods_spec(), _const_spec((2, D))],
        out_specs=_tile_spec(D),
        out_shape=jax.ShapeDtypeStruct((BATCH, SEQ, D), F32),
        compiler_params=pltpu.CompilerParams(
            dimension_semantics=("arbitrary", "arbitrary"), vmem_limit_bytes=VMEM_LIMIT),
        name="final_combine",
    )(x1, y0, y1, wc, mods_l, ln)


def _plan(counts):
    padded = (counts + BM - 1) // BM * BM
    pad_ends = jnp.cumsum(padded)
    pad_starts = (pad_ends - padded).astype(I32)
    n_used = (pad_ends[-1] // BM).astype(I32)
    starts = jnp.arange(NB, dtype=I32) * BM
    starts = jnp.minimum(starts, pad_ends[-1] - BM)
    block_e = jnp.minimum(jnp.sum(pad_ends[None, :] <= starts[:, None], axis=-1), NE - 1).astype(I32)
    return pad_starts, block_e, n_used.reshape(1)


def _router_params(w_group, b_group, w_expert, b_expert):
    wr = jnp.zeros((NR, D), F32)
    wr = wr.at[0:N_GROUPS].set(w_group.T).at[SUBLANES:SUBLANES + NE].set(w_expert.T)
    rb = jnp.full((NR,), NEG, F32)
    rb = rb.at[0:N_GROUPS].set(b_group).at[SUBLANES:SUBLANES + NE].set(b_expert)
    rb = rb.at[SUBLANES + NE:].set(0.0)
    return wr.astype(BF16), rb.reshape(NR, 1)


def kernel(x, c, ada_w, ada_b, a_w_in, a_b_in, a_w_dw, a_b_dw, a_ln_g, a_ln_b, a_w_out, a_b_out,
           b_w_in, b_w_dw, b_w_out, mix_ln_g, mix_ln_b, ffn_ln_g, ffn_ln_b,
           r_w_group, r_b_group, r_w_expert, r_b_expert, e_w_gate, e_w_up, e_w_down):
    mods = _ada_mods(c, ada_w, ada_b)
    tri = (jnp.arange(TS)[:, None] < jnp.arange(TS)[None, :]).astype(BF16)
    prev = None
    xin = x
    for i in range(DEPTH):
        j = i // 2
        mods_l = mods[i:i + 1]
        mix_ln = jnp.stack([mix_ln_g[i], mix_ln_b[i]])
        wr, rb = _router_params(r_w_group[i], r_b_group[i], r_w_expert[i], r_b_expert[i])
        if i % 2 == 0:
            weights = [a_w_in[j].astype(BF16), a_b_in[j].reshape(1, 2 * D), a_w_dw[j],
                       jnp.stack([a_b_dw[j], a_ln_g[j], a_ln_b[j], a_b_out[j]]),
                       a_w_out[j].astype(BF16)]
            kind = "a"
        else:
            weights = [b_w_in[j].astype(BF16), b_w_dw[j], b_w_out[j].astype(BF16)]
            kind = "b"
        x1, h2, ri, wc, counts = _mixer(kind, xin, prev, mods_l, mix_ln, weights, wr, rb, tri,
                                        name=f"mixer_{kind}{i}")
        pad_starts, block_e, n_used = _plan(counts[:, 0])
        dest = _dest_slots(pad_starts, ri)
        d0, d1 = dest[0], dest[1]
        h2f = h2.reshape(T, D)
        xs = jnp.zeros((NSLOT, D), BF16).at[d0].set(h2f).at[d1].set(h2f)
        yb = _experts(block_e, n_used, xs, e_w_gate[i], e_w_up[i], e_w_down[i], name=f"experts{i}")
        y0 = yb[d0].reshape(BATCH, SEQ, D)
        y1 = yb[d1].reshape(BATCH, SEQ, D)
        prev = (y0, y1, wc, mods_l, jnp.stack([ffn_ln_g[i], ffn_ln_b[i]]))
        xin = x1
    y0, y1, wc, mods_l, ln = prev
    return _final(xin, y0, y1, wc, mods_l, ln)
```

```python
import functools

import jax
import jax.numpy as jnp
from jax import lax
from jax.experimental import pallas as pl
from jax.experimental.pallas import tpu as pltpu
from jax.experimental.pallas import tpu_sc as plsc

F32 = jnp.float32
BF16 = jnp.bfloat16
I32 = jnp.int32
U32 = jnp.uint32

D = 1024
BATCH = 4
SEQ = 8192
T = BATCH * SEQ
DEPTH = 4
N_GROUPS = 4
EPG = 8
NE = N_GROUPS * EPG
TOP_K = 2
F = D // 2
CONV_A = 31
CONV_B = 3
ALPHA = (2.0 * DEPTH) ** 0.25
LN_EPS = 1e-5

LANES = 128
SUBLANES = 8
VMEM_LIMIT = 56 * 1024 * 1024

TS = 512
NS = SEQ // TS
HALO_A = 32
HALO_B = 8
BM = 256
NSLOT = T * TOP_K + NE * BM
NB = NSLOT // BM
NR = 48
ADA_TN = 1536
NEG = -1e30
DP = D // 2

SC_CORES = 2
SC_SUBCORES = 16
SC_WORKERS = SC_CORES * SC_SUBCORES
SC_CHUNK = 64


def _sigmoid(x):
    return 1.0 / (1.0 + jnp.exp(-x))


def _pack_rows(x):
    return pltpu.pack_elementwise([x[:, :DP], x[:, DP:]], packed_dtype=BF16)


def _unpack_rows(p):
    lo = pltpu.unpack_elementwise(p, index=0, packed_dtype=BF16, unpacked_dtype=F32)
    hi = pltpu.unpack_elementwise(p, index=1, packed_dtype=BF16, unpacked_dtype=F32)
    return jnp.concatenate([lo, hi], axis=1)


def _layer_norm(x, g, b):
    mu = jnp.mean(x, axis=-1, keepdims=True)
    xc = x - mu
    var = jnp.mean(xc * xc, axis=-1, keepdims=True)
    return xc * lax.rsqrt(var + LN_EPS) * g + b


def _ada_kernel(c_ref, w_ref, b_ref, o_ref):
    c = c_ref[...]
    ca = (c * _sigmoid(c)).astype(BF16)
    w = w_ref[0].astype(BF16)
    o_ref[0] = jnp.dot(ca, w, preferred_element_type=F32) + b_ref[0]


def _ada_mods(c, ada_w, ada_b):
    out = pl.pallas_call(
        _ada_kernel,
        grid=(DEPTH, 6 * D // ADA_TN),
        in_specs=[
            pl.BlockSpec((BATCH, D), lambda i, j: (0, 0)),
            pl.BlockSpec((1, D, ADA_TN), lambda i, j: (i, 0, j)),
            pl.BlockSpec((1, 1, ADA_TN), lambda i, j: (i, 0, j)),
        ],
        out_specs=pl.BlockSpec((1, BATCH, ADA_TN), lambda i, j: (i, 0, j)),
        out_shape=jax.ShapeDtypeStruct((DEPTH, BATCH, 6 * D), F32),
        compiler_params=pltpu.CompilerParams(
            dimension_semantics=("arbitrary", "arbitrary"), vmem_limit_bytes=VMEM_LIMIT),
        name="ada_mods",
    )(c, ada_w, ada_b.reshape(DEPTH, 1, 6 * D))
    return out.reshape(DEPTH, BATCH, 6, D)


def _combine(x1, y0_ref, y1_ref, wc_ref, g_f, ln_g, ln_b):
    wc = wc_ref[0]
    w0 = pltpu.repeat(wc[:, :LANES], D // LANES, axis=1)
    w1 = pltpu.repeat(wc[:, LANES:], D // LANES, axis=1)
    y = w0 * _unpack_rows(y0_ref[0]) + w1 * _unpack_rows(y1_ref[0])
    return _layer_norm(ALPHA * x1 + (1.0 + g_f) * y, ln_g, ln_b)


def _route(h2, wr_ref, rb_ref, tri_ref, cnt_ref, ri_ref, wc_ref, cnto_ref):
    lt = lax.dot_general(wr_ref[...], h2, (((1,), (1,)), ((), ())),
                         preferred_element_type=F32) + rb_ref[...]
    iota8 = lax.broadcasted_iota(I32, (SUBLANES, TS), 0).astype(F32)
    gl = lt[0:SUBLANES]
    gmax = jnp.max(gl, axis=0, keepdims=True)
    gidx = jnp.min(jnp.where(gl == gmax, iota8, float(SUBLANES)), axis=0, keepdims=True)
    gw = 1.0 / jnp.sum(jnp.exp(gl - gmax), axis=0, keepdims=True)
    el = lt[SUBLANES:2 * SUBLANES]
    for g in range(1, N_GROUPS):
        el = jnp.where(gidx == float(g), lt[SUBLANES * (g + 1):SUBLANES * (g + 2)], el)
    m1 = jnp.max(el, axis=0, keepdims=True)
    i1 = jnp.min(jnp.where(el == m1, iota8, float(SUBLANES)), axis=0, keepdims=True)
    el2 = jnp.where(iota8 == i1, -jnp.inf, el)
    m2 = jnp.max(el2, axis=0, keepdims=True)
    i2 = jnp.min(jnp.where(el2 == m2, iota8, float(SUBLANES)), axis=0, keepdims=True)
    r = jnp.exp(m2 - m1)
    w_a = gw / (1.0 + r)
    w_b = gw * r / (1.0 + r)
    e1 = gidx * float(EPG) + i1
    e2 = gidx * float(EPG) + i2

    iota_e = lax.broadcasted_iota(I32, (NE, TS), 0).astype(F32)
    oh1 = iota_e == e1
    oh2 = iota_e == e2
    oh = jnp.concatenate([jnp.where(oh1, 1.0, 0.0), jnp.where(oh2, 1.0, 0.0)], axis=0)
    before = jnp.dot(oh.astype(BF16), tri_ref[...], preferred_element_type=F32)
    tot = jnp.sum(oh, axis=1, keepdims=True)
    cnt = cnt_ref[...]
    base = pltpu.repeat(cnt, TS // LANES, axis=1)
    tot1 = tot[:NE]
    tot2 = tot[NE:]
    rank1 = jnp.sum(jnp.where(oh1, base + before[:NE], 0.0), axis=0, keepdims=True)
    rank2 = jnp.sum(jnp.where(oh2, base + tot1 + before[NE:], 0.0), axis=0, keepdims=True)
    new_cnt = cnt + tot1 + tot2
    cnt_ref[...] = new_cnt
    cnto_ref[...] = new_cnt.astype(I32)

    ri_ref[0:1, :] = e1.astype(I32)
    ri_ref[1:2, :] = e2.astype(I32)
    ri_ref[2:3, :] = rank1.astype(I32)
    ri_ref[3:4, :] = rank2.astype(I32)
    ri_ref[4:8, :] = jnp.zeros((4, TS), I32)
    wc_ref[0, :, :LANES] = jnp.broadcast_to(w_a, (LANES, TS)).T
    wc_ref[0, :, LANES:] = jnp.broadcast_to(w_b, (LANES, TS)).T


def _conv_taps(uext_ref, u, w_dw, halo, width):
    s = pl.program_id(1)

    @pl.when(s == 0)
    def _():
        uext_ref[0:halo, :] = jnp.zeros((halo, D), F32)

    uext_ref[halo:halo + TS, :] = u
    acc = None
    for k in range(width):
        off = halo - (width - 1) + k
        term = w_dw[k:k + 1, :] * uext_ref[off:off + TS, :]
        acc = term if acc is None else acc + term
    uext_ref[0:halo, :] = uext_ref[TS:TS + halo, :]
    return acc


def _mixer_kernel(*refs, kind, has_prev):
    it = iter(refs)
    xin_ref = next(it)
    if has_prev:
        y0_ref, y1_ref, wcin_ref, pmods_ref, pln_ref = (next(it) for _ in range(5))
    mods_ref, mln_ref, win_ref = next(it), next(it), next(it)
    if kind == "a":
        bin_ref, wdw_ref, vec_ref = next(it), next(it), next(it)
    else:
        wdw_ref = next(it)
    wout_ref, wr_ref, rb_ref, tri_ref = (next(it) for _ in range(4))
    x1_ref, h2_ref, ri_ref, wc_ref, cnto_ref = (next(it) for _ in range(5))
    uext_ref, cnt_ref = next(it), next(it)

    first = (pl.program_id(0) == 0) & (pl.program_id(1) == 0)

    @pl.when(first)
    def _():
        cnt_ref[...] = jnp.zeros((NE, LANES), F32)

    m = mods_ref[0, 0]
    x = xin_ref[0]
    if has_prev:
        pln = pln_ref[...]
        x = _combine(x, y0_ref, y1_ref, wcin_ref, pmods_ref[0, 0][5:6, :], pln[0:1, :], pln[1:2, :])

    h = (x * (1.0 + m[1:2, :]) + m[0:1, :]).astype(BF16)
    p = jnp.dot(h, win_ref[...], preferred_element_type=F32)
    if kind == "a":
        p = p + bin_ref[...]
        vec = vec_ref[...]
        u = p[:, :D] * _sigmoid(p[:, D:])
        u = _conv_taps(uext_ref, u, wdw_ref[...], HALO_A, CONV_A) + vec[0:1, :]
        u = _layer_norm(u, vec[1:2, :], vec[2:3, :])
        u = u * _sigmoid(u)
        y = jnp.dot(u.astype(BF16), wout_ref[...], preferred_element_type=F32) + vec[3:4, :]
    else:
        gb = p[:, :D]
        q = p[:, D:2 * D] * p[:, 2 * D:]
        u = _conv_taps(uext_ref, q, wdw_ref[...], HALO_B, CONV_B)
        y = jnp.dot((gb * u).astype(BF16), wout_ref[...], preferred_element_type=F32)

    mln = mln_ref[...]
    x1 = _layer_norm(ALPHA * x + (1.0 + m[2:3, :]) * y, mln[0:1, :], mln[1:2, :])
    x1_ref[0] = x1
    h2 = x1 * (1.0 + m[4:5, :]) + m[3:4, :]
    h2_ref[0] = _pack_rows(h2)
    _route(h2.astype(BF16), wr_ref, rb_ref, tri_ref, cnt_ref, ri_ref, wc_ref, cnto_ref)


def _tile_spec(width):
    return pl.BlockSpec((1, TS, width), lambda b, s: (b, s, 0))


def _const_spec(shape):
    nd = len(shape)
    return pl.BlockSpec(shape, lambda b, s: (0,) * nd)


def _mods_spec():
    return pl.BlockSpec((1, 1, 6, D), lambda b, s: (0, b, 0, 0))


def _mixer(kind, xin, prev, mods_l, mix_ln, weights, wr, rb, tri, name):
    has_prev = prev is not None
    args = [xin]
    specs = [_tile_spec(D)]
    if has_prev:
        y0, y1, wcin, pmods, pln = prev
        args += [y0, y1, wcin, pmods, pln]
        specs += [_tile_spec(DP), _tile_spec(DP), _tile_spec(2 * LANES), _mods_spec(), _const_spec((2, D))]
    args += [mods_l, mix_ln]
    specs += [_mods_spec(), _const_spec((2, D))]
    for w in weights:
        args.append(w)
        specs.append(_const_spec(w.shape))
    args += [wr, rb, tri]
    specs += [_const_spec(wr.shape), _const_spec(rb.shape), _const_spec(tri.shape)]
    halo = HALO_A if kind == "a" else HALO_B
    out_shape = (
        jax.ShapeDtypeStruct((BATCH, SEQ, D), F32),
        jax.ShapeDtypeStruct((BATCH, SEQ, DP), U32),
        jax.ShapeDtypeStruct((SUBLANES, T), I32),
        jax.ShapeDtypeStruct((BATCH, SEQ, 2 * LANES), F32),
        jax.ShapeDtypeStruct((NE, LANES), I32),
    )
    out_specs = (
        _tile_spec(D), _tile_spec(DP),
        pl.BlockSpec((SUBLANES, TS), lambda b, s: (0, b * NS + s)),
        _tile_spec(2 * LANES),
        pl.BlockSpec((NE, LANES), lambda b, s: (0, 0)),
    )
    return pl.pallas_call(
        functools.partial(_mixer_kernel, kind=kind, has_prev=has_prev),
        grid=(BATCH, NS),
        in_specs=specs,
        out_specs=out_specs,
        out_shape=out_shape,
        scratch_shapes=[pltpu.VMEM((TS + halo, D), F32), pltpu.VMEM((NE, LANES), F32)],
        compiler_params=pltpu.CompilerParams(
            dimension_semantics=("arbitrary", "arbitrary"), vmem_limit_bytes=VMEM_LIMIT),
        name=name,
    )(*args)


def _dest_kernel(ps_ref, ri_ref, o_ref):
    ri = ri_ref[...]
    e = ri[0:2, :]
    start = jnp.zeros_like(e)
    for k in range(NE):
        start = jnp.where(e == k, ps_ref[k], start)
    o_ref[0:2, :] = start + ri[2:4, :]
    o_ref[2:8, :] = jnp.zeros((6, ri.shape[1]), I32)


def _dest_slots(pad_starts, ri):
    tn = 4096
    return pl.pallas_call(
        _dest_kernel,
        grid_spec=pltpu.PrefetchScalarGridSpec(
            num_scalar_prefetch=1,
            grid=(T // tn,),
            in_specs=[pl.BlockSpec((SUBLANES, tn), lambda i, ps: (0, i))],
            out_specs=pl.BlockSpec((SUBLANES, tn), lambda i, ps: (0, i)),
        ),
        out_shape=jax.ShapeDtypeStruct((SUBLANES, T), I32),
        compiler_params=pltpu.CompilerParams(dimension_semantics=("arbitrary",)),
        name="dest_slots",
    )(pad_starts, ri)


def _expert_kernel(be_ref, nu_ref, x_ref, wg_ref, wu_ref, wd_ref, o_ref, wgu_s, wd_s):
    j = pl.program_id(0)

    @pl.when(j < nu_ref[0])
    def _():
        prev_e = be_ref[jnp.maximum(j - 1, 0)]

        @pl.when((j == 0) | (be_ref[j] != prev_e))
        def _():
            wgu_s[:, :F] = wg_ref[0].astype(BF16)
            wgu_s[:, F:] = wu_ref[0].astype(BF16)
            wd_s[...] = wd_ref[0].astype(BF16)

        x = _unpack_rows(x_ref[...]).astype(BF16)
        gu = jnp.dot(x, wgu_s[...], preferred_element_type=F32)
        g = gu[:, :F]
        hid = (g * _sigmoid(g) * gu[:, F:]).astype(BF16)
        o_ref[...] = _pack_rows(jnp.dot(hid, wd_s[...], preferred_element_type=F32))


def _experts(block_e, n_used, xs, w_gate, w_up, w_down, name):
    def row_map(j, be, nu):
        return (jnp.minimum(j, nu[0] - 1), 0)

    def w_map(j, be, nu):
        return (be[j], 0, 0)

    return pl.pallas_call(
        _expert_kernel,
        grid_spec=pltpu.PrefetchScalarGridSpec(
            num_scalar_prefetch=2,
            grid=(NB,),
            in_specs=[
                pl.BlockSpec((BM, DP), row_map),
                pl.BlockSpec((1, D, F), w_map),
                pl.BlockSpec((1, D, F), w_map),
                pl.BlockSpec((1, F, D), w_map),
            ],
            out_specs=pl.BlockSpec((BM, DP), row_map),
            scratch_shapes=[pltpu.VMEM((D, 2 * F), BF16), pltpu.VMEM((F, D), BF16)],
        ),
        out_shape=jax.ShapeDtypeStruct((NSLOT, DP), U32),
        compiler_params=pltpu.CompilerParams(
            dimension_semantics=("arbitrary",), vmem_limit_bytes=VMEM_LIMIT),
        name=name,
    )(block_e, n_used, xs, w_gate, w_up, w_down)


def _sc_worker_id():
    return lax.axis_index("s") * SC_CORES + lax.axis_index("c")


def _sc_mesh():
    return plsc.VectorSubcoreMesh(core_axis_name="c", subcore_axis_name="s")


def _sc_scratch(n_index_rows):
    return [
        pltpu.VMEM((n_index_rows, SC_CHUNK), I32),
        pltpu.VMEM((2, SC_CHUNK, DP), U32),
        pltpu.SemaphoreType.DMA((2,)),
        pltpu.SemaphoreType.DMA((2,)),
    ]


def _dispatch_rows(h2p, dest):
    per_w = T // SC_WORKERS
    n_chunks = per_w // SC_CHUNK

    @functools.partial(
        pl.kernel, mesh=_sc_mesh(),
        out_type=jax.ShapeDtypeStruct((NSLOT, DP), U32),
        scratch_types=_sc_scratch(TOP_K * n_chunks),
        name="dispatch_rows",
    )
    def k(h2_hbm, dest_hbm, out_hbm, dest_v, rows_v, rsem, wsem):
        wid = _sc_worker_id()
        for kk in range(TOP_K):
            pltpu.sync_copy(dest_hbm.at[kk, pl.ds(wid * n_chunks, n_chunks)],
                            dest_v.at[pl.ds(kk * n_chunks, n_chunks)])
        base = wid * per_w

        def read(c, slot):
            return pltpu.make_async_copy(h2_hbm.at[pl.ds(base + c * SC_CHUNK, SC_CHUNK)],
                                         rows_v.at[slot], rsem.at[slot])

        def write(c, kk, slot):
            return pltpu.make_async_copy(rows_v.at[slot], out_hbm.at[dest_v.at[kk * n_chunks + c]],
                                         wsem.at[slot])

        read(0, 0).start()

        @pl.loop(0, n_chunks, step=2)
        def _(c):
            for b in range(2):
                cc = c + b
                read(cc, b).wait()

                @pl.when(cc + 1 < n_chunks)
                def _():
                    @pl.when(cc >= 1)
                    def _():
                        for kk in range(TOP_K):
                            write(cc - 1, kk, 1 - b).wait()
                    read(cc + 1, 1 - b).start()

                for kk in range(TOP_K):
                    write(cc, kk, b).start()

        for slot, cc in ((0, n_chunks - 2), (1, n_chunks - 1)):
            for kk in range(TOP_K):
                write(cc, kk, slot).wait()

    return k(h2p, dest.reshape(TOP_K, T // SC_CHUNK, SC_CHUNK))


def _return_rows(yb, dest):
    m = TOP_K * T
    per_w = m // SC_WORKERS
    n_chunks = per_w // SC_CHUNK

    @functools.partial(
        pl.kernel, mesh=_sc_mesh(),
        out_type=jax.ShapeDtypeStruct((m, DP), U32),
        scratch_types=_sc_scratch(n_chunks),
        name="return_rows",
    )
    def k(yb_hbm, dest_hbm, out_hbm, idx_v, rows_v, gsem, wsem):
        wid = _sc_worker_id()
        pltpu.sync_copy(dest_hbm.at[pl.ds(wid * n_chunks, n_chunks)], idx_v)
        base = wid * per_w

        def gather(c, slot):
            return pltpu.make_async_copy(yb_hbm.at[idx_v.at[c]], rows_v.at[slot], gsem.at[slot])

        def write(c, slot):
            return pltpu.make_async_copy(rows_v.at[slot], out_hbm.at[pl.ds(base + c * SC_CHUNK, SC_CHUNK)],
                                         wsem.at[slot])

        gather(0, 0).start()

        @pl.loop(0, n_chunks, step=2)
        def _(c):
            for b in range(2):
                cc = c + b
                gather(cc, b).wait()

                @pl.when(cc + 1 < n_chunks)
                def _():
                    @pl.when(cc >= 1)
                    def _():
                        write(cc - 1, 1 - b).wait()
                    gather(cc + 1, 1 - b).start()

                write(cc, b).start()

        write(n_chunks - 2, 0).wait()
        write(n_chunks - 1, 1).wait()

    return k(yb, dest.reshape(m // SC_CHUNK, SC_CHUNK))


def _final_kernel(x1_ref, y0_ref, y1_ref, wc_ref, mods_ref, ln_ref, o_ref):
    ln = ln_ref[...]
    o_ref[0] = _combine(x1_ref[0], y0_ref, y1_ref, wc_ref, mods_ref[0, 0][5:6, :], ln[0:1, :], ln[1:2, :])


def _final(x1, y0, y1, wc, mods_l, ln):
    return pl.pallas_call(
        _final_kernel,
        grid=(BATCH, NS),
        in_specs=[_tile_spec(D), _tile_spec(DP), _tile_spec(DP), _tile_spec(2 * LANES),
                  _mods_spec(), _const_spec((2, D))],
        out_specs=_tile_spec(D),
        out_shape=jax.ShapeDtypeStruct((BATCH, SEQ, D), F32),
        compiler_params=pltpu.CompilerParams(
            dimension_semantics=("arbitrary", "arbitrary"), vmem_limit_bytes=VMEM_LIMIT),
        name="final_combine",
    )(x1, y0, y1, wc, mods_l, ln)


def _plan(counts):
    padded = (counts + BM - 1) // BM * BM
    pad_ends = jnp.cumsum(padded)
    pad_starts = (pad_ends - padded).astype(I32)
    n_used = (pad_ends[-1] // BM).astype(I32)
    starts = jnp.arange(NB, dtype=I32) * BM
    starts = jnp.minimum(starts, pad_ends[-1] - BM)
    block_e = jnp.minimum(jnp.sum(pad_ends[None, :] <= starts[:, None], axis=-1), NE - 1).astype(I32)
    return pad_starts, block_e, n_used.reshape(1)


def _router_params(w_group, b_group, w_expert, b_expert):
    wr = jnp.zeros((NR, D), F32)
    wr = wr.at[0:N_GROUPS].set(w_group.T).at[SUBLANES:SUBLANES + NE].set(w_expert.T)
    rb = jnp.full((NR,), NEG, F32)
    rb = rb.at[0:N_GROUPS].set(b_group).at[SUBLANES:SUBLANES + NE].set(b_expert)
    rb = rb.at[SUBLANES + NE:].set(0.0)
    return wr.astype(BF16), rb.reshape(NR, 1)


def kernel(x, c, ada_w, ada_b, a_w_in, a_b_in, a_w_dw, a_b_dw, a_ln_g, a_ln_b, a_w_out, a_b_out,
           b_w_in, b_w_dw, b_w_out, mix_ln_g, mix_ln_b, ffn_ln_g, ffn_ln_b,
           r_w_group, r_b_group, r_w_expert, r_b_expert, e_w_gate, e_w_up, e_w_down):
    mods = _ada_mods(c, ada_w, ada_b)
    tri = (jnp.arange(TS)[:, None] < jnp.arange(TS)[None, :]).astype(BF16)
    prev = None
    xin = x
    for i in range(DEPTH):
        j = i // 2
        mods_l = mods[i:i + 1]
        mix_ln = jnp.stack([mix_ln_g[i], mix_ln_b[i]])
        wr, rb = _router_params(r_w_group[i], r_b_group[i], r_w_expert[i], r_b_expert[i])
        if i % 2 == 0:
            weights = [a_w_in[j].astype(BF16), a_b_in[j].reshape(1, 2 * D), a_w_dw[j],
                       jnp.stack([a_b_dw[j], a_ln_g[j], a_ln_b[j], a_b_out[j]]),
                       a_w_out[j].astype(BF16)]
            kind = "a"
        else:
            weights = [b_w_in[j].astype(BF16), b_w_dw[j], b_w_out[j].astype(BF16)]
            kind = "b"
        x1, h2, ri, wc, counts = _mixer(kind, xin, prev, mods_l, mix_ln, weights, wr, rb, tri,
                                        name=f"mixer_{kind}{i}")
        pad_starts, block_e, n_used = _plan(counts[:, 0])
        dest = _dest_slots(pad_starts, ri)[0:TOP_K]
        xs = _dispatch_rows(h2.reshape(T, DP), dest)
        yb = _experts(block_e, n_used, xs, e_w_gate[i], e_w_up[i], e_w_down[i], name=f"experts{i}")
        yk = _return_rows(yb, dest)
        y0 = yk[:T].reshape(BATCH, SEQ, DP)
        y1 = yk[T:].reshape(BATCH, SEQ, DP)
        prev = (y0, y1, wc, mods_l, jnp.stack([ffn_ln_g[i], ffn_ln_b[i]]))
        xin = x1
    y0, y1, wc, mods_l, ln = prev
    return _final(xin, y0, y1, wc, mods_l, ln)
```

```python
import functools

import jax
import jax.numpy as jnp
from jax import lax
from jax.experimental import pallas as pl
from jax.experimental.pallas import tpu as pltpu
from jax.experimental.pallas import tpu_sc as plsc

F32 = jnp.float32
BF16 = jnp.bfloat16
I32 = jnp.int32
U32 = jnp.uint32

D = 1024
BATCH = 4
SEQ = 8192
T = BATCH * SEQ
DEPTH = 4
N_GROUPS = 4
EPG = 8
NE = N_GROUPS * EPG
TOP_K = 2
F = D // 2
CONV_A = 31
CONV_B = 3
ALPHA = (2.0 * DEPTH) ** 0.25
LN_EPS = 1e-5

LANES = 128
SUBLANES = 8
VMEM_LIMIT = 56 * 1024 * 1024

TS = 512
NS = SEQ // TS
HALO_A = 32
HALO_B = 8
BM = 256
NSLOT = T * TOP_K + NE * BM
NB = NSLOT // BM
NR = 48
ADA_TN = 1536
NEG = -1e30
DP = D // 2
DT = D // LANES
assert DT == SUBLANES

SC_CORES = 2
SC_SUBCORES = 16
SC_WORKERS = SC_CORES * SC_SUBCORES
SC_CHUNK = 64


def _sigmoid(x):
    return 1.0 / (1.0 + jnp.exp(-x))


def _pack_rows(x):
    return pltpu.pack_elementwise([x[:, :DP], x[:, DP:]], packed_dtype=BF16)


def _unpack_rows(p):
    lo = pltpu.unpack_elementwise(p, index=0, packed_dtype=BF16, unpacked_dtype=F32)
    hi = pltpu.unpack_elementwise(p, index=1, packed_dtype=BF16, unpacked_dtype=F32)
    return jnp.concatenate([lo, hi], axis=1)


def _layer_norm(x, g, b):
    mu = jnp.mean(x, axis=-1, keepdims=True)
    xc = x - mu
    var = jnp.mean(xc * xc, axis=-1, keepdims=True)
    return xc * lax.rsqrt(var + LN_EPS) * g + b


def _ada_kernel(c_ref, w_ref, b_ref, o_ref):
    c = c_ref[...]
    ca = (c * _sigmoid(c)).astype(BF16)
    w = w_ref[0].astype(BF16)
    o_ref[0] = jnp.dot(ca, w, preferred_element_type=F32) + b_ref[0]


def _ada_mods(c, ada_w, ada_b):
    out = pl.pallas_call(
        _ada_kernel,
        grid=(DEPTH, 6 * D // ADA_TN),
        in_specs=[
            pl.BlockSpec((BATCH, D), lambda i, j: (0, 0)),
            pl.BlockSpec((1, D, ADA_TN), lambda i, j: (i, 0, j)),
            pl.BlockSpec((1, 1, ADA_TN), lambda i, j: (i, 0, j)),
        ],
        out_specs=pl.BlockSpec((1, BATCH, ADA_TN), lambda i, j: (i, 0, j)),
        out_shape=jax.ShapeDtypeStruct((DEPTH, BATCH, 6 * D), F32),
        compiler_params=pltpu.CompilerParams(
            dimension_semantics=("arbitrary", "arbitrary"), vmem_limit_bytes=VMEM_LIMIT),
        name="ada_mods",
    )(c, ada_w, ada_b.reshape(DEPTH, 1, 6 * D))
    return out.reshape(DEPTH, BATCH, 6, D)


def _combine(x1, y0_ref, y1_ref, wc_ref, g_f, ln_g, ln_b):
    wc = wc_ref[0]
    w0 = jnp.tile(wc[:, :LANES], (1, D // LANES))
    w1 = jnp.tile(wc[:, LANES:], (1, D // LANES))
    y = w0 * _unpack_rows(y0_ref[0]) + w1 * _unpack_rows(y1_ref[0])
    return _layer_norm(ALPHA * x1 + (1.0 + g_f) * y, ln_g, ln_b)


def _route(h2, wr_ref, rb_ref, tri_ref, cnt_ref, ri_ref, wc_ref, cnto_ref):
    lt = lax.dot_general(wr_ref[...], h2, (((1,), (1,)), ((), ())),
                         preferred_element_type=F32) + rb_ref[...]
    iota8 = lax.broadcasted_iota(I32, (SUBLANES, TS), 0).astype(F32)
    gl = lt[0:SUBLANES]
    gmax = jnp.max(gl, axis=0, keepdims=True)
    gidx = jnp.min(jnp.where(gl == gmax, iota8, float(SUBLANES)), axis=0, keepdims=True)
    gw = 1.0 / jnp.sum(jnp.exp(gl - gmax), axis=0, keepdims=True)
    el = lt[SUBLANES:2 * SUBLANES]
    for g in range(1, N_GROUPS):
        el = jnp.where(gidx == float(g), lt[SUBLANES * (g + 1):SUBLANES * (g + 2)], el)
    m1 = jnp.max(el, axis=0, keepdims=True)
    i1 = jnp.min(jnp.where(el == m1, iota8, float(SUBLANES)), axis=0, keepdims=True)
    el2 = jnp.where(iota8 == i1, -jnp.inf, el)
    m2 = jnp.max(el2, axis=0, keepdims=True)
    i2 = jnp.min(jnp.where(el2 == m2, iota8, float(SUBLANES)), axis=0, keepdims=True)
    r = jnp.exp(m2 - m1)
    w_a = gw / (1.0 + r)
    w_b = gw * r / (1.0 + r)
    e1 = gidx * float(EPG) + i1
    e2 = gidx * float(EPG) + i2

    iota_e = lax.broadcasted_iota(I32, (NE, TS), 0).astype(F32)
    oh1 = iota_e == e1
    oh2 = iota_e == e2
    oh = jnp.concatenate([jnp.where(oh1, 1.0, 0.0), jnp.where(oh2, 1.0, 0.0)], axis=0)
    before = jnp.dot(oh.astype(BF16), tri_ref[...], preferred_element_type=F32)
    tot = jnp.sum(oh, axis=1, keepdims=True)
    cnt = cnt_ref[...]
    base = jnp.tile(cnt, (1, TS // LANES))
    tot1 = tot[:NE]
    tot2 = tot[NE:]
    rank1 = jnp.sum(jnp.where(oh1, base + before[:NE], 0.0), axis=0, keepdims=True)
    rank2 = jnp.sum(jnp.where(oh2, base + tot1 + before[NE:], 0.0), axis=0, keepdims=True)
    new_cnt = cnt + tot1 + tot2
    cnt_ref[...] = new_cnt
    cnto_ref[...] = new_cnt.astype(I32)

    ri_ref[0:1, :] = e1.astype(I32)
    ri_ref[1:2, :] = e2.astype(I32)
    ri_ref[2:3, :] = rank1.astype(I32)
    ri_ref[3:4, :] = rank2.astype(I32)
    ri_ref[4:8, :] = jnp.zeros((4, TS), I32)
    wc_ref[0, :, :LANES] = jnp.broadcast_to(w_a, (LANES, TS)).T
    wc_ref[0, :, LANES:] = jnp.broadcast_to(w_b, (LANES, TS)).T


def _conv_taps(uext_ref, u, w_dw, halo, width):
    s = pl.program_id(1)

    @pl.when(s == 0)
    def _():
        uext_ref[0:halo, :] = jnp.zeros((halo, D), F32)

    uext_ref[halo:halo + TS, :] = u
    acc = None
    for k in range(width):
        off = halo - (width - 1) + k
        term = w_dw[k:k + 1, :] * uext_ref[off:off + TS, :]
        acc = term if acc is None else acc + term
    uext_ref[0:halo, :] = uext_ref[TS:TS + halo, :]
    return acc


def _conv_time_major(tm_ref, o2_ref, u, wk_ref, bias):
    s = pl.program_id(1)

    @pl.when(s == 0)
    def _():
        tm_ref[0:HALO_A * DT, :] = jnp.zeros((HALO_A * DT, LANES), F32)

    for j in range(DT):
        tm_ref[pl.ds(HALO_A * DT + j, TS, stride=DT), :] = u[:, j * LANES:(j + 1) * LANES]
    acc = None
    for k in range(CONV_A):
        off = (HALO_A - (CONV_A - 1) + k) * DT
        term = tm_ref[off:off + TS * DT, :].reshape(TS, DT, LANES) * wk_ref[k]
        acc = term if acc is None else acc + term
    acc = acc + bias
    tm_ref[0:HALO_A * DT, :] = tm_ref[TS * DT:(TS + HALO_A) * DT, :]
    o2_ref[...] = acc.reshape(TS * DT, LANES)
    return jnp.concatenate([o2_ref[pl.ds(j, TS, stride=DT), :] for j in range(DT)], axis=1)


def _mixer_kernel(*refs, kind, has_prev):
    it = iter(refs)
    xin_ref = next(it)
    if has_prev:
        y0_ref, y1_ref, wcin_ref, pmods_ref, pln_ref = (next(it) for _ in range(5))
    mods_ref, mln_ref, win_ref = next(it), next(it), next(it)
    if kind == "a":
        bin_ref, wdw_ref, bdw_ref, vec_ref = next(it), next(it), next(it), next(it)
    else:
        wdw_ref = next(it)
    wout_ref, wr_ref, rb_ref, tri_ref = (next(it) for _ in range(4))
    x1_ref, h2_ref, ri_ref, wc_ref, cnto_ref = (next(it) for _ in range(5))
    if kind == "a":
        uext_ref, o2_ref, cnt_ref = next(it), next(it), next(it)
    else:
        uext_ref, cnt_ref = next(it), next(it)

    first = (pl.program_id(0) == 0) & (pl.program_id(1) == 0)

    @pl.when(first)
    def _():
        cnt_ref[...] = jnp.zeros((NE, LANES), F32)

    m = mods_ref[0, 0]
    x = xin_ref[0]
    if has_prev:
        pln = pln_ref[...]
        x = _combine(x, y0_ref, y1_ref, wcin_ref, pmods_ref[0, 0][5:6, :], pln[0:1, :], pln[1:2, :])

    h = (x * (1.0 + m[1:2, :]) + m[0:1, :]).astype(BF16)
    p = jnp.dot(h, win_ref[...], preferred_element_type=F32)
    if kind == "a":
        p = p + bin_ref[...]
        vec = vec_ref[...]
        u = p[:, :D] * _sigmoid(p[:, D:])
        u = _conv_time_major(uext_ref, o2_ref, u, wdw_ref, bdw_ref[...])
        u = _layer_norm(u, vec[0:1, :], vec[1:2, :])
        u = u * _sigmoid(u)
        y = jnp.dot(u.astype(BF16), wout_ref[...], preferred_element_type=F32) + vec[2:3, :]
    else:
        gb = p[:, :D]
        q = p[:, D:2 * D] * p[:, 2 * D:]
        u = _conv_taps(uext_ref, q, wdw_ref[...], HALO_B, CONV_B)
        y = jnp.dot((gb * u).astype(BF16), wout_ref[...], preferred_element_type=F32)

    mln = mln_ref[...]
    x1 = _layer_norm(ALPHA * x + (1.0 + m[2:3, :]) * y, mln[0:1, :], mln[1:2, :])
    x1_ref[0] = x1
    h2 = x1 * (1.0 + m[4:5, :]) + m[3:4, :]
    h2_ref[0] = _pack_rows(h2)
    _route(h2.astype(BF16), wr_ref, rb_ref, tri_ref, cnt_ref, ri_ref, wc_ref, cnto_ref)


def _tile_spec(width):
    return pl.BlockSpec((1, TS, width), lambda b, s: (b, s, 0))


def _yk_spec(k):
    return pl.BlockSpec((1, TS, DP), lambda b, s: (k, b * NS + s, 0))


def _const_spec(shape):
    nd = len(shape)
    return pl.BlockSpec(shape, lambda b, s: (0,) * nd)


def _mods_spec():
    return pl.BlockSpec((1, 1, 6, D), lambda b, s: (0, b, 0, 0))


def _mixer(kind, xin, prev, mods_l, mix_ln, weights, wr, rb, tri, name):
    has_prev = prev is not None
    args = [xin]
    specs = [_tile_spec(D)]
    if has_prev:
        yk, wcin, pmods, pln = prev
        args += [yk, yk, wcin, pmods, pln]
        specs += [_yk_spec(0), _yk_spec(1), _tile_spec(2 * LANES), _mods_spec(), _const_spec((2, D))]
    args += [mods_l, mix_ln]
    specs += [_mods_spec(), _const_spec((2, D))]
    for w in weights:
        args.append(w)
        specs.append(_const_spec(w.shape))
    args += [wr, rb, tri]
    specs += [_const_spec(wr.shape), _const_spec(rb.shape), _const_spec(tri.shape)]
    if kind == "a":
        conv_scratch = [pltpu.VMEM(((TS + HALO_A) * DT, LANES), F32), pltpu.VMEM((TS * DT, LANES), F32)]
    else:
        conv_scratch = [pltpu.VMEM((TS + HALO_B, D), F32)]
    out_shape = (
        jax.ShapeDtypeStruct((BATCH, SEQ, D), F32),
        jax.ShapeDtypeStruct((BATCH, SEQ, DP), U32),
        jax.ShapeDtypeStruct((SUBLANES, T), I32),
        jax.ShapeDtypeStruct((BATCH, SEQ, 2 * LANES), F32),
        jax.ShapeDtypeStruct((NE, LANES), I32),
    )
    out_specs = (
        _tile_spec(D), _tile_spec(DP),
        pl.BlockSpec((SUBLANES, TS), lambda b, s: (0, b * NS + s)),
        _tile_spec(2 * LANES),
        pl.BlockSpec((NE, LANES), lambda b, s: (0, 0)),
    )
    return pl.pallas_call(
        functools.partial(_mixer_kernel, kind=kind, has_prev=has_prev),
        grid=(BATCH, NS),
        in_specs=specs,
        out_specs=out_specs,
        out_shape=out_shape,
        scratch_shapes=conv_scratch + [pltpu.VMEM((NE, LANES), F32)],
        compiler_params=pltpu.CompilerParams(
            dimension_semantics=("arbitrary", "arbitrary"), vmem_limit_bytes=VMEM_LIMIT),
        name=name,
    )(*args)


def _dest_kernel(ps_ref, ri_ref, o_ref):
    ri = ri_ref[...]
    e = ri[0:2, :]
    start = jnp.zeros_like(e)
    for k in range(NE):
        start = jnp.where(e == k, ps_ref[k], start)
    o_ref[0:2, :] = start + ri[2:4, :]
    o_ref[2:8, :] = jnp.zeros((6, ri.shape[1]), I32)


def _dest_slots(pad_starts, ri):
    tn = 4096
    return pl.pallas_call(
        _dest_kernel,
        grid_spec=pltpu.PrefetchScalarGridSpec(
            num_scalar_prefetch=1,
            grid=(T // tn,),
            in_specs=[pl.BlockSpec((SUBLANES, tn), lambda i, ps: (0, i))],
            out_specs=pl.BlockSpec((SUBLANES, tn), lambda i, ps: (0, i)),
        ),
        out_shape=jax.ShapeDtypeStruct((SUBLANES, T), I32),
        compiler_params=pltpu.CompilerParams(dimension_semantics=("arbitrary",)),
        name="dest_slots",
    )(pad_starts, ri)


def _expert_kernel(be_ref, nu_ref, x_ref, wg_ref, wu_ref, wd_ref, o_ref, wgu_s, wd_s):
    j = pl.program_id(0)

    @pl.when(j < nu_ref[0])
    def _():
        prev_e = be_ref[jnp.maximum(j - 1, 0)]

        @pl.when((j == 0) | (be_ref[j] != prev_e))
        def _():
            wgu_s[:, :F] = wg_ref[0, 0].astype(BF16)
            wgu_s[:, F:] = wu_ref[0, 0].astype(BF16)
            wd_s[...] = wd_ref[0, 0].astype(BF16)

        x = _unpack_rows(x_ref[...]).astype(BF16)
        gu = jnp.dot(x, wgu_s[...], preferred_element_type=F32)
        g = gu[:, :F]
        hid = (g * _sigmoid(g) * gu[:, F:]).astype(BF16)
        o_ref[...] = _pack_rows(jnp.dot(hid, wd_s[...], preferred_element_type=F32))


def _experts(layer, block_e, n_used, xs, w_gate, w_up, w_down, name):
    def row_map(j, be, nu):
        return (jnp.minimum(j, nu[0] - 1), 0)

    def w_map(j, be, nu):
        return (layer, be[j], 0, 0)

    return pl.pallas_call(
        _expert_kernel,
        grid_spec=pltpu.PrefetchScalarGridSpec(
            num_scalar_prefetch=2,
            grid=(NB,),
            in_specs=[
                pl.BlockSpec((BM, DP), row_map),
                pl.BlockSpec((1, 1, D, F), w_map),
                pl.BlockSpec((1, 1, D, F), w_map),
                pl.BlockSpec((1, 1, F, D), w_map),
            ],
            out_specs=pl.BlockSpec((BM, DP), row_map),
            scratch_shapes=[pltpu.VMEM((D, 2 * F), BF16), pltpu.VMEM((F, D), BF16)],
        ),
        out_shape=jax.ShapeDtypeStruct((NSLOT, DP), U32),
        compiler_params=pltpu.CompilerParams(
            dimension_semantics=("arbitrary",), vmem_limit_bytes=VMEM_LIMIT),
        name=name,
    )(block_e, n_used, xs, w_gate, w_up, w_down)


def _sc_worker_id():
    return lax.axis_index("s") * SC_CORES + lax.axis_index("c")


def _sc_mesh():
    return plsc.VectorSubcoreMesh(core_axis_name="c", subcore_axis_name="s")


def _sc_scratch(n_index_rows):
    return [
        pltpu.VMEM((n_index_rows, SC_CHUNK), I32),
        pltpu.VMEM((2, SC_CHUNK, DP), U32),
        pltpu.SemaphoreType.DMA((2,)),
        pltpu.SemaphoreType.DMA((2,)),
    ]


def _dispatch_rows(h2p, dest):
    per_w = T // SC_WORKERS
    n_chunks = per_w // SC_CHUNK

    @functools.partial(
        pl.kernel, mesh=_sc_mesh(),
        out_type=jax.ShapeDtypeStruct((NSLOT, DP), U32),
        scratch_types=_sc_scratch(TOP_K * n_chunks),
        name="dispatch_rows",
    )
    def k(h2_hbm, dest_hbm, out_hbm, dest_v, rows_v, rsem, wsem):
        wid = _sc_worker_id()
        for kk in range(TOP_K):
            pltpu.sync_copy(dest_hbm.at[kk, pl.ds(wid * n_chunks, n_chunks)],
                            dest_v.at[pl.ds(kk * n_chunks, n_chunks)])
        base = wid * per_w

        def read(c, slot):
            return pltpu.make_async_copy(h2_hbm.at[pl.ds(base + c * SC_CHUNK, SC_CHUNK)],
                                         rows_v.at[slot], rsem.at[slot])

        def write(c, kk, slot):
            return pltpu.make_async_copy(rows_v.at[slot], out_hbm.at[dest_v.at[kk * n_chunks + c]],
                                         wsem.at[slot])

        read(0, 0).start()

        @pl.loop(0, n_chunks, step=2)
        def _(c):
            for b in range(2):
                cc = c + b
                read(cc, b).wait()

                @pl.when(cc + 1 < n_chunks)
                def _():
                    @pl.when(cc >= 1)
                    def _():
                        for kk in range(TOP_K):
                            write(cc - 1, kk, 1 - b).wait()
                    read(cc + 1, 1 - b).start()

                for kk in range(TOP_K):
                    write(cc, kk, b).start()

        for slot, cc in ((0, n_chunks - 2), (1, n_chunks - 1)):
            for kk in range(TOP_K):
                write(cc, kk, slot).wait()

    return k(h2p, dest.reshape(TOP_K, T // SC_CHUNK, SC_CHUNK))


def _return_rows(yb, dest):
    m = TOP_K * T
    per_w = m // SC_WORKERS
    n_chunks = per_w // SC_CHUNK

    @functools.partial(
        pl.kernel, mesh=_sc_mesh(),
        out_type=jax.ShapeDtypeStruct((m, DP), U32),
        scratch_types=_sc_scratch(n_chunks),
        name="return_rows",
    )
    def k(yb_hbm, dest_hbm, out_hbm, idx_v, rows_v, gsem, wsem):
        wid = _sc_worker_id()
        pltpu.sync_copy(dest_hbm.at[pl.ds(wid * n_chunks, n_chunks)], idx_v)
        base = wid * per_w

        def gather(c, slot):
            return pltpu.make_async_copy(yb_hbm.at[idx_v.at[c]], rows_v.at[slot], gsem.at[slot])

        def write(c, slot):
            return pltpu.make_async_copy(rows_v.at[slot], out_hbm.at[pl.ds(base + c * SC_CHUNK, SC_CHUNK)],
                                         wsem.at[slot])

        gather(0, 0).start()

        @pl.loop(0, n_chunks, step=2)
        def _(c):
            for b in range(2):
                cc = c + b
                gather(cc, b).wait()

                @pl.when(cc + 1 < n_chunks)
                def _():
                    @pl.when(cc >= 1)
                    def _():
                        write(cc - 1, 1 - b).wait()
                    gather(cc + 1, 1 - b).start()

                write(cc, b).start()

        write(n_chunks - 2, 0).wait()
        write(n_chunks - 1, 1).wait()

    return k(yb, dest.reshape(m // SC_CHUNK, SC_CHUNK))


def _final_kernel(x1_ref, y0_ref, y1_ref, wc_ref, mods_ref, ln_ref, o_ref):
    ln = ln_ref[...]
    o_ref[0] = _combine(x1_ref[0], y0_ref, y1_ref, wc_ref, mods_ref[0, 0][5:6, :], ln[0:1, :], ln[1:2, :])


def _final(x1, yk, wc, mods_l, ln):
    return pl.pallas_call(
        _final_kernel,
        grid=(BATCH, NS),
        in_specs=[_tile_spec(D), _yk_spec(0), _yk_spec(1), _tile_spec(2 * LANES),
                  _mods_spec(), _const_spec((2, D))],
        out_specs=_tile_spec(D),
        out_shape=jax.ShapeDtypeStruct((BATCH, SEQ, D), F32),
        compiler_params=pltpu.CompilerParams(
            dimension_semantics=("arbitrary", "arbitrary"), vmem_limit_bytes=VMEM_LIMIT),
        name="final_combine",
    )(x1, yk, yk, wc, mods_l, ln)


def _plan(counts):
    padded = (counts + BM - 1) // BM * BM
    pad_ends = jnp.cumsum(padded)
    pad_starts = (pad_ends - padded).astype(I32)
    n_used = (pad_ends[-1] // BM).astype(I32)
    starts = jnp.arange(NB, dtype=I32) * BM
    starts = jnp.minimum(starts, pad_ends[-1] - BM)
    block_e = jnp.minimum(jnp.sum(pad_ends[None, :] <= starts[:, None], axis=-1), NE - 1).astype(I32)
    return pad_starts, block_e, n_used.reshape(1)


def _router_params(w_group, b_group, w_expert, b_expert):
    wr = jnp.zeros((NR, D), F32)
    wr = wr.at[0:N_GROUPS].set(w_group.T).at[SUBLANES:SUBLANES + NE].set(w_expert.T)
    rb = jnp.full((NR,), NEG, F32)
    rb = rb.at[0:N_GROUPS].set(b_group).at[SUBLANES:SUBLANES + NE].set(b_expert)
    rb = rb.at[SUBLANES + NE:].set(0.0)
    return wr.astype(BF16), rb.reshape(NR, 1)


def kernel(x, c, ada_w, ada_b, a_w_in, a_b_in, a_w_dw, a_b_dw, a_ln_g, a_ln_b, a_w_out, a_b_out,
           b_w_in, b_w_dw, b_w_out, mix_ln_g, mix_ln_b, ffn_ln_g, ffn_ln_b,
           r_w_group, r_b_group, r_w_expert, r_b_expert, e_w_gate, e_w_up, e_w_down):
    mods = _ada_mods(c, ada_w, ada_b)
    tri = (jnp.arange(TS)[:, None] < jnp.arange(TS)[None, :]).astype(BF16)
    prev = None
    xin = x
    for i in range(DEPTH):
        j = i // 2
        mods_l = mods[i:i + 1]
        mix_ln = jnp.stack([mix_ln_g[i], mix_ln_b[i]])
        wr, rb = _router_params(r_w_group[i], r_b_group[i], r_w_expert[i], r_b_expert[i])
        if i % 2 == 0:
            weights = [a_w_in[j].astype(BF16), a_b_in[j].reshape(1, 2 * D),
                       a_w_dw[j].reshape(CONV_A, DT, LANES), a_b_dw[j].reshape(DT, LANES),
                       jnp.stack([a_ln_g[j], a_ln_b[j], a_b_out[j]]),
                       a_w_out[j].astype(BF16)]
            kind = "a"
        else:
            weights = [b_w_in[j].astype(BF16), b_w_dw[j], b_w_out[j].astype(BF16)]
            kind = "b"
        x1, h2, ri, wc, counts = _mixer(kind, xin, prev, mods_l, mix_ln, weights, wr, rb, tri,
                                        name=f"mixer_{kind}{i}")
        pad_starts, block_e, n_used = _plan(counts[:, 0])
        dest = _dest_slots(pad_starts, ri)[0:TOP_K]
        xs = _dispatch_rows(h2.reshape(T, DP), dest)
        yb = _experts(i, block_e, n_used, xs, e_w_gate, e_w_up, e_w_down, name=f"experts{i}")
        yk = _return_rows(yb, dest).reshape(TOP_K, T, DP)
        prev = (yk, wc, mods_l, jnp.stack([ffn_ln_g[i], ffn_ln_b[i]]))
        xin = x1
    yk, wc, mods_l, ln = prev
    return _final(xin, yk, wc, mods_l, ln)
```

```python
import functools

import jax
import jax.numpy as jnp
from jax import lax
from jax.experimental import pallas as pl
from jax.experimental.pallas import tpu as pltpu
from jax.experimental.pallas import tpu_sc as plsc

F32 = jnp.float32
BF16 = jnp.bfloat16
I32 = jnp.int32
U32 = jnp.uint32

D = 1024
BATCH = 4
SEQ = 8192
T = BATCH * SEQ
DEPTH = 4
N_GROUPS = 4
EPG = 8
NE = N_GROUPS * EPG
TOP_K = 2
F = D // 2
CONV_A = 31
CONV_B = 3
ALPHA = (2.0 * DEPTH) ** 0.25
LN_EPS = 1e-5

LANES = 128
SUBLANES = 8
VMEM_LIMIT = 56 * 1024 * 1024

TS = 512
SR = 256
NS = SEQ // TS
HALO_A = 32
HALO_B = 8
BM = 512
NSLOT = T * TOP_K + NE * BM
NB = NSLOT // BM
NR = 48
ADA_TN = 1536
NEG = -1e30
DP = D // 2
DT = D // LANES
assert DT == SUBLANES

SC_CORES = 2
SC_SUBCORES = 16
SC_WORKERS = SC_CORES * SC_SUBCORES
SC_CHUNK = 64


def _sigmoid(x):
    return 1.0 / (1.0 + jnp.exp(-x))


def _pack_rows(x):
    return pltpu.pack_elementwise([x[:, :DP], x[:, DP:]], packed_dtype=BF16)


def _unpack_rows(p):
    lo = pltpu.unpack_elementwise(p, index=0, packed_dtype=BF16, unpacked_dtype=F32)
    hi = pltpu.unpack_elementwise(p, index=1, packed_dtype=BF16, unpacked_dtype=F32)
    return jnp.concatenate([lo, hi], axis=1)


def _layer_norm(x, g, b):
    mu = jnp.mean(x, axis=-1, keepdims=True)
    xc = x - mu
    var = jnp.mean(xc * xc, axis=-1, keepdims=True)
    return xc * lax.rsqrt(var + LN_EPS) * g + b


def _ada_kernel(c_ref, w_ref, b_ref, o_ref):
    c = c_ref[...]
    ca = (c * _sigmoid(c)).astype(BF16)
    w = w_ref[0].astype(BF16)
    o_ref[0] = jnp.dot(ca, w, preferred_element_type=F32) + b_ref[0]


def _ada_mods(c, ada_w, ada_b):
    out = pl.pallas_call(
        _ada_kernel,
        grid=(DEPTH, 6 * D // ADA_TN),
        in_specs=[
            pl.BlockSpec((BATCH, D), lambda i, j: (0, 0)),
            pl.BlockSpec((1, D, ADA_TN), lambda i, j: (i, 0, j)),
            pl.BlockSpec((1, 1, ADA_TN), lambda i, j: (i, 0, j)),
        ],
        out_specs=pl.BlockSpec((1, BATCH, ADA_TN), lambda i, j: (i, 0, j)),
        out_shape=jax.ShapeDtypeStruct((DEPTH, BATCH, 6 * D), F32),
        compiler_params=pltpu.CompilerParams(
            dimension_semantics=("arbitrary", "arbitrary"), vmem_limit_bytes=VMEM_LIMIT),
        name="ada_mods",
    )(c, ada_w, ada_b.reshape(DEPTH, 1, 6 * D))
    return out.reshape(DEPTH, BATCH, 6, D)


def _combine(x1, y0p, y1p, wc, g_f, ln_g, ln_b):
    w0 = jnp.tile(wc[:, :LANES], (1, D // LANES))
    w1 = jnp.tile(wc[:, LANES:], (1, D // LANES))
    y = w0 * _unpack_rows(y0p) + w1 * _unpack_rows(y1p)
    return _layer_norm(ALPHA * x1 + (1.0 + g_f) * y, ln_g, ln_b)


def _route(h2, wr_ref, rb_ref, tri_ref, cnt_ref, ri_ref, wc_ref, cnto_ref):
    lt = lax.dot_general(wr_ref[...], h2, (((1,), (1,)), ((), ())),
                         preferred_element_type=F32) + rb_ref[...]
    iota8 = lax.broadcasted_iota(I32, (SUBLANES, TS), 0).astype(F32)
    gl = lt[0:SUBLANES]
    gmax = jnp.max(gl, axis=0, keepdims=True)
    gidx = jnp.min(jnp.where(gl == gmax, iota8, float(SUBLANES)), axis=0, keepdims=True)
    gw = 1.0 / jnp.sum(jnp.exp(gl - gmax), axis=0, keepdims=True)
    el = lt[SUBLANES:2 * SUBLANES]
    for g in range(1, N_GROUPS):
        el = jnp.where(gidx == float(g), lt[SUBLANES * (g + 1):SUBLANES * (g + 2)], el)
    m1 = jnp.max(el, axis=0, keepdims=True)
    i1 = jnp.min(jnp.where(el == m1, iota8, float(SUBLANES)), axis=0, keepdims=True)
    el2 = jnp.where(iota8 == i1, -jnp.inf, el)
    m2 = jnp.max(el2, axis=0, keepdims=True)
    i2 = jnp.min(jnp.where(el2 == m2, iota8, float(SUBLANES)), axis=0, keepdims=True)
    r = jnp.exp(m2 - m1)
    w_a = gw / (1.0 + r)
    w_b = gw * r / (1.0 + r)
    e1 = gidx * float(EPG) + i1
    e2 = gidx * float(EPG) + i2

    iota_e = lax.broadcasted_iota(I32, (NE, TS), 0).astype(F32)
    oh1 = iota_e == e1
    oh2 = iota_e == e2
    oh = jnp.concatenate([jnp.where(oh1, 1.0, 0.0), jnp.where(oh2, 1.0, 0.0)], axis=0)
    before = jnp.dot(oh.astype(BF16), tri_ref[...], preferred_element_type=F32)
    tot = jnp.sum(oh, axis=1, keepdims=True)
    cnt = cnt_ref[...]
    base = jnp.tile(cnt, (1, TS // LANES))
    tot1 = tot[:NE]
    tot2 = tot[NE:]
    rank1 = jnp.sum(jnp.where(oh1, base + before[:NE], 0.0), axis=0, keepdims=True)
    rank2 = jnp.sum(jnp.where(oh2, base + tot1 + before[NE:], 0.0), axis=0, keepdims=True)
    new_cnt = cnt + tot1 + tot2
    cnt_ref[...] = new_cnt
    cnto_ref[...] = new_cnt.astype(I32)

    ri_ref[0:1, :] = e1.astype(I32)
    ri_ref[1:2, :] = e2.astype(I32)
    ri_ref[2:3, :] = rank1.astype(I32)
    ri_ref[3:4, :] = rank2.astype(I32)
    ri_ref[4:8, :] = jnp.zeros((4, TS), I32)
    wc_ref[0, :, :LANES] = jnp.broadcast_to(w_a, (LANES, TS)).T
    wc_ref[0, :, LANES:] = jnp.broadcast_to(w_b, (LANES, TS)).T


def _conv_taps(uext_ref, u, w_dw, halo, width, r0):
    uext_ref[halo + r0:halo + r0 + SR, :] = u
    acc = None
    for k in range(width):
        off = r0 + halo - (width - 1) + k
        term = w_dw[k:k + 1, :] * uext_ref[off:off + SR, :]
        acc = term if acc is None else acc + term
    return acc


def _conv_time_major(tm_ref, o2_ref, u, wk_ref, bias, r0):
    for j in range(DT):
        tm_ref[pl.ds((HALO_A + r0) * DT + j, SR, stride=DT), :] = u[:, j * LANES:(j + 1) * LANES]
    acc = None
    for k in range(CONV_A):
        off = (r0 + HALO_A - (CONV_A - 1) + k) * DT
        term = tm_ref[off:off + SR * DT, :].reshape(SR, DT, LANES) * wk_ref[k]
        acc = term if acc is None else acc + term
    acc = acc + bias
    o2_ref[r0 * DT:(r0 + SR) * DT, :] = acc.reshape(SR * DT, LANES)
    return jnp.concatenate([o2_ref[pl.ds(r0 * DT + j, SR, stride=DT), :] for j in range(DT)], axis=1)


def _mixer_kernel(*refs, kind, has_prev):
    it = iter(refs)
    xin_ref = next(it)
    if has_prev:
        y0_ref, y1_ref, wcin_ref, pmods_ref, pln_ref = (next(it) for _ in range(5))
    mods_ref, mln_ref, win_ref = next(it), next(it), next(it)
    if kind == "a":
        bin_ref, wdw_ref, bdw_ref, vec_ref = next(it), next(it), next(it), next(it)
    else:
        wdw_ref = next(it)
    wout_ref, wr_ref, rb_ref, tri_ref = (next(it) for _ in range(4))
    x1_ref, h2_ref, ri_ref, wc_ref, cnto_ref = (next(it) for _ in range(5))
    if kind == "a":
        uext_ref, o2_ref, h2b_ref, cnt_ref = (next(it) for _ in range(4))
    else:
        uext_ref, h2b_ref, cnt_ref = (next(it) for _ in range(3))

    first = (pl.program_id(0) == 0) & (pl.program_id(1) == 0)

    @pl.when(first)
    def _():
        cnt_ref[...] = jnp.zeros((NE, LANES), F32)

    halo_rows = HALO_A * DT if kind == "a" else HALO_B

    @pl.when(pl.program_id(1) == 0)
    def _():
        uext_ref[0:halo_rows, :] = jnp.zeros((halo_rows, uext_ref.shape[1]), F32)

    m = mods_ref[0, 0]
    mln = mln_ref[...]
    for i in range(TS // SR):
        r0 = i * SR
        rows = slice(r0, r0 + SR)
        x = xin_ref[0, rows, :]
        if has_prev:
            pln = pln_ref[...]
            x = _combine(x, y0_ref[0, rows, :], y1_ref[0, rows, :], wcin_ref[0, rows, :],
                         pmods_ref[0, 0][5:6, :], pln[0:1, :], pln[1:2, :])

        h = (x * (1.0 + m[1:2, :]) + m[0:1, :]).astype(BF16)
        p = jnp.dot(h, win_ref[...], preferred_element_type=F32)
        if kind == "a":
            p = p + bin_ref[...]
            vec = vec_ref[...]
            u = p[:, :D] * _sigmoid(p[:, D:])
            u = _conv_time_major(uext_ref, o2_ref, u, wdw_ref, bdw_ref[...], r0)
            u = _layer_norm(u, vec[0:1, :], vec[1:2, :])
            u = u * _sigmoid(u)
            y = jnp.dot(u.astype(BF16), wout_ref[...], preferred_element_type=F32) + vec[2:3, :]
        else:
            gb = p[:, :D]
            q = p[:, D:2 * D] * p[:, 2 * D:]
            u = _conv_taps(uext_ref, q, wdw_ref[...], HALO_B, CONV_B, r0)
            y = jnp.dot((gb * u).astype(BF16), wout_ref[...], preferred_element_type=F32)

        x1 = _layer_norm(ALPHA * x + (1.0 + m[2:3, :]) * y, mln[0:1, :], mln[1:2, :])
        x1_ref[0, rows, :] = x1
        h2 = x1 * (1.0 + m[4:5, :]) + m[3:4, :]
        h2_ref[0, rows, :] = _pack_rows(h2)
        h2b_ref[rows, :] = h2.astype(BF16)

    uext_ref[0:halo_rows, :] = uext_ref[uext_ref.shape[0] - halo_rows:uext_ref.shape[0], :]
    _route(h2b_ref[...], wr_ref, rb_ref, tri_ref, cnt_ref, ri_ref, wc_ref, cnto_ref)


def _tile_spec(width):
    return pl.BlockSpec((1, TS, width), lambda b, s: (b, s, 0))


def _yk_spec(k):
    return pl.BlockSpec((1, TS, DP), lambda b, s: (k, b * NS + s, 0))


def _const_spec(shape):
    nd = len(shape)
    return pl.BlockSpec(shape, lambda b, s: (0,) * nd)


def _mods_spec():
    return pl.BlockSpec((1, 1, 6, D), lambda b, s: (0, b, 0, 0))


def _mixer(kind, xin, prev, mods_l, mix_ln, weights, wr, rb, tri, name):
    has_prev = prev is not None
    args = [xin]
    specs = [_tile_spec(D)]
    if has_prev:
        yk, wcin, pmods, pln = prev
        args += [yk, yk, wcin, pmods, pln]
        specs += [_yk_spec(0), _yk_spec(1), _tile_spec(2 * LANES), _mods_spec(), _const_spec((2, D))]
    args += [mods_l, mix_ln]
    specs += [_mods_spec(), _const_spec((2, D))]
    for w in weights:
        args.append(w)
        specs.append(_const_spec(w.shape))
    args += [wr, rb, tri]
    specs += [_const_spec(wr.shape), _const_spec(rb.shape), _const_spec(tri.shape)]
    if kind == "a":
        conv_scratch = [pltpu.VMEM(((TS + HALO_A) * DT, LANES), F32), pltpu.VMEM((TS * DT, LANES), F32)]
    else:
        conv_scratch = [pltpu.VMEM((TS + HALO_B, D), F32)]
    out_shape = (
        jax.ShapeDtypeStruct((BATCH, SEQ, D), F32),
        jax.ShapeDtypeStruct((BATCH, SEQ, DP), U32),
        jax.ShapeDtypeStruct((SUBLANES, T), I32),
        jax.ShapeDtypeStruct((BATCH, SEQ, 2 * LANES), F32),
        jax.ShapeDtypeStruct((NE, LANES), I32),
    )
    out_specs = (
        _tile_spec(D), _tile_spec(DP),
        pl.BlockSpec((SUBLANES, TS), lambda b, s: (0, b * NS + s)),
        _tile_spec(2 * LANES),
        pl.BlockSpec((NE, LANES), lambda b, s: (0, 0)),
    )
    return pl.pallas_call(
        functools.partial(_mixer_kernel, kind=kind, has_prev=has_prev),
        grid=(BATCH, NS),
        in_specs=specs,
        out_specs=out_specs,
        out_shape=out_shape,
        scratch_shapes=conv_scratch + [pltpu.VMEM((TS, D), BF16), pltpu.VMEM((NE, LANES), F32)],
        compiler_params=pltpu.CompilerParams(
            dimension_semantics=("arbitrary", "arbitrary"), vmem_limit_bytes=VMEM_LIMIT),
        name=name,
    )(*args)


def _dest_kernel(ps_ref, ri_ref, o_ref):
    ri = ri_ref[...]
    e = ri[0:2, :]
    start = jnp.zeros_like(e)
    for k in range(NE):
        start = jnp.where(e == k, ps_ref[k], start)
    o_ref[0:2, :] = start + ri[2:4, :]
    o_ref[2:8, :] = jnp.zeros((6, ri.shape[1]), I32)


def _dest_slots(pad_starts, ri):
    tn = 4096
    return pl.pallas_call(
        _dest_kernel,
        grid_spec=pltpu.PrefetchScalarGridSpec(
            num_scalar_prefetch=1,
            grid=(T // tn,),
            in_specs=[pl.BlockSpec((SUBLANES, tn), lambda i, ps: (0, i))],
            out_specs=pl.BlockSpec((SUBLANES, tn), lambda i, ps: (0, i)),
        ),
        out_shape=jax.ShapeDtypeStruct((SUBLANES, T), I32),
        compiler_params=pltpu.CompilerParams(dimension_semantics=("arbitrary",)),
        name="dest_slots",
    )(pad_starts, ri)


def _expert_kernel(be_ref, nu_ref, x_ref, wg_ref, wu_ref, wd_ref, o_ref, wgu_s, wd_s):
    j = pl.program_id(0)

    @pl.when(j < nu_ref[0])
    def _():
        prev_e = be_ref[jnp.maximum(j - 1, 0)]

        @pl.when((j == 0) | (be_ref[j] != prev_e))
        def _():
            wgu_s[:, :F] = wg_ref[0, 0].astype(BF16)
            wgu_s[:, F:] = wu_ref[0, 0].astype(BF16)
            wd_s[...] = wd_ref[0, 0].astype(BF16)

        x = _unpack_rows(x_ref[...]).astype(BF16)
        gu = jnp.dot(x, wgu_s[...], preferred_element_type=F32)
        g = gu[:, :F]
        hid = (g * _sigmoid(g) * gu[:, F:]).astype(BF16)
        o_ref[...] = _pack_rows(jnp.dot(hid, wd_s[...], preferred_element_type=F32))


def _experts(layer, block_e, n_used, xs, w_gate, w_up, w_down, name):
    def row_map(j, be, nu):
        return (jnp.minimum(j, nu[0] - 1), 0)

    def w_map(j, be, nu):
        return (layer, be[j], 0, 0)

    return pl.pallas_call(
        _expert_kernel,
        grid_spec=pltpu.PrefetchScalarGridSpec(
            num_scalar_prefetch=2,
            grid=(NB,),
            in_specs=[
                pl.BlockSpec((BM, DP), row_map),
                pl.BlockSpec((1, 1, D, F), w_map),
                pl.BlockSpec((1, 1, D, F), w_map),
                pl.BlockSpec((1, 1, F, D), w_map),
            ],
            out_specs=pl.BlockSpec((BM, DP), row_map),
            scratch_shapes=[pltpu.VMEM((D, 2 * F), BF16), pltpu.VMEM((F, D), BF16)],
        ),
        out_shape=jax.ShapeDtypeStruct((NSLOT, DP), U32),
        compiler_params=pltpu.CompilerParams(
            dimension_semantics=("arbitrary",), vmem_limit_bytes=VMEM_LIMIT),
        name=name,
    )(block_e, n_used, xs, w_gate, w_up, w_down)


def _sc_worker_id():
    return lax.axis_index("s") * SC_CORES + lax.axis_index("c")


def _sc_mesh():
    return plsc.VectorSubcoreMesh(core_axis_name="c", subcore_axis_name="s")


def _sc_scratch(n_index_rows):
    return [
        pltpu.VMEM((n_index_rows, SC_CHUNK), I32),
        pltpu.VMEM((2, SC_CHUNK, DP), U32),
        pltpu.SemaphoreType.DMA((2,)),
        pltpu.SemaphoreType.DMA((2,)),
    ]


def _dispatch_rows(h2p, dest):
    per_w = T // SC_WORKERS
    n_chunks = per_w // SC_CHUNK

    @functools.partial(
        pl.kernel, mesh=_sc_mesh(),
        out_type=jax.ShapeDtypeStruct((NSLOT, DP), U32),
        scratch_types=_sc_scratch(TOP_K * n_chunks),
        name="dispatch_rows",
    )
    def k(h2_hbm, dest_hbm, out_hbm, dest_v, rows_v, rsem, wsem):
        wid = _sc_worker_id()
        for kk in range(TOP_K):
            pltpu.sync_copy(dest_hbm.at[kk, pl.ds(wid * n_chunks, n_chunks)],
                            dest_v.at[pl.ds(kk * n_chunks, n_chunks)])
        base = wid * per_w

        def read(c, slot):
            return pltpu.make_async_copy(h2_hbm.at[pl.ds(base + c * SC_CHUNK, SC_CHUNK)],
                                         rows_v.at[slot], rsem.at[slot])

        def write(c, kk, slot):
            return pltpu.make_async_copy(rows_v.at[slot], out_hbm.at[dest_v.at[kk * n_chunks + c]],
                                         wsem.at[slot])

        read(0, 0).start()

        @pl.loop(0, n_chunks, step=2)
        def _(c):
            for b in range(2):
                cc = c + b
                read(cc, b).wait()

                @pl.when(cc + 1 < n_chunks)
                def _():
                    @pl.when(cc >= 1)
                    def _():
                        for kk in range(TOP_K):
                            write(cc - 1, kk, 1 - b).wait()
                    read(cc + 1, 1 - b).start()

                for kk in range(TOP_K):
                    write(cc, kk, b).start()

        for slot, cc in ((0, n_chunks - 2), (1, n_chunks - 1)):
            for kk in range(TOP_K):
                write(cc, kk, slot).wait()

    return k(h2p, dest.reshape(TOP_K, T // SC_CHUNK, SC_CHUNK))


def _return_rows(yb, dest):
    m = TOP_K * T
    per_w = m // SC_WORKERS
    n_chunks = per_w // SC_CHUNK

    @functools.partial(
        pl.kernel, mesh=_sc_mesh(),
        out_type=jax.ShapeDtypeStruct((m, DP), U32),
        scratch_types=_sc_scratch(n_chunks),
        name="return_rows",
    )
    def k(yb_hbm, dest_hbm, out_hbm, idx_v, rows_v, gsem, wsem):
        wid = _sc_worker_id()
        pltpu.sync_copy(dest_hbm.at[pl.ds(wid * n_chunks, n_chunks)], idx_v)
        base = wid * per_w

        def gather(c, slot):
            return pltpu.make_async_copy(yb_hbm.at[idx_v.at[c]], rows_v.at[slot], gsem.at[slot])

        def write(c, slot):
            return pltpu.make_async_copy(rows_v.at[slot], out_hbm.at[pl.ds(base + c * SC_CHUNK, SC_CHUNK)],
                                         wsem.at[slot])

        gather(0, 0).start()

        @pl.loop(0, n_chunks, step=2)
        def _(c):
            for b in range(2):
                cc = c + b
                gather(cc, b).wait()

                @pl.when(cc + 1 < n_chunks)
                def _():
                    @pl.when(cc >= 1)
                    def _():
                        write(cc - 1, 1 - b).wait()
                    gather(cc + 1, 1 - b).start()

                write(cc, b).start()

        write(n_chunks - 2, 0).wait()
        write(n_chunks - 1, 1).wait()

    return k(yb, dest.reshape(m // SC_CHUNK, SC_CHUNK))


def _final_kernel(x1_ref, y0_ref, y1_ref, wc_ref, mods_ref, ln_ref, o_ref):
    ln = ln_ref[...]
    o_ref[0] = _combine(x1_ref[0], y0_ref[0], y1_ref[0], wc_ref[0], mods_ref[0, 0][5:6, :],
                        ln[0:1, :], ln[1:2, :])


def _final(x1, yk, wc, mods_l, ln):
    return pl.pallas_call(
        _final_kernel,
        grid=(BATCH, NS),
        in_specs=[_tile_spec(D), _yk_spec(0), _yk_spec(1), _tile_spec(2 * LANES),
                  _mods_spec(), _const_spec((2, D))],
        out_specs=_tile_spec(D),
        out_shape=jax.ShapeDtypeStruct((BATCH, SEQ, D), F32),
        compiler_params=pltpu.CompilerParams(
            dimension_semantics=("arbitrary", "arbitrary"), vmem_limit_bytes=VMEM_LIMIT),
        name="final_combine",
    )(x1, yk, yk, wc, mods_l, ln)


def _plan(counts):
    padded = (counts + BM - 1) // BM * BM
    pad_ends = jnp.cumsum(padded)
    pad_starts = (pad_ends - padded).astype(I32)
    n_used = (pad_ends[-1] // BM).astype(I32)
    starts = jnp.arange(NB, dtype=I32) * BM
    starts = jnp.minimum(starts, pad_ends[-1] - BM)
    block_e = jnp.minimum(jnp.sum(pad_ends[None, :] <= starts[:, None], axis=-1), NE - 1).astype(I32)
    return pad_starts, block_e, n_used.reshape(1)


def _router_params(w_group, b_group, w_expert, b_expert):
    wr = jnp.zeros((NR, D), F32)
    wr = wr.at[0:N_GROUPS].set(w_group.T).at[SUBLANES:SUBLANES + NE].set(w_expert.T)
    rb = jnp.full((NR,), NEG, F32)
    rb = rb.at[0:N_GROUPS].set(b_group).at[SUBLANES:SUBLANES + NE].set(b_expert)
    rb = rb.at[SUBLANES + NE:].set(0.0)
    return wr.astype(BF16), rb.reshape(NR, 1)


def kernel(x, c, ada_w, ada_b, a_w_in, a_b_in, a_w_dw, a_b_dw, a_ln_g, a_ln_b, a_w_out, a_b_out,
           b_w_in, b_w_dw, b_w_out, mix_ln_g, mix_ln_b, ffn_ln_g, ffn_ln_b,
           r_w_group, r_b_group, r_w_expert, r_b_expert, e_w_gate, e_w_up, e_w_down):
    mods = _ada_mods(c, ada_w, ada_b)
    tri = (jnp.arange(TS)[:, None] < jnp.arange(TS)[None, :]).astype(BF16)
    prev = None
    xin = x
    for i in range(DEPTH):
        j = i // 2
        mods_l = mods[i:i + 1]
        mix_ln = jnp.stack([mix_ln_g[i], mix_ln_b[i]])
        wr, rb = _router_params(r_w_group[i], r_b_group[i], r_w_expert[i], r_b_expert[i])
        if i % 2 == 0:
            weights = [a_w_in[j].astype(BF16), a_b_in[j].reshape(1, 2 * D),
                       a_w_dw[j].reshape(CONV_A, DT, LANES), a_b_dw[j].reshape(DT, LANES),
                       jnp.stack([a_ln_g[j], a_ln_b[j], a_b_out[j]]),
                       a_w_out[j].astype(BF16)]
            kind = "a"
        else:
            weights = [b_w_in[j].astype(BF16), b_w_dw[j], b_w_out[j].astype(BF16)]
            kind = "b"
        x1, h2, ri, wc, counts = _mixer(kind, xin, prev, mods_l, mix_ln, weights, wr, rb, tri,
                                        name=f"mixer_{kind}{i}")
        pad_starts, block_e, n_used = _plan(counts[:, 0])
        dest = _dest_slots(pad_starts, ri)[0:TOP_K]
        xs = _dispatch_rows(h2.reshape(T, DP), dest)
        yb = _experts(i, block_e, n_used, xs, e_w_gate, e_w_up, e_w_down, name=f"experts{i}")
        yk = _return_rows(yb, dest).reshape(TOP_K, T, DP)
        prev = (yk, wc, mods_l, jnp.stack([ffn_ln_g[i], ffn_ln_b[i]]))
        xin = x1
    yk, wc, mods_l, ln = prev
    return _final(xin, yk, wc, mods_l, ln)
```

```python
import functools

import jax
import jax.numpy as jnp
from jax import lax
from jax.experimental import pallas as pl
from jax.experimental.pallas import tpu as pltpu
from jax.experimental.pallas import tpu_sc as plsc

F32 = jnp.float32
BF16 = jnp.bfloat16
I32 = jnp.int32
U32 = jnp.uint32

D = 1024
BATCH = 4
SEQ = 8192
T = BATCH * SEQ
DEPTH = 4
N_GROUPS = 4
EPG = 8
NE = N_GROUPS * EPG
TOP_K = 2
F = D // 2
CONV_A = 31
CONV_B = 3
ALPHA = (2.0 * DEPTH) ** 0.25
LN_EPS = 1e-5

LANES = 128
SUBLANES = 8
VMEM_LIMIT = 56 * 1024 * 1024

TS = 512
SR = 256
NS = SEQ // TS
HALO_A = 32
HALO_B = 8
BM = 512
NSLOT = T * TOP_K + NE * BM
NR = 48
ADA_TN = 1536
NEG = -1e30
DP = D // 2
DT = D // LANES
assert DT == SUBLANES

SC_CORES = 2
SC_SUBCORES = 16
SC_WORKERS = SC_CORES * SC_SUBCORES
SC_CHUNK = 64


def _sigmoid(x):
    return 1.0 / (1.0 + jnp.exp(-x))


def _pack_rows(x):
    return pltpu.pack_elementwise([x[:, :DP], x[:, DP:]], packed_dtype=BF16)


def _unpack_rows(p):
    lo = pltpu.unpack_elementwise(p, index=0, packed_dtype=BF16, unpacked_dtype=F32)
    hi = pltpu.unpack_elementwise(p, index=1, packed_dtype=BF16, unpacked_dtype=F32)
    return jnp.concatenate([lo, hi], axis=1)


def _layer_norm(x, g, b):
    mu = jnp.mean(x, axis=-1, keepdims=True)
    xc = x - mu
    var = jnp.mean(xc * xc, axis=-1, keepdims=True)
    return xc * lax.rsqrt(var + LN_EPS) * g + b


def _ada_kernel(c_ref, w_ref, b_ref, o_ref):
    c = c_ref[...]
    ca = (c * _sigmoid(c)).astype(BF16)
    w = w_ref[0].astype(BF16)
    o_ref[0] = jnp.dot(ca, w, preferred_element_type=F32) + b_ref[0]


def _ada_mods(c, ada_w, ada_b):
    out = pl.pallas_call(
        _ada_kernel,
        grid=(DEPTH, 6 * D // ADA_TN),
        in_specs=[
            pl.BlockSpec((BATCH, D), lambda i, j: (0, 0)),
            pl.BlockSpec((1, D, ADA_TN), lambda i, j: (i, 0, j)),
            pl.BlockSpec((1, 1, ADA_TN), lambda i, j: (i, 0, j)),
        ],
        out_specs=pl.BlockSpec((1, BATCH, ADA_TN), lambda i, j: (i, 0, j)),
        out_shape=jax.ShapeDtypeStruct((DEPTH, BATCH, 6 * D), F32),
        compiler_params=pltpu.CompilerParams(
            dimension_semantics=("arbitrary", "arbitrary"), vmem_limit_bytes=VMEM_LIMIT),
        name="ada_mods",
    )(c, ada_w, ada_b.reshape(DEPTH, 1, 6 * D))
    return out.reshape(DEPTH, BATCH, 6, D)


def _combine(x1, y0p, y1p, wc, g_f, ln_g, ln_b):
    w0 = jnp.tile(wc[:, :LANES], (1, D // LANES))
    w1 = jnp.tile(wc[:, LANES:], (1, D // LANES))
    y = w0 * _unpack_rows(y0p) + w1 * _unpack_rows(y1p)
    return _layer_norm(ALPHA * x1 + (1.0 + g_f) * y, ln_g, ln_b)


def _route(h2, wr_ref, rb_ref, tri_ref, cnt_ref, ri_ref, wc_ref, cnto_ref):
    lt = lax.dot_general(wr_ref[...], h2, (((1,), (1,)), ((), ())),
                         preferred_element_type=F32) + rb_ref[...]
    iota8 = lax.broadcasted_iota(I32, (SUBLANES, TS), 0).astype(F32)
    gl = lt[0:SUBLANES]
    gmax = jnp.max(gl, axis=0, keepdims=True)
    gidx = jnp.min(jnp.where(gl == gmax, iota8, float(SUBLANES)), axis=0, keepdims=True)
    gw = 1.0 / jnp.sum(jnp.exp(gl - gmax), axis=0, keepdims=True)
    el = lt[SUBLANES:2 * SUBLANES]
    for g in range(1, N_GROUPS):
        el = jnp.where(gidx == float(g), lt[SUBLANES * (g + 1):SUBLANES * (g + 2)], el)
    m1 = jnp.max(el, axis=0, keepdims=True)
    i1 = jnp.min(jnp.where(el == m1, iota8, float(SUBLANES)), axis=0, keepdims=True)
    el2 = jnp.where(iota8 == i1, -jnp.inf, el)
    m2 = jnp.max(el2, axis=0, keepdims=True)
    i2 = jnp.min(jnp.where(el2 == m2, iota8, float(SUBLANES)), axis=0, keepdims=True)
    r = jnp.exp(m2 - m1)
    w_a = gw / (1.0 + r)
    w_b = gw * r / (1.0 + r)
    e1 = gidx * float(EPG) + i1
    e2 = gidx * float(EPG) + i2

    iota_e = lax.broadcasted_iota(I32, (NE, TS), 0).astype(F32)
    oh1 = iota_e == e1
    oh2 = iota_e == e2
    oh = jnp.concatenate([jnp.where(oh1, 1.0, 0.0), jnp.where(oh2, 1.0, 0.0)], axis=0)
    before = jnp.dot(oh.astype(BF16), tri_ref[...], preferred_element_type=F32)
    tot = jnp.sum(oh, axis=1, keepdims=True)
    cnt = cnt_ref[...]
    base = jnp.tile(cnt, (1, TS // LANES))
    tot1 = tot[:NE]
    tot2 = tot[NE:]
    rank1 = jnp.sum(jnp.where(oh1, base + before[:NE], 0.0), axis=0, keepdims=True)
    rank2 = jnp.sum(jnp.where(oh2, base + tot1 + before[NE:], 0.0), axis=0, keepdims=True)
    new_cnt = cnt + tot1 + tot2
    cnt_ref[...] = new_cnt
    cnto_ref[...] = new_cnt.astype(I32)

    ri_ref[0:1, :] = e1.astype(I32)
    ri_ref[1:2, :] = e2.astype(I32)
    ri_ref[2:3, :] = rank1.astype(I32)
    ri_ref[3:4, :] = rank2.astype(I32)
    ri_ref[4:8, :] = jnp.zeros((4, TS), I32)
    wc_ref[0, :, :LANES] = jnp.broadcast_to(w_a, (LANES, TS)).T
    wc_ref[0, :, LANES:] = jnp.broadcast_to(w_b, (LANES, TS)).T


def _conv_taps(uext_ref, u, w_dw, halo, width, r0):
    uext_ref[halo + r0:halo + r0 + SR, :] = u
    acc = None
    for k in range(width):
        off = r0 + halo - (width - 1) + k
        term = w_dw[k:k + 1, :] * uext_ref[off:off + SR, :]
        acc = term if acc is None else acc + term
    return acc


def _conv_time_major(tm_ref, o2_ref, u, wk_ref, bias, r0, width, halo):
    for j in range(DT):
        tm_ref[pl.ds((halo + r0) * DT + j, SR, stride=DT), :] = u[:, j * LANES:(j + 1) * LANES]
    acc = None
    for k in range(width):
        off = (r0 + halo - (width - 1) + k) * DT
        term = tm_ref[off:off + SR * DT, :].reshape(SR, DT, LANES) * wk_ref[k]
        acc = term if acc is None else acc + term
    if bias is not None:
        acc = acc + bias
    o2_ref[r0 * DT:(r0 + SR) * DT, :] = acc.reshape(SR * DT, LANES)
    return jnp.concatenate([o2_ref[pl.ds(r0 * DT + j, SR, stride=DT), :] for j in range(DT)], axis=1)


def _mixer_kernel(*refs, kind, has_prev):
    it = iter(refs)
    xin_ref = next(it)
    if has_prev:
        y0_ref, y1_ref, wcin_ref, pmods_ref, pln_ref = (next(it) for _ in range(5))
    mods_ref, mln_ref, win_ref = next(it), next(it), next(it)
    if kind == "a":
        bin_ref, wdw_ref, bdw_ref, vec_ref = next(it), next(it), next(it), next(it)
    else:
        wdw_ref = next(it)
    wout_ref, wr_ref, rb_ref, tri_ref = (next(it) for _ in range(4))
    x1_ref, h2_ref, ri_ref, wc_ref, cnto_ref = (next(it) for _ in range(5))
    if kind == "a":
        uext_ref, o2_ref, h2b_ref, cnt_ref = (next(it) for _ in range(4))
    else:
        uext_ref, h2b_ref, cnt_ref = (next(it) for _ in range(3))

    first = (pl.program_id(0) == 0) & (pl.program_id(1) == 0)

    @pl.when(first)
    def _():
        cnt_ref[...] = jnp.zeros((NE, LANES), F32)

    halo_rows = HALO_A * DT if kind == "a" else HALO_B

    @pl.when(pl.program_id(1) == 0)
    def _():
        uext_ref[0:halo_rows, :] = jnp.zeros((halo_rows, uext_ref.shape[1]), F32)

    m = mods_ref[0, 0]
    mln = mln_ref[...]
    for i in range(TS // SR):
        r0 = i * SR
        rows = slice(r0, r0 + SR)
        x = xin_ref[0, rows, :]
        if has_prev:
            pln = pln_ref[...]
            x = _combine(x, y0_ref[0, rows, :], y1_ref[0, rows, :], wcin_ref[0, rows, :],
                         pmods_ref[0, 0][5:6, :], pln[0:1, :], pln[1:2, :])

        h = (x * (1.0 + m[1:2, :]) + m[0:1, :]).astype(BF16)
        cols = [jnp.dot(h, win_ref[:, c * D:(c + 1) * D], preferred_element_type=F32)
                for c in range(win_ref.shape[1] // D)]
        if kind == "a":
            b_in = bin_ref[...]
            vec = vec_ref[...]
            u = (cols[0] + b_in[:, :D]) * _sigmoid(cols[1] + b_in[:, D:])
            u = _conv_time_major(uext_ref, o2_ref, u, wdw_ref, bdw_ref[...], r0, CONV_A, HALO_A)
            u = _layer_norm(u, vec[0:1, :], vec[1:2, :])
            u = u * _sigmoid(u)
            y = jnp.dot(u.astype(BF16), wout_ref[...], preferred_element_type=F32) + vec[2:3, :]
        else:
            gb = cols[0]
            q = cols[1] * cols[2]
            u = _conv_taps(uext_ref, q, wdw_ref[...], HALO_B, CONV_B, r0)
            y = jnp.dot((gb * u).astype(BF16), wout_ref[...], preferred_element_type=F32)

        x1 = _layer_norm(ALPHA * x + (1.0 + m[2:3, :]) * y, mln[0:1, :], mln[1:2, :])
        x1_ref[0, rows, :] = x1
        h2 = x1 * (1.0 + m[4:5, :]) + m[3:4, :]
        h2_ref[0, rows, :] = _pack_rows(h2)
        h2b_ref[rows, :] = h2.astype(BF16)

    uext_ref[0:halo_rows, :] = uext_ref[uext_ref.shape[0] - halo_rows:uext_ref.shape[0], :]
    _route(h2b_ref[...], wr_ref, rb_ref, tri_ref, cnt_ref, ri_ref, wc_ref, cnto_ref)


def _tile_spec(width):
    return pl.BlockSpec((1, TS, width), lambda b, s: (b, s, 0))


def _yk_spec(k):
    return pl.BlockSpec((1, TS, DP), lambda b, s: (k, b * NS + s, 0))


def _const_spec(shape):
    nd = len(shape)
    return pl.BlockSpec(shape, lambda b, s: (0,) * nd)


def _mods_spec():
    return pl.BlockSpec((1, 1, 6, D), lambda b, s: (0, b, 0, 0))


def _mixer(kind, xin, prev, mods_l, mix_ln, weights, wr, rb, tri, name):
    has_prev = prev is not None
    args = [xin]
    specs = [_tile_spec(D)]
    if has_prev:
        yk, wcin, pmods, pln = prev
        args += [yk, yk, wcin, pmods, pln]
        specs += [_yk_spec(0), _yk_spec(1), _tile_spec(2 * LANES), _mods_spec(), _const_spec((2, D))]
    args += [mods_l, mix_ln]
    specs += [_mods_spec(), _const_spec((2, D))]
    for w in weights:
        args.append(w)
        specs.append(_const_spec(w.shape))
    args += [wr, rb, tri]
    specs += [_const_spec(wr.shape), _const_spec(rb.shape), _const_spec(tri.shape)]
    if kind == "a":
        conv_scratch = [pltpu.VMEM(((TS + HALO_A) * DT, LANES), F32), pltpu.VMEM((TS * DT, LANES), F32)]
    else:
        conv_scratch = [pltpu.VMEM((TS + HALO_B, D), F32)]
    out_shape = (
        jax.ShapeDtypeStruct((BATCH, SEQ, D), F32),
        jax.ShapeDtypeStruct((BATCH, SEQ, DP), U32),
        jax.ShapeDtypeStruct((SUBLANES, T), I32),
        jax.ShapeDtypeStruct((BATCH, SEQ, 2 * LANES), F32),
        jax.ShapeDtypeStruct((NE, LANES), I32),
    )
    out_specs = (
        _tile_spec(D), _tile_spec(DP),
        pl.BlockSpec((SUBLANES, TS), lambda b, s: (0, b * NS + s)),
        _tile_spec(2 * LANES),
        pl.BlockSpec((NE, LANES), lambda b, s: (0, 0)),
    )
    return pl.pallas_call(
        functools.partial(_mixer_kernel, kind=kind, has_prev=has_prev),
        grid=(BATCH, NS),
        in_specs=specs,
        out_specs=out_specs,
        out_shape=out_shape,
        scratch_shapes=conv_scratch + [pltpu.VMEM((TS, D), BF16), pltpu.VMEM((NE, LANES), F32)],
        compiler_params=pltpu.CompilerParams(
            dimension_semantics=("arbitrary", "arbitrary"), vmem_limit_bytes=VMEM_LIMIT),
        name=name,
    )(*args)


def _dest_kernel(ps_ref, ri_ref, o_ref):
    ri = ri_ref[...]
    e = ri[0:2, :]
    start = jnp.zeros_like(e)
    for k in range(NE):
        start = jnp.where(e == k, ps_ref[k], start)
    o_ref[0:2, :] = start + ri[2:4, :]
    o_ref[2:8, :] = jnp.zeros((6, ri.shape[1]), I32)


def _dest_slots(pad_starts, ri):
    tn = 4096
    return pl.pallas_call(
        _dest_kernel,
        grid_spec=pltpu.PrefetchScalarGridSpec(
            num_scalar_prefetch=1,
            grid=(T // tn,),
            in_specs=[pl.BlockSpec((SUBLANES, tn), lambda i, ps: (0, i))],
            out_specs=pl.BlockSpec((SUBLANES, tn), lambda i, ps: (0, i)),
        ),
        out_shape=jax.ShapeDtypeStruct((SUBLANES, T), I32),
        compiler_params=pltpu.CompilerParams(dimension_semantics=("arbitrary",)),
        name="dest_slots",
    )(pad_starts, ri)


def _expert_kernel(st_ref, nb_ref, xs_hbm, wg_ref, wu_ref, wd_ref, yb_hbm, wgu_s, wd_s, xbuf, obuf, xsem, osem):
    e = pl.program_id(0)
    nb = nb_ref[e]
    start = st_ref[e]

    def x_copy(i, slot):
        return pltpu.make_async_copy(xs_hbm.at[pl.ds(pl.multiple_of(start + i * BM, BM), BM)],
                                     xbuf.at[slot], xsem.at[slot])

    def o_copy(i, slot):
        return pltpu.make_async_copy(obuf.at[slot],
                                     yb_hbm.at[pl.ds(pl.multiple_of(start + i * BM, BM), BM)], osem.at[slot])

    @pl.when(nb > 0)
    def _():
        x_copy(0, 0).start()
        wgu_s[:, :F] = wg_ref[0, 0].astype(BF16)
        wgu_s[:, F:] = wu_ref[0, 0].astype(BF16)
        wd_s[...] = wd_ref[0, 0].astype(BF16)

    @pl.loop(0, nb, step=2)
    def _(i0):
        for slot in range(2):
            i = i0 + slot

            @pl.when(i < nb)
            def _():
                x_copy(i, slot).wait()

                @pl.when(i + 1 < nb)
                def _():
                    x_copy(i + 1, 1 - slot).start()

                @pl.when(i >= 2)
                def _():
                    o_copy(i - 2, slot).wait()

                x = _unpack_rows(xbuf[slot]).astype(BF16)
                gu = jnp.dot(x, wgu_s[...], preferred_element_type=F32)
                g = gu[:, :F]
                hid = (g * _sigmoid(g) * gu[:, F:]).astype(BF16)
                obuf[slot] = _pack_rows(jnp.dot(hid, wd_s[...], preferred_element_type=F32))
                o_copy(i, slot).start()

    for back in (1, 2):
        @pl.when(nb >= back)
        def _():
            last = nb - back
            o_copy(last, last % 2).wait()


def _experts(layer, starts, n_blocks, xs, w_gate, w_up, w_down, name):
    def w_map(e, st, nb):
        return (layer, e, 0, 0)

    return pl.pallas_call(
        _expert_kernel,
        grid_spec=pltpu.PrefetchScalarGridSpec(
            num_scalar_prefetch=2,
            grid=(NE,),
            in_specs=[
                pl.BlockSpec(memory_space=pl.ANY),
                pl.BlockSpec((1, 1, D, F), w_map),
                pl.BlockSpec((1, 1, D, F), w_map),
                pl.BlockSpec((1, 1, F, D), w_map),
            ],
            out_specs=pl.BlockSpec(memory_space=pl.ANY),
            scratch_shapes=[
                pltpu.VMEM((D, 2 * F), BF16), pltpu.VMEM((F, D), BF16),
                pltpu.VMEM((2, BM, DP), U32), pltpu.VMEM((2, BM, DP), U32),
                pltpu.SemaphoreType.DMA((2,)), pltpu.SemaphoreType.DMA((2,)),
            ],
        ),
        out_shape=jax.ShapeDtypeStruct((NSLOT, DP), U32),
        compiler_params=pltpu.CompilerParams(
            dimension_semantics=("arbitrary",), vmem_limit_bytes=VMEM_LIMIT),
        name=name,
    )(starts, n_blocks, xs, w_gate, w_up, w_down)


def _sc_worker_id():
    return lax.axis_index("s") * SC_CORES + lax.axis_index("c")


def _sc_mesh():
    return plsc.VectorSubcoreMesh(core_axis_name="c", subcore_axis_name="s")


def _sc_scratch(n_index_rows):
    return [
        pltpu.VMEM((n_index_rows, SC_CHUNK), I32),
        pltpu.VMEM((2, SC_CHUNK, DP), U32),
        pltpu.SemaphoreType.DMA((2,)),
        pltpu.SemaphoreType.DMA((2,)),
    ]


def _dispatch_rows(h2p, dest):
    per_w = T // SC_WORKERS
    n_chunks = per_w // SC_CHUNK

    @functools.partial(
        pl.kernel, mesh=_sc_mesh(),
        out_type=jax.ShapeDtypeStruct((NSLOT, DP), U32),
        scratch_types=_sc_scratch(TOP_K * n_chunks),
        name="dispatch_rows",
    )
    def k(h2_hbm, dest_hbm, out_hbm, dest_v, rows_v, rsem, wsem):
        wid = _sc_worker_id()
        for kk in range(TOP_K):
            pltpu.sync_copy(dest_hbm.at[kk, pl.ds(wid * n_chunks, n_chunks)],
                            dest_v.at[pl.ds(kk * n_chunks, n_chunks)])
        base = wid * per_w

        def read(c, slot):
            return pltpu.make_async_copy(h2_hbm.at[pl.ds(base + c * SC_CHUNK, SC_CHUNK)],
                                         rows_v.at[slot], rsem.at[slot])

        def write(c, kk, slot):
            return pltpu.make_async_copy(rows_v.at[slot], out_hbm.at[dest_v.at[kk * n_chunks + c]],
                                         wsem.at[slot])

        read(0, 0).start()

        @pl.loop(0, n_chunks, step=2)
        def _(c):
            for b in range(2):
                cc = c + b
                read(cc, b).wait()

                @pl.when(cc + 1 < n_chunks)
                def _():
                    @pl.when(cc >= 1)
                    def _():
                        for kk in range(TOP_K):
                            write(cc - 1, kk, 1 - b).wait()
                    read(cc + 1, 1 - b).start()

                for kk in range(TOP_K):
                    write(cc, kk, b).start()

        for slot, cc in ((0, n_chunks - 2), (1, n_chunks - 1)):
            for kk in range(TOP_K):
                write(cc, kk, slot).wait()

    return k(h2p, dest.reshape(TOP_K, T // SC_CHUNK, SC_CHUNK))


def _return_rows(yb, dest):
    m = TOP_K * T
    per_w = m // SC_WORKERS
    n_chunks = per_w // SC_CHUNK

    @functools.partial(
        pl.kernel, mesh=_sc_mesh(),
        out_type=jax.ShapeDtypeStruct((m, DP), U32),
        scratch_types=_sc_scratch(n_chunks),
        name="return_rows",
    )
    def k(yb_hbm, dest_hbm, out_hbm, idx_v, rows_v, gsem, wsem):
        wid = _sc_worker_id()
        pltpu.sync_copy(dest_hbm.at[pl.ds(wid * n_chunks, n_chunks)], idx_v)
        base = wid * per_w

        def gather(c, slot):
            return pltpu.make_async_copy(yb_hbm.at[idx_v.at[c]], rows_v.at[slot], gsem.at[slot])

        def write(c, slot):
            return pltpu.make_async_copy(rows_v.at[slot], out_hbm.at[pl.ds(base + c * SC_CHUNK, SC_CHUNK)],
                                         wsem.at[slot])

        gather(0, 0).start()

        @pl.loop(0, n_chunks, step=2)
        def _(c):
            for b in range(2):
                cc = c + b
                gather(cc, b).wait()

                @pl.when(cc + 1 < n_chunks)
                def _():
                    @pl.when(cc >= 1)
                    def _():
                        write(cc - 1, 1 - b).wait()
                    gather(cc + 1, 1 - b).start()

                write(cc, b).start()

        write(n_chunks - 2, 0).wait()
        write(n_chunks - 1, 1).wait()

    return k(yb, dest.reshape(m // SC_CHUNK, SC_CHUNK))


def _final_kernel(x1_ref, y0_ref, y1_ref, wc_ref, mods_ref, ln_ref, o_ref):
    ln = ln_ref[...]
    o_ref[0] = _combine(x1_ref[0], y0_ref[0], y1_ref[0], wc_ref[0], mods_ref[0, 0][5:6, :],
                        ln[0:1, :], ln[1:2, :])


def _final(x1, yk, wc, mods_l, ln):
    return pl.pallas_call(
        _final_kernel,
        grid=(BATCH, NS),
        in_specs=[_tile_spec(D), _yk_spec(0), _yk_spec(1), _tile_spec(2 * LANES),
                  _mods_spec(), _const_spec((2, D))],
        out_specs=_tile_spec(D),
        out_shape=jax.ShapeDtypeStruct((BATCH, SEQ, D), F32),
        compiler_params=pltpu.CompilerParams(
            dimension_semantics=("arbitrary", "arbitrary"), vmem_limit_bytes=VMEM_LIMIT),
        name="final_combine",
    )(x1, yk, yk, wc, mods_l, ln)


def _plan(counts):
    n_blocks = (counts + BM - 1) // BM
    padded = n_blocks * BM
    pad_starts = (jnp.cumsum(padded) - padded).astype(I32)
    return pad_starts, n_blocks.astype(I32)


def _router_params(w_group, b_group, w_expert, b_expert):
    wr = jnp.zeros((NR, D), F32)
    wr = wr.at[0:N_GROUPS].set(w_group.T).at[SUBLANES:SUBLANES + NE].set(w_expert.T)
    rb = jnp.full((NR,), NEG, F32)
    rb = rb.at[0:N_GROUPS].set(b_group).at[SUBLANES:SUBLANES + NE].set(b_expert)
    rb = rb.at[SUBLANES + NE:].set(0.0)
    return wr.astype(BF16), rb.reshape(NR, 1)


def kernel(x, c, ada_w, ada_b, a_w_in, a_b_in, a_w_dw, a_b_dw, a_ln_g, a_ln_b, a_w_out, a_b_out,
           b_w_in, b_w_dw, b_w_out, mix_ln_g, mix_ln_b, ffn_ln_g, ffn_ln_b,
           r_w_group, r_b_group, r_w_expert, r_b_expert, e_w_gate, e_w_up, e_w_down):
    mods = _ada_mods(c, ada_w, ada_b)
    tri = (jnp.arange(TS)[:, None] < jnp.arange(TS)[None, :]).astype(BF16)
    prev = None
    xin = x
    for i in range(DEPTH):
        j = i // 2
        mods_l = mods[i:i + 1]
        mix_ln = jnp.stack([mix_ln_g[i], mix_ln_b[i]])
        wr, rb = _router_params(r_w_group[i], r_b_group[i], r_w_expert[i], r_b_expert[i])
        if i % 2 == 0:
            weights = [a_w_in[j].astype(BF16), a_b_in[j].reshape(1, 2 * D),
                       a_w_dw[j].reshape(CONV_A, DT, LANES), a_b_dw[j].reshape(DT, LANES),
                       jnp.stack([a_ln_g[j], a_ln_b[j], a_b_out[j]]),
                       a_w_out[j].astype(BF16)]
            kind = "a"
        else:
            weights = [b_w_in[j].astype(BF16), b_w_dw[j], b_w_out[j].astype(BF16)]
            kind = "b"
        x1, h2, ri, wc, counts = _mixer(kind, xin, prev, mods_l, mix_ln, weights, wr, rb, tri,
                                        name=f"mixer_{kind}{i}")
        pad_starts, n_blocks = _plan(counts[:, 0])
        dest = _dest_slots(pad_starts, ri)[0:TOP_K]
        xs = _dispatch_rows(h2.reshape(T, DP), dest)
        yb = _experts(i, pad_starts, n_blocks, xs, e_w_gate, e_w_up, e_w_down, name=f"experts{i}")
        yk = _return_rows(yb, dest).reshape(TOP_K, T, DP)
        prev = (yk, wc, mods_l, jnp.stack([ffn_ln_g[i], ffn_ln_b[i]]))
        xin = x1
    yk, wc, mods_l, ln = prev
    return _final(xin, yk, wc, mods_l, ln)
```

```python
import functools

import jax
import jax.numpy as jnp
from jax import lax
from jax.experimental import pallas as pl
from jax.experimental.pallas import tpu as pltpu
from jax.experimental.pallas import tpu_sc as plsc

F32 = jnp.float32
BF16 = jnp.bfloat16
I32 = jnp.int32
U32 = jnp.uint32

D = 1024
BATCH = 4
SEQ = 8192
T = BATCH * SEQ
DEPTH = 4
N_GROUPS = 4
EPG = 8
NE = N_GROUPS * EPG
TOP_K = 2
F = D // 2
CONV_A = 31
CONV_B = 3
ALPHA = (2.0 * DEPTH) ** 0.25
LN_EPS = 1e-5

LANES = 128
SUBLANES = 8
VMEM_LIMIT = 56 * 1024 * 1024

TS = 512
SR = 256
NS = SEQ // TS
HALO_A = 32
HALO_B = 8
BM = 512
ROW_DMA_PRIORITY = 1
NSLOT = T * TOP_K + NE * BM
NR = 48
ADA_TN = 1536
NEG = -1e30
DP = D // 2
DT = D // LANES
assert DT == SUBLANES

SC_CORES = 2
SC_SUBCORES = 16
SC_WORKERS = SC_CORES * SC_SUBCORES
SC_CHUNK = 64


def _sigmoid(x):
    return 1.0 / (1.0 + jnp.exp(-x))


def _pack_rows(x):
    return pltpu.pack_elementwise([x[:, :DP], x[:, DP:]], packed_dtype=BF16)


def _unpack_rows(p):
    lo = pltpu.unpack_elementwise(p, index=0, packed_dtype=BF16, unpacked_dtype=F32)
    hi = pltpu.unpack_elementwise(p, index=1, packed_dtype=BF16, unpacked_dtype=F32)
    return jnp.concatenate([lo, hi], axis=1)


def _layer_norm(x, g, b):
    mu = jnp.mean(x, axis=-1, keepdims=True)
    xc = x - mu
    var = jnp.mean(xc * xc, axis=-1, keepdims=True)
    return xc * lax.rsqrt(var + LN_EPS) * g + b


def _ada_kernel(c_ref, w_ref, b_ref, o_ref):
    c = c_ref[...]
    ca = (c * _sigmoid(c)).astype(BF16)
    w = w_ref[0].astype(BF16)
    o_ref[0] = jnp.dot(ca, w, preferred_element_type=F32) + b_ref[0]


def _ada_mods(c, ada_w, ada_b):
    out = pl.pallas_call(
        _ada_kernel,
        grid=(DEPTH, 6 * D // ADA_TN),
        in_specs=[
            pl.BlockSpec((BATCH, D), lambda i, j: (0, 0)),
            pl.BlockSpec((1, D, ADA_TN), lambda i, j: (i, 0, j)),
            pl.BlockSpec((1, 1, ADA_TN), lambda i, j: (i, 0, j)),
        ],
        out_specs=pl.BlockSpec((1, BATCH, ADA_TN), lambda i, j: (i, 0, j)),
        out_shape=jax.ShapeDtypeStruct((DEPTH, BATCH, 6 * D), F32),
        compiler_params=pltpu.CompilerParams(
            dimension_semantics=("arbitrary", "arbitrary"), vmem_limit_bytes=VMEM_LIMIT),
        name="ada_mods",
    )(c, ada_w, ada_b.reshape(DEPTH, 1, 6 * D))
    return out.reshape(DEPTH, BATCH, 6, D)


def _combine(x1, y0p, y1p, wc, g_f, ln_g, ln_b):
    w0 = jnp.tile(wc[:, :LANES], (1, D // LANES))
    w1 = jnp.tile(wc[:, LANES:], (1, D // LANES))
    y = w0 * _unpack_rows(y0p) + w1 * _unpack_rows(y1p)
    return _layer_norm(ALPHA * x1 + (1.0 + g_f) * y, ln_g, ln_b)


def _route(h2, wr_ref, rb_ref, tri_ref, cnt_ref, ri_ref, wc_ref, cnto_ref):
    lt = lax.dot_general(wr_ref[...], h2, (((1,), (1,)), ((), ())),
                         preferred_element_type=F32) + rb_ref[...]
    iota8 = lax.broadcasted_iota(I32, (SUBLANES, TS), 0).astype(F32)
    gl = lt[0:SUBLANES]
    gmax = jnp.max(gl, axis=0, keepdims=True)
    gidx = jnp.min(jnp.where(gl == gmax, iota8, float(SUBLANES)), axis=0, keepdims=True)
    gw = 1.0 / jnp.sum(jnp.exp(gl - gmax), axis=0, keepdims=True)
    el = lt[SUBLANES:2 * SUBLANES]
    for g in range(1, N_GROUPS):
        el = jnp.where(gidx == float(g), lt[SUBLANES * (g + 1):SUBLANES * (g + 2)], el)
    m1 = jnp.max(el, axis=0, keepdims=True)
    i1 = jnp.min(jnp.where(el == m1, iota8, float(SUBLANES)), axis=0, keepdims=True)
    el2 = jnp.where(iota8 == i1, -jnp.inf, el)
    m2 = jnp.max(el2, axis=0, keepdims=True)
    i2 = jnp.min(jnp.where(el2 == m2, iota8, float(SUBLANES)), axis=0, keepdims=True)
    r = jnp.exp(m2 - m1)
    w_a = gw / (1.0 + r)
    w_b = gw * r / (1.0 + r)
    e1 = gidx * float(EPG) + i1
    e2 = gidx * float(EPG) + i2

    iota_e = lax.broadcasted_iota(I32, (NE, TS), 0).astype(F32)
    oh1 = iota_e == e1
    oh2 = iota_e == e2
    oh = jnp.concatenate([jnp.where(oh1, 1.0, 0.0), jnp.where(oh2, 1.0, 0.0)], axis=0)
    before = jnp.dot(oh.astype(BF16), tri_ref[...], preferred_element_type=F32)
    tot = jnp.sum(oh, axis=1, keepdims=True)
    cnt = cnt_ref[...]
    base = jnp.tile(cnt, (1, TS // LANES))
    tot1 = tot[:NE]
    tot2 = tot[NE:]
    rank1 = jnp.sum(jnp.where(oh1, base + before[:NE], 0.0), axis=0, keepdims=True)
    rank2 = jnp.sum(jnp.where(oh2, base + tot1 + before[NE:], 0.0), axis=0, keepdims=True)
    new_cnt = cnt + tot1 + tot2
    cnt_ref[...] = new_cnt
    cnto_ref[...] = new_cnt.astype(I32)

    ri_ref[0:1, :] = e1.astype(I32)
    ri_ref[1:2, :] = e2.astype(I32)
    ri_ref[2:3, :] = rank1.astype(I32)
    ri_ref[3:4, :] = rank2.astype(I32)
    ri_ref[4:8, :] = jnp.zeros((4, TS), I32)
    wc_ref[0, :, :LANES] = jnp.broadcast_to(w_a, (LANES, TS)).T
    wc_ref[0, :, LANES:] = jnp.broadcast_to(w_b, (LANES, TS)).T


def _conv_taps(uext_ref, u, w_dw, halo, width, r0):
    uext_ref[halo + r0:halo + r0 + SR, :] = u
    acc = None
    for k in range(width):
        off = r0 + halo - (width - 1) + k
        term = w_dw[k:k + 1, :] * uext_ref[off:off + SR, :]
        acc = term if acc is None else acc + term
    return acc


def _conv_time_major(tm_ref, o2_ref, u, wk_ref, bias, r0, width, halo):
    for j in range(DT):
        tm_ref[pl.ds((halo + r0) * DT + j, SR, stride=DT), :] = u[:, j * LANES:(j + 1) * LANES]
    acc = None
    for k in range(width):
        off = (r0 + halo - (width - 1) + k) * DT
        term = tm_ref[off:off + SR * DT, :].reshape(SR, DT, LANES) * wk_ref[k]
        acc = term if acc is None else acc + term
    if bias is not None:
        acc = acc + bias
    o2_ref[r0 * DT:(r0 + SR) * DT, :] = acc.reshape(SR * DT, LANES)
    return jnp.concatenate([o2_ref[pl.ds(r0 * DT + j, SR, stride=DT), :] for j in range(DT)], axis=1)


def _mixer_kernel(*refs, kind, has_prev):
    it = iter(refs)
    xin_ref = next(it)
    if has_prev:
        y0_ref, y1_ref, wcin_ref, pmods_ref, pln_ref = (next(it) for _ in range(5))
    mods_ref, mln_ref, win_ref = next(it), next(it), next(it)
    if kind == "a":
        bin_ref, wdw_ref, bdw_ref, vec_ref = next(it), next(it), next(it), next(it)
    else:
        wdw_ref = next(it)
    wout_ref, wr_ref, rb_ref, tri_ref = (next(it) for _ in range(4))
    x1_ref, h2_ref, ri_ref, wc_ref, cnto_ref = (next(it) for _ in range(5))
    if kind == "a":
        uext_ref, o2_ref, h2b_ref, cnt_ref = (next(it) for _ in range(4))
    else:
        uext_ref, h2b_ref, cnt_ref = (next(it) for _ in range(3))

    first = (pl.program_id(0) == 0) & (pl.program_id(1) == 0)

    @pl.when(first)
    def _():
        cnt_ref[...] = jnp.zeros((NE, LANES), F32)

    halo_rows = HALO_A * DT if kind == "a" else HALO_B

    @pl.when(pl.program_id(1) == 0)
    def _():
        uext_ref[0:halo_rows, :] = jnp.zeros((halo_rows, uext_ref.shape[1]), F32)

    m = mods_ref[0, 0]
    mln = mln_ref[...]
    for i in range(TS // SR):
        r0 = i * SR
        rows = slice(r0, r0 + SR)
        x = xin_ref[0, rows, :]
        if has_prev:
            pln = pln_ref[...]
            x = _combine(x, y0_ref[0, rows, :], y1_ref[0, rows, :], wcin_ref[0, rows, :],
                         pmods_ref[0, 0][5:6, :], pln[0:1, :], pln[1:2, :])

        h = (x * (1.0 + m[1:2, :]) + m[0:1, :]).astype(BF16)
        cols = [jnp.dot(h, win_ref[:, c * D:(c + 1) * D], preferred_element_type=F32)
                for c in range(win_ref.shape[1] // D)]
        if kind == "a":
            b_in = bin_ref[...]
            vec = vec_ref[...]
            u = (cols[0] + b_in[:, :D]) * _sigmoid(cols[1] + b_in[:, D:])
            u = _conv_time_major(uext_ref, o2_ref, u, wdw_ref, bdw_ref[...], r0, CONV_A, HALO_A)
            u = _layer_norm(u, vec[0:1, :], vec[1:2, :])
            u = u * _sigmoid(u)
            y = jnp.dot(u.astype(BF16), wout_ref[...], preferred_element_type=F32) + vec[2:3, :]
        else:
            gb = cols[0]
            q = cols[1] * cols[2]
            u = _conv_taps(uext_ref, q, wdw_ref[...], HALO_B, CONV_B, r0)
            y = jnp.dot((gb * u).astype(BF16), wout_ref[...], preferred_element_type=F32)

        x1 = _layer_norm(ALPHA * x + (1.0 + m[2:3, :]) * y, mln[0:1, :], mln[1:2, :])
        x1_ref[0, rows, :] = x1
        h2 = x1 * (1.0 + m[4:5, :]) + m[3:4, :]
        h2_ref[0, rows, :] = _pack_rows(h2)
        h2b_ref[rows, :] = h2.astype(BF16)

    uext_ref[0:halo_rows, :] = uext_ref[uext_ref.shape[0] - halo_rows:uext_ref.shape[0], :]
    _route(h2b_ref[...], wr_ref, rb_ref, tri_ref, cnt_ref, ri_ref, wc_ref, cnto_ref)


def _tile_spec(width):
    return pl.BlockSpec((1, TS, width), lambda b, s: (b, s, 0))


def _yk_spec(k):
    return pl.BlockSpec((1, TS, DP), lambda b, s: (k, b * NS + s, 0))


def _const_spec(shape):
    nd = len(shape)
    return pl.BlockSpec(shape, lambda b, s: (0,) * nd)


def _mods_spec():
    return pl.BlockSpec((1, 1, 6, D), lambda b, s: (0, b, 0, 0))


def _mixer(kind, xin, prev, mods_l, mix_ln, weights, wr, rb, tri, name):
    has_prev = prev is not None
    args = [xin]
    specs = [_tile_spec(D)]
    if has_prev:
        yk, wcin, pmods, pln = prev
        args += [yk, yk, wcin, pmods, pln]
        specs += [_yk_spec(0), _yk_spec(1), _tile_spec(2 * LANES), _mods_spec(), _const_spec((2, D))]
    args += [mods_l, mix_ln]
    specs += [_mods_spec(), _const_spec((2, D))]
    for w in weights:
        args.append(w)
        specs.append(_const_spec(w.shape))
    args += [wr, rb, tri]
    specs += [_const_spec(wr.shape), _const_spec(rb.shape), _const_spec(tri.shape)]
    if kind == "a":
        conv_scratch = [pltpu.VMEM(((TS + HALO_A) * DT, LANES), F32), pltpu.VMEM((TS * DT, LANES), F32)]
    else:
        conv_scratch = [pltpu.VMEM((TS + HALO_B, D), F32)]
    out_shape = (
        jax.ShapeDtypeStruct((BATCH, SEQ, D), F32),
        jax.ShapeDtypeStruct((BATCH, SEQ, DP), U32),
        jax.ShapeDtypeStruct((SUBLANES, T), I32),
        jax.ShapeDtypeStruct((BATCH, SEQ, 2 * LANES), F32),
        jax.ShapeDtypeStruct((NE, LANES), I32),
    )
    out_specs = (
        _tile_spec(D), _tile_spec(DP),
        pl.BlockSpec((SUBLANES, TS), lambda b, s: (0, b * NS + s)),
        _tile_spec(2 * LANES),
        pl.BlockSpec((NE, LANES), lambda b, s: (0, 0)),
    )
    return pl.pallas_call(
        functools.partial(_mixer_kernel, kind=kind, has_prev=has_prev),
        grid=(BATCH, NS),
        in_specs=specs,
        out_specs=out_specs,
        out_shape=out_shape,
        scratch_shapes=conv_scratch + [pltpu.VMEM((TS, D), BF16), pltpu.VMEM((NE, LANES), F32)],
        compiler_params=pltpu.CompilerParams(
            dimension_semantics=("arbitrary", "arbitrary"), vmem_limit_bytes=VMEM_LIMIT),
        name=name,
    )(*args)


def _dest_kernel(ps_ref, ri_ref, o_ref):
    ri = ri_ref[...]
    e = ri[0:2, :]
    start = jnp.zeros_like(e)
    for k in range(NE):
        start = jnp.where(e == k, ps_ref[k], start)
    o_ref[0:2, :] = start + ri[2:4, :]
    o_ref[2:8, :] = jnp.zeros((6, ri.shape[1]), I32)


def _dest_slots(pad_starts, ri):
    tn = 4096
    return pl.pallas_call(
        _dest_kernel,
        grid_spec=pltpu.PrefetchScalarGridSpec(
            num_scalar_prefetch=1,
            grid=(T // tn,),
            in_specs=[pl.BlockSpec((SUBLANES, tn), lambda i, ps: (0, i))],
            out_specs=pl.BlockSpec((SUBLANES, tn), lambda i, ps: (0, i)),
        ),
        out_shape=jax.ShapeDtypeStruct((SUBLANES, T), I32),
        compiler_params=pltpu.CompilerParams(dimension_semantics=("arbitrary",)),
        name="dest_slots",
    )(pad_starts, ri)


def _expert_kernel(st_ref, nb_ref, xs_hbm, wg_ref, wu_ref, wd_ref, yb_hbm, wgu_s, wd_s, xbuf, obuf, xsem, osem):
    e = pl.program_id(0)
    nb = nb_ref[e]
    start = st_ref[e]

    def x_copy(first_row, i, slot):
        return pltpu.make_async_copy(xs_hbm.at[pl.ds(pl.multiple_of(first_row + i * BM, BM), BM)],
                                     xbuf.at[slot], xsem.at[slot])

    def o_copy(i, slot):
        return pltpu.make_async_copy(obuf.at[slot],
                                     yb_hbm.at[pl.ds(pl.multiple_of(start + i * BM, BM), BM)], osem.at[slot])

    @pl.when((nb > 0) & (e == 0))
    def _():
        x_copy(start, 0, 0).start(priority=ROW_DMA_PRIORITY)

    @pl.when(nb > 0)
    def _():
        wgu_s[:, :F] = wg_ref[0, 0].astype(BF16)
        wgu_s[:, F:] = wu_ref[0, 0].astype(BF16)
        wd_s[...] = wd_ref[0, 0].astype(BF16)

    @pl.loop(0, nb, step=2)
    def _(i0):
        for slot in range(2):
            i = i0 + slot

            @pl.when(i < nb)
            def _():
                x_copy(start, i, slot).wait()

                @pl.when(i + 1 < nb)
                def _():
                    x_copy(start, i + 1, 1 - slot).start(priority=ROW_DMA_PRIORITY)

                @pl.when(i >= 2)
                def _():
                    o_copy(i - 2, slot).wait()

                x = _unpack_rows(xbuf[slot]).astype(BF16)
                gu = jnp.dot(x, wgu_s[...], preferred_element_type=F32)
                g = gu[:, :F]
                hid = (g * _sigmoid(g) * gu[:, F:]).astype(BF16)
                obuf[slot] = _pack_rows(jnp.dot(hid, wd_s[...], preferred_element_type=F32))
                o_copy(i, slot).start(priority=ROW_DMA_PRIORITY)

    nxt = jnp.minimum(e + 1, NE - 1)

    @pl.when((e + 1 < NE) & (nb_ref[nxt] > 0))
    def _():
        x_copy(st_ref[nxt], 0, 0).start(priority=ROW_DMA_PRIORITY)

    for back in (1, 2):
        @pl.when(nb >= back)
        def _():
            last = nb - back
            o_copy(last, last % 2).wait()


def _experts(layer, starts, n_blocks, xs, w_gate, w_up, w_down, name):
    def w_map(e, st, nb):
        return (layer, e, 0, 0)

    return pl.pallas_call(
        _expert_kernel,
        grid_spec=pltpu.PrefetchScalarGridSpec(
            num_scalar_prefetch=2,
            grid=(NE,),
            in_specs=[
                pl.BlockSpec(memory_space=pl.ANY),
                pl.BlockSpec((1, 1, D, F), w_map),
                pl.BlockSpec((1, 1, D, F), w_map),
                pl.BlockSpec((1, 1, F, D), w_map),
            ],
            out_specs=pl.BlockSpec(memory_space=pl.ANY),
            scratch_shapes=[
                pltpu.VMEM((D, 2 * F), BF16), pltpu.VMEM((F, D), BF16),
                pltpu.VMEM((2, BM, DP), U32), pltpu.VMEM((2, BM, DP), U32),
                pltpu.SemaphoreType.DMA((2,)), pltpu.SemaphoreType.DMA((2,)),
            ],
        ),
        out_shape=jax.ShapeDtypeStruct((NSLOT, DP), U32),
        compiler_params=pltpu.CompilerParams(
            dimension_semantics=("arbitrary",), vmem_limit_bytes=VMEM_LIMIT),
        name=name,
    )(starts, n_blocks, xs, w_gate, w_up, w_down)


def _sc_worker_id():
    return lax.axis_index("s") * SC_CORES + lax.axis_index("c")


def _sc_mesh():
    return plsc.VectorSubcoreMesh(core_axis_name="c", subcore_axis_name="s")


def _sc_scratch(n_index_rows):
    return [
        pltpu.VMEM((n_index_rows, SC_CHUNK), I32),
        pltpu.VMEM((2, SC_CHUNK, DP), U32),
        pltpu.SemaphoreType.DMA((2,)),
        pltpu.SemaphoreType.DMA((2,)),
    ]


def _dispatch_rows(h2p, dest):
    per_w = T // SC_WORKERS
    n_chunks = per_w // SC_CHUNK

    @functools.partial(
        pl.kernel, mesh=_sc_mesh(),
        out_type=jax.ShapeDtypeStruct((NSLOT, DP), U32),
        scratch_types=_sc_scratch(TOP_K * n_chunks),
        name="dispatch_rows",
    )
    def k(h2_hbm, dest_hbm, out_hbm, dest_v, rows_v, rsem, wsem):
        wid = _sc_worker_id()
        for kk in range(TOP_K):
            pltpu.sync_copy(dest_hbm.at[kk, pl.ds(wid * n_chunks, n_chunks)],
                            dest_v.at[pl.ds(kk * n_chunks, n_chunks)])
        base = wid * per_w

        def read(c, slot):
            return pltpu.make_async_copy(h2_hbm.at[pl.ds(base + c * SC_CHUNK, SC_CHUNK)],
                                         rows_v.at[slot], rsem.at[slot])

        def write(c, kk, slot):
            return pltpu.make_async_copy(rows_v.at[slot], out_hbm.at[dest_v.at[kk * n_chunks + c]],
                                         wsem.at[slot])

        read(0, 0).start()

        @pl.loop(0, n_chunks, step=2)
        def _(c):
            for b in range(2):
                cc = c + b
                read(cc, b).wait()

                @pl.when(cc + 1 < n_chunks)
                def _():
                    @pl.when(cc >= 1)
                    def _():
                        for kk in range(TOP_K):
                            write(cc - 1, kk, 1 - b).wait()
                    read(cc + 1, 1 - b).start()

                for kk in range(TOP_K):
                    write(cc, kk, b).start()

        for slot, cc in ((0, n_chunks - 2), (1, n_chunks - 1)):
            for kk in range(TOP_K):
                write(cc, kk, slot).wait()

    return k(h2p, dest.reshape(TOP_K, T // SC_CHUNK, SC_CHUNK))


def _return_rows(yb, dest):
    m = TOP_K * T
    per_w = m // SC_WORKERS
    n_chunks = per_w // SC_CHUNK

    @functools.partial(
        pl.kernel, mesh=_sc_mesh(),
        out_type=jax.ShapeDtypeStruct((m, DP), U32),
        scratch_types=_sc_scratch(n_chunks),
        name="return_rows",
    )
    def k(yb_hbm, dest_hbm, out_hbm, idx_v, rows_v, gsem, wsem):
        wid = _sc_worker_id()
        pltpu.sync_copy(dest_hbm.at[pl.ds(wid * n_chunks, n_chunks)], idx_v)
        base = wid * per_w

        def gather(c, slot):
            return pltpu.make_async_copy(yb_hbm.at[idx_v.at[c]], rows_v.at[slot], gsem.at[slot])

        def write(c, slot):
            return pltpu.make_async_copy(rows_v.at[slot], out_hbm.at[pl.ds(base + c * SC_CHUNK, SC_CHUNK)],
                                         wsem.at[slot])

        gather(0, 0).start()

        @pl.loop(0, n_chunks, step=2)
        def _(c):
            for b in range(2):
                cc = c + b
                gather(cc, b).wait()

                @pl.when(cc + 1 < n_chunks)
                def _():
                    @pl.when(cc >= 1)
                    def _():
                        write(cc - 1, 1 - b).wait()
                    gather(cc + 1, 1 - b).start()

                write(cc, b).start()

        write(n_chunks - 2, 0).wait()
        write(n_chunks - 1, 1).wait()

    return k(yb, dest.reshape(m // SC_CHUNK, SC_CHUNK))


def _final_kernel(x1_ref, y0_ref, y1_ref, wc_ref, mods_ref, ln_ref, o_ref):
    ln = ln_ref[...]
    o_ref[0] = _combine(x1_ref[0], y0_ref[0], y1_ref[0], wc_ref[0], mods_ref[0, 0][5:6, :],
                        ln[0:1, :], ln[1:2, :])


def _final(x1, yk, wc, mods_l, ln):
    return pl.pallas_call(
        _final_kernel,
        grid=(BATCH, NS),
        in_specs=[_tile_spec(D), _yk_spec(0), _yk_spec(1), _tile_spec(2 * LANES),
                  _mods_spec(), _const_spec((2, D))],
        out_specs=_tile_spec(D),
        out_shape=jax.ShapeDtypeStruct((BATCH, SEQ, D), F32),
        compiler_params=pltpu.CompilerParams(
            dimension_semantics=("arbitrary", "arbitrary"), vmem_limit_bytes=VMEM_LIMIT),
        name="final_combine",
    )(x1, yk, yk, wc, mods_l, ln)


def _plan(counts):
    n_blocks = (counts + BM - 1) // BM
    padded = n_blocks * BM
    pad_starts = (jnp.cumsum(padded) - padded).astype(I32)
    return pad_starts, n_blocks.astype(I32)


def _router_params(w_group, b_group, w_expert, b_expert):
    wr = jnp.zeros((NR, D), F32)
    wr = wr.at[0:N_GROUPS].set(w_group.T).at[SUBLANES:SUBLANES + NE].set(w_expert.T)
    rb = jnp.full((NR,), NEG, F32)
    rb = rb.at[0:N_GROUPS].set(b_group).at[SUBLANES:SUBLANES + NE].set(b_expert)
    rb = rb.at[SUBLANES + NE:].set(0.0)
    return wr.astype(BF16), rb.reshape(NR, 1)


def kernel(x, c, ada_w, ada_b, a_w_in, a_b_in, a_w_dw, a_b_dw, a_ln_g, a_ln_b, a_w_out, a_b_out,
           b_w_in, b_w_dw, b_w_out, mix_ln_g, mix_ln_b, ffn_ln_g, ffn_ln_b,
           r_w_group, r_b_group, r_w_expert, r_b_expert, e_w_gate, e_w_up, e_w_down):
    mods = _ada_mods(c, ada_w, ada_b)
    tri = (jnp.arange(TS)[:, None] < jnp.arange(TS)[None, :]).astype(BF16)
    prev = None
    xin = x
    for i in range(DEPTH):
        j = i // 2
        mods_l = mods[i:i + 1]
        mix_ln = jnp.stack([mix_ln_g[i], mix_ln_b[i]])
        wr, rb = _router_params(r_w_group[i], r_b_group[i], r_w_expert[i], r_b_expert[i])
        if i % 2 == 0:
            weights = [a_w_in[j].astype(BF16), a_b_in[j].reshape(1, 2 * D),
                       a_w_dw[j].reshape(CONV_A, DT, LANES), a_b_dw[j].reshape(DT, LANES),
                       jnp.stack([a_ln_g[j], a_ln_b[j], a_b_out[j]]),
                       a_w_out[j].astype(BF16)]
            kind = "a"
        else:
            weights = [b_w_in[j].astype(BF16), b_w_dw[j], b_w_out[j].astype(BF16)]
            kind = "b"
        x1, h2, ri, wc, counts = _mixer(kind, xin, prev, mods_l, mix_ln, weights, wr, rb, tri,
                                        name=f"mixer_{kind}{i}")
        pad_starts, n_blocks = _plan(counts[:, 0])
        dest = _dest_slots(pad_starts, ri)[0:TOP_K]
        xs = _dispatch_rows(h2.reshape(T, DP), dest)
        yb = _experts(i, pad_starts, n_blocks, xs, e_w_gate, e_w_up, e_w_down, name=f"experts{i}")
        yk = _return_rows(yb, dest).reshape(TOP_K, T, DP)
        prev = (yk, wc, mods_l, jnp.stack([ffn_ln_g[i], ffn_ln_b[i]]))
        xin = x1
    yk, wc, mods_l, ln = prev
    return _final(xin, yk, wc, mods_l, ln)
```

```python
import functools

import jax
import jax.numpy as jnp
from jax import lax
from jax.experimental import pallas as pl
from jax.experimental.pallas import tpu as pltpu
from jax.experimental.pallas import tpu_sc as plsc

F32 = jnp.float32
BF16 = jnp.bfloat16
I32 = jnp.int32
U32 = jnp.uint32

D = 1024
BATCH = 4
SEQ = 8192
T = BATCH * SEQ
DEPTH = 4
N_GROUPS = 4
EPG = 8
NE = N_GROUPS * EPG
TOP_K = 2
F = D // 2
CONV_A = 31
CONV_B = 3
ALPHA = (2.0 * DEPTH) ** 0.25
LN_EPS = 1e-5

LANES = 128
SUBLANES = 8
VMEM_LIMIT = 56 * 1024 * 1024

TS = 512
SR = 256
NS = SEQ // TS
HALO_A = 32
HALO_B = 8
BM = 512
ROW_DMA_PRIORITY = 1
NSLOT = T * TOP_K + NE * BM
NR = 48
ADA_TN = 1536
NEG = -1e30
DP = D // 2
DT = D // LANES
assert DT == SUBLANES

SC_CORES = 2
SC_SUBCORES = 16
SC_WORKERS = SC_CORES * SC_SUBCORES
SC_CHUNK = 64


def _sigmoid(x):
    return 1.0 / (1.0 + jnp.exp(-x))


def _pack_rows(x):
    return pltpu.pack_elementwise([x[:, :DP], x[:, DP:]], packed_dtype=BF16)


def _unpack_rows(p):
    lo = pltpu.unpack_elementwise(p, index=0, packed_dtype=BF16, unpacked_dtype=F32)
    hi = pltpu.unpack_elementwise(p, index=1, packed_dtype=BF16, unpacked_dtype=F32)
    return jnp.concatenate([lo, hi], axis=1)


def _layer_norm(x, g, b):
    mu = jnp.mean(x, axis=-1, keepdims=True)
    xc = x - mu
    var = jnp.mean(xc * xc, axis=-1, keepdims=True)
    return xc * lax.rsqrt(var + LN_EPS) * g + b


def _ada_kernel(c_ref, w_ref, b_ref, o_ref):
    c = c_ref[...]
    ca = (c * _sigmoid(c)).astype(BF16)
    w = w_ref[0].astype(BF16)
    o_ref[0] = jnp.dot(ca, w, preferred_element_type=F32) + b_ref[0]


def _ada_mods(c, ada_w, ada_b):
    out = pl.pallas_call(
        _ada_kernel,
        grid=(DEPTH, 6 * D // ADA_TN),
        in_specs=[
            pl.BlockSpec((BATCH, D), lambda i, j: (0, 0)),
            pl.BlockSpec((1, D, ADA_TN), lambda i, j: (i, 0, j)),
            pl.BlockSpec((1, 1, ADA_TN), lambda i, j: (i, 0, j)),
        ],
        out_specs=pl.BlockSpec((1, BATCH, ADA_TN), lambda i, j: (i, 0, j)),
        out_shape=jax.ShapeDtypeStruct((DEPTH, BATCH, 6 * D), F32),
        compiler_params=pltpu.CompilerParams(
            dimension_semantics=("arbitrary", "arbitrary"), vmem_limit_bytes=VMEM_LIMIT),
        name="ada_mods",
    )(c, ada_w, ada_b.reshape(DEPTH, 1, 6 * D))
    return out.reshape(DEPTH, BATCH, 6, D)


def _combine(x1, y0p, y1p, wc, g_f, ln_g, ln_b):
    w0 = jnp.tile(wc[:, :LANES], (1, D // LANES))
    w1 = jnp.tile(wc[:, LANES:], (1, D // LANES))
    y = w0 * _unpack_rows(y0p) + w1 * _unpack_rows(y1p)
    return _layer_norm(ALPHA * x1 + (1.0 + g_f) * y, ln_g, ln_b)


def _route(h2, wr_ref, rb_ref, tri_ref, cnt_ref, ri_ref, wc_ref, cnto_ref):
    lt = lax.dot_general(wr_ref[...], h2, (((1,), (1,)), ((), ())),
                         preferred_element_type=F32) + rb_ref[...]
    iota8 = lax.broadcasted_iota(I32, (SUBLANES, TS), 0).astype(F32)
    gl = lt[0:SUBLANES]
    gmax = jnp.max(gl, axis=0, keepdims=True)
    gidx = jnp.min(jnp.where(gl == gmax, iota8, float(SUBLANES)), axis=0, keepdims=True)
    gw = 1.0 / jnp.sum(jnp.exp(gl - gmax), axis=0, keepdims=True)
    el = lt[SUBLANES:2 * SUBLANES]
    for g in range(1, N_GROUPS):
        el = jnp.where(gidx == float(g), lt[SUBLANES * (g + 1):SUBLANES * (g + 2)], el)
    m1 = jnp.max(el, axis=0, keepdims=True)
    i1 = jnp.min(jnp.where(el == m1, iota8, float(SUBLANES)), axis=0, keepdims=True)
    el2 = jnp.where(iota8 == i1, -jnp.inf, el)
    m2 = jnp.max(el2, axis=0, keepdims=True)
    i2 = jnp.min(jnp.where(el2 == m2, iota8, float(SUBLANES)), axis=0, keepdims=True)
    r = jnp.exp(m2 - m1)
    w_a = gw / (1.0 + r)
    w_b = gw * r / (1.0 + r)
    e1 = gidx * float(EPG) + i1
    e2 = gidx * float(EPG) + i2

    iota_e = lax.broadcasted_iota(I32, (NE, TS), 0).astype(F32)
    oh1 = iota_e == e1
    oh2 = iota_e == e2
    oh = jnp.concatenate([jnp.where(oh1, 1.0, 0.0), jnp.where(oh2, 1.0, 0.0)], axis=0)
    before = jnp.dot(oh.astype(BF16), tri_ref[...], preferred_element_type=F32)
    tot = jnp.sum(oh, axis=1, keepdims=True)
    cnt = cnt_ref[...]
    base = jnp.tile(cnt, (1, TS // LANES))
    tot1 = tot[:NE]
    tot2 = tot[NE:]
    rank1 = jnp.sum(jnp.where(oh1, base + before[:NE], 0.0), axis=0, keepdims=True)
    rank2 = jnp.sum(jnp.where(oh2, base + tot1 + before[NE:], 0.0), axis=0, keepdims=True)
    new_cnt = cnt + tot1 + tot2
    cnt_ref[...] = new_cnt
    cnto_ref[...] = new_cnt.astype(I32)

    ri_ref[0:1, :] = e1.astype(I32)
    ri_ref[1:2, :] = e2.astype(I32)
    ri_ref[2:3, :] = rank1.astype(I32)
    ri_ref[3:4, :] = rank2.astype(I32)
    ri_ref[4:8, :] = jnp.zeros((4, TS), I32)
    wc_ref[0, :, :LANES] = jnp.broadcast_to(w_a, (LANES, TS)).T
    wc_ref[0, :, LANES:] = jnp.broadcast_to(w_b, (LANES, TS)).T


def _conv_taps(uext_ref, u, w_dw, halo, width, r0):
    uext_ref[halo + r0:halo + r0 + SR, :] = u
    acc = None
    for k in range(width):
        off = r0 + halo - (width - 1) + k
        term = w_dw[k:k + 1, :] * uext_ref[off:off + SR, :]
        acc = term if acc is None else acc + term
    return acc


def _conv_time_major(tm_ref, o2_ref, u, wk_ref, bias, r0, width, halo):
    for j in range(DT):
        tm_ref[pl.ds((halo + r0) * DT + j, SR, stride=DT), :] = u[:, j * LANES:(j + 1) * LANES]
    acc = None
    for k in range(width):
        off = (r0 + halo - (width - 1) + k) * DT
        term = tm_ref[off:off + SR * DT, :].reshape(SR, DT, LANES) * wk_ref[k]
        acc = term if acc is None else acc + term
    if bias is not None:
        acc = acc + bias
    o2_ref[r0 * DT:(r0 + SR) * DT, :] = acc.reshape(SR * DT, LANES)
    return jnp.concatenate([o2_ref[pl.ds(r0 * DT + j, SR, stride=DT), :] for j in range(DT)], axis=1)


def _mixer_kernel(*refs, kind, has_prev):
    it = iter(refs)
    xin_ref = next(it)
    if has_prev:
        y0_ref, y1_ref, wcin_ref, pmods_ref, pln_ref = (next(it) for _ in range(5))
    mods_ref, mln_ref, win_ref = next(it), next(it), next(it)
    if kind == "a":
        bin_ref, wdw_ref, bdw_ref, vec_ref = next(it), next(it), next(it), next(it)
    else:
        wdw_ref = next(it)
    wout_ref, wr_ref, rb_ref, tri_ref = (next(it) for _ in range(4))
    x1_ref, h2_ref, ri_ref, wc_ref, cnto_ref = (next(it) for _ in range(5))
    if kind == "a":
        uext_ref, o2_ref, h2b_ref, cnt_ref = (next(it) for _ in range(4))
    else:
        uext_ref, h2b_ref, cnt_ref = (next(it) for _ in range(3))
    win_s, wout_s = next(it), next(it)

    first = (pl.program_id(0) == 0) & (pl.program_id(1) == 0)

    @pl.when(first)
    def _():
        cnt_ref[...] = jnp.zeros((NE, LANES), F32)
        win_s[...] = win_ref[...]
        wout_s[...] = wout_ref[...]

    halo_rows = HALO_A * DT if kind == "a" else HALO_B

    @pl.when(pl.program_id(1) == 0)
    def _():
        uext_ref[0:halo_rows, :] = jnp.zeros((halo_rows, uext_ref.shape[1]), F32)

    m = mods_ref[0, 0]
    mln = mln_ref[...]
    for i in range(TS // SR):
        r0 = i * SR
        rows = slice(r0, r0 + SR)
        x = xin_ref[0, rows, :]
        if has_prev:
            pln = pln_ref[...]
            x = _combine(x, y0_ref[0, rows, :], y1_ref[0, rows, :], wcin_ref[0, rows, :],
                         pmods_ref[0, 0][5:6, :], pln[0:1, :], pln[1:2, :])

        h = (x * (1.0 + m[1:2, :]) + m[0:1, :]).astype(BF16)
        cols = [jnp.dot(h, win_s[:, c * D:(c + 1) * D], preferred_element_type=F32)
                for c in range(win_s.shape[1] // D)]
        if kind == "a":
            b_in = bin_ref[...]
            vec = vec_ref[...]
            u = (cols[0] + b_in[:, :D]) * _sigmoid(cols[1] + b_in[:, D:])
            u = _conv_time_major(uext_ref, o2_ref, u, wdw_ref, bdw_ref[...], r0, CONV_A, HALO_A)
            u = _layer_norm(u, vec[0:1, :], vec[1:2, :])
            u = u * _sigmoid(u)
            y = jnp.dot(u.astype(BF16), wout_s[...], preferred_element_type=F32) + vec[2:3, :]
        else:
            gb = cols[0]
            q = cols[1] * cols[2]
            u = _conv_taps(uext_ref, q, wdw_ref[...], HALO_B, CONV_B, r0)
            y = jnp.dot((gb * u).astype(BF16), wout_s[...], preferred_element_type=F32)

        x1 = _layer_norm(ALPHA * x + (1.0 + m[2:3, :]) * y, mln[0:1, :], mln[1:2, :])
        x1_ref[0, rows, :] = x1
        h2 = x1 * (1.0 + m[4:5, :]) + m[3:4, :]
        h2_ref[0, rows, :] = _pack_rows(h2)
        h2b_ref[rows, :] = h2.astype(BF16)

    uext_ref[0:halo_rows, :] = uext_ref[uext_ref.shape[0] - halo_rows:uext_ref.shape[0], :]
    _route(h2b_ref[...], wr_ref, rb_ref, tri_ref, cnt_ref, ri_ref, wc_ref, cnto_ref)


def _tile_spec(width):
    return pl.BlockSpec((1, TS, width), lambda b, s: (b, s, 0))


def _yk_spec(k):
    return pl.BlockSpec((1, TS, DP), lambda b, s: (k, b * NS + s, 0))


def _const_spec(shape):
    nd = len(shape)
    return pl.BlockSpec(shape, lambda b, s: (0,) * nd)


def _mods_spec():
    return pl.BlockSpec((1, 1, 6, D), lambda b, s: (0, b, 0, 0))


def _mixer(kind, xin, prev, mods_l, mix_ln, weights, wr, rb, tri, name):
    has_prev = prev is not None
    args = [xin]
    specs = [_tile_spec(D)]
    if has_prev:
        yk, wcin, pmods, pln = prev
        args += [yk, yk, wcin, pmods, pln]
        specs += [_yk_spec(0), _yk_spec(1), _tile_spec(2 * LANES), _mods_spec(), _const_spec((2, D))]
    args += [mods_l, mix_ln]
    specs += [_mods_spec(), _const_spec((2, D))]
    for w in weights:
        args.append(w)
        specs.append(_const_spec(w.shape))
    args += [wr, rb, tri]
    specs += [_const_spec(wr.shape), _const_spec(rb.shape), _const_spec(tri.shape)]
    if kind == "a":
        conv_scratch = [pltpu.VMEM(((TS + HALO_A) * DT, LANES), F32), pltpu.VMEM((TS * DT, LANES), F32)]
    else:
        conv_scratch = [pltpu.VMEM((TS + HALO_B, D), F32)]
    out_shape = (
        jax.ShapeDtypeStruct((BATCH, SEQ, D), F32),
        jax.ShapeDtypeStruct((BATCH, SEQ, DP), U32),
        jax.ShapeDtypeStruct((SUBLANES, T), I32),
        jax.ShapeDtypeStruct((BATCH, SEQ, 2 * LANES), F32),
        jax.ShapeDtypeStruct((NE, LANES), I32),
    )
    out_specs = (
        _tile_spec(D), _tile_spec(DP),
        pl.BlockSpec((SUBLANES, TS), lambda b, s: (0, b * NS + s)),
        _tile_spec(2 * LANES),
        pl.BlockSpec((NE, LANES), lambda b, s: (0, 0)),
    )
    return pl.pallas_call(
        functools.partial(_mixer_kernel, kind=kind, has_prev=has_prev),
        grid=(BATCH, NS),
        in_specs=specs,
        out_specs=out_specs,
        out_shape=out_shape,
        scratch_shapes=conv_scratch + [pltpu.VMEM((TS, D), BF16), pltpu.VMEM((NE, LANES), F32),
                                       pltpu.VMEM(weights[0].shape, BF16), pltpu.VMEM(weights[-1].shape, BF16)],
        compiler_params=pltpu.CompilerParams(
            dimension_semantics=("arbitrary", "arbitrary"), vmem_limit_bytes=VMEM_LIMIT),
        name=name,
    )(*args)


def _dest_kernel(ps_ref, ri_ref, o_ref):
    ri = ri_ref[...]
    e = ri[0:2, :]
    start = jnp.zeros_like(e)
    for k in range(NE):
        start = jnp.where(e == k, ps_ref[k], start)
    o_ref[0:2, :] = start + ri[2:4, :]
    o_ref[2:8, :] = jnp.zeros((6, ri.shape[1]), I32)


def _dest_slots(pad_starts, ri):
    tn = 4096
    return pl.pallas_call(
        _dest_kernel,
        grid_spec=pltpu.PrefetchScalarGridSpec(
            num_scalar_prefetch=1,
            grid=(T // tn,),
            in_specs=[pl.BlockSpec((SUBLANES, tn), lambda i, ps: (0, i))],
            out_specs=pl.BlockSpec((SUBLANES, tn), lambda i, ps: (0, i)),
        ),
        out_shape=jax.ShapeDtypeStruct((SUBLANES, T), I32),
        compiler_params=pltpu.CompilerParams(dimension_semantics=("arbitrary",)),
        name="dest_slots",
    )(pad_starts, ri)


def _expert_kernel(st_ref, nb_ref, xs_hbm, wg_ref, wu_ref, wd_ref, yb_hbm, wgu_s, wd_s, xbuf, obuf, xsem, osem):
    e = pl.program_id(0)
    nb = nb_ref[e]
    start = st_ref[e]

    def x_copy(first_row, i, slot):
        return pltpu.make_async_copy(xs_hbm.at[pl.ds(pl.multiple_of(first_row + i * BM, BM), BM)],
                                     xbuf.at[slot], xsem.at[slot])

    def o_copy(i, slot):
        return pltpu.make_async_copy(obuf.at[slot],
                                     yb_hbm.at[pl.ds(pl.multiple_of(start + i * BM, BM), BM)], osem.at[slot])

    @pl.when((nb > 0) & (e == 0))
    def _():
        x_copy(start, 0, 0).start(priority=ROW_DMA_PRIORITY)

    @pl.when(nb > 0)
    def _():
        wgu_s[:, :F] = wg_ref[0, 0].astype(BF16)
        wgu_s[:, F:] = wu_ref[0, 0].astype(BF16)
        wd_s[...] = wd_ref[0, 0].astype(BF16)

    @pl.loop(0, nb, step=2)
    def _(i0):
        for slot in range(2):
            i = i0 + slot

            @pl.when(i < nb)
            def _():
                x_copy(start, i, slot).wait()

                @pl.when(i + 1 < nb)
                def _():
                    x_copy(start, i + 1, 1 - slot).start(priority=ROW_DMA_PRIORITY)

                @pl.when(i >= 2)
                def _():
                    o_copy(i - 2, slot).wait()

                x = _unpack_rows(xbuf[slot]).astype(BF16)
                gu = jnp.dot(x, wgu_s[...], preferred_element_type=F32)
                g = gu[:, :F]
                hid = (g * _sigmoid(g) * gu[:, F:]).astype(BF16)
                obuf[slot] = _pack_rows(jnp.dot(hid, wd_s[...], preferred_element_type=F32))
                o_copy(i, slot).start(priority=ROW_DMA_PRIORITY)

    nxt = jnp.minimum(e + 1, NE - 1)

    @pl.when((e + 1 < NE) & (nb_ref[nxt] > 0))
    def _():
        x_copy(st_ref[nxt], 0, 0).start(priority=ROW_DMA_PRIORITY)

    for back in (1, 2):
        @pl.when(nb >= back)
        def _():
            last = nb - back
            o_copy(last, last % 2).wait()


def _experts(layer, starts, n_blocks, xs, w_gate, w_up, w_down, name):
    def w_map(e, st, nb):
        return (layer, e, 0, 0)

    return pl.pallas_call(
        _expert_kernel,
        grid_spec=pltpu.PrefetchScalarGridSpec(
            num_scalar_prefetch=2,
            grid=(NE,),
            in_specs=[
                pl.BlockSpec(memory_space=pl.ANY),
                pl.BlockSpec((1, 1, D, F), w_map),
                pl.BlockSpec((1, 1, D, F), w_map),
                pl.BlockSpec((1, 1, F, D), w_map),
            ],
            out_specs=pl.BlockSpec(memory_space=pl.ANY),
            scratch_shapes=[
                pltpu.VMEM((D, 2 * F), BF16), pltpu.VMEM((F, D), BF16),
                pltpu.VMEM((2, BM, DP), U32), pltpu.VMEM((2, BM, DP), U32),
                pltpu.SemaphoreType.DMA((2,)), pltpu.SemaphoreType.DMA((2,)),
            ],
        ),
        out_shape=jax.ShapeDtypeStruct((NSLOT, DP), U32),
        compiler_params=pltpu.CompilerParams(
            dimension_semantics=("arbitrary",), vmem_limit_bytes=VMEM_LIMIT),
        name=name,
    )(starts, n_blocks, xs, w_gate, w_up, w_down)


def _sc_worker_id():
    return lax.axis_index("s") * SC_CORES + lax.axis_index("c")


def _sc_mesh():
    return plsc.VectorSubcoreMesh(core_axis_name="c", subcore_axis_name="s")


def _sc_scratch(n_index_rows):
    return [
        pltpu.VMEM((n_index_rows, SC_CHUNK), I32),
        pltpu.VMEM((2, SC_CHUNK, DP), U32),
        pltpu.SemaphoreType.DMA((2,)),
        pltpu.SemaphoreType.DMA((2,)),
    ]


def _dispatch_rows(h2p, dest):
    per_w = T // SC_WORKERS
    n_chunks = per_w // SC_CHUNK

    @functools.partial(
        pl.kernel, mesh=_sc_mesh(),
        out_type=jax.ShapeDtypeStruct((NSLOT, DP), U32),
        scratch_types=_sc_scratch(TOP_K * n_chunks),
        name="dispatch_rows",
    )
    def k(h2_hbm, dest_hbm, out_hbm, dest_v, rows_v, rsem, wsem):
        wid = _sc_worker_id()
        for kk in range(TOP_K):
            pltpu.sync_copy(dest_hbm.at[kk, pl.ds(wid * n_chunks, n_chunks)],
                            dest_v.at[pl.ds(kk * n_chunks, n_chunks)])
        base = wid * per_w

        def read(c, slot):
            return pltpu.make_async_copy(h2_hbm.at[pl.ds(base + c * SC_CHUNK, SC_CHUNK)],
                                         rows_v.at[slot], rsem.at[slot])

        def write(c, kk, slot):
            return pltpu.make_async_copy(rows_v.at[slot], out_hbm.at[dest_v.at[kk * n_chunks + c]],
                                         wsem.at[slot])

        read(0, 0).start()

        @pl.loop(0, n_chunks, step=2)
        def _(c):
            for b in range(2):
                cc = c + b
                read(cc, b).wait()

                @pl.when(cc + 1 < n_chunks)
                def _():
                    @pl.when(cc >= 1)
                    def _():
                        for kk in range(TOP_K):
                            write(cc - 1, kk, 1 - b).wait()
                    read(cc + 1, 1 - b).start()

                for kk in range(TOP_K):
                    write(cc, kk, b).start()

        for slot, cc in ((0, n_chunks - 2), (1, n_chunks - 1)):
            for kk in range(TOP_K):
                write(cc, kk, slot).wait()

    return k(h2p, dest.reshape(TOP_K, T // SC_CHUNK, SC_CHUNK))


def _return_rows(yb, dest):
    m = TOP_K * T
    per_w = m // SC_WORKERS
    n_chunks = per_w // SC_CHUNK

    @functools.partial(
        pl.kernel, mesh=_sc_mesh(),
        out_type=jax.ShapeDtypeStruct((m, DP), U32),
        scratch_types=_sc_scratch(n_chunks),
        name="return_rows",
    )
    def k(yb_hbm, dest_hbm, out_hbm, idx_v, rows_v, gsem, wsem):
        wid = _sc_worker_id()
        pltpu.sync_copy(dest_hbm.at[pl.ds(wid * n_chunks, n_chunks)], idx_v)
        base = wid * per_w

        def gather(c, slot):
            return pltpu.make_async_copy(yb_hbm.at[idx_v.at[c]], rows_v.at[slot], gsem.at[slot])

        def write(c, slot):
            return pltpu.make_async_copy(rows_v.at[slot], out_hbm.at[pl.ds(base + c * SC_CHUNK, SC_CHUNK)],
                                         wsem.at[slot])

        gather(0, 0).start()

        @pl.loop(0, n_chunks, step=2)
        def _(c):
            for b in range(2):
                cc = c + b
                gather(cc, b).wait()

                @pl.when(cc + 1 < n_chunks)
                def _():
                    @pl.when(cc >= 1)
                    def _():
                        write(cc - 1, 1 - b).wait()
                    gather(cc + 1, 1 - b).start()

                write(cc, b).start()

        write(n_chunks - 2, 0).wait()
        write(n_chunks - 1, 1).wait()

    return k(yb, dest.reshape(m // SC_CHUNK, SC_CHUNK))


def _final_kernel(x1_ref, y0_ref, y1_ref, wc_ref, mods_ref, ln_ref, o_ref):
    ln = ln_ref[...]
    o_ref[0] = _combine(x1_ref[0], y0_ref[0], y1_ref[0], wc_ref[0], mods_ref[0, 0][5:6, :],
                        ln[0:1, :], ln[1:2, :])


def _final(x1, yk, wc, mods_l, ln):
    return pl.pallas_call(
        _final_kernel,
        grid=(BATCH, NS),
        in_specs=[_tile_spec(D), _yk_spec(0), _yk_spec(1), _tile_spec(2 * LANES),
                  _mods_spec(), _const_spec((2, D))],
        out_specs=_tile_spec(D),
        out_shape=jax.ShapeDtypeStruct((BATCH, SEQ, D), F32),
        compiler_params=pltpu.CompilerParams(
            dimension_semantics=("arbitrary", "arbitrary"), vmem_limit_bytes=VMEM_LIMIT),
        name="final_combine",
    )(x1, yk, yk, wc, mods_l, ln)


def _plan(counts):
    n_blocks = (counts + BM - 1) // BM
    padded = n_blocks * BM
    pad_starts = (jnp.cumsum(padded) - padded).astype(I32)
    return pad_starts, n_blocks.astype(I32)


def _router_params(w_group, b_group, w_expert, b_expert):
    wr = jnp.zeros((NR, D), F32)
    wr = wr.at[0:N_GROUPS].set(w_group.T).at[SUBLANES:SUBLANES + NE].set(w_expert.T)
    rb = jnp.full((NR,), NEG, F32)
    rb = rb.at[0:N_GROUPS].set(b_group).at[SUBLANES:SUBLANES + NE].set(b_expert)
    rb = rb.at[SUBLANES + NE:].set(0.0)
    return wr.astype(BF16), rb.reshape(NR, 1)


def kernel(x, c, ada_w, ada_b, a_w_in, a_b_in, a_w_dw, a_b_dw, a_ln_g, a_ln_b, a_w_out, a_b_out,
           b_w_in, b_w_dw, b_w_out, mix_ln_g, mix_ln_b, ffn_ln_g, ffn_ln_b,
           r_w_group, r_b_group, r_w_expert, r_b_expert, e_w_gate, e_w_up, e_w_down):
    mods = _ada_mods(c, ada_w, ada_b)
    tri = (jnp.arange(TS)[:, None] < jnp.arange(TS)[None, :]).astype(BF16)
    prev = None
    xin = x
    for i in range(DEPTH):
        j = i // 2
        mods_l = mods[i:i + 1]
        mix_ln = jnp.stack([mix_ln_g[i], mix_ln_b[i]])
        wr, rb = _router_params(r_w_group[i], r_b_group[i], r_w_expert[i], r_b_expert[i])
        if i % 2 == 0:
            weights = [a_w_in[j].astype(BF16), a_b_in[j].reshape(1, 2 * D),
                       a_w_dw[j].reshape(CONV_A, DT, LANES), a_b_dw[j].reshape(DT, LANES),
                       jnp.stack([a_ln_g[j], a_ln_b[j], a_b_out[j]]),
                       a_w_out[j].astype(BF16)]
            kind = "a"
        else:
            weights = [b_w_in[j].astype(BF16), b_w_dw[j], b_w_out[j].astype(BF16)]
            kind = "b"
        x1, h2, ri, wc, counts = _mixer(kind, xin, prev, mods_l, mix_ln, weights, wr, rb, tri,
                                        name=f"mixer_{kind}{i}")
        pad_starts, n_blocks = _plan(counts[:, 0])
        dest = _dest_slots(pad_starts, ri)[0:TOP_K]
        xs = _dispatch_rows(h2.reshape(T, DP), dest)
        yb = _experts(i, pad_starts, n_blocks, xs, e_w_gate, e_w_up, e_w_down, name=f"experts{i}")
        yk = _return_rows(yb, dest).reshape(TOP_K, T, DP)
        prev = (yk, wc, mods_l, jnp.stack([ffn_ln_g[i], ffn_ln_b[i]]))
        xin = x1
    yk, wc, mods_l, ln = prev
    return _final(xin, yk, wc, mods_l, ln)
```

```python
import functools

import jax
import jax.numpy as jnp
from jax import lax
from jax.experimental import pallas as pl
from jax.experimental.pallas import tpu as pltpu
from jax.experimental.pallas import tpu_sc as plsc

F32 = jnp.float32
BF16 = jnp.bfloat16
I32 = jnp.int32
U32 = jnp.uint32

D = 1024
BATCH = 4
SEQ = 8192
T = BATCH * SEQ
DEPTH = 4
N_GROUPS = 4
EPG = 8
NE = N_GROUPS * EPG
TOP_K = 2
F = D // 2
CONV_A = 31
CONV_B = 3
ALPHA = (2.0 * DEPTH) ** 0.25
LN_EPS = 1e-5

LANES = 128
SUBLANES = 8
VMEM_LIMIT = 56 * 1024 * 1024

TS = 512
SR = 256
NS = SEQ // TS
N_HALVES = 2
HALF_BATCH = BATCH // N_HALVES
HALF_T = HALF_BATCH * SEQ
HALO_A = 32
HALO_B = 8
BM = 512
ROW_DMA_PRIORITY = 1
NSLOT = T * TOP_K + NE * BM
NR = 48
ADA_TN = 1536
NEG = -1e30
DP = D // 2
DT = D // LANES
assert DT == SUBLANES

SC_CORES = 2
SC_SUBCORES = 16
SC_WORKERS = SC_CORES * SC_SUBCORES
SC_CHUNK = 64


def _sigmoid(x):
    return 1.0 / (1.0 + jnp.exp(-x))


def _pack_rows(x):
    return pltpu.pack_elementwise([x[:, :DP], x[:, DP:]], packed_dtype=BF16)


def _unpack_rows(p):
    lo = pltpu.unpack_elementwise(p, index=0, packed_dtype=BF16, unpacked_dtype=F32)
    hi = pltpu.unpack_elementwise(p, index=1, packed_dtype=BF16, unpacked_dtype=F32)
    return jnp.concatenate([lo, hi], axis=1)


def _layer_norm(x, g, b):
    mu = jnp.mean(x, axis=-1, keepdims=True)
    xc = x - mu
    var = jnp.mean(xc * xc, axis=-1, keepdims=True)
    return xc * lax.rsqrt(var + LN_EPS) * g + b


def _ada_kernel(c_ref, w_ref, b_ref, o_ref):
    c = c_ref[...]
    ca = (c * _sigmoid(c)).astype(BF16)
    w = w_ref[0].astype(BF16)
    o_ref[0] = jnp.dot(ca, w, preferred_element_type=F32) + b_ref[0]


def _ada_mods(c, ada_w, ada_b):
    out = pl.pallas_call(
        _ada_kernel,
        grid=(DEPTH, 6 * D // ADA_TN),
        in_specs=[
            pl.BlockSpec((BATCH, D), lambda i, j: (0, 0)),
            pl.BlockSpec((1, D, ADA_TN), lambda i, j: (i, 0, j)),
            pl.BlockSpec((1, 1, ADA_TN), lambda i, j: (i, 0, j)),
        ],
        out_specs=pl.BlockSpec((1, BATCH, ADA_TN), lambda i, j: (i, 0, j)),
        out_shape=jax.ShapeDtypeStruct((DEPTH, BATCH, 6 * D), F32),
        compiler_params=pltpu.CompilerParams(
            dimension_semantics=("arbitrary", "arbitrary"), vmem_limit_bytes=VMEM_LIMIT),
        name="ada_mods",
    )(c, ada_w, ada_b.reshape(DEPTH, 1, 6 * D))
    return out.reshape(DEPTH, BATCH, 6, D)


def _combine(x1, y0p, y1p, wc, g_f, ln_g, ln_b):
    w0 = jnp.tile(wc[:, :LANES], (1, D // LANES))
    w1 = jnp.tile(wc[:, LANES:], (1, D // LANES))
    y = w0 * _unpack_rows(y0p) + w1 * _unpack_rows(y1p)
    return _layer_norm(ALPHA * x1 + (1.0 + g_f) * y, ln_g, ln_b)


def _route(h2, wr_ref, rb_ref, tri_ref, cnt_ref, ri_ref, wc_ref, cnto_ref):
    lt = lax.dot_general(wr_ref[...], h2, (((1,), (1,)), ((), ())),
                         preferred_element_type=F32) + rb_ref[...]
    iota8 = lax.broadcasted_iota(I32, (SUBLANES, TS), 0).astype(F32)
    gl = lt[0:SUBLANES]
    gmax = jnp.max(gl, axis=0, keepdims=True)
    gidx = jnp.min(jnp.where(gl == gmax, iota8, float(SUBLANES)), axis=0, keepdims=True)
    gw = 1.0 / jnp.sum(jnp.exp(gl - gmax), axis=0, keepdims=True)
    el = lt[SUBLANES:2 * SUBLANES]
    for g in range(1, N_GROUPS):
        el = jnp.where(gidx == float(g), lt[SUBLANES * (g + 1):SUBLANES * (g + 2)], el)
    m1 = jnp.max(el, axis=0, keepdims=True)
    i1 = jnp.min(jnp.where(el == m1, iota8, float(SUBLANES)), axis=0, keepdims=True)
    el2 = jnp.where(iota8 == i1, -jnp.inf, el)
    m2 = jnp.max(el2, axis=0, keepdims=True)
    i2 = jnp.min(jnp.where(el2 == m2, iota8, float(SUBLANES)), axis=0, keepdims=True)
    r = jnp.exp(m2 - m1)
    w_a = gw / (1.0 + r)
    w_b = gw * r / (1.0 + r)
    e1 = gidx * float(EPG) + i1
    e2 = gidx * float(EPG) + i2

    iota_e = lax.broadcasted_iota(I32, (NE, TS), 0).astype(F32)
    oh1 = iota_e == e1
    oh2 = iota_e == e2
    oh = jnp.concatenate([jnp.where(oh1, 1.0, 0.0), jnp.where(oh2, 1.0, 0.0)], axis=0)
    before = jnp.dot(oh.astype(BF16), tri_ref[...], preferred_element_type=F32)
    tot = jnp.sum(oh, axis=1, keepdims=True)
    cnt = cnt_ref[...]
    base = jnp.tile(cnt, (1, TS // LANES))
    tot1 = tot[:NE]
    tot2 = tot[NE:]
    rank1 = jnp.sum(jnp.where(oh1, base + before[:NE], 0.0), axis=0, keepdims=True)
    rank2 = jnp.sum(jnp.where(oh2, base + tot1 + before[NE:], 0.0), axis=0, keepdims=True)
    new_cnt = cnt + tot1 + tot2
    cnt_ref[...] = new_cnt
    cnto_ref[...] = new_cnt.astype(I32)

    ri_ref[0:1, :] = e1.astype(I32)
    ri_ref[1:2, :] = e2.astype(I32)
    ri_ref[2:3, :] = rank1.astype(I32)
    ri_ref[3:4, :] = rank2.astype(I32)
    ri_ref[4:8, :] = jnp.zeros((4, TS), I32)
    wc_ref[0, :, :LANES] = jnp.broadcast_to(w_a, (LANES, TS)).T
    wc_ref[0, :, LANES:] = jnp.broadcast_to(w_b, (LANES, TS)).T


def _conv_taps(uext_ref, u, w_dw, halo, width, r0):
    uext_ref[halo + r0:halo + r0 + SR, :] = u
    acc = None
    for k in range(width):
        off = r0 + halo - (width - 1) + k
        term = w_dw[k:k + 1, :] * uext_ref[off:off + SR, :]
        acc = term if acc is None else acc + term
    return acc


def _conv_time_major(tm_ref, o2_ref, u, wk_ref, bias, r0, width, halo):
    for j in range(DT):
        tm_ref[pl.ds((halo + r0) * DT + j, SR, stride=DT), :] = u[:, j * LANES:(j + 1) * LANES]
    acc = None
    for k in range(width):
        off = (r0 + halo - (width - 1) + k) * DT
        term = tm_ref[off:off + SR * DT, :].reshape(SR, DT, LANES) * wk_ref[k]
        acc = term if acc is None else acc + term
    if bias is not None:
        acc = acc + bias
    o2_ref[r0 * DT:(r0 + SR) * DT, :] = acc.reshape(SR * DT, LANES)
    return jnp.concatenate([o2_ref[pl.ds(r0 * DT + j, SR, stride=DT), :] for j in range(DT)], axis=1)


def _mixer_kernel(*refs, kind, has_prev, carry):
    it = iter(refs)
    xin_ref = next(it)
    if has_prev:
        y0_ref, y1_ref, wcin_ref, pmods_ref, pln_ref = (next(it) for _ in range(5))
    if carry:
        for _ in range(4):
            next(it)
        cnt_in_ref = next(it)
    mods_ref, mln_ref, win_ref = next(it), next(it), next(it)
    if kind == "a":
        bin_ref, wdw_ref, bdw_ref, vec_ref = next(it), next(it), next(it), next(it)
    else:
        wdw_ref = next(it)
    wout_ref, wr_ref, rb_ref, tri_ref = (next(it) for _ in range(4))
    x1_ref, h2_ref, ri_ref, wc_ref, cnto_ref = (next(it) for _ in range(5))
    if kind == "a":
        uext_ref, o2_ref, h2b_ref, cnt_ref = (next(it) for _ in range(4))
    else:
        uext_ref, h2b_ref, cnt_ref = (next(it) for _ in range(3))
    win_s, wout_s = next(it), next(it)

    first = (pl.program_id(0) == 0) & (pl.program_id(1) == 0)

    @pl.when(first)
    def _():
        if carry:
            cnt_ref[...] = cnt_in_ref[...].astype(F32)
        else:
            cnt_ref[...] = jnp.zeros((NE, LANES), F32)
        win_s[...] = win_ref[...]
        wout_s[...] = wout_ref[...]

    halo_rows = HALO_A * DT if kind == "a" else HALO_B

    @pl.when(pl.program_id(1) == 0)
    def _():
        uext_ref[0:halo_rows, :] = jnp.zeros((halo_rows, uext_ref.shape[1]), F32)

    m = mods_ref[0, 0]
    mln = mln_ref[...]
    for i in range(TS // SR):
        r0 = i * SR
        rows = slice(r0, r0 + SR)
        x = xin_ref[0, rows, :]
        if has_prev:
            pln = pln_ref[...]
            x = _combine(x, y0_ref[0, rows, :], y1_ref[0, rows, :], wcin_ref[0, rows, :],
                         pmods_ref[0, 0][5:6, :], pln[0:1, :], pln[1:2, :])

        h = (x * (1.0 + m[1:2, :]) + m[0:1, :]).astype(BF16)
        cols = [jnp.dot(h, win_s[:, c * D:(c + 1) * D], preferred_element_type=F32)
                for c in range(win_s.shape[1] // D)]
        if kind == "a":
            b_in = bin_ref[...]
            vec = vec_ref[...]
            u = (cols[0] + b_in[:, :D]) * _sigmoid(cols[1] + b_in[:, D:])
            u = _conv_time_major(uext_ref, o2_ref, u, wdw_ref, bdw_ref[...], r0, CONV_A, HALO_A)
            u = _layer_norm(u, vec[0:1, :], vec[1:2, :])
            u = u * _sigmoid(u)
            y = jnp.dot(u.astype(BF16), wout_s[...], preferred_element_type=F32) + vec[2:3, :]
        else:
            gb = cols[0]
            q = cols[1] * cols[2]
            u = _conv_taps(uext_ref, q, wdw_ref[...], HALO_B, CONV_B, r0)
            y = jnp.dot((gb * u).astype(BF16), wout_s[...], preferred_element_type=F32)

        x1 = _layer_norm(ALPHA * x + (1.0 + m[2:3, :]) * y, mln[0:1, :], mln[1:2, :])
        x1_ref[0, rows, :] = x1
        h2 = x1 * (1.0 + m[4:5, :]) + m[3:4, :]
        h2_ref[0, rows, :] = _pack_rows(h2)
        h2b_ref[rows, :] = h2.astype(BF16)

    uext_ref[0:halo_rows, :] = uext_ref[uext_ref.shape[0] - halo_rows:uext_ref.shape[0], :]
    _route(h2b_ref[...], wr_ref, rb_ref, tri_ref, cnt_ref, ri_ref, wc_ref, cnto_ref)


def _tile_spec(width, b0=0):
    return pl.BlockSpec((1, TS, width), lambda b, s: (b + b0, s, 0))


def _yk_spec(k):
    return pl.BlockSpec((1, TS, DP), lambda b, s: (k, b * NS + s, 0))


def _const_spec(shape):
    nd = len(shape)
    return pl.BlockSpec(shape, lambda b, s: (0,) * nd)


def _mods_spec(b0=0):
    return pl.BlockSpec((1, 1, 6, D), lambda b, s: (0, b + b0, 0, 0))


def _any_spec():
    return pl.BlockSpec(memory_space=pl.ANY)


def _mixer_call(kind, xin, prev, mods_l, mix_ln, weights, wr, rb, tri, name, b0, n_batch, carried):
    has_prev = prev is not None
    carry = carried is not None
    args = [xin]
    specs = [_tile_spec(D, b0)]
    if has_prev:
        yk, wcin, pmods, pln = prev
        args += [yk, yk, wcin, pmods, pln]
        specs += [_yk_spec(0), _yk_spec(1), _tile_spec(2 * LANES, b0), _mods_spec(b0), _const_spec((2, D))]
    aliases = {}
    if carry:
        for out_idx, arr in enumerate(carried):
            aliases[len(args)] = out_idx
            args.append(arr)
            specs.append(_any_spec() if out_idx < 4 else _const_spec((NE, LANES)))
    args += [mods_l, mix_ln]
    specs += [_mods_spec(b0), _const_spec((2, D))]
    for w in weights:
        args.append(w)
        specs.append(_const_spec(w.shape))
    args += [wr, rb, tri]
    specs += [_const_spec(wr.shape), _const_spec(rb.shape), _const_spec(tri.shape)]
    if kind == "a":
        conv_scratch = [pltpu.VMEM(((TS + HALO_A) * DT, LANES), F32), pltpu.VMEM((TS * DT, LANES), F32)]
    else:
        conv_scratch = [pltpu.VMEM((TS + HALO_B, D), F32)]
    out_shape = (
        jax.ShapeDtypeStruct((BATCH, SEQ, D), F32),
        jax.ShapeDtypeStruct((BATCH, SEQ, DP), U32),
        jax.ShapeDtypeStruct((SUBLANES, T), I32),
        jax.ShapeDtypeStruct((BATCH, SEQ, 2 * LANES), F32),
        jax.ShapeDtypeStruct((NE, LANES), I32),
    )
    out_specs = (
        _tile_spec(D, b0), _tile_spec(DP, b0),
        pl.BlockSpec((SUBLANES, TS), lambda b, s: (0, (b + b0) * NS + s)),
        _tile_spec(2 * LANES, b0),
        pl.BlockSpec((NE, LANES), lambda b, s: (0, 0)),
    )
    return pl.pallas_call(
        functools.partial(_mixer_kernel, kind=kind, has_prev=has_prev, carry=carry),
        grid=(n_batch, NS),
        in_specs=specs,
        out_specs=out_specs,
        out_shape=out_shape,
        input_output_aliases=aliases,
        scratch_shapes=conv_scratch + [pltpu.VMEM((TS, D), BF16), pltpu.VMEM((NE, LANES), F32),
                                       pltpu.VMEM(weights[0].shape, BF16), pltpu.VMEM(weights[-1].shape, BF16)],
        compiler_params=pltpu.CompilerParams(
            dimension_semantics=("arbitrary", "arbitrary"), vmem_limit_bytes=VMEM_LIMIT),
        name=name,
    )(*args)


def _mixer(kind, xin, prev, mods_l, mix_ln, weights, wr, rb, tri, name):
    if prev is None:
        return _mixer_call(kind, xin, None, mods_l, mix_ln, weights, wr, rb, tri, name, 0, BATCH, None)
    yk_halves, wcin, pmods, pln = prev
    out = None
    for hh in range(N_HALVES):
        out = _mixer_call(kind, xin, (yk_halves[hh], wcin, pmods, pln), mods_l, mix_ln, weights, wr, rb, tri,
                          f"{name}_h{hh}", hh * HALF_BATCH, HALF_BATCH, out)
    return out


def _dest_kernel(ps_ref, ri_ref, o_ref):
    ri = ri_ref[...]
    e = ri[0:2, :]
    start = jnp.zeros_like(e)
    for k in range(NE):
        start = jnp.where(e == k, ps_ref[k], start)
    o_ref[0:2, :] = start + ri[2:4, :]
    o_ref[2:8, :] = jnp.zeros((6, ri.shape[1]), I32)


def _dest_slots(pad_starts, ri):
    tn = 4096
    return pl.pallas_call(
        _dest_kernel,
        grid_spec=pltpu.PrefetchScalarGridSpec(
            num_scalar_prefetch=1,
            grid=(T // tn,),
            in_specs=[pl.BlockSpec((SUBLANES, tn), lambda i, ps: (0, i))],
            out_specs=pl.BlockSpec((SUBLANES, tn), lambda i, ps: (0, i)),
        ),
        out_shape=jax.ShapeDtypeStruct((SUBLANES, T), I32),
        compiler_params=pltpu.CompilerParams(dimension_semantics=("arbitrary",)),
        name="dest_slots",
    )(pad_starts, ri)


def _expert_kernel(st_ref, nb_ref, xs_hbm, wg_ref, wu_ref, wd_ref, yb_hbm, wgu_s, wd_s, xbuf, obuf, xsem, osem):
    e = pl.program_id(0)
    nb = nb_ref[e]
    start = st_ref[e]

    def x_copy(first_row, i, slot):
        return pltpu.make_async_copy(xs_hbm.at[pl.ds(pl.multiple_of(first_row + i * BM, BM), BM)],
                                     xbuf.at[slot], xsem.at[slot])

    def o_copy(i, slot):
        return pltpu.make_async_copy(obuf.at[slot],
                                     yb_hbm.at[pl.ds(pl.multiple_of(start + i * BM, BM), BM)], osem.at[slot])

    @pl.when((nb > 0) & (e == 0))
    def _():
        x_copy(start, 0, 0).start(priority=ROW_DMA_PRIORITY)

    @pl.when(nb > 0)
    def _():
        wgu_s[:, :F] = wg_ref[0, 0].astype(BF16)
        wgu_s[:, F:] = wu_ref[0, 0].astype(BF16)
        wd_s[...] = wd_ref[0, 0].astype(BF16)

    @pl.loop(0, nb, step=2)
    def _(i0):
        for slot in range(2):
            i = i0 + slot

            @pl.when(i < nb)
            def _():
                x_copy(start, i, slot).wait()

                @pl.when(i + 1 < nb)
                def _():
                    x_copy(start, i + 1, 1 - slot).start(priority=ROW_DMA_PRIORITY)

                @pl.when(i >= 2)
                def _():
                    o_copy(i - 2, slot).wait()

                x = _unpack_rows(xbuf[slot]).astype(BF16)
                gu = jnp.dot(x, wgu_s[...], preferred_element_type=F32)
                g = gu[:, :F]
                hid = (g * _sigmoid(g) * gu[:, F:]).astype(BF16)
                obuf[slot] = _pack_rows(jnp.dot(hid, wd_s[...], preferred_element_type=F32))
                o_copy(i, slot).start(priority=ROW_DMA_PRIORITY)

    nxt = jnp.minimum(e + 1, NE - 1)

    @pl.when((e + 1 < NE) & (nb_ref[nxt] > 0))
    def _():
        x_copy(st_ref[nxt], 0, 0).start(priority=ROW_DMA_PRIORITY)

    for back in (1, 2):
        @pl.when(nb >= back)
        def _():
            last = nb - back
            o_copy(last, last % 2).wait()


def _experts(layer, starts, n_blocks, xs, w_gate, w_up, w_down, name):
    def w_map(e, st, nb):
        return (layer, e, 0, 0)

    return pl.pallas_call(
        _expert_kernel,
        grid_spec=pltpu.PrefetchScalarGridSpec(
            num_scalar_prefetch=2,
            grid=(NE,),
            in_specs=[
                pl.BlockSpec(memory_space=pl.ANY),
                pl.BlockSpec((1, 1, D, F), w_map),
                pl.BlockSpec((1, 1, D, F), w_map),
                pl.BlockSpec((1, 1, F, D), w_map),
            ],
            out_specs=pl.BlockSpec(memory_space=pl.ANY),
            scratch_shapes=[
                pltpu.VMEM((D, 2 * F), BF16), pltpu.VMEM((F, D), BF16),
                pltpu.VMEM((2, BM, DP), U32), pltpu.VMEM((2, BM, DP), U32),
                pltpu.SemaphoreType.DMA((2,)), pltpu.SemaphoreType.DMA((2,)),
            ],
        ),
        out_shape=jax.ShapeDtypeStruct((NSLOT, DP), U32),
        compiler_params=pltpu.CompilerParams(
            dimension_semantics=("arbitrary",), vmem_limit_bytes=VMEM_LIMIT),
        name=name,
    )(starts, n_blocks, xs, w_gate, w_up, w_down)


def _sc_worker_id():
    return lax.axis_index("s") * SC_CORES + lax.axis_index("c")


def _sc_mesh():
    return plsc.VectorSubcoreMesh(core_axis_name="c", subcore_axis_name="s")


def _sc_scratch(n_index_rows):
    return [
        pltpu.VMEM((n_index_rows, SC_CHUNK), I32),
        pltpu.VMEM((2, SC_CHUNK, DP), U32),
        pltpu.SemaphoreType.DMA((2,)),
        pltpu.SemaphoreType.DMA((2,)),
    ]


def _dispatch_rows(h2p, dest):
    per_w = T // SC_WORKERS
    n_chunks = per_w // SC_CHUNK

    @functools.partial(
        pl.kernel, mesh=_sc_mesh(),
        out_type=jax.ShapeDtypeStruct((NSLOT, DP), U32),
        scratch_types=_sc_scratch(TOP_K * n_chunks),
        name="dispatch_rows",
    )
    def k(h2_hbm, dest_hbm, out_hbm, dest_v, rows_v, rsem, wsem):
        wid = _sc_worker_id()
        for kk in range(TOP_K):
            pltpu.sync_copy(dest_hbm.at[kk, pl.ds(wid * n_chunks, n_chunks)],
                            dest_v.at[pl.ds(kk * n_chunks, n_chunks)])
        base = wid * per_w

        def read(c, slot):
            return pltpu.make_async_copy(h2_hbm.at[pl.ds(base + c * SC_CHUNK, SC_CHUNK)],
                                         rows_v.at[slot], rsem.at[slot])

        def write(c, kk, slot):
            return pltpu.make_async_copy(rows_v.at[slot], out_hbm.at[dest_v.at[kk * n_chunks + c]],
                                         wsem.at[slot])

        read(0, 0).start()

        @pl.loop(0, n_chunks, step=2)
        def _(c):
            for b in range(2):
                cc = c + b
                read(cc, b).wait()

                @pl.when(cc + 1 < n_chunks)
                def _():
                    @pl.when(cc >= 1)
                    def _():
                        for kk in range(TOP_K):
                            write(cc - 1, kk, 1 - b).wait()
                    read(cc + 1, 1 - b).start()

                for kk in range(TOP_K):
                    write(cc, kk, b).start()

        for slot, cc in ((0, n_chunks - 2), (1, n_chunks - 1)):
            for kk in range(TOP_K):
                write(cc, kk, slot).wait()

    return k(h2p, dest.reshape(TOP_K, T // SC_CHUNK, SC_CHUNK))


def _return_rows(yb, dest):
    m = dest.size
    per_w = m // SC_WORKERS
    n_chunks = per_w // SC_CHUNK

    @functools.partial(
        pl.kernel, mesh=_sc_mesh(),
        out_type=jax.ShapeDtypeStruct((m, DP), U32),
        scratch_types=_sc_scratch(n_chunks),
        name="return_rows",
    )
    def k(yb_hbm, dest_hbm, out_hbm, idx_v, rows_v, gsem, wsem):
        wid = _sc_worker_id()
        pltpu.sync_copy(dest_hbm.at[pl.ds(wid * n_chunks, n_chunks)], idx_v)
        base = wid * per_w

        def gather(c, slot):
            return pltpu.make_async_copy(yb_hbm.at[idx_v.at[c]], rows_v.at[slot], gsem.at[slot])

        def write(c, slot):
            return pltpu.make_async_copy(rows_v.at[slot], out_hbm.at[pl.ds(base + c * SC_CHUNK, SC_CHUNK)],
                                         wsem.at[slot])

        gather(0, 0).start()

        @pl.loop(0, n_chunks, step=2)
        def _(c):
            for b in range(2):
                cc = c + b
                gather(cc, b).wait()

                @pl.when(cc + 1 < n_chunks)
                def _():
                    @pl.when(cc >= 1)
                    def _():
                        write(cc - 1, 1 - b).wait()
                    gather(cc + 1, 1 - b).start()

                write(cc, b).start()

        write(n_chunks - 2, 0).wait()
        write(n_chunks - 1, 1).wait()

    return k(yb, dest.reshape(m // SC_CHUNK, SC_CHUNK))


def _final_kernel(x1_ref, y0_ref, y1_ref, wc_ref, mods_ref, ln_ref, *rest):
    o_ref = rest[-1]
    ln = ln_ref[...]
    o_ref[0] = _combine(x1_ref[0], y0_ref[0], y1_ref[0], wc_ref[0], mods_ref[0, 0][5:6, :],
                        ln[0:1, :], ln[1:2, :])


def _final(x1, yk_halves, wc, mods_l, ln):
    out = None
    for hh in range(N_HALVES):
        b0 = hh * HALF_BATCH
        args = [x1, yk_halves[hh], yk_halves[hh], wc, mods_l, ln]
        specs = [_tile_spec(D, b0), _yk_spec(0), _yk_spec(1), _tile_spec(2 * LANES, b0),
                 _mods_spec(b0), _const_spec((2, D))]
        aliases = {}
        if out is not None:
            aliases[len(args)] = 0
            args.append(out)
            specs.append(_any_spec())
        out = pl.pallas_call(
            _final_kernel,
            grid=(HALF_BATCH, NS),
            in_specs=specs,
            out_specs=_tile_spec(D, b0),
            out_shape=jax.ShapeDtypeStruct((BATCH, SEQ, D), F32),
            input_output_aliases=aliases,
            compiler_params=pltpu.CompilerParams(
                dimension_semantics=("arbitrary", "arbitrary"), vmem_limit_bytes=VMEM_LIMIT),
            name=f"final_combine_h{hh}",
        )(*args)
    return out


def _plan(counts):
    n_blocks = (counts + BM - 1) // BM
    padded = n_blocks * BM
    pad_starts = (jnp.cumsum(padded) - padded).astype(I32)
    return pad_starts, n_blocks.astype(I32)


def _router_params(w_group, b_group, w_expert, b_expert):
    wr = jnp.zeros((NR, D), F32)
    wr = wr.at[0:N_GROUPS].set(w_group.T).at[SUBLANES:SUBLANES + NE].set(w_expert.T)
    rb = jnp.full((NR,), NEG, F32)
    rb = rb.at[0:N_GROUPS].set(b_group).at[SUBLANES:SUBLANES + NE].set(b_expert)
    rb = rb.at[SUBLANES + NE:].set(0.0)
    return wr.astype(BF16), rb.reshape(NR, 1)


def kernel(x, c, ada_w, ada_b, a_w_in, a_b_in, a_w_dw, a_b_dw, a_ln_g, a_ln_b, a_w_out, a_b_out,
           b_w_in, b_w_dw, b_w_out, mix_ln_g, mix_ln_b, ffn_ln_g, ffn_ln_b,
           r_w_group, r_b_group, r_w_expert, r_b_expert, e_w_gate, e_w_up, e_w_down):
    mods = _ada_mods(c, ada_w, ada_b)
    tri = (jnp.arange(TS)[:, None] < jnp.arange(TS)[None, :]).astype(BF16)
    prev = None
    xin = x
    for i in range(DEPTH):
        j = i // 2
        mods_l = mods[i:i + 1]
        mix_ln = jnp.stack([mix_ln_g[i], mix_ln_b[i]])
        wr, rb = _router_params(r_w_group[i], r_b_group[i], r_w_expert[i], r_b_expert[i])
        if i % 2 == 0:
            weights = [a_w_in[j].astype(BF16), a_b_in[j].reshape(1, 2 * D),
                       a_w_dw[j].reshape(CONV_A, DT, LANES), a_b_dw[j].reshape(DT, LANES),
                       jnp.stack([a_ln_g[j], a_ln_b[j], a_b_out[j]]),
                       a_w_out[j].astype(BF16)]
            kind = "a"
        else:
            weights = [b_w_in[j].astype(BF16), b_w_dw[j], b_w_out[j].astype(BF16)]
            kind = "b"
        x1, h2, ri, wc, counts = _mixer(kind, xin, prev, mods_l, mix_ln, weights, wr, rb, tri,
                                        name=f"mixer_{kind}{i}")
        pad_starts, n_blocks = _plan(counts[:, 0])
        dest = _dest_slots(pad_starts, ri)[0:TOP_K]
        xs = _dispatch_rows(h2.reshape(T, DP), dest)
        yb = _experts(i, pad_starts, n_blocks, xs, e_w_gate, e_w_up, e_w_down, name=f"experts{i}")
        yk_halves = [_return_rows(yb, dest[:, hh * HALF_T:(hh + 1) * HALF_T]).reshape(TOP_K, HALF_T, DP)
                     for hh in range(N_HALVES)]
        prev = (yk_halves, wc, mods_l, jnp.stack([ffn_ln_g[i], ffn_ln_b[i]]))
        xin = x1
    yk_halves, wc, mods_l, ln = prev
    return _final(xin, yk_halves, wc, mods_l, ln)
```

```python
import functools

import jax
import jax.numpy as jnp
from jax import lax
from jax.experimental import pallas as pl
from jax.experimental.pallas import tpu as pltpu
from jax.experimental.pallas import tpu_sc as plsc

F32 = jnp.float32
BF16 = jnp.bfloat16
I32 = jnp.int32
U32 = jnp.uint32

D = 1024
BATCH = 4
SEQ = 8192
T = BATCH * SEQ
DEPTH = 4
N_GROUPS = 4
EPG = 8
NE = N_GROUPS * EPG
TOP_K = 2
F = D // 2
CONV_A = 31
CONV_B = 3
ALPHA = (2.0 * DEPTH) ** 0.25
LN_EPS = 1e-5

LANES = 128
SUBLANES = 8
VMEM_LIMIT = 56 * 1024 * 1024

TS = 512
SR = 256
NS = SEQ // TS
N_HALVES = 2
HALF_BATCH = BATCH // N_HALVES
HALF_T = HALF_BATCH * SEQ
HALO_A = 32
HALO_B = 8
BM = 512
ROW_DMA_PRIORITY = 1
NSLOT = T * TOP_K + NE * BM
NR = 48
ADA_TN = 1536
NEG = -1e30
DP = D // 2
DT = D // LANES
assert DT == SUBLANES

SC_CORES = 2
SC_SUBCORES = 16
SC_WORKERS = SC_CORES * SC_SUBCORES
SC_CHUNK = 64


def _sigmoid(x):
    return 1.0 / (1.0 + jnp.exp(-x))


def _pack_rows(x):
    return pltpu.pack_elementwise([x[:, :DP], x[:, DP:]], packed_dtype=BF16)


def _unpack_rows(p):
    lo = pltpu.unpack_elementwise(p, index=0, packed_dtype=BF16, unpacked_dtype=F32)
    hi = pltpu.unpack_elementwise(p, index=1, packed_dtype=BF16, unpacked_dtype=F32)
    return jnp.concatenate([lo, hi], axis=1)


def _layer_norm(x, g, b):
    mu = jnp.mean(x, axis=-1, keepdims=True)
    xc = x - mu
    var = jnp.mean(xc * xc, axis=-1, keepdims=True)
    return xc * lax.rsqrt(var + LN_EPS) * g + b


def _ada_kernel(c_ref, w_ref, b_ref, o_ref):
    c = c_ref[...]
    ca = (c * _sigmoid(c)).astype(BF16)
    w = w_ref[0].astype(BF16)
    o_ref[0] = jnp.dot(ca, w, preferred_element_type=F32) + b_ref[0]


def _ada_mods(c, ada_w, ada_b):
    out = pl.pallas_call(
        _ada_kernel,
        grid=(DEPTH, 6 * D // ADA_TN),
        in_specs=[
            pl.BlockSpec((BATCH, D), lambda i, j: (0, 0)),
            pl.BlockSpec((1, D, ADA_TN), lambda i, j: (i, 0, j)),
            pl.BlockSpec((1, 1, ADA_TN), lambda i, j: (i, 0, j)),
        ],
        out_specs=pl.BlockSpec((1, BATCH, ADA_TN), lambda i, j: (i, 0, j)),
        out_shape=jax.ShapeDtypeStruct((DEPTH, BATCH, 6 * D), F32),
        compiler_params=pltpu.CompilerParams(
            dimension_semantics=("arbitrary", "arbitrary"), vmem_limit_bytes=VMEM_LIMIT),
        name="ada_mods",
    )(c, ada_w, ada_b.reshape(DEPTH, 1, 6 * D))
    return out.reshape(DEPTH, BATCH, 6, D)


def _combine(x1, y0p, y1p, wc, g_f, ln_g, ln_b):
    w0 = jnp.tile(wc[:, :LANES], (1, D // LANES))
    w1 = jnp.tile(wc[:, LANES:], (1, D // LANES))
    y = w0 * _unpack_rows(y0p) + w1 * _unpack_rows(y1p)
    return _layer_norm(ALPHA * x1 + (1.0 + g_f) * y, ln_g, ln_b)


def _route(h2, wr_ref, rb_ref, tri_ref, cnt_ref, ri_ref, wc_ref, cnto_ref):
    lt = lax.dot_general(wr_ref[...], h2, (((1,), (1,)), ((), ())),
                         preferred_element_type=F32) + rb_ref[...]
    iota8 = lax.broadcasted_iota(I32, (SUBLANES, TS), 0).astype(F32)
    gl = lt[0:SUBLANES]
    gmax = jnp.max(gl, axis=0, keepdims=True)
    gidx = jnp.min(jnp.where(gl == gmax, iota8, float(SUBLANES)), axis=0, keepdims=True)
    gw = 1.0 / jnp.sum(jnp.exp(gl - gmax), axis=0, keepdims=True)
    el = lt[SUBLANES:2 * SUBLANES]
    for g in range(1, N_GROUPS):
        el = jnp.where(gidx == float(g), lt[SUBLANES * (g + 1):SUBLANES * (g + 2)], el)
    m1 = jnp.max(el, axis=0, keepdims=True)
    i1 = jnp.min(jnp.where(el == m1, iota8, float(SUBLANES)), axis=0, keepdims=True)
    el2 = jnp.where(iota8 == i1, -jnp.inf, el)
    m2 = jnp.max(el2, axis=0, keepdims=True)
    i2 = jnp.min(jnp.where(el2 == m2, iota8, float(SUBLANES)), axis=0, keepdims=True)
    r = jnp.exp(m2 - m1)
    w_a = gw / (1.0 + r)
    w_b = gw * r / (1.0 + r)
    e1 = gidx * float(EPG) + i1
    e2 = gidx * float(EPG) + i2

    iota_e = lax.broadcasted_iota(I32, (NE, TS), 0).astype(F32)
    oh1 = iota_e == e1
    oh2 = iota_e == e2
    oh = jnp.concatenate([jnp.where(oh1, 1.0, 0.0), jnp.where(oh2, 1.0, 0.0)], axis=0)
    before = jnp.dot(oh.astype(BF16), tri_ref[...], preferred_element_type=F32)
    tot = jnp.sum(oh, axis=1, keepdims=True)
    cnt = cnt_ref[...]
    base = jnp.tile(cnt, (1, TS // LANES))
    tot1 = tot[:NE]
    tot2 = tot[NE:]
    rank1 = jnp.sum(jnp.where(oh1, base + before[:NE], 0.0), axis=0, keepdims=True)
    rank2 = jnp.sum(jnp.where(oh2, base + tot1 + before[NE:], 0.0), axis=0, keepdims=True)
    new_cnt = cnt + tot1 + tot2
    cnt_ref[...] = new_cnt
    cnto_ref[...] = new_cnt.astype(I32)

    ri_ref[0:1, :] = e1.astype(I32)
    ri_ref[1:2, :] = e2.astype(I32)
    ri_ref[2:3, :] = rank1.astype(I32)
    ri_ref[3:4, :] = rank2.astype(I32)
    ri_ref[4:8, :] = jnp.zeros((4, TS), I32)
    wc_ref[0, :, :LANES] = jnp.broadcast_to(w_a, (LANES, TS)).T
    wc_ref[0, :, LANES:] = jnp.broadcast_to(w_b, (LANES, TS)).T


def _conv_taps(uext_ref, u, w_dw, halo, width, r0):
    uext_ref[halo + r0:halo + r0 + SR, :] = u
    acc = None
    for k in range(width):
        off = r0 + halo - (width - 1) + k
        term = w_dw[k:k + 1, :] * uext_ref[off:off + SR, :]
        acc = term if acc is None else acc + term
    return acc


def _conv_time_major(tm_ref, o2_ref, u, wk_ref, bias, r0, width, halo):
    for j in range(DT):
        tm_ref[pl.ds((halo + r0) * DT + j, SR, stride=DT), :] = u[:, j * LANES:(j + 1) * LANES]
    acc = None
    for k in range(width):
        off = (r0 + halo - (width - 1) + k) * DT
        term = tm_ref[off:off + SR * DT, :].reshape(SR, DT, LANES) * wk_ref[k]
        acc = term if acc is None else acc + term
    if bias is not None:
        acc = acc + bias
    o2_ref[r0 * DT:(r0 + SR) * DT, :] = acc.reshape(SR * DT, LANES)
    return jnp.concatenate([o2_ref[pl.ds(r0 * DT + j, SR, stride=DT), :] for j in range(DT)], axis=1)


def _mixer_kernel(*refs, kind, has_prev, carry):
    it = iter(refs)
    xin_ref = next(it)
    if has_prev:
        y0_ref, y1_ref, wcin_ref, pmods_ref, pln_ref = (next(it) for _ in range(5))
    if carry:
        for _ in range(4):
            next(it)
        cnt_in_ref = next(it)
    mods_ref, mln_ref, win_ref = next(it), next(it), next(it)
    if kind == "a":
        bin_ref, wdw_ref, bdw_ref, vec_ref = next(it), next(it), next(it), next(it)
    else:
        wdw_ref = next(it)
    wout_ref, wr_ref, rb_ref, tri_ref = (next(it) for _ in range(4))
    x1_ref, h2_ref, ri_ref, wc_ref, cnto_ref = (next(it) for _ in range(5))
    if kind == "a":
        uext_ref, o2_ref, h2b_ref, cnt_ref = (next(it) for _ in range(4))
    else:
        uext_ref, h2b_ref, cnt_ref = (next(it) for _ in range(3))
    win_s, wout_s = next(it), next(it)
    mln_ref, win_ref, wdw_ref, wout_ref, wr_ref, rb_ref = (
        r.at[0] for r in (mln_ref, win_ref, wdw_ref, wout_ref, wr_ref, rb_ref))
    if has_prev:
        pln_ref = pln_ref.at[0]
    if kind == "a":
        bin_ref, bdw_ref, vec_ref = (r.at[0] for r in (bin_ref, bdw_ref, vec_ref))

    first = (pl.program_id(0) == 0) & (pl.program_id(1) == 0)

    @pl.when(first)
    def _():
        if carry:
            cnt_ref[...] = cnt_in_ref[...].astype(F32)
        else:
            cnt_ref[...] = jnp.zeros((NE, LANES), F32)
        win_s[...] = win_ref[...]
        wout_s[...] = wout_ref[...]

    halo_rows = HALO_A * DT if kind == "a" else HALO_B

    @pl.when(pl.program_id(1) == 0)
    def _():
        uext_ref[0:halo_rows, :] = jnp.zeros((halo_rows, uext_ref.shape[1]), F32)

    m = mods_ref[0, 0]
    mln = mln_ref[...]
    for i in range(TS // SR):
        r0 = i * SR
        rows = slice(r0, r0 + SR)
        x = xin_ref[0, rows, :]
        if has_prev:
            pln = pln_ref[...]
            x = _combine(x, y0_ref[0, rows, :], y1_ref[0, rows, :], wcin_ref[0, rows, :],
                         pmods_ref[0, 0][5:6, :], pln[0:1, :], pln[1:2, :])

        h = (x * (1.0 + m[1:2, :]) + m[0:1, :]).astype(BF16)
        cols = [jnp.dot(h, win_s[:, c * D:(c + 1) * D], preferred_element_type=F32)
                for c in range(win_s.shape[1] // D)]
        if kind == "a":
            b_in = bin_ref[...]
            vec = vec_ref[...]
            u = (cols[0] + b_in[:, :D]) * _sigmoid(cols[1] + b_in[:, D:])
            u = _conv_time_major(uext_ref, o2_ref, u, wdw_ref, bdw_ref[...], r0, CONV_A, HALO_A)
            u = _layer_norm(u, vec[0:1, :], vec[1:2, :])
            u = u * _sigmoid(u)
            y = jnp.dot(u.astype(BF16), wout_s[...], preferred_element_type=F32) + vec[2:3, :]
        else:
            gb = cols[0]
            q = cols[1] * cols[2]
            u = _conv_taps(uext_ref, q, wdw_ref[...], HALO_B, CONV_B, r0)
            y = jnp.dot((gb * u).astype(BF16), wout_s[...], preferred_element_type=F32)

        x1 = _layer_norm(ALPHA * x + (1.0 + m[2:3, :]) * y, mln[0:1, :], mln[1:2, :])
        x1_ref[0, rows, :] = x1
        h2 = x1 * (1.0 + m[4:5, :]) + m[3:4, :]
        h2_ref[0, rows, :] = _pack_rows(h2)
        h2b_ref[rows, :] = h2.astype(BF16)

    uext_ref[0:halo_rows, :] = uext_ref[uext_ref.shape[0] - halo_rows:uext_ref.shape[0], :]
    _route(h2b_ref[...], wr_ref, rb_ref, tri_ref, cnt_ref, ri_ref, wc_ref, cnto_ref)


def _tile_spec(width, b0=0):
    return pl.BlockSpec((1, TS, width), lambda b, s: (b + b0, s, 0))


def _yk_spec(k):
    return pl.BlockSpec((1, TS, DP), lambda b, s: (k, b * NS + s, 0))


def _const_spec(shape):
    nd = len(shape)
    return pl.BlockSpec(shape, lambda b, s: (0,) * nd)


def _mods_spec(layer, b0=0):
    return pl.BlockSpec((1, 1, 6, D), lambda b, s: (layer, b + b0, 0, 0))


def _layer_spec(arr, idx):
    tail = arr.shape[1:]
    return pl.BlockSpec((1,) + tail, lambda b, s: (idx,) + (0,) * len(tail))


def _any_spec():
    return pl.BlockSpec(memory_space=pl.ANY)


def _mixer_call(kind, layer, xin, prev, params, name, b0, n_batch, carried):
    mods, mix_ln, ffn_ln, wr, rb, tri = (params[k] for k in ("mods", "mix_ln", "ffn_ln", "wr", "rb", "tri"))
    weights = params[kind]
    j = layer // 2
    has_prev = prev is not None
    carry = carried is not None
    args = [xin]
    specs = [_tile_spec(D, b0)]
    if has_prev:
        yk, wcin = prev
        args += [yk, yk, wcin, mods, ffn_ln]
        specs += [_yk_spec(0), _yk_spec(1), _tile_spec(2 * LANES, b0), _mods_spec(layer - 1, b0),
                  _layer_spec(ffn_ln, layer - 1)]
    aliases = {}
    if carry:
        for out_idx, arr in enumerate(carried):
            aliases[len(args)] = out_idx
            args.append(arr)
            specs.append(_any_spec() if out_idx < 4 else _const_spec((NE, LANES)))
    args += [mods, mix_ln]
    specs += [_mods_spec(layer, b0), _layer_spec(mix_ln, layer)]
    for w in weights:
        args.append(w)
        specs.append(_layer_spec(w, j))
    args += [wr, rb, tri]
    specs += [_layer_spec(wr, layer), _layer_spec(rb, layer), _const_spec(tri.shape)]
    if kind == "a":
        conv_scratch = [pltpu.VMEM(((TS + HALO_A) * DT, LANES), F32), pltpu.VMEM((TS * DT, LANES), F32)]
    else:
        conv_scratch = [pltpu.VMEM((TS + HALO_B, D), F32)]
    out_shape = (
        jax.ShapeDtypeStruct((BATCH, SEQ, D), F32),
        jax.ShapeDtypeStruct((BATCH, SEQ, DP), U32),
        jax.ShapeDtypeStruct((SUBLANES, T), I32),
        jax.ShapeDtypeStruct((BATCH, SEQ, 2 * LANES), F32),
        jax.ShapeDtypeStruct((NE, LANES), I32),
    )
    out_specs = (
        _tile_spec(D, b0), _tile_spec(DP, b0),
        pl.BlockSpec((SUBLANES, TS), lambda b, s: (0, (b + b0) * NS + s)),
        _tile_spec(2 * LANES, b0),
        pl.BlockSpec((NE, LANES), lambda b, s: (0, 0)),
    )
    return pl.pallas_call(
        functools.partial(_mixer_kernel, kind=kind, has_prev=has_prev, carry=carry),
        grid=(n_batch, NS),
        in_specs=specs,
        out_specs=out_specs,
        out_shape=out_shape,
        input_output_aliases=aliases,
        scratch_shapes=conv_scratch + [pltpu.VMEM((TS, D), BF16), pltpu.VMEM((NE, LANES), F32),
                                       pltpu.VMEM(weights[0].shape[1:], BF16),
                                       pltpu.VMEM(weights[-1].shape[1:], BF16)],
        compiler_params=pltpu.CompilerParams(
            dimension_semantics=("arbitrary", "arbitrary"), vmem_limit_bytes=VMEM_LIMIT),
        name=name,
    )(*args)


def _mixer(kind, layer, xin, prev, params):
    name = f"mixer_{kind}{layer}"
    if prev is None:
        return _mixer_call(kind, layer, xin, None, params, name, 0, BATCH, None)
    yk_halves, wcin = prev
    out = None
    for hh in range(N_HALVES):
        out = _mixer_call(kind, layer, xin, (yk_halves[hh], wcin), params, f"{name}_h{hh}",
                          hh * HALF_BATCH, HALF_BATCH, out)
    return out


def _dest_kernel(ps_ref, ri_ref, o_ref):
    ri = ri_ref[...]
    e = ri[0:2, :]
    start = jnp.zeros_like(e)
    for k in range(NE):
        start = jnp.where(e == k, ps_ref[k], start)
    o_ref[0:2, :] = start + ri[2:4, :]
    o_ref[2:8, :] = jnp.zeros((6, ri.shape[1]), I32)


def _dest_slots(pad_starts, ri):
    tn = 4096
    return pl.pallas_call(
        _dest_kernel,
        grid_spec=pltpu.PrefetchScalarGridSpec(
            num_scalar_prefetch=1,
            grid=(T // tn,),
            in_specs=[pl.BlockSpec((SUBLANES, tn), lambda i, ps: (0, i))],
            out_specs=pl.BlockSpec((SUBLANES, tn), lambda i, ps: (0, i)),
        ),
        out_shape=jax.ShapeDtypeStruct((SUBLANES, T), I32),
        compiler_params=pltpu.CompilerParams(dimension_semantics=("arbitrary",)),
        name="dest_slots",
    )(pad_starts, ri)


def _expert_kernel(st_ref, nb_ref, xs_hbm, wg_ref, wu_ref, wd_ref, yb_hbm, wgu_s, wd_s, xbuf, obuf, xsem, osem):
    e = pl.program_id(0)
    nb = nb_ref[e]
    start = st_ref[e]

    def x_copy(first_row, i, slot):
        return pltpu.make_async_copy(xs_hbm.at[pl.ds(pl.multiple_of(first_row + i * BM, BM), BM)],
                                     xbuf.at[slot], xsem.at[slot])

    def o_copy(i, slot):
        return pltpu.make_async_copy(obuf.at[slot],
                                     yb_hbm.at[pl.ds(pl.multiple_of(start + i * BM, BM), BM)], osem.at[slot])

    @pl.when((nb > 0) & (e == 0))
    def _():
        x_copy(start, 0, 0).start(priority=ROW_DMA_PRIORITY)

    @pl.when(nb > 0)
    def _():
        wgu_s[:, :F] = wg_ref[0, 0].astype(BF16)
        wgu_s[:, F:] = wu_ref[0, 0].astype(BF16)
        wd_s[...] = wd_ref[0, 0].astype(BF16)

    @pl.loop(0, nb, step=2)
    def _(i0):
        for slot in range(2):
            i = i0 + slot

            @pl.when(i < nb)
            def _():
                x_copy(start, i, slot).wait()

                @pl.when(i + 1 < nb)
                def _():
                    x_copy(start, i + 1, 1 - slot).start(priority=ROW_DMA_PRIORITY)

                @pl.when(i >= 2)
                def _():
                    o_copy(i - 2, slot).wait()

                x = _unpack_rows(xbuf[slot]).astype(BF16)
                gu = jnp.dot(x, wgu_s[...], preferred_element_type=F32)
                g = gu[:, :F]
                hid = (g * _sigmoid(g) * gu[:, F:]).astype(BF16)
                obuf[slot] = _pack_rows(jnp.dot(hid, wd_s[...], preferred_element_type=F32))
                o_copy(i, slot).start(priority=ROW_DMA_PRIORITY)

    nxt = jnp.minimum(e + 1, NE - 1)

    @pl.when((e + 1 < NE) & (nb_ref[nxt] > 0))
    def _():
        x_copy(st_ref[nxt], 0, 0).start(priority=ROW_DMA_PRIORITY)

    for back in (1, 2):
        @pl.when(nb >= back)
        def _():
            last = nb - back
            o_copy(last, last % 2).wait()


def _experts(layer, starts, n_blocks, xs, w_gate, w_up, w_down, name):
    def w_map(e, st, nb):
        return (layer, e, 0, 0)

    return pl.pallas_call(
        _expert_kernel,
        grid_spec=pltpu.PrefetchScalarGridSpec(
            num_scalar_prefetch=2,
            grid=(NE,),
            in_specs=[
                pl.BlockSpec(memory_space=pl.ANY),
                pl.BlockSpec((1, 1, D, F), w_map),
                pl.BlockSpec((1, 1, D, F), w_map),
                pl.BlockSpec((1, 1, F, D), w_map),
            ],
            out_specs=pl.BlockSpec(memory_space=pl.ANY),
            scratch_shapes=[
                pltpu.VMEM((D, 2 * F), BF16), pltpu.VMEM((F, D), BF16),
                pltpu.VMEM((2, BM, DP), U32), pltpu.VMEM((2, BM, DP), U32),
                pltpu.SemaphoreType.DMA((2,)), pltpu.SemaphoreType.DMA((2,)),
            ],
        ),
        out_shape=jax.ShapeDtypeStruct((NSLOT, DP), U32),
        compiler_params=pltpu.CompilerParams(
            dimension_semantics=("arbitrary",), vmem_limit_bytes=VMEM_LIMIT),
        name=name,
    )(starts, n_blocks, xs, w_gate, w_up, w_down)


def _sc_worker_id():
    return lax.axis_index("s") * SC_CORES + lax.axis_index("c")


def _sc_mesh():
    return plsc.VectorSubcoreMesh(core_axis_name="c", subcore_axis_name="s")


def _sc_scratch(n_index_rows):
    return [
        pltpu.VMEM((n_index_rows, SC_CHUNK), I32),
        pltpu.VMEM((2, SC_CHUNK, DP), U32),
        pltpu.SemaphoreType.DMA((2,)),
        pltpu.SemaphoreType.DMA((2,)),
    ]


def _dispatch_rows(h2p, dest):
    per_w = T // SC_WORKERS
    n_chunks = per_w // SC_CHUNK

    @functools.partial(
        pl.kernel, mesh=_sc_mesh(),
        out_type=jax.ShapeDtypeStruct((NSLOT, DP), U32),
        scratch_types=_sc_scratch(TOP_K * n_chunks),
        name="dispatch_rows",
    )
    def k(h2_hbm, dest_hbm, out_hbm, dest_v, rows_v, rsem, wsem):
        wid = _sc_worker_id()
        for kk in range(TOP_K):
            pltpu.sync_copy(dest_hbm.at[kk, pl.ds(wid * n_chunks, n_chunks)],
                            dest_v.at[pl.ds(kk * n_chunks, n_chunks)])
        base = wid * per_w

        def read(c, slot):
            return pltpu.make_async_copy(h2_hbm.at[pl.ds(base + c * SC_CHUNK, SC_CHUNK)],
                                         rows_v.at[slot], rsem.at[slot])

        def write(c, kk, slot):
            return pltpu.make_async_copy(rows_v.at[slot], out_hbm.at[dest_v.at[kk * n_chunks + c]],
                                         wsem.at[slot])

        read(0, 0).start()

        @pl.loop(0, n_chunks, step=2)
        def _(c):
            for b in range(2):
                cc = c + b
                read(cc, b).wait()

                @pl.when(cc + 1 < n_chunks)
                def _():
                    @pl.when(cc >= 1)
                    def _():
                        for kk in range(TOP_K):
                            write(cc - 1, kk, 1 - b).wait()
                    read(cc + 1, 1 - b).start()

                for kk in range(TOP_K):
                    write(cc, kk, b).start()

        for slot, cc in ((0, n_chunks - 2), (1, n_chunks - 1)):
            for kk in range(TOP_K):
                write(cc, kk, slot).wait()

    return k(h2p, dest.reshape(TOP_K, T // SC_CHUNK, SC_CHUNK))


def _return_rows(yb, dest):
    m = dest.size
    per_w = m // SC_WORKERS
    n_chunks = per_w // SC_CHUNK

    @functools.partial(
        pl.kernel, mesh=_sc_mesh(),
        out_type=jax.ShapeDtypeStruct((m, DP), U32),
        scratch_types=_sc_scratch(n_chunks),
        name="return_rows",
    )
    def k(yb_hbm, dest_hbm, out_hbm, idx_v, rows_v, gsem, wsem):
        wid = _sc_worker_id()
        pltpu.sync_copy(dest_hbm.at[pl.ds(wid * n_chunks, n_chunks)], idx_v)
        base = wid * per_w

        def gather(c, slot):
            return pltpu.make_async_copy(yb_hbm.at[idx_v.at[c]], rows_v.at[slot], gsem.at[slot])

        def write(c, slot):
            return pltpu.make_async_copy(rows_v.at[slot], out_hbm.at[pl.ds(base + c * SC_CHUNK, SC_CHUNK)],
                                         wsem.at[slot])

        gather(0, 0).start()

        @pl.loop(0, n_chunks, step=2)
        def _(c):
            for b in range(2):
                cc = c + b
                gather(cc, b).wait()

                @pl.when(cc + 1 < n_chunks)
                def _():
                    @pl.when(cc >= 1)
                    def _():
                        write(cc - 1, 1 - b).wait()
                    gather(cc + 1, 1 - b).start()

                write(cc, b).start()

        write(n_chunks - 2, 0).wait()
        write(n_chunks - 1, 1).wait()

    return k(yb, dest.reshape(m // SC_CHUNK, SC_CHUNK))


def _final_kernel(x1_ref, y0_ref, y1_ref, wc_ref, mods_ref, ln_ref, *rest):
    o_ref = rest[-1]
    ln = ln_ref[0]
    o_ref[0] = _combine(x1_ref[0], y0_ref[0], y1_ref[0], wc_ref[0], mods_ref[0, 0][5:6, :],
                        ln[0:1, :], ln[1:2, :])


def _final(x1, yk_halves, wc, params):
    mods, ffn_ln = params["mods"], params["ffn_ln"]
    out = None
    for hh in range(N_HALVES):
        b0 = hh * HALF_BATCH
        args = [x1, yk_halves[hh], yk_halves[hh], wc, mods, ffn_ln]
        specs = [_tile_spec(D, b0), _yk_spec(0), _yk_spec(1), _tile_spec(2 * LANES, b0),
                 _mods_spec(DEPTH - 1, b0), _layer_spec(ffn_ln, DEPTH - 1)]
        aliases = {}
        if out is not None:
            aliases[len(args)] = 0
            args.append(out)
            specs.append(_any_spec())
        out = pl.pallas_call(
            _final_kernel,
            grid=(HALF_BATCH, NS),
            in_specs=specs,
            out_specs=_tile_spec(D, b0),
            out_shape=jax.ShapeDtypeStruct((BATCH, SEQ, D), F32),
            input_output_aliases=aliases,
            compiler_params=pltpu.CompilerParams(
                dimension_semantics=("arbitrary", "arbitrary"), vmem_limit_bytes=VMEM_LIMIT),
            name=f"final_combine_h{hh}",
        )(*args)
    return out


def _plan(counts):
    n_blocks = (counts + BM - 1) // BM
    padded = n_blocks * BM
    pad_starts = (jnp.cumsum(padded) - padded).astype(I32)
    return pad_starts, n_blocks.astype(I32)


def _router_params(w_group, b_group, w_expert, b_expert):
    wr = jnp.zeros((DEPTH, NR, D), F32)
    wr = wr.at[:, 0:N_GROUPS].set(jnp.swapaxes(w_group, 1, 2))
    wr = wr.at[:, SUBLANES:SUBLANES + NE].set(jnp.swapaxes(w_expert, 1, 2))
    rb = jnp.full((DEPTH, NR), NEG, F32)
    rb = rb.at[:, 0:N_GROUPS].set(b_group).at[:, SUBLANES:SUBLANES + NE].set(b_expert)
    rb = rb.at[:, SUBLANES + NE:].set(0.0)
    return wr.astype(BF16), rb.reshape(DEPTH, NR, 1)


def kernel(x, c, ada_w, ada_b, a_w_in, a_b_in, a_w_dw, a_b_dw, a_ln_g, a_ln_b, a_w_out, a_b_out,
           b_w_in, b_w_dw, b_w_out, mix_ln_g, mix_ln_b, ffn_ln_g, ffn_ln_b,
           r_w_group, r_b_group, r_w_expert, r_b_expert, e_w_gate, e_w_up, e_w_down):
    n_a = a_w_in.shape[0]
    wr, rb = _router_params(r_w_group, r_b_group, r_w_expert, r_b_expert)
    params = {
        "mods": _ada_mods(c, ada_w, ada_b),
        "mix_ln": jnp.stack([mix_ln_g, mix_ln_b], axis=1),
        "ffn_ln": jnp.stack([ffn_ln_g, ffn_ln_b], axis=1),
        "wr": wr, "rb": rb,
        "tri": (jnp.arange(TS)[:, None] < jnp.arange(TS)[None, :]).astype(BF16),
        "a": [a_w_in.astype(BF16), a_b_in.reshape(n_a, 1, 2 * D), a_w_dw.reshape(n_a, CONV_A, DT, LANES),
              a_b_dw.reshape(n_a, DT, LANES), jnp.stack([a_ln_g, a_ln_b, a_b_out], axis=1),
              a_w_out.astype(BF16)],
        "b": [b_w_in.astype(BF16), b_w_dw, b_w_out.astype(BF16)],
    }
    prev = None
    xin = x
    for i in range(DEPTH):
        kind = "a" if i % 2 == 0 else "b"
        x1, h2, ri, wc, counts = _mixer(kind, i, xin, prev, params)
        pad_starts, n_blocks = _plan(counts[:, 0])
        dest = _dest_slots(pad_starts, ri)[0:TOP_K]
        xs = _dispatch_rows(h2.reshape(T, DP), dest)
        yb = _experts(i, pad_starts, n_blocks, xs, e_w_gate, e_w_up, e_w_down, name=f"experts{i}")
        yk_halves = [_return_rows(yb, dest[:, hh * HALF_T:(hh + 1) * HALF_T]).reshape(TOP_K, HALF_T, DP)
                     for hh in range(N_HALVES)]
        prev = (yk_halves, wc)
        xin = x1
    yk_halves, wc = prev
    return _final(xin, yk_halves, wc, params)
```

```python
import functools

import jax
import jax.numpy as jnp
from jax import lax
from jax.experimental import pallas as pl
from jax.experimental.pallas import tpu as pltpu
from jax.experimental.pallas import tpu_sc as plsc

F32 = jnp.float32
BF16 = jnp.bfloat16
I32 = jnp.int32
U32 = jnp.uint32

D = 1024
BATCH = 4
SEQ = 8192
T = BATCH * SEQ
DEPTH = 4
N_GROUPS = 4
EPG = 8
NE = N_GROUPS * EPG
TOP_K = 2
F = D // 2
CONV_A = 31
CONV_B = 3
ALPHA = (2.0 * DEPTH) ** 0.25
LN_EPS = 1e-5

LANES = 128
SUBLANES = 8
VMEM_LIMIT = 56 * 1024 * 1024

TS = 512
SR = 256
NS = SEQ // TS
N_HALVES = 2
HALF_BATCH = BATCH // N_HALVES
HALF_T = HALF_BATCH * SEQ
HALO_A = 32
HALO_B = 8
BM = 1024
ROW_DMA_PRIORITY = 1
NSLOT = T * TOP_K + NE * BM
NR = 48
ADA_TN = 1536
NEG = -1e30
DP = D // 2
DT = D // LANES
assert DT == SUBLANES

SC_CORES = 2
SC_SUBCORES = 16
SC_WORKERS = SC_CORES * SC_SUBCORES
SC_CHUNK = 64


def _sigmoid(x):
    return 1.0 / (1.0 + jnp.exp(-x))


def _pack_rows(x):
    return pltpu.pack_elementwise([x[:, :DP], x[:, DP:]], packed_dtype=BF16)


def _unpack_rows(p):
    lo = pltpu.unpack_elementwise(p, index=0, packed_dtype=BF16, unpacked_dtype=F32)
    hi = pltpu.unpack_elementwise(p, index=1, packed_dtype=BF16, unpacked_dtype=F32)
    return jnp.concatenate([lo, hi], axis=1)


def _layer_norm(x, g, b):
    mu = jnp.mean(x, axis=-1, keepdims=True)
    xc = x - mu
    var = jnp.mean(xc * xc, axis=-1, keepdims=True)
    return xc * lax.rsqrt(var + LN_EPS) * g + b


def _ada_kernel(c_ref, w_ref, b_ref, o_ref):
    c = c_ref[...]
    ca = (c * _sigmoid(c)).astype(BF16)
    w = w_ref[0].astype(BF16)
    o_ref[0] = jnp.dot(ca, w, preferred_element_type=F32) + b_ref[0]


def _ada_mods(c, ada_w, ada_b):
    out = pl.pallas_call(
        _ada_kernel,
        grid=(DEPTH, 6 * D // ADA_TN),
        in_specs=[
            pl.BlockSpec((BATCH, D), lambda i, j: (0, 0)),
            pl.BlockSpec((1, D, ADA_TN), lambda i, j: (i, 0, j)),
            pl.BlockSpec((1, 1, ADA_TN), lambda i, j: (i, 0, j)),
        ],
        out_specs=pl.BlockSpec((1, BATCH, ADA_TN), lambda i, j: (i, 0, j)),
        out_shape=jax.ShapeDtypeStruct((DEPTH, BATCH, 6 * D), F32),
        compiler_params=pltpu.CompilerParams(
            dimension_semantics=("arbitrary", "arbitrary"), vmem_limit_bytes=VMEM_LIMIT),
        name="ada_mods",
    )(c, ada_w, ada_b.reshape(DEPTH, 1, 6 * D))
    return out.reshape(DEPTH, BATCH, 6, D)


def _combine(x1, y0p, y1p, wc, g_f, ln_g, ln_b):
    w0 = jnp.tile(wc[:, :LANES], (1, D // LANES))
    w1 = jnp.tile(wc[:, LANES:], (1, D // LANES))
    y = w0 * _unpack_rows(y0p) + w1 * _unpack_rows(y1p)
    return _layer_norm(ALPHA * x1 + (1.0 + g_f) * y, ln_g, ln_b)


def _route(h2, wr_ref, rb_ref, tri_ref, cnt_ref, ri_ref, wc_ref, cnto_ref):
    lt = lax.dot_general(wr_ref[...], h2, (((1,), (1,)), ((), ())),
                         preferred_element_type=F32) + rb_ref[...]
    iota8 = lax.broadcasted_iota(I32, (SUBLANES, TS), 0).astype(F32)
    gl = lt[0:SUBLANES]
    gmax = jnp.max(gl, axis=0, keepdims=True)
    gidx = jnp.min(jnp.where(gl == gmax, iota8, float(SUBLANES)), axis=0, keepdims=True)
    gw = 1.0 / jnp.sum(jnp.exp(gl - gmax), axis=0, keepdims=True)
    el = lt[SUBLANES:2 * SUBLANES]
    for g in range(1, N_GROUPS):
        el = jnp.where(gidx == float(g), lt[SUBLANES * (g + 1):SUBLANES * (g + 2)], el)
    m1 = jnp.max(el, axis=0, keepdims=True)
    i1 = jnp.min(jnp.where(el == m1, iota8, float(SUBLANES)), axis=0, keepdims=True)
    el2 = jnp.where(iota8 == i1, -jnp.inf, el)
    m2 = jnp.max(el2, axis=0, keepdims=True)
    i2 = jnp.min(jnp.where(el2 == m2, iota8, float(SUBLANES)), axis=0, keepdims=True)
    r = jnp.exp(m2 - m1)
    w_a = gw / (1.0 + r)
    w_b = gw * r / (1.0 + r)
    e1 = gidx * float(EPG) + i1
    e2 = gidx * float(EPG) + i2

    iota_e = lax.broadcasted_iota(I32, (NE, TS), 0).astype(F32)
    oh1 = iota_e == e1
    oh2 = iota_e == e2
    oh = jnp.concatenate([jnp.where(oh1, 1.0, 0.0), jnp.where(oh2, 1.0, 0.0)], axis=0)
    before = jnp.dot(oh.astype(BF16), tri_ref[...], preferred_element_type=F32)
    tot = jnp.sum(oh, axis=1, keepdims=True)
    cnt = cnt_ref[...]
    base = jnp.tile(cnt, (1, TS // LANES))
    tot1 = tot[:NE]
    tot2 = tot[NE:]
    rank1 = jnp.sum(jnp.where(oh1, base + before[:NE], 0.0), axis=0, keepdims=True)
    rank2 = jnp.sum(jnp.where(oh2, base + tot1 + before[NE:], 0.0), axis=0, keepdims=True)
    new_cnt = cnt + tot1 + tot2
    cnt_ref[...] = new_cnt
    cnto_ref[...] = new_cnt.astype(I32)

    ri_ref[0:1, :] = e1.astype(I32)
    ri_ref[1:2, :] = e2.astype(I32)
    ri_ref[2:3, :] = rank1.astype(I32)
    ri_ref[3:4, :] = rank2.astype(I32)
    ri_ref[4:8, :] = jnp.zeros((4, TS), I32)
    wc_ref[0, :, :LANES] = jnp.broadcast_to(w_a, (LANES, TS)).T
    wc_ref[0, :, LANES:] = jnp.broadcast_to(w_b, (LANES, TS)).T


def _conv_taps(uext_ref, u, w_dw, halo, width, r0):
    uext_ref[halo + r0:halo + r0 + SR, :] = u
    acc = None
    for k in range(width):
        off = r0 + halo - (width - 1) + k
        term = w_dw[k:k + 1, :] * uext_ref[off:off + SR, :]
        acc = term if acc is None else acc + term
    return acc


def _conv_time_major(tm_ref, o2_ref, u, wk_ref, bias, r0, width, halo):
    for j in range(DT):
        tm_ref[pl.ds((halo + r0) * DT + j, SR, stride=DT), :] = u[:, j * LANES:(j + 1) * LANES]
    acc = None
    for k in range(width):
        off = (r0 + halo - (width - 1) + k) * DT
        term = tm_ref[off:off + SR * DT, :].reshape(SR, DT, LANES) * wk_ref[k]
        acc = term if acc is None else acc + term
    if bias is not None:
        acc = acc + bias
    o2_ref[r0 * DT:(r0 + SR) * DT, :] = acc.reshape(SR * DT, LANES)
    return jnp.concatenate([o2_ref[pl.ds(r0 * DT + j, SR, stride=DT), :] for j in range(DT)], axis=1)


def _mixer_kernel(*refs, kind, has_prev, carry):
    it = iter(refs)
    xin_ref = next(it)
    if has_prev:
        y0_ref, y1_ref, wcin_ref, pmods_ref, pln_ref = (next(it) for _ in range(5))
    if carry:
        for _ in range(4):
            next(it)
        cnt_in_ref = next(it)
    mods_ref, mln_ref, win_ref = next(it), next(it), next(it)
    if kind == "a":
        bin_ref, wdw_ref, bdw_ref, vec_ref = next(it), next(it), next(it), next(it)
    else:
        wdw_ref = next(it)
    wout_ref, wr_ref, rb_ref, tri_ref = (next(it) for _ in range(4))
    x1_ref, h2_ref, ri_ref, wc_ref, cnto_ref = (next(it) for _ in range(5))
    if kind == "a":
        uext_ref, o2_ref, h2b_ref, cnt_ref = (next(it) for _ in range(4))
    else:
        uext_ref, h2b_ref, cnt_ref = (next(it) for _ in range(3))
    win_s, wout_s = next(it), next(it)
    mln_ref, win_ref, wdw_ref, wout_ref, wr_ref, rb_ref = (
        r.at[0] for r in (mln_ref, win_ref, wdw_ref, wout_ref, wr_ref, rb_ref))
    if has_prev:
        pln_ref = pln_ref.at[0]
    if kind == "a":
        bin_ref, bdw_ref, vec_ref = (r.at[0] for r in (bin_ref, bdw_ref, vec_ref))

    first = (pl.program_id(0) == 0) & (pl.program_id(1) == 0)

    @pl.when(first)
    def _():
        if carry:
            cnt_ref[...] = cnt_in_ref[...].astype(F32)
        else:
            cnt_ref[...] = jnp.zeros((NE, LANES), F32)
        win_s[...] = win_ref[...]
        wout_s[...] = wout_ref[...]

    halo_rows = HALO_A * DT if kind == "a" else HALO_B

    @pl.when(pl.program_id(1) == 0)
    def _():
        uext_ref[0:halo_rows, :] = jnp.zeros((halo_rows, uext_ref.shape[1]), F32)

    m = mods_ref[0, 0]
    mln = mln_ref[...]
    for i in range(TS // SR):
        r0 = i * SR
        rows = slice(r0, r0 + SR)
        x = xin_ref[0, rows, :]
        if has_prev:
            pln = pln_ref[...]
            x = _combine(x, y0_ref[0, rows, :], y1_ref[0, rows, :], wcin_ref[0, rows, :],
                         pmods_ref[0, 0][5:6, :], pln[0:1, :], pln[1:2, :])

        h = (x * (1.0 + m[1:2, :]) + m[0:1, :]).astype(BF16)
        cols = [jnp.dot(h, win_s[:, c * D:(c + 1) * D], preferred_element_type=F32)
                for c in range(win_s.shape[1] // D)]
        if kind == "a":
            b_in = bin_ref[...]
            vec = vec_ref[...]
            u = (cols[0] + b_in[:, :D]) * _sigmoid(cols[1] + b_in[:, D:])
            u = _conv_time_major(uext_ref, o2_ref, u, wdw_ref, bdw_ref[...], r0, CONV_A, HALO_A)
            u = _layer_norm(u, vec[0:1, :], vec[1:2, :])
            u = u * _sigmoid(u)
            y = jnp.dot(u.astype(BF16), wout_s[...], preferred_element_type=F32) + vec[2:3, :]
        else:
            gb = cols[0]
            q = cols[1] * cols[2]
            u = _conv_taps(uext_ref, q, wdw_ref[...], HALO_B, CONV_B, r0)
            y = jnp.dot((gb * u).astype(BF16), wout_s[...], preferred_element_type=F32)

        x1 = _layer_norm(ALPHA * x + (1.0 + m[2:3, :]) * y, mln[0:1, :], mln[1:2, :])
        x1_ref[0, rows, :] = x1
        h2 = x1 * (1.0 + m[4:5, :]) + m[3:4, :]
        h2_ref[0, rows, :] = _pack_rows(h2)
        h2b_ref[rows, :] = h2.astype(BF16)

    uext_ref[0:halo_rows, :] = uext_ref[uext_ref.shape[0] - halo_rows:uext_ref.shape[0], :]
    _route(h2b_ref[...], wr_ref, rb_ref, tri_ref, cnt_ref, ri_ref, wc_ref, cnto_ref)


def _tile_spec(width, b0=0):
    return pl.BlockSpec((1, TS, width), lambda b, s: (b + b0, s, 0))


def _yk_spec(k):
    return pl.BlockSpec((1, TS, DP), lambda b, s: (k, b * NS + s, 0))


def _const_spec(shape):
    nd = len(shape)
    return pl.BlockSpec(shape, lambda b, s: (0,) * nd)


def _mods_spec(layer, b0=0):
    return pl.BlockSpec((1, 1, 6, D), lambda b, s: (layer, b + b0, 0, 0))


def _layer_spec(arr, idx):
    tail = arr.shape[1:]
    return pl.BlockSpec((1,) + tail, lambda b, s: (idx,) + (0,) * len(tail))


def _any_spec():
    return pl.BlockSpec(memory_space=pl.ANY)


def _mixer_call(kind, layer, xin, prev, params, name, b0, n_batch, carried):
    mods, mix_ln, ffn_ln, wr, rb, tri = (params[k] for k in ("mods", "mix_ln", "ffn_ln", "wr", "rb", "tri"))
    weights = params[kind]
    j = layer // 2
    has_prev = prev is not None
    carry = carried is not None
    args = [xin]
    specs = [_tile_spec(D, b0)]
    if has_prev:
        yk, wcin = prev
        args += [yk, yk, wcin, mods, ffn_ln]
        specs += [_yk_spec(0), _yk_spec(1), _tile_spec(2 * LANES, b0), _mods_spec(layer - 1, b0),
                  _layer_spec(ffn_ln, layer - 1)]
    aliases = {}
    if carry:
        for out_idx, arr in enumerate(carried):
            aliases[len(args)] = out_idx
            args.append(arr)
            specs.append(_any_spec() if out_idx < 4 else _const_spec((NE, LANES)))
    args += [mods, mix_ln]
    specs += [_mods_spec(layer, b0), _layer_spec(mix_ln, layer)]
    for w in weights:
        args.append(w)
        specs.append(_layer_spec(w, j))
    args += [wr, rb, tri]
    specs += [_layer_spec(wr, layer), _layer_spec(rb, layer), _const_spec(tri.shape)]
    if kind == "a":
        conv_scratch = [pltpu.VMEM(((TS + HALO_A) * DT, LANES), F32), pltpu.VMEM((TS * DT, LANES), F32)]
    else:
        conv_scratch = [pltpu.VMEM((TS + HALO_B, D), F32)]
    out_shape = (
        jax.ShapeDtypeStruct((BATCH, SEQ, D), F32),
        jax.ShapeDtypeStruct((BATCH, SEQ, DP), U32),
        jax.ShapeDtypeStruct((SUBLANES, T), I32),
        jax.ShapeDtypeStruct((BATCH, SEQ, 2 * LANES), F32),
        jax.ShapeDtypeStruct((NE, LANES), I32),
    )
    out_specs = (
        _tile_spec(D, b0), _tile_spec(DP, b0),
        pl.BlockSpec((SUBLANES, TS), lambda b, s: (0, (b + b0) * NS + s)),
        _tile_spec(2 * LANES, b0),
        pl.BlockSpec((NE, LANES), lambda b, s: (0, 0)),
    )
    return pl.pallas_call(
        functools.partial(_mixer_kernel, kind=kind, has_prev=has_prev, carry=carry),
        grid=(n_batch, NS),
        in_specs=specs,
        out_specs=out_specs,
        out_shape=out_shape,
        input_output_aliases=aliases,
        scratch_shapes=conv_scratch + [pltpu.VMEM((TS, D), BF16), pltpu.VMEM((NE, LANES), F32),
                                       pltpu.VMEM(weights[0].shape[1:], BF16),
                                       pltpu.VMEM(weights[-1].shape[1:], BF16)],
        compiler_params=pltpu.CompilerParams(
            dimension_semantics=("arbitrary", "arbitrary"), vmem_limit_bytes=VMEM_LIMIT),
        name=name,
    )(*args)


def _mixer(kind, layer, xin, prev, params):
    name = f"mixer_{kind}{layer}"
    if prev is None:
        return _mixer_call(kind, layer, xin, None, params, name, 0, BATCH, None)
    yk_halves, wcin = prev
    out = None
    for hh in range(N_HALVES):
        out = _mixer_call(kind, layer, xin, (yk_halves[hh], wcin), params, f"{name}_h{hh}",
                          hh * HALF_BATCH, HALF_BATCH, out)
    return out


def _dest_kernel(ps_ref, ri_ref, o_ref):
    ri = ri_ref[...]
    e = ri[0:2, :]
    start = jnp.zeros_like(e)
    for k in range(NE):
        start = jnp.where(e == k, ps_ref[k], start)
    o_ref[0:2, :] = start + ri[2:4, :]
    o_ref[2:8, :] = jnp.zeros((6, ri.shape[1]), I32)


def _dest_slots(pad_starts, ri):
    tn = 4096
    return pl.pallas_call(
        _dest_kernel,
        grid_spec=pltpu.PrefetchScalarGridSpec(
            num_scalar_prefetch=1,
            grid=(T // tn,),
            in_specs=[pl.BlockSpec((SUBLANES, tn), lambda i, ps: (0, i))],
            out_specs=pl.BlockSpec((SUBLANES, tn), lambda i, ps: (0, i)),
        ),
        out_shape=jax.ShapeDtypeStruct((SUBLANES, T), I32),
        compiler_params=pltpu.CompilerParams(dimension_semantics=("arbitrary",)),
        name="dest_slots",
    )(pad_starts, ri)


def _expert_kernel(st_ref, nb_ref, xs_hbm, wg_ref, wu_ref, wd_ref, yb_hbm, wgu_s, wd_s, xbuf, obuf, xsem, osem):
    e = pl.program_id(0)
    nb = nb_ref[e]
    start = st_ref[e]

    def x_copy(first_row, i, slot):
        return pltpu.make_async_copy(xs_hbm.at[pl.ds(pl.multiple_of(first_row + i * BM, BM), BM)],
                                     xbuf.at[slot], xsem.at[slot])

    def o_copy(i, slot):
        return pltpu.make_async_copy(obuf.at[slot],
                                     yb_hbm.at[pl.ds(pl.multiple_of(start + i * BM, BM), BM)], osem.at[slot])

    @pl.when((nb > 0) & (e == 0))
    def _():
        x_copy(start, 0, 0).start(priority=ROW_DMA_PRIORITY)

    @pl.when(nb > 0)
    def _():
        wgu_s[:, :F] = wg_ref[0, 0].astype(BF16)
        wgu_s[:, F:] = wu_ref[0, 0].astype(BF16)
        wd_s[...] = wd_ref[0, 0].astype(BF16)

    @pl.loop(0, nb, step=2)
    def _(i0):
        for slot in range(2):
            i = i0 + slot

            @pl.when(i < nb)
            def _():
                x_copy(start, i, slot).wait()

                @pl.when(i + 1 < nb)
                def _():
                    x_copy(start, i + 1, 1 - slot).start(priority=ROW_DMA_PRIORITY)

                @pl.when(i >= 2)
                def _():
                    o_copy(i - 2, slot).wait()

                x = _unpack_rows(xbuf[slot]).astype(BF16)
                gu = jnp.dot(x, wgu_s[...], preferred_element_type=F32)
                g = gu[:, :F]
                hid = (g * _sigmoid(g) * gu[:, F:]).astype(BF16)
                obuf[slot] = _pack_rows(jnp.dot(hid, wd_s[...], preferred_element_type=F32))
                o_copy(i, slot).start(priority=ROW_DMA_PRIORITY)

    nxt = jnp.minimum(e + 1, NE - 1)

    @pl.when((e + 1 < NE) & (nb_ref[nxt] > 0))
    def _():
        x_copy(st_ref[nxt], 0, 0).start(priority=ROW_DMA_PRIORITY)

    for back in (1, 2):
        @pl.when(nb >= back)
        def _():
            last = nb - back
            o_copy(last, last % 2).wait()


def _experts(layer, starts, n_blocks, xs, w_gate, w_up, w_down, name):
    def w_map(e, st, nb):
        return (layer, e, 0, 0)

    return pl.pallas_call(
        _expert_kernel,
        grid_spec=pltpu.PrefetchScalarGridSpec(
            num_scalar_prefetch=2,
            grid=(NE,),
            in_specs=[
                pl.BlockSpec(memory_space=pl.ANY),
                pl.BlockSpec((1, 1, D, F), w_map),
                pl.BlockSpec((1, 1, D, F), w_map),
                pl.BlockSpec((1, 1, F, D), w_map),
            ],
            out_specs=pl.BlockSpec(memory_space=pl.ANY),
            scratch_shapes=[
                pltpu.VMEM((D, 2 * F), BF16), pltpu.VMEM((F, D), BF16),
                pltpu.VMEM((2, BM, DP), U32), pltpu.VMEM((2, BM, DP), U32),
                pltpu.SemaphoreType.DMA((2,)), pltpu.SemaphoreType.DMA((2,)),
            ],
        ),
        out_shape=jax.ShapeDtypeStruct((NSLOT, DP), U32),
        compiler_params=pltpu.CompilerParams(
            dimension_semantics=("arbitrary",), vmem_limit_bytes=VMEM_LIMIT),
        name=name,
    )(starts, n_blocks, xs, w_gate, w_up, w_down)


def _sc_worker_id():
    return lax.axis_index("s") * SC_CORES + lax.axis_index("c")


def _sc_mesh():
    return plsc.VectorSubcoreMesh(core_axis_name="c", subcore_axis_name="s")


def _sc_scratch(n_index_rows):
    return [
        pltpu.VMEM((n_index_rows, SC_CHUNK), I32),
        pltpu.VMEM((2, SC_CHUNK, DP), U32),
        pltpu.SemaphoreType.DMA((2,)),
        pltpu.SemaphoreType.DMA((2,)),
    ]


def _dispatch_rows(h2p, dest):
    per_w = T // SC_WORKERS
    n_chunks = per_w // SC_CHUNK

    @functools.partial(
        pl.kernel, mesh=_sc_mesh(),
        out_type=jax.ShapeDtypeStruct((NSLOT, DP), U32),
        scratch_types=_sc_scratch(TOP_K * n_chunks),
        name="dispatch_rows",
    )
    def k(h2_hbm, dest_hbm, out_hbm, dest_v, rows_v, rsem, wsem):
        wid = _sc_worker_id()
        for kk in range(TOP_K):
            pltpu.sync_copy(dest_hbm.at[kk, pl.ds(wid * n_chunks, n_chunks)],
                            dest_v.at[pl.ds(kk * n_chunks, n_chunks)])
        base = wid * per_w

        def read(c, slot):
            return pltpu.make_async_copy(h2_hbm.at[pl.ds(base + c * SC_CHUNK, SC_CHUNK)],
                                         rows_v.at[slot], rsem.at[slot])

        def write(c, kk, slot):
            return pltpu.make_async_copy(rows_v.at[slot], out_hbm.at[dest_v.at[kk * n_chunks + c]],
                                         wsem.at[slot])

        read(0, 0).start()

        @pl.loop(0, n_chunks, step=2)
        def _(c):
            for b in range(2):
                cc = c + b
                read(cc, b).wait()

                @pl.when(cc + 1 < n_chunks)
                def _():
                    @pl.when(cc >= 1)
                    def _():
                        for kk in range(TOP_K):
                            write(cc - 1, kk, 1 - b).wait()
                    read(cc + 1, 1 - b).start()

                for kk in range(TOP_K):
                    write(cc, kk, b).start()

        for slot, cc in ((0, n_chunks - 2), (1, n_chunks - 1)):
            for kk in range(TOP_K):
                write(cc, kk, slot).wait()

    return k(h2p, dest.reshape(TOP_K, T // SC_CHUNK, SC_CHUNK))


def _return_rows(yb, dest):
    m = dest.size
    per_w = m // SC_WORKERS
    n_chunks = per_w // SC_CHUNK

    @functools.partial(
        pl.kernel, mesh=_sc_mesh(),
        out_type=jax.ShapeDtypeStruct((m, DP), U32),
        scratch_types=_sc_scratch(n_chunks),
        name="return_rows",
    )
    def k(yb_hbm, dest_hbm, out_hbm, idx_v, rows_v, gsem, wsem):
        wid = _sc_worker_id()
        pltpu.sync_copy(dest_hbm.at[pl.ds(wid * n_chunks, n_chunks)], idx_v)
        base = wid * per_w

        def gather(c, slot):
            return pltpu.make_async_copy(yb_hbm.at[idx_v.at[c]], rows_v.at[slot], gsem.at[slot])

        def write(c, slot):
            return pltpu.make_async_copy(rows_v.at[slot], out_hbm.at[pl.ds(base + c * SC_CHUNK, SC_CHUNK)],
                                         wsem.at[slot])

        gather(0, 0).start()

        @pl.loop(0, n_chunks, step=2)
        def _(c):
            for b in range(2):
                cc = c + b
                gather(cc, b).wait()

                @pl.when(cc + 1 < n_chunks)
                def _():
                    @pl.when(cc >= 1)
                    def _():
                        write(cc - 1, 1 - b).wait()
                    gather(cc + 1, 1 - b).start()

                write(cc, b).start()

        write(n_chunks - 2, 0).wait()
        write(n_chunks - 1, 1).wait()

    return k(yb, dest.reshape(m // SC_CHUNK, SC_CHUNK))


def _final_kernel(x1_ref, y0_ref, y1_ref, wc_ref, mods_ref, ln_ref, *rest):
    o_ref = rest[-1]
    ln = ln_ref[0]
    o_ref[0] = _combine(x1_ref[0], y0_ref[0], y1_ref[0], wc_ref[0], mods_ref[0, 0][5:6, :],
                        ln[0:1, :], ln[1:2, :])


def _final(x1, yk_halves, wc, params):
    mods, ffn_ln = params["mods"], params["ffn_ln"]
    out = None
    for hh in range(N_HALVES):
        b0 = hh * HALF_BATCH
        args = [x1, yk_halves[hh], yk_halves[hh], wc, mods, ffn_ln]
        specs = [_tile_spec(D, b0), _yk_spec(0), _yk_spec(1), _tile_spec(2 * LANES, b0),
                 _mods_spec(DEPTH - 1, b0), _layer_spec(ffn_ln, DEPTH - 1)]
        aliases = {}
        if out is not None:
            aliases[len(args)] = 0
            args.append(out)
            specs.append(_any_spec())
        out = pl.pallas_call(
            _final_kernel,
            grid=(HALF_BATCH, NS),
            in_specs=specs,
            out_specs=_tile_spec(D, b0),
            out_shape=jax.ShapeDtypeStruct((BATCH, SEQ, D), F32),
            input_output_aliases=aliases,
            compiler_params=pltpu.CompilerParams(
                dimension_semantics=("arbitrary", "arbitrary"), vmem_limit_bytes=VMEM_LIMIT),
            name=f"final_combine_h{hh}",
        )(*args)
    return out


def _plan(counts):
    n_blocks = (counts + BM - 1) // BM
    padded = n_blocks * BM
    pad_starts = (jnp.cumsum(padded) - padded).astype(I32)
    return pad_starts, n_blocks.astype(I32)


def _router_params(w_group, b_group, w_expert, b_expert):
    wr = jnp.zeros((DEPTH, NR, D), F32)
    wr = wr.at[:, 0:N_GROUPS].set(jnp.swapaxes(w_group, 1, 2))
    wr = wr.at[:, SUBLANES:SUBLANES + NE].set(jnp.swapaxes(w_expert, 1, 2))
    rb = jnp.full((DEPTH, NR), NEG, F32)
    rb = rb.at[:, 0:N_GROUPS].set(b_group).at[:, SUBLANES:SUBLANES + NE].set(b_expert)
    rb = rb.at[:, SUBLANES + NE:].set(0.0)
    return wr.astype(BF16), rb.reshape(DEPTH, NR, 1)


def kernel(x, c, ada_w, ada_b, a_w_in, a_b_in, a_w_dw, a_b_dw, a_ln_g, a_ln_b, a_w_out, a_b_out,
           b_w_in, b_w_dw, b_w_out, mix_ln_g, mix_ln_b, ffn_ln_g, ffn_ln_b,
           r_w_group, r_b_group, r_w_expert, r_b_expert, e_w_gate, e_w_up, e_w_down):
    n_a = a_w_in.shape[0]
    wr, rb = _router_params(r_w_group, r_b_group, r_w_expert, r_b_expert)
    params = {
        "mods": _ada_mods(c, ada_w, ada_b),
        "mix_ln": jnp.stack([mix_ln_g, mix_ln_b], axis=1),
        "ffn_ln": jnp.stack([ffn_ln_g, ffn_ln_b], axis=1),
        "wr": wr, "rb": rb,
        "tri": (jnp.arange(TS)[:, None] < jnp.arange(TS)[None, :]).astype(BF16),
        "a": [a_w_in.astype(BF16), a_b_in.reshape(n_a, 1, 2 * D), a_w_dw.reshape(n_a, CONV_A, DT, LANES),
              a_b_dw.reshape(n_a, DT, LANES), jnp.stack([a_ln_g, a_ln_b, a_b_out], axis=1),
              a_w_out.astype(BF16)],
        "b": [b_w_in.astype(BF16), b_w_dw, b_w_out.astype(BF16)],
    }
    prev = None
    xin = x
    for i in range(DEPTH):
        kind = "a" if i % 2 == 0 else "b"
        x1, h2, ri, wc, counts = _mixer(kind, i, xin, prev, params)
        pad_starts, n_blocks = _plan(counts[:, 0])
        dest = _dest_slots(pad_starts, ri)[0:TOP_K]
        xs = _dispatch_rows(h2.reshape(T, DP), dest)
        yb = _experts(i, pad_starts, n_blocks, xs, e_w_gate, e_w_up, e_w_down, name=f"experts{i}")
        yk_halves = [_return_rows(yb, dest[:, hh * HALF_T:(hh + 1) * HALF_T]).reshape(TOP_K, HALF_T, DP)
                     for hh in range(N_HALVES)]
        prev = (yk_halves, wc)
        xin = x1
    yk_halves, wc = prev
    return _final(xin, yk_halves, wc, params)
```

```python
import functools

import jax
import jax.numpy as jnp
from jax import lax
from jax.experimental import pallas as pl
from jax.experimental.pallas import tpu as pltpu
from jax.experimental.pallas import tpu_sc as plsc

F32 = jnp.float32
BF16 = jnp.bfloat16
I32 = jnp.int32
U32 = jnp.uint32

D = 1024
BATCH = 4
SEQ = 8192
T = BATCH * SEQ
DEPTH = 4
N_GROUPS = 4
EPG = 8
NE = N_GROUPS * EPG
TOP_K = 2
F = D // 2
CONV_A = 31
CONV_B = 3
ALPHA = (2.0 * DEPTH) ** 0.25
LN_EPS = 1e-5

LANES = 128
SUBLANES = 8
VMEM_LIMIT = 56 * 1024 * 1024

TS = 512
SR = 256
NS = SEQ // TS
N_HALVES = 2
HALF_BATCH = BATCH // N_HALVES
HALF_T = HALF_BATCH * SEQ
HALO_A = 32
HALO_B = 8
BM = 512
UNIT = 2 * BM
ROW_DMA_PRIORITY = 1
NSLOT = T * TOP_K + NE * BM
NR = 48
ADA_TN = 1536
NEG = -1e30
DP = D // 2
DT = D // LANES
assert DT == SUBLANES

SC_CORES = 2
SC_SUBCORES = 16
SC_WORKERS = SC_CORES * SC_SUBCORES
SC_CHUNK = 64


def _sigmoid(x):
    return 1.0 / (1.0 + jnp.exp(-x))


def _pack_rows(x):
    return pltpu.pack_elementwise([x[:, :DP], x[:, DP:]], packed_dtype=BF16)


def _unpack_rows(p):
    lo = pltpu.unpack_elementwise(p, index=0, packed_dtype=BF16, unpacked_dtype=F32)
    hi = pltpu.unpack_elementwise(p, index=1, packed_dtype=BF16, unpacked_dtype=F32)
    return jnp.concatenate([lo, hi], axis=1)


def _layer_norm(x, g, b):
    mu = jnp.mean(x, axis=-1, keepdims=True)
    xc = x - mu
    var = jnp.mean(xc * xc, axis=-1, keepdims=True)
    return xc * lax.rsqrt(var + LN_EPS) * g + b


def _ada_kernel(c_ref, w_ref, b_ref, o_ref):
    c = c_ref[...]
    ca = (c * _sigmoid(c)).astype(BF16)
    w = w_ref[0].astype(BF16)
    o_ref[0] = jnp.dot(ca, w, preferred_element_type=F32) + b_ref[0]


def _ada_mods(c, ada_w, ada_b):
    out = pl.pallas_call(
        _ada_kernel,
        grid=(DEPTH, 6 * D // ADA_TN),
        in_specs=[
            pl.BlockSpec((BATCH, D), lambda i, j: (0, 0)),
            pl.BlockSpec((1, D, ADA_TN), lambda i, j: (i, 0, j)),
            pl.BlockSpec((1, 1, ADA_TN), lambda i, j: (i, 0, j)),
        ],
        out_specs=pl.BlockSpec((1, BATCH, ADA_TN), lambda i, j: (i, 0, j)),
        out_shape=jax.ShapeDtypeStruct((DEPTH, BATCH, 6 * D), F32),
        compiler_params=pltpu.CompilerParams(
            dimension_semantics=("arbitrary", "arbitrary"), vmem_limit_bytes=VMEM_LIMIT),
        name="ada_mods",
    )(c, ada_w, ada_b.reshape(DEPTH, 1, 6 * D))
    return out.reshape(DEPTH, BATCH, 6, D)


def _combine(x1, y0p, y1p, wc, g_f, ln_g, ln_b):
    w0 = jnp.tile(wc[:, :LANES], (1, D // LANES))
    w1 = jnp.tile(wc[:, LANES:], (1, D // LANES))
    y = w0 * _unpack_rows(y0p) + w1 * _unpack_rows(y1p)
    return _layer_norm(ALPHA * x1 + (1.0 + g_f) * y, ln_g, ln_b)


def _route(h2, wr_ref, rb_ref, tri_ref, cnt_ref, ri_ref, wc_ref, cnto_ref):
    lt = lax.dot_general(wr_ref[...], h2, (((1,), (1,)), ((), ())),
                         preferred_element_type=F32) + rb_ref[...]
    iota8 = lax.broadcasted_iota(I32, (SUBLANES, TS), 0).astype(F32)
    gl = lt[0:SUBLANES]
    gmax = jnp.max(gl, axis=0, keepdims=True)
    gidx = jnp.min(jnp.where(gl == gmax, iota8, float(SUBLANES)), axis=0, keepdims=True)
    gw = 1.0 / jnp.sum(jnp.exp(gl - gmax), axis=0, keepdims=True)
    el = lt[SUBLANES:2 * SUBLANES]
    for g in range(1, N_GROUPS):
        el = jnp.where(gidx == float(g), lt[SUBLANES * (g + 1):SUBLANES * (g + 2)], el)
    m1 = jnp.max(el, axis=0, keepdims=True)
    i1 = jnp.min(jnp.where(el == m1, iota8, float(SUBLANES)), axis=0, keepdims=True)
    el2 = jnp.where(iota8 == i1, -jnp.inf, el)
    m2 = jnp.max(el2, axis=0, keepdims=True)
    i2 = jnp.min(jnp.where(el2 == m2, iota8, float(SUBLANES)), axis=0, keepdims=True)
    r = jnp.exp(m2 - m1)
    w_a = gw / (1.0 + r)
    w_b = gw * r / (1.0 + r)
    e1 = gidx * float(EPG) + i1
    e2 = gidx * float(EPG) + i2

    iota_e = lax.broadcasted_iota(I32, (NE, TS), 0).astype(F32)
    oh1 = iota_e == e1
    oh2 = iota_e == e2
    oh = jnp.concatenate([jnp.where(oh1, 1.0, 0.0), jnp.where(oh2, 1.0, 0.0)], axis=0)
    before = jnp.dot(oh.astype(BF16), tri_ref[...], preferred_element_type=F32)
    tot = jnp.sum(oh, axis=1, keepdims=True)
    cnt = cnt_ref[...]
    base = jnp.tile(cnt, (1, TS // LANES))
    tot1 = tot[:NE]
    tot2 = tot[NE:]
    rank1 = jnp.sum(jnp.where(oh1, base + before[:NE], 0.0), axis=0, keepdims=True)
    rank2 = jnp.sum(jnp.where(oh2, base + tot1 + before[NE:], 0.0), axis=0, keepdims=True)
    new_cnt = cnt + tot1 + tot2
    cnt_ref[...] = new_cnt
    cnto_ref[...] = new_cnt.astype(I32)

    ri_ref[0:1, :] = e1.astype(I32)
    ri_ref[1:2, :] = e2.astype(I32)
    ri_ref[2:3, :] = rank1.astype(I32)
    ri_ref[3:4, :] = rank2.astype(I32)
    ri_ref[4:8, :] = jnp.zeros((4, TS), I32)
    wc_ref[0, :, :LANES] = jnp.broadcast_to(w_a, (LANES, TS)).T
    wc_ref[0, :, LANES:] = jnp.broadcast_to(w_b, (LANES, TS)).T


def _conv_taps(uext_ref, u, w_dw, halo, width, r0):
    uext_ref[halo + r0:halo + r0 + SR, :] = u
    acc = None
    for k in range(width):
        off = r0 + halo - (width - 1) + k
        term = w_dw[k:k + 1, :] * uext_ref[off:off + SR, :]
        acc = term if acc is None else acc + term
    return acc


def _conv_time_major(tm_ref, o2_ref, u, wk_ref, bias, r0, width, halo):
    for j in range(DT):
        tm_ref[pl.ds((halo + r0) * DT + j, SR, stride=DT), :] = u[:, j * LANES:(j + 1) * LANES]
    acc = None
    for k in range(width):
        off = (r0 + halo - (width - 1) + k) * DT
        term = tm_ref[off:off + SR * DT, :].reshape(SR, DT, LANES) * wk_ref[k]
        acc = term if acc is None else acc + term
    if bias is not None:
        acc = acc + bias
    o2_ref[r0 * DT:(r0 + SR) * DT, :] = acc.reshape(SR * DT, LANES)
    return jnp.concatenate([o2_ref[pl.ds(r0 * DT + j, SR, stride=DT), :] for j in range(DT)], axis=1)


def _mixer_kernel(*refs, kind, has_prev, carry):
    it = iter(refs)
    xin_ref = next(it)
    if has_prev:
        y0_ref, y1_ref, wcin_ref, pmods_ref, pln_ref = (next(it) for _ in range(5))
    if carry:
        for _ in range(4):
            next(it)
        cnt_in_ref = next(it)
    mods_ref, mln_ref, win_ref = next(it), next(it), next(it)
    if kind == "a":
        bin_ref, wdw_ref, bdw_ref, vec_ref = next(it), next(it), next(it), next(it)
    else:
        wdw_ref = next(it)
    wout_ref, wr_ref, rb_ref, tri_ref = (next(it) for _ in range(4))
    x1_ref, h2_ref, ri_ref, wc_ref, cnto_ref = (next(it) for _ in range(5))
    if kind == "a":
        uext_ref, o2_ref, h2b_ref, cnt_ref = (next(it) for _ in range(4))
    else:
        uext_ref, h2b_ref, cnt_ref = (next(it) for _ in range(3))
    win_s, wout_s = next(it), next(it)
    mln_ref, win_ref, wdw_ref, wout_ref, wr_ref, rb_ref = (
        r.at[0] for r in (mln_ref, win_ref, wdw_ref, wout_ref, wr_ref, rb_ref))
    if has_prev:
        pln_ref = pln_ref.at[0]
    if kind == "a":
        bin_ref, bdw_ref, vec_ref = (r.at[0] for r in (bin_ref, bdw_ref, vec_ref))

    first = (pl.program_id(0) == 0) & (pl.program_id(1) == 0)

    @pl.when(first)
    def _():
        if carry:
            cnt_ref[...] = cnt_in_ref[...].astype(F32)
        else:
            cnt_ref[...] = jnp.zeros((NE, LANES), F32)
        win_s[...] = win_ref[...]
        wout_s[...] = wout_ref[...]

    halo_rows = HALO_A * DT if kind == "a" else HALO_B

    @pl.when(pl.program_id(1) == 0)
    def _():
        uext_ref[0:halo_rows, :] = jnp.zeros((halo_rows, uext_ref.shape[1]), F32)

    m = mods_ref[0, 0]
    mln = mln_ref[...]
    for i in range(TS // SR):
        r0 = i * SR
        rows = slice(r0, r0 + SR)
        x = xin_ref[0, rows, :]
        if has_prev:
            pln = pln_ref[...]
            x = _combine(x, y0_ref[0, rows, :], y1_ref[0, rows, :], wcin_ref[0, rows, :],
                         pmods_ref[0, 0][5:6, :], pln[0:1, :], pln[1:2, :])

        h = (x * (1.0 + m[1:2, :]) + m[0:1, :]).astype(BF16)
        cols = [jnp.dot(h, win_s[:, c * D:(c + 1) * D], preferred_element_type=F32)
                for c in range(win_s.shape[1] // D)]
        if kind == "a":
            b_in = bin_ref[...]
            vec = vec_ref[...]
            u = (cols[0] + b_in[:, :D]) * _sigmoid(cols[1] + b_in[:, D:])
            u = _conv_time_major(uext_ref, o2_ref, u, wdw_ref, bdw_ref[...], r0, CONV_A, HALO_A)
            u = _layer_norm(u, vec[0:1, :], vec[1:2, :])
            u = u * _sigmoid(u)
            y = jnp.dot(u.astype(BF16), wout_s[...], preferred_element_type=F32) + vec[2:3, :]
        else:
            gb = cols[0]
            q = cols[1] * cols[2]
            u = _conv_taps(uext_ref, q, wdw_ref[...], HALO_B, CONV_B, r0)
            y = jnp.dot((gb * u).astype(BF16), wout_s[...], preferred_element_type=F32)

        x1 = _layer_norm(ALPHA * x + (1.0 + m[2:3, :]) * y, mln[0:1, :], mln[1:2, :])
        x1_ref[0, rows, :] = x1
        h2 = x1 * (1.0 + m[4:5, :]) + m[3:4, :]
        h2_ref[0, rows, :] = _pack_rows(h2)
        h2b_ref[rows, :] = h2.astype(BF16)

    uext_ref[0:halo_rows, :] = uext_ref[uext_ref.shape[0] - halo_rows:uext_ref.shape[0], :]
    _route(h2b_ref[...], wr_ref, rb_ref, tri_ref, cnt_ref, ri_ref, wc_ref, cnto_ref)


def _tile_spec(width, b0=0):
    return pl.BlockSpec((1, TS, width), lambda b, s: (b + b0, s, 0))


def _yk_spec(k):
    return pl.BlockSpec((1, TS, DP), lambda b, s: (k, b * NS + s, 0))


def _const_spec(shape):
    nd = len(shape)
    return pl.BlockSpec(shape, lambda b, s: (0,) * nd)


def _mods_spec(layer, b0=0):
    return pl.BlockSpec((1, 1, 6, D), lambda b, s: (layer, b + b0, 0, 0))


def _layer_spec(arr, idx):
    tail = arr.shape[1:]
    return pl.BlockSpec((1,) + tail, lambda b, s: (idx,) + (0,) * len(tail))


def _any_spec():
    return pl.BlockSpec(memory_space=pl.ANY)


def _mixer_call(kind, layer, xin, prev, params, name, b0, n_batch, carried):
    mods, mix_ln, ffn_ln, wr, rb, tri = (params[k] for k in ("mods", "mix_ln", "ffn_ln", "wr", "rb", "tri"))
    weights = params[kind]
    j = layer // 2
    has_prev = prev is not None
    carry = carried is not None
    args = [xin]
    specs = [_tile_spec(D, b0)]
    if has_prev:
        yk, wcin = prev
        args += [yk, yk, wcin, mods, ffn_ln]
        specs += [_yk_spec(0), _yk_spec(1), _tile_spec(2 * LANES, b0), _mods_spec(layer - 1, b0),
                  _layer_spec(ffn_ln, layer - 1)]
    aliases = {}
    if carry:
        for out_idx, arr in enumerate(carried):
            aliases[len(args)] = out_idx
            args.append(arr)
            specs.append(_any_spec() if out_idx < 4 else _const_spec((NE, LANES)))
    args += [mods, mix_ln]
    specs += [_mods_spec(layer, b0), _layer_spec(mix_ln, layer)]
    for w in weights:
        args.append(w)
        specs.append(_layer_spec(w, j))
    args += [wr, rb, tri]
    specs += [_layer_spec(wr, layer), _layer_spec(rb, layer), _const_spec(tri.shape)]
    if kind == "a":
        conv_scratch = [pltpu.VMEM(((TS + HALO_A) * DT, LANES), F32), pltpu.VMEM((TS * DT, LANES), F32)]
    else:
        conv_scratch = [pltpu.VMEM((TS + HALO_B, D), F32)]
    out_shape = (
        jax.ShapeDtypeStruct((BATCH, SEQ, D), F32),
        jax.ShapeDtypeStruct((BATCH, SEQ, DP), U32),
        jax.ShapeDtypeStruct((SUBLANES, T), I32),
        jax.ShapeDtypeStruct((BATCH, SEQ, 2 * LANES), F32),
        jax.ShapeDtypeStruct((NE, LANES), I32),
    )
    out_specs = (
        _tile_spec(D, b0), _tile_spec(DP, b0),
        pl.BlockSpec((SUBLANES, TS), lambda b, s: (0, (b + b0) * NS + s)),
        _tile_spec(2 * LANES, b0),
        pl.BlockSpec((NE, LANES), lambda b, s: (0, 0)),
    )
    return pl.pallas_call(
        functools.partial(_mixer_kernel, kind=kind, has_prev=has_prev, carry=carry),
        grid=(n_batch, NS),
        in_specs=specs,
        out_specs=out_specs,
        out_shape=out_shape,
        input_output_aliases=aliases,
        scratch_shapes=conv_scratch + [pltpu.VMEM((TS, D), BF16), pltpu.VMEM((NE, LANES), F32),
                                       pltpu.VMEM(weights[0].shape[1:], BF16),
                                       pltpu.VMEM(weights[-1].shape[1:], BF16)],
        compiler_params=pltpu.CompilerParams(
            dimension_semantics=("arbitrary", "arbitrary"), vmem_limit_bytes=VMEM_LIMIT),
        name=name,
    )(*args)


def _mixer(kind, layer, xin, prev, params):
    name = f"mixer_{kind}{layer}"
    if prev is None:
        return _mixer_call(kind, layer, xin, None, params, name, 0, BATCH, None)
    yk_halves, wcin = prev
    out = None
    for hh in range(N_HALVES):
        out = _mixer_call(kind, layer, xin, (yk_halves[hh], wcin), params, f"{name}_h{hh}",
                          hh * HALF_BATCH, HALF_BATCH, out)
    return out


def _dest_kernel(ps_ref, ri_ref, o_ref):
    ri = ri_ref[...]
    e = ri[0:2, :]
    start = jnp.zeros_like(e)
    for k in range(NE):
        start = jnp.where(e == k, ps_ref[k], start)
    o_ref[0:2, :] = start + ri[2:4, :]
    o_ref[2:8, :] = jnp.zeros((6, ri.shape[1]), I32)


def _dest_slots(pad_starts, ri):
    tn = 4096
    return pl.pallas_call(
        _dest_kernel,
        grid_spec=pltpu.PrefetchScalarGridSpec(
            num_scalar_prefetch=1,
            grid=(T // tn,),
            in_specs=[pl.BlockSpec((SUBLANES, tn), lambda i, ps: (0, i))],
            out_specs=pl.BlockSpec((SUBLANES, tn), lambda i, ps: (0, i)),
        ),
        out_shape=jax.ShapeDtypeStruct((SUBLANES, T), I32),
        compiler_params=pltpu.CompilerParams(dimension_semantics=("arbitrary",)),
        name="dest_slots",
    )(pad_starts, ri)


def _expert_rows(x_packed, wgu_s, wd_s):
    x = _unpack_rows(x_packed).astype(BF16)
    gu = jnp.dot(x, wgu_s[...], preferred_element_type=F32)
    g = gu[:, :F]
    hid = (g * _sigmoid(g) * gu[:, F:]).astype(BF16)
    return _pack_rows(jnp.dot(hid, wd_s[...], preferred_element_type=F32))


def _expert_kernel(st_ref, nb_ref, xs_hbm, wg_ref, wu_ref, wd_ref, yb_hbm,
                   wgu_s, wd_s, xbuf, obuf, xtail, otail, xsem, osem, tsem):
    e = pl.program_id(0)
    start = st_ref[e]
    n_units = nb_ref[e] // 2
    has_tail = nb_ref[e] % 2 == 1

    def x_copy(first_row, u, slot):
        return pltpu.make_async_copy(xs_hbm.at[pl.ds(pl.multiple_of(first_row + u * UNIT, BM), UNIT)],
                                     xbuf.at[slot], xsem.at[slot])

    def o_copy(u, slot):
        return pltpu.make_async_copy(obuf.at[slot],
                                     yb_hbm.at[pl.ds(pl.multiple_of(start + u * UNIT, BM), UNIT)], osem.at[slot])

    def xt_copy(first_row, units_before):
        return pltpu.make_async_copy(xs_hbm.at[pl.ds(pl.multiple_of(first_row + units_before * UNIT, BM), BM)],
                                     xtail, tsem.at[0])

    def ot_copy():
        return pltpu.make_async_copy(otail, yb_hbm.at[pl.ds(pl.multiple_of(start + n_units * UNIT, BM), BM)],
                                     tsem.at[1])

    def fetch_first(first_row, units, tail):
        @pl.when(units > 0)
        def _():
            x_copy(first_row, 0, 0).start(priority=ROW_DMA_PRIORITY)

        @pl.when(tail)
        def _():
            xt_copy(first_row, units).start(priority=ROW_DMA_PRIORITY)

    @pl.when(e == 0)
    def _():
        fetch_first(start, n_units, has_tail)

    @pl.when(nb_ref[e] > 0)
    def _():
        wgu_s[:, :F] = wg_ref[0, 0].astype(BF16)
        wgu_s[:, F:] = wu_ref[0, 0].astype(BF16)
        wd_s[...] = wd_ref[0, 0].astype(BF16)

    @pl.loop(0, n_units, step=2)
    def _(u0):
        for slot in range(2):
            u = u0 + slot

            @pl.when(u < n_units)
            def _():
                x_copy(start, u, slot).wait()

                @pl.when(u + 1 < n_units)
                def _():
                    x_copy(start, u + 1, 1 - slot).start(priority=ROW_DMA_PRIORITY)

                @pl.when(u >= 2)
                def _():
                    o_copy(u - 2, slot).wait()

                obuf[slot] = _expert_rows(xbuf[slot], wgu_s, wd_s)
                o_copy(u, slot).start(priority=ROW_DMA_PRIORITY)

    @pl.when(has_tail)
    def _():
        xt_copy(start, n_units).wait()
        otail[...] = _expert_rows(xtail[...], wgu_s, wd_s)
        ot_copy().start(priority=ROW_DMA_PRIORITY)

    nxt = jnp.minimum(e + 1, NE - 1)

    @pl.when(e + 1 < NE)
    def _():
        fetch_first(st_ref[nxt], nb_ref[nxt] // 2, nb_ref[nxt] % 2 == 1)

    for back in (1, 2):
        @pl.when(n_units >= back)
        def _():
            last = n_units - back
            o_copy(last, last % 2).wait()

    @pl.when(has_tail)
    def _():
        ot_copy().wait()


def _experts(layer, starts, n_blocks, xs, w_gate, w_up, w_down, name):
    def w_map(e, st, nb):
        return (layer, e, 0, 0)

    return pl.pallas_call(
        _expert_kernel,
        grid_spec=pltpu.PrefetchScalarGridSpec(
            num_scalar_prefetch=2,
            grid=(NE,),
            in_specs=[
                pl.BlockSpec(memory_space=pl.ANY),
                pl.BlockSpec((1, 1, D, F), w_map),
                pl.BlockSpec((1, 1, D, F), w_map),
                pl.BlockSpec((1, 1, F, D), w_map),
            ],
            out_specs=pl.BlockSpec(memory_space=pl.ANY),
            scratch_shapes=[
                pltpu.VMEM((D, 2 * F), BF16), pltpu.VMEM((F, D), BF16),
                pltpu.VMEM((2, UNIT, DP), U32), pltpu.VMEM((2, UNIT, DP), U32),
                pltpu.VMEM((BM, DP), U32), pltpu.VMEM((BM, DP), U32),
                pltpu.SemaphoreType.DMA((2,)), pltpu.SemaphoreType.DMA((2,)), pltpu.SemaphoreType.DMA((2,)),
            ],
        ),
        out_shape=jax.ShapeDtypeStruct((NSLOT, DP), U32),
        compiler_params=pltpu.CompilerParams(
            dimension_semantics=("arbitrary",), vmem_limit_bytes=VMEM_LIMIT),
        name=name,
    )(starts, n_blocks, xs, w_gate, w_up, w_down)


def _sc_worker_id():
    return lax.axis_index("s") * SC_CORES + lax.axis_index("c")


def _sc_mesh():
    return plsc.VectorSubcoreMesh(core_axis_name="c", subcore_axis_name="s")


def _sc_scratch(n_index_rows):
    return [
        pltpu.VMEM((n_index_rows, SC_CHUNK), I32),
        pltpu.VMEM((2, SC_CHUNK, DP), U32),
        pltpu.SemaphoreType.DMA((2,)),
        pltpu.SemaphoreType.DMA((2,)),
    ]


def _dispatch_rows(h2p, dest):
    per_w = T // SC_WORKERS
    n_chunks = per_w // SC_CHUNK

    @functools.partial(
        pl.kernel, mesh=_sc_mesh(),
        out_type=jax.ShapeDtypeStruct((NSLOT, DP), U32),
        scratch_types=_sc_scratch(TOP_K * n_chunks),
        name="dispatch_rows",
    )
    def k(h2_hbm, dest_hbm, out_hbm, dest_v, rows_v, rsem, wsem):
        wid = _sc_worker_id()
        for kk in range(TOP_K):
            pltpu.sync_copy(dest_hbm.at[kk, pl.ds(wid * n_chunks, n_chunks)],
                            dest_v.at[pl.ds(kk * n_chunks, n_chunks)])
        base = wid * per_w

        def read(c, slot):
            return pltpu.make_async_copy(h2_hbm.at[pl.ds(base + c * SC_CHUNK, SC_CHUNK)],
                                         rows_v.at[slot], rsem.at[slot])

        def write(c, kk, slot):
            return pltpu.make_async_copy(rows_v.at[slot], out_hbm.at[dest_v.at[kk * n_chunks + c]],
                                         wsem.at[slot])

        read(0, 0).start()

        @pl.loop(0, n_chunks, step=2)
        def _(c):
            for b in range(2):
                cc = c + b
                read(cc, b).wait()

                @pl.when(cc + 1 < n_chunks)
                def _():
                    @pl.when(cc >= 1)
                    def _():
                        for kk in range(TOP_K):
                            write(cc - 1, kk, 1 - b).wait()
                    read(cc + 1, 1 - b).start()

                for kk in range(TOP_K):
                    write(cc, kk, b).start()

        for slot, cc in ((0, n_chunks - 2), (1, n_chunks - 1)):
            for kk in range(TOP_K):
                write(cc, kk, slot).wait()

    return k(h2p, dest.reshape(TOP_K, T // SC_CHUNK, SC_CHUNK))


def _return_rows(yb, dest):
    m = dest.size
    per_w = m // SC_WORKERS
    n_chunks = per_w // SC_CHUNK

    @functools.partial(
        pl.kernel, mesh=_sc_mesh(),
        out_type=jax.ShapeDtypeStruct((m, DP), U32),
        scratch_types=_sc_scratch(n_chunks),
        name="return_rows",
    )
    def k(yb_hbm, dest_hbm, out_hbm, idx_v, rows_v, gsem, wsem):
        wid = _sc_worker_id()
        pltpu.sync_copy(dest_hbm.at[pl.ds(wid * n_chunks, n_chunks)], idx_v)
        base = wid * per_w

        def gather(c, slot):
            return pltpu.make_async_copy(yb_hbm.at[idx_v.at[c]], rows_v.at[slot], gsem.at[slot])

        def write(c, slot):
            return pltpu.make_async_copy(rows_v.at[slot], out_hbm.at[pl.ds(base + c * SC_CHUNK, SC_CHUNK)],
                                         wsem.at[slot])

        gather(0, 0).start()

        @pl.loop(0, n_chunks, step=2)
        def _(c):
            for b in range(2):
                cc = c + b
                gather(cc, b).wait()

                @pl.when(cc + 1 < n_chunks)
                def _():
                    @pl.when(cc >= 1)
                    def _():
                        write(cc - 1, 1 - b).wait()
                    gather(cc + 1, 1 - b).start()

                write(cc, b).start()

        write(n_chunks - 2, 0).wait()
        write(n_chunks - 1, 1).wait()

    return k(yb, dest.reshape(m // SC_CHUNK, SC_CHUNK))


def _final_kernel(x1_ref, y0_ref, y1_ref, wc_ref, mods_ref, ln_ref, *rest):
    o_ref = rest[-1]
    ln = ln_ref[0]
    o_ref[0] = _combine(x1_ref[0], y0_ref[0], y1_ref[0], wc_ref[0], mods_ref[0, 0][5:6, :],
                        ln[0:1, :], ln[1:2, :])


def _final(x1, yk_halves, wc, params):
    mods, ffn_ln = params["mods"], params["ffn_ln"]
    out = None
    for hh in range(N_HALVES):
        b0 = hh * HALF_BATCH
        args = [x1, yk_halves[hh], yk_halves[hh], wc, mods, ffn_ln]
        specs = [_tile_spec(D, b0), _yk_spec(0), _yk_spec(1), _tile_spec(2 * LANES, b0),
                 _mods_spec(DEPTH - 1, b0), _layer_spec(ffn_ln, DEPTH - 1)]
        aliases = {}
        if out is not None:
            aliases[len(args)] = 0
            args.append(out)
            specs.append(_any_spec())
        out = pl.pallas_call(
            _final_kernel,
            grid=(HALF_BATCH, NS),
            in_specs=specs,
            out_specs=_tile_spec(D, b0),
            out_shape=jax.ShapeDtypeStruct((BATCH, SEQ, D), F32),
            input_output_aliases=aliases,
            compiler_params=pltpu.CompilerParams(
                dimension_semantics=("arbitrary", "arbitrary"), vmem_limit_bytes=VMEM_LIMIT),
            name=f"final_combine_h{hh}",
        )(*args)
    return out


def _plan(counts):
    n_blocks = (counts + BM - 1) // BM
    padded = n_blocks * BM
    pad_starts = (jnp.cumsum(padded) - padded).astype(I32)
    return pad_starts, n_blocks.astype(I32)


def _router_params(w_group, b_group, w_expert, b_expert):
    wr = jnp.zeros((DEPTH, NR, D), F32)
    wr = wr.at[:, 0:N_GROUPS].set(jnp.swapaxes(w_group, 1, 2))
    wr = wr.at[:, SUBLANES:SUBLANES + NE].set(jnp.swapaxes(w_expert, 1, 2))
    rb = jnp.full((DEPTH, NR), NEG, F32)
    rb = rb.at[:, 0:N_GROUPS].set(b_group).at[:, SUBLANES:SUBLANES + NE].set(b_expert)
    rb = rb.at[:, SUBLANES + NE:].set(0.0)
    return wr.astype(BF16), rb.reshape(DEPTH, NR, 1)


def kernel(x, c, ada_w, ada_b, a_w_in, a_b_in, a_w_dw, a_b_dw, a_ln_g, a_ln_b, a_w_out, a_b_out,
           b_w_in, b_w_dw, b_w_out, mix_ln_g, mix_ln_b, ffn_ln_g, ffn_ln_b,
           r_w_group, r_b_group, r_w_expert, r_b_expert, e_w_gate, e_w_up, e_w_down):
    n_a = a_w_in.shape[0]
    wr, rb = _router_params(r_w_group, r_b_group, r_w_expert, r_b_expert)
    params = {
        "mods": _ada_mods(c, ada_w, ada_b),
        "mix_ln": jnp.stack([mix_ln_g, mix_ln_b], axis=1),
        "ffn_ln": jnp.stack([ffn_ln_g, ffn_ln_b], axis=1),
        "wr": wr, "rb": rb,
        "tri": (jnp.arange(TS)[:, None] < jnp.arange(TS)[None, :]).astype(BF16),
        "a": [a_w_in.astype(BF16), a_b_in.reshape(n_a, 1, 2 * D), a_w_dw.reshape(n_a, CONV_A, DT, LANES),
              a_b_dw.reshape(n_a, DT, LANES), jnp.stack([a_ln_g, a_ln_b, a_b_out], axis=1),
              a_w_out.astype(BF16)],
        "b": [b_w_in.astype(BF16), b_w_dw, b_w_out.astype(BF16)],
    }
    prev = None
    xin = x
    for i in range(DEPTH):
        kind = "a" if i % 2 == 0 else "b"
        x1, h2, ri, wc, counts = _mixer(kind, i, xin, prev, params)
        pad_starts, n_blocks = _plan(counts[:, 0])
        dest = _dest_slots(pad_starts, ri)[0:TOP_K]
        xs = _dispatch_rows(h2.reshape(T, DP), dest)
        yb = _experts(i, pad_starts, n_blocks, xs, e_w_gate, e_w_up, e_w_down, name=f"experts{i}")
        yk_halves = [_return_rows(yb, dest[:, hh * HALF_T:(hh + 1) * HALF_T]).reshape(TOP_K, HALF_T, DP)
                     for hh in range(N_HALVES)]
        prev = (yk_halves, wc)
        xin = x1
    yk_halves, wc = prev
    return _final(xin, yk_halves, wc, params)
```

```python
import functools

import jax
import jax.numpy as jnp
from jax import lax
from jax.experimental import pallas as pl
from jax.experimental.pallas import tpu as pltpu
from jax.experimental.pallas import tpu_sc as plsc

F32 = jnp.float32
BF16 = jnp.bfloat16
I32 = jnp.int32
U32 = jnp.uint32

D = 1024
BATCH = 4
SEQ = 8192
T = BATCH * SEQ
DEPTH = 4
N_GROUPS = 4
EPG = 8
NE = N_GROUPS * EPG
TOP_K = 2
F = D // 2
CONV_A = 31
CONV_B = 3
ALPHA = (2.0 * DEPTH) ** 0.25
LN_EPS = 1e-5

LANES = 128
SUBLANES = 8
VMEM_LIMIT = 56 * 1024 * 1024

TS = 512
SR = 512
NS = SEQ // TS
N_HALVES = 2
HALF_BATCH = BATCH // N_HALVES
HALF_T = HALF_BATCH * SEQ
HALO_A = 32
HALO_B = 8
BM = 512
UNIT = 2 * BM
ROW_DMA_PRIORITY = 1
NSLOT = T * TOP_K + NE * BM
NR = 48
ADA_TN = 1536
NEG = -1e30
DP = D // 2
DT = D // LANES
assert DT == SUBLANES

SC_CORES = 2
SC_SUBCORES = 16
SC_WORKERS = SC_CORES * SC_SUBCORES
SC_CHUNK = 64


def _sigmoid(x):
    return 1.0 / (1.0 + jnp.exp(-x))


def _pack_rows(x):
    return pltpu.pack_elementwise([x[:, :DP], x[:, DP:]], packed_dtype=BF16)


def _unpack_rows(p):
    lo = pltpu.unpack_elementwise(p, index=0, packed_dtype=BF16, unpacked_dtype=F32)
    hi = pltpu.unpack_elementwise(p, index=1, packed_dtype=BF16, unpacked_dtype=F32)
    return jnp.concatenate([lo, hi], axis=1)


def _layer_norm(x, g, b):
    mu = jnp.mean(x, axis=-1, keepdims=True)
    xc = x - mu
    var = jnp.mean(xc * xc, axis=-1, keepdims=True)
    return xc * lax.rsqrt(var + LN_EPS) * g + b


def _ada_kernel(c_ref, w_ref, b_ref, o_ref):
    c = c_ref[...]
    ca = (c * _sigmoid(c)).astype(BF16)
    w = w_ref[0].astype(BF16)
    o_ref[0] = jnp.dot(ca, w, preferred_element_type=F32) + b_ref[0]


def _ada_mods(c, ada_w, ada_b):
    out = pl.pallas_call(
        _ada_kernel,
        grid=(DEPTH, 6 * D // ADA_TN),
        in_specs=[
            pl.BlockSpec((BATCH, D), lambda i, j: (0, 0)),
            pl.BlockSpec((1, D, ADA_TN), lambda i, j: (i, 0, j)),
            pl.BlockSpec((1, 1, ADA_TN), lambda i, j: (i, 0, j)),
        ],
        out_specs=pl.BlockSpec((1, BATCH, ADA_TN), lambda i, j: (i, 0, j)),
        out_shape=jax.ShapeDtypeStruct((DEPTH, BATCH, 6 * D), F32),
        compiler_params=pltpu.CompilerParams(
            dimension_semantics=("arbitrary", "arbitrary"), vmem_limit_bytes=VMEM_LIMIT),
        name="ada_mods",
    )(c, ada_w, ada_b.reshape(DEPTH, 1, 6 * D))
    return out.reshape(DEPTH, BATCH, 6, D)


def _combine(x1, y0p, y1p, wc, g_f, ln_g, ln_b):
    w0 = jnp.tile(wc[:, :LANES], (1, D // LANES))
    w1 = jnp.tile(wc[:, LANES:], (1, D // LANES))
    y = w0 * _unpack_rows(y0p) + w1 * _unpack_rows(y1p)
    return _layer_norm(ALPHA * x1 + (1.0 + g_f) * y, ln_g, ln_b)


def _route(h2, wr_ref, rb_ref, tri_ref, cnt_ref, ri_ref, wc_ref, cnto_ref):
    lt = lax.dot_general(wr_ref[...], h2, (((1,), (1,)), ((), ())),
                         preferred_element_type=F32) + rb_ref[...]
    iota8 = lax.broadcasted_iota(I32, (SUBLANES, TS), 0).astype(F32)
    gl = lt[0:SUBLANES]
    gmax = jnp.max(gl, axis=0, keepdims=True)
    gidx = jnp.min(jnp.where(gl == gmax, iota8, float(SUBLANES)), axis=0, keepdims=True)
    gw = 1.0 / jnp.sum(jnp.exp(gl - gmax), axis=0, keepdims=True)
    el = lt[SUBLANES:2 * SUBLANES]
    for g in range(1, N_GROUPS):
        el = jnp.where(gidx == float(g), lt[SUBLANES * (g + 1):SUBLANES * (g + 2)], el)
    m1 = jnp.max(el, axis=0, keepdims=True)
    i1 = jnp.min(jnp.where(el == m1, iota8, float(SUBLANES)), axis=0, keepdims=True)
    el2 = jnp.where(iota8 == i1, -jnp.inf, el)
    m2 = jnp.max(el2, axis=0, keepdims=True)
    i2 = jnp.min(jnp.where(el2 == m2, iota8, float(SUBLANES)), axis=0, keepdims=True)
    r = jnp.exp(m2 - m1)
    w_a = gw / (1.0 + r)
    w_b = gw * r / (1.0 + r)
    e1 = gidx * float(EPG) + i1
    e2 = gidx * float(EPG) + i2

    iota_e = lax.broadcasted_iota(I32, (NE, TS), 0).astype(F32)
    oh1 = iota_e == e1
    oh2 = iota_e == e2
    oh = jnp.concatenate([jnp.where(oh1, 1.0, 0.0), jnp.where(oh2, 1.0, 0.0)], axis=0)
    before = jnp.dot(oh.astype(BF16), tri_ref[...], preferred_element_type=F32)
    tot = jnp.sum(oh, axis=1, keepdims=True)
    cnt = cnt_ref[...]
    base = jnp.tile(cnt, (1, TS // LANES))
    tot1 = tot[:NE]
    tot2 = tot[NE:]
    rank1 = jnp.sum(jnp.where(oh1, base + before[:NE], 0.0), axis=0, keepdims=True)
    rank2 = jnp.sum(jnp.where(oh2, base + tot1 + before[NE:], 0.0), axis=0, keepdims=True)
    new_cnt = cnt + tot1 + tot2
    cnt_ref[...] = new_cnt
    cnto_ref[...] = new_cnt.astype(I32)

    ri_ref[0:1, :] = e1.astype(I32)
    ri_ref[1:2, :] = e2.astype(I32)
    ri_ref[2:3, :] = rank1.astype(I32)
    ri_ref[3:4, :] = rank2.astype(I32)
    ri_ref[4:8, :] = jnp.zeros((4, TS), I32)
    wc_ref[0, :, :LANES] = jnp.broadcast_to(w_a, (LANES, TS)).T
    wc_ref[0, :, LANES:] = jnp.broadcast_to(w_b, (LANES, TS)).T


def _conv_taps(uext_ref, u, w_dw, halo, width, r0):
    uext_ref[halo + r0:halo + r0 + SR, :] = u
    acc = None
    for k in range(width):
        off = r0 + halo - (width - 1) + k
        term = w_dw[k:k + 1, :] * uext_ref[off:off + SR, :]
        acc = term if acc is None else acc + term
    return acc


def _conv_time_major(tm_ref, o2_ref, u, wk_ref, bias, r0, width, halo):
    for j in range(DT):
        tm_ref[pl.ds((halo + r0) * DT + j, SR, stride=DT), :] = u[:, j * LANES:(j + 1) * LANES]
    acc = None
    for k in range(width):
        off = (r0 + halo - (width - 1) + k) * DT
        term = tm_ref[off:off + SR * DT, :].reshape(SR, DT, LANES) * wk_ref[k]
        acc = term if acc is None else acc + term
    if bias is not None:
        acc = acc + bias
    o2_ref[r0 * DT:(r0 + SR) * DT, :] = acc.reshape(SR * DT, LANES)
    return jnp.concatenate([o2_ref[pl.ds(r0 * DT + j, SR, stride=DT), :] for j in range(DT)], axis=1)


def _mixer_kernel(*refs, kind, has_prev, carry):
    it = iter(refs)
    xin_ref = next(it)
    if has_prev:
        y0_ref, y1_ref, wcin_ref, pmods_ref, pln_ref = (next(it) for _ in range(5))
    if carry:
        for _ in range(4):
            next(it)
        cnt_in_ref = next(it)
    mods_ref, mln_ref, win_ref = next(it), next(it), next(it)
    if kind == "a":
        bin_ref, wdw_ref, bdw_ref, vec_ref = next(it), next(it), next(it), next(it)
    else:
        wdw_ref = next(it)
    wout_ref, wr_ref, rb_ref, tri_ref = (next(it) for _ in range(4))
    x1_ref, h2_ref, ri_ref, wc_ref, cnto_ref = (next(it) for _ in range(5))
    if kind == "a":
        uext_ref, o2_ref, h2b_ref, cnt_ref = (next(it) for _ in range(4))
    else:
        uext_ref, h2b_ref, cnt_ref = (next(it) for _ in range(3))
    win_s, wout_s = next(it), next(it)
    mln_ref, win_ref, wdw_ref, wout_ref, wr_ref, rb_ref = (
        r.at[0] for r in (mln_ref, win_ref, wdw_ref, wout_ref, wr_ref, rb_ref))
    if has_prev:
        pln_ref = pln_ref.at[0]
    if kind == "a":
        bin_ref, bdw_ref, vec_ref = (r.at[0] for r in (bin_ref, bdw_ref, vec_ref))

    first = (pl.program_id(0) == 0) & (pl.program_id(1) == 0)

    @pl.when(first)
    def _():
        if carry:
            cnt_ref[...] = cnt_in_ref[...].astype(F32)
        else:
            cnt_ref[...] = jnp.zeros((NE, LANES), F32)
        win_s[...] = win_ref[...]
        wout_s[...] = wout_ref[...]

    halo_rows = HALO_A * DT if kind == "a" else HALO_B

    @pl.when(pl.program_id(1) == 0)
    def _():
        uext_ref[0:halo_rows, :] = jnp.zeros((halo_rows, uext_ref.shape[1]), F32)

    m = mods_ref[0, 0]
    mln = mln_ref[...]
    for i in range(TS // SR):
        r0 = i * SR
        rows = slice(r0, r0 + SR)
        x = xin_ref[0, rows, :]
        if has_prev:
            pln = pln_ref[...]
            x = _combine(x, y0_ref[0, rows, :], y1_ref[0, rows, :], wcin_ref[0, rows, :],
                         pmods_ref[0, 0][5:6, :], pln[0:1, :], pln[1:2, :])

        h = (x * (1.0 + m[1:2, :]) + m[0:1, :]).astype(BF16)
        cols = [jnp.dot(h, win_s[:, c * D:(c + 1) * D], preferred_element_type=F32)
                for c in range(win_s.shape[1] // D)]
        if kind == "a":
            b_in = bin_ref[...]
            vec = vec_ref[...]
            u = (cols[0] + b_in[:, :D]) * _sigmoid(cols[1] + b_in[:, D:])
            u = _conv_time_major(uext_ref, o2_ref, u, wdw_ref, bdw_ref[...], r0, CONV_A, HALO_A)
            u = _layer_norm(u, vec[0:1, :], vec[1:2, :])
            u = u * _sigmoid(u)
            y = jnp.dot(u.astype(BF16), wout_s[...], preferred_element_type=F32) + vec[2:3, :]
        else:
            gb = cols[0]
            q = cols[1] * cols[2]
            u = _conv_taps(uext_ref, q, wdw_ref[...], HALO_B, CONV_B, r0)
            y = jnp.dot((gb * u).astype(BF16), wout_s[...], preferred_element_type=F32)

        x1 = _layer_norm(ALPHA * x + (1.0 + m[2:3, :]) * y, mln[0:1, :], mln[1:2, :])
        x1_ref[0, rows, :] = x1
        h2 = x1 * (1.0 + m[4:5, :]) + m[3:4, :]
        h2_ref[0, rows, :] = _pack_rows(h2)
        h2b_ref[rows, :] = h2.astype(BF16)

    uext_ref[0:halo_rows, :] = uext_ref[uext_ref.shape[0] - halo_rows:uext_ref.shape[0], :]
    _route(h2b_ref[...], wr_ref, rb_ref, tri_ref, cnt_ref, ri_ref, wc_ref, cnto_ref)


def _tile_spec(width, b0=0):
    return pl.BlockSpec((1, TS, width), lambda b, s: (b + b0, s, 0))


def _yk_spec(k):
    return pl.BlockSpec((1, TS, DP), lambda b, s: (k, b * NS + s, 0))


def _const_spec(shape):
    nd = len(shape)
    return pl.BlockSpec(shape, lambda b, s: (0,) * nd)


def _mods_spec(layer, b0=0):
    return pl.BlockSpec((1, 1, 6, D), lambda b, s: (layer, b + b0, 0, 0))


def _layer_spec(arr, idx):
    tail = arr.shape[1:]
    return pl.BlockSpec((1,) + tail, lambda b, s: (idx,) + (0,) * len(tail))


def _any_spec():
    return pl.BlockSpec(memory_space=pl.ANY)


def _mixer_call(kind, layer, xin, prev, params, name, b0, n_batch, carried):
    mods, mix_ln, ffn_ln, wr, rb, tri = (params[k] for k in ("mods", "mix_ln", "ffn_ln", "wr", "rb", "tri"))
    weights = params[kind]
    j = layer // 2
    has_prev = prev is not None
    carry = carried is not None
    args = [xin]
    specs = [_tile_spec(D, b0)]
    if has_prev:
        yk, wcin = prev
        args += [yk, yk, wcin, mods, ffn_ln]
        specs += [_yk_spec(0), _yk_spec(1), _tile_spec(2 * LANES, b0), _mods_spec(layer - 1, b0),
                  _layer_spec(ffn_ln, layer - 1)]
    aliases = {}
    if carry:
        for out_idx, arr in enumerate(carried):
            aliases[len(args)] = out_idx
            args.append(arr)
            specs.append(_any_spec() if out_idx < 4 else _const_spec((NE, LANES)))
    args += [mods, mix_ln]
    specs += [_mods_spec(layer, b0), _layer_spec(mix_ln, layer)]
    for w in weights:
        args.append(w)
        specs.append(_layer_spec(w, j))
    args += [wr, rb, tri]
    specs += [_layer_spec(wr, layer), _layer_spec(rb, layer), _const_spec(tri.shape)]
    if kind == "a":
        conv_scratch = [pltpu.VMEM(((TS + HALO_A) * DT, LANES), F32), pltpu.VMEM((TS * DT, LANES), F32)]
    else:
        conv_scratch = [pltpu.VMEM((TS + HALO_B, D), F32)]
    out_shape = (
        jax.ShapeDtypeStruct((BATCH, SEQ, D), F32),
        jax.ShapeDtypeStruct((BATCH, SEQ, DP), U32),
        jax.ShapeDtypeStruct((SUBLANES, T), I32),
        jax.ShapeDtypeStruct((BATCH, SEQ, 2 * LANES), F32),
        jax.ShapeDtypeStruct((NE, LANES), I32),
    )
    out_specs = (
        _tile_spec(D, b0), _tile_spec(DP, b0),
        pl.BlockSpec((SUBLANES, TS), lambda b, s: (0, (b + b0) * NS + s)),
        _tile_spec(2 * LANES, b0),
        pl.BlockSpec((NE, LANES), lambda b, s: (0, 0)),
    )
    return pl.pallas_call(
        functools.partial(_mixer_kernel, kind=kind, has_prev=has_prev, carry=carry),
        grid=(n_batch, NS),
        in_specs=specs,
        out_specs=out_specs,
        out_shape=out_shape,
        input_output_aliases=aliases,
        scratch_shapes=conv_scratch + [pltpu.VMEM((TS, D), BF16), pltpu.VMEM((NE, LANES), F32),
                                       pltpu.VMEM(weights[0].shape[1:], BF16),
                                       pltpu.VMEM(weights[-1].shape[1:], BF16)],
        compiler_params=pltpu.CompilerParams(
            dimension_semantics=("arbitrary", "arbitrary"), vmem_limit_bytes=VMEM_LIMIT),
        name=name,
    )(*args)


def _mixer(kind, layer, xin, prev, params):
    name = f"mixer_{kind}{layer}"
    if prev is None:
        return _mixer_call(kind, layer, xin, None, params, name, 0, BATCH, None)
    yk_halves, wcin = prev
    out = None
    for hh in range(N_HALVES):
        out = _mixer_call(kind, layer, xin, (yk_halves[hh], wcin), params, f"{name}_h{hh}",
                          hh * HALF_BATCH, HALF_BATCH, out)
    return out


def _dest_kernel(ps_ref, ri_ref, o_ref):
    ri = ri_ref[...]
    e = ri[0:2, :]
    start = jnp.zeros_like(e)
    for k in range(NE):
        start = jnp.where(e == k, ps_ref[k], start)
    o_ref[0:2, :] = start + ri[2:4, :]
    o_ref[2:8, :] = jnp.zeros((6, ri.shape[1]), I32)


def _dest_slots(pad_starts, ri):
    tn = 4096
    return pl.pallas_call(
        _dest_kernel,
        grid_spec=pltpu.PrefetchScalarGridSpec(
            num_scalar_prefetch=1,
            grid=(T // tn,),
            in_specs=[pl.BlockSpec((SUBLANES, tn), lambda i, ps: (0, i))],
            out_specs=pl.BlockSpec((SUBLANES, tn), lambda i, ps: (0, i)),
        ),
        out_shape=jax.ShapeDtypeStruct((SUBLANES, T), I32),
        compiler_params=pltpu.CompilerParams(dimension_semantics=("arbitrary",)),
        name="dest_slots",
    )(pad_starts, ri)


def _expert_rows(x_packed, wgu_s, wd_s):
    x = _unpack_rows(x_packed).astype(BF16)
    gu = jnp.dot(x, wgu_s[...], preferred_element_type=F32)
    g = gu[:, :F]
    hid = (g * _sigmoid(g) * gu[:, F:]).astype(BF16)
    return _pack_rows(jnp.dot(hid, wd_s[...], preferred_element_type=F32))


def _expert_kernel(st_ref, nb_ref, xs_hbm, wg_ref, wu_ref, wd_ref, yb_hbm,
                   wgu_s, wd_s, xbuf, obuf, xtail, otail, xsem, osem, tsem):
    e = pl.program_id(0)
    start = st_ref[e]
    n_units = nb_ref[e] // 2
    has_tail = nb_ref[e] % 2 == 1

    def x_copy(first_row, u, slot):
        return pltpu.make_async_copy(xs_hbm.at[pl.ds(pl.multiple_of(first_row + u * UNIT, BM), UNIT)],
                                     xbuf.at[slot], xsem.at[slot])

    def o_copy(u, slot):
        return pltpu.make_async_copy(obuf.at[slot],
                                     yb_hbm.at[pl.ds(pl.multiple_of(start + u * UNIT, BM), UNIT)], osem.at[slot])

    def xt_copy(first_row, units_before):
        return pltpu.make_async_copy(xs_hbm.at[pl.ds(pl.multiple_of(first_row + units_before * UNIT, BM), BM)],
                                     xtail, tsem.at[0])

    def ot_copy():
        return pltpu.make_async_copy(otail, yb_hbm.at[pl.ds(pl.multiple_of(start + n_units * UNIT, BM), BM)],
                                     tsem.at[1])

    def fetch_first(first_row, units, tail):
        @pl.when(units > 0)
        def _():
            x_copy(first_row, 0, 0).start(priority=ROW_DMA_PRIORITY)

        @pl.when(tail)
        def _():
            xt_copy(first_row, units).start(priority=ROW_DMA_PRIORITY)

    @pl.when(e == 0)
    def _():
        fetch_first(start, n_units, has_tail)

    @pl.when(nb_ref[e] > 0)
    def _():
        wgu_s[:, :F] = wg_ref[0, 0].astype(BF16)
        wgu_s[:, F:] = wu_ref[0, 0].astype(BF16)
        wd_s[...] = wd_ref[0, 0].astype(BF16)

    @pl.loop(0, n_units, step=2)
    def _(u0):
        for slot in range(2):
            u = u0 + slot

            @pl.when(u < n_units)
            def _():
                x_copy(start, u, slot).wait()

                @pl.when(u + 1 < n_units)
                def _():
                    x_copy(start, u + 1, 1 - slot).start(priority=ROW_DMA_PRIORITY)

                @pl.when(u >= 2)
                def _():
                    o_copy(u - 2, slot).wait()

                obuf[slot] = _expert_rows(xbuf[slot], wgu_s, wd_s)
                o_copy(u, slot).start(priority=ROW_DMA_PRIORITY)

    @pl.when(has_tail)
    def _():
        xt_copy(start, n_units).wait()
        otail[...] = _expert_rows(xtail[...], wgu_s, wd_s)
        ot_copy().start(priority=ROW_DMA_PRIORITY)

    nxt = jnp.minimum(e + 1, NE - 1)

    @pl.when(e + 1 < NE)
    def _():
        fetch_first(st_ref[nxt], nb_ref[nxt] // 2, nb_ref[nxt] % 2 == 1)

    for back in (1, 2):
        @pl.when(n_units >= back)
        def _():
            last = n_units - back
            o_copy(last, last % 2).wait()

    @pl.when(has_tail)
    def _():
        ot_copy().wait()


def _experts(layer, starts, n_blocks, xs, w_gate, w_up, w_down, name):
    def w_map(e, st, nb):
        return (layer, e, 0, 0)

    return pl.pallas_call(
        _expert_kernel,
        grid_spec=pltpu.PrefetchScalarGridSpec(
            num_scalar_prefetch=2,
            grid=(NE,),
            in_specs=[
                pl.BlockSpec(memory_space=pl.ANY),
                pl.BlockSpec((1, 1, D, F), w_map),
                pl.BlockSpec((1, 1, D, F), w_map),
                pl.BlockSpec((1, 1, F, D), w_map),
            ],
            out_specs=pl.BlockSpec(memory_space=pl.ANY),
            scratch_shapes=[
                pltpu.VMEM((D, 2 * F), BF16), pltpu.VMEM((F, D), BF16),
                pltpu.VMEM((2, UNIT, DP), U32), pltpu.VMEM((2, UNIT, DP), U32),
                pltpu.VMEM((BM, DP), U32), pltpu.VMEM((BM, DP), U32),
                pltpu.SemaphoreType.DMA((2,)), pltpu.SemaphoreType.DMA((2,)), pltpu.SemaphoreType.DMA((2,)),
            ],
        ),
        out_shape=jax.ShapeDtypeStruct((NSLOT, DP), U32),
        compiler_params=pltpu.CompilerParams(
            dimension_semantics=("arbitrary",), vmem_limit_bytes=VMEM_LIMIT),
        name=name,
    )(starts, n_blocks, xs, w_gate, w_up, w_down)


def _sc_worker_id():
    return lax.axis_index("s") * SC_CORES + lax.axis_index("c")


def _sc_mesh():
    return plsc.VectorSubcoreMesh(core_axis_name="c", subcore_axis_name="s")


def _sc_scratch(n_index_rows):
    return [
        pltpu.VMEM((n_index_rows, SC_CHUNK), I32),
        pltpu.VMEM((2, SC_CHUNK, DP), U32),
        pltpu.SemaphoreType.DMA((2,)),
        pltpu.SemaphoreType.DMA((2,)),
    ]


def _dispatch_rows(h2p, dest):
    per_w = T // SC_WORKERS
    n_chunks = per_w // SC_CHUNK

    @functools.partial(
        pl.kernel, mesh=_sc_mesh(),
        out_type=jax.ShapeDtypeStruct((NSLOT, DP), U32),
        scratch_types=_sc_scratch(TOP_K * n_chunks),
        name="dispatch_rows",
    )
    def k(h2_hbm, dest_hbm, out_hbm, dest_v, rows_v, rsem, wsem):
        wid = _sc_worker_id()
        for kk in range(TOP_K):
            pltpu.sync_copy(dest_hbm.at[kk, pl.ds(wid * n_chunks, n_chunks)],
                            dest_v.at[pl.ds(kk * n_chunks, n_chunks)])
        base = wid * per_w

        def read(c, slot):
            return pltpu.make_async_copy(h2_hbm.at[pl.ds(base + c * SC_CHUNK, SC_CHUNK)],
                                         rows_v.at[slot], rsem.at[slot])

        def write(c, kk, slot):
            return pltpu.make_async_copy(rows_v.at[slot], out_hbm.at[dest_v.at[kk * n_chunks + c]],
                                         wsem.at[slot])

        read(0, 0).start()

        @pl.loop(0, n_chunks, step=2)
        def _(c):
            for b in range(2):
                cc = c + b
                read(cc, b).wait()

                @pl.when(cc + 1 < n_chunks)
                def _():
                    @pl.when(cc >= 1)
                    def _():
                        for kk in range(TOP_K):
                            write(cc - 1, kk, 1 - b).wait()
                    read(cc + 1, 1 - b).start()

                for kk in range(TOP_K):
                    write(cc, kk, b).start()

        for slot, cc in ((0, n_chunks - 2), (1, n_chunks - 1)):
            for kk in range(TOP_K):
                write(cc, kk, slot).wait()

    return k(h2p, dest.reshape(TOP_K, T // SC_CHUNK, SC_CHUNK))


def _return_rows(yb, dest):
    m = dest.size
    per_w = m // SC_WORKERS
    n_chunks = per_w // SC_CHUNK

    @functools.partial(
        pl.kernel, mesh=_sc_mesh(),
        out_type=jax.ShapeDtypeStruct((m, DP), U32),
        scratch_types=_sc_scratch(n_chunks),
        name="return_rows",
    )
    def k(yb_hbm, dest_hbm, out_hbm, idx_v, rows_v, gsem, wsem):
        wid = _sc_worker_id()
        pltpu.sync_copy(dest_hbm.at[pl.ds(wid * n_chunks, n_chunks)], idx_v)
        base = wid * per_w

        def gather(c, slot):
            return pltpu.make_async_copy(yb_hbm.at[idx_v.at[c]], rows_v.at[slot], gsem.at[slot])

        def write(c, slot):
            return pltpu.make_async_copy(rows_v.at[slot], out_hbm.at[pl.ds(base + c * SC_CHUNK, SC_CHUNK)],
                                         wsem.at[slot])

        gather(0, 0).start()

        @pl.loop(0, n_chunks, step=2)
        def _(c):
            for b in range(2):
                cc = c + b
                gather(cc, b).wait()

                @pl.when(cc + 1 < n_chunks)
                def _():
                    @pl.when(cc >= 1)
                    def _():
                        write(cc - 1, 1 - b).wait()
                    gather(cc + 1, 1 - b).start()

                write(cc, b).start()

        write(n_chunks - 2, 0).wait()
        write(n_chunks - 1, 1).wait()

    return k(yb, dest.reshape(m // SC_CHUNK, SC_CHUNK))


def _final_kernel(x1_ref, y0_ref, y1_ref, wc_ref, mods_ref, ln_ref, *rest):
    o_ref = rest[-1]
    ln = ln_ref[0]
    o_ref[0] = _combine(x1_ref[0], y0_ref[0], y1_ref[0], wc_ref[0], mods_ref[0, 0][5:6, :],
                        ln[0:1, :], ln[1:2, :])


def _final(x1, yk_halves, wc, params):
    mods, ffn_ln = params["mods"], params["ffn_ln"]
    out = None
    for hh in range(N_HALVES):
        b0 = hh * HALF_BATCH
        args = [x1, yk_halves[hh], yk_halves[hh], wc, mods, ffn_ln]
        specs = [_tile_spec(D, b0), _yk_spec(0), _yk_spec(1), _tile_spec(2 * LANES, b0),
                 _mods_spec(DEPTH - 1, b0), _layer_spec(ffn_ln, DEPTH - 1)]
        aliases = {}
        if out is not None:
            aliases[len(args)] = 0
            args.append(out)
            specs.append(_any_spec())
        out = pl.pallas_call(
            _final_kernel,
            grid=(HALF_BATCH, NS),
            in_specs=specs,
            out_specs=_tile_spec(D, b0),
            out_shape=jax.ShapeDtypeStruct((BATCH, SEQ, D), F32),
            input_output_aliases=aliases,
            compiler_params=pltpu.CompilerParams(
                dimension_semantics=("arbitrary", "arbitrary"), vmem_limit_bytes=VMEM_LIMIT),
            name=f"final_combine_h{hh}",
        )(*args)
    return out


def _plan(counts):
    n_blocks = (counts + BM - 1) // BM
    padded = n_blocks * BM
    pad_starts = (jnp.cumsum(padded) - padded).astype(I32)
    return pad_starts, n_blocks.astype(I32)


def _router_params(w_group, b_group, w_expert, b_expert):
    wr = jnp.zeros((DEPTH, NR, D), F32)
    wr = wr.at[:, 0:N_GROUPS].set(jnp.swapaxes(w_group, 1, 2))
    wr = wr.at[:, SUBLANES:SUBLANES + NE].set(jnp.swapaxes(w_expert, 1, 2))
    rb = jnp.full((DEPTH, NR), NEG, F32)
    rb = rb.at[:, 0:N_GROUPS].set(b_group).at[:, SUBLANES:SUBLANES + NE].set(b_expert)
    rb = rb.at[:, SUBLANES + NE:].set(0.0)
    return wr.astype(BF16), rb.reshape(DEPTH, NR, 1)


def kernel(x, c, ada_w, ada_b, a_w_in, a_b_in, a_w_dw, a_b_dw, a_ln_g, a_ln_b, a_w_out, a_b_out,
           b_w_in, b_w_dw, b_w_out, mix_ln_g, mix_ln_b, ffn_ln_g, ffn_ln_b,
           r_w_group, r_b_group, r_w_expert, r_b_expert, e_w_gate, e_w_up, e_w_down):
    n_a = a_w_in.shape[0]
    wr, rb = _router_params(r_w_group, r_b_group, r_w_expert, r_b_expert)
    params = {
        "mods": _ada_mods(c, ada_w, ada_b),
        "mix_ln": jnp.stack([mix_ln_g, mix_ln_b], axis=1),
        "ffn_ln": jnp.stack([ffn_ln_g, ffn_ln_b], axis=1),
        "wr": wr, "rb": rb,
        "tri": (jnp.arange(TS)[:, None] < jnp.arange(TS)[None, :]).astype(BF16),
        "a": [a_w_in.astype(BF16), a_b_in.reshape(n_a, 1, 2 * D), a_w_dw.reshape(n_a, CONV_A, DT, LANES),
              a_b_dw.reshape(n_a, DT, LANES), jnp.stack([a_ln_g, a_ln_b, a_b_out], axis=1),
              a_w_out.astype(BF16)],
        "b": [b_w_in.astype(BF16), b_w_dw, b_w_out.astype(BF16)],
    }
    prev = None
    xin = x
    for i in range(DEPTH):
        kind = "a" if i % 2 == 0 else "b"
        x1, h2, ri, wc, counts = _mixer(kind, i, xin, prev, params)
        pad_starts, n_blocks = _plan(counts[:, 0])
        dest = _dest_slots(pad_starts, ri)[0:TOP_K]
        xs = _dispatch_rows(h2.reshape(T, DP), dest)
        yb = _experts(i, pad_starts, n_blocks, xs, e_w_gate, e_w_up, e_w_down, name=f"experts{i}")
        yk_halves = [_return_rows(yb, dest[:, hh * HALF_T:(hh + 1) * HALF_T]).reshape(TOP_K, HALF_T, DP)
                     for hh in range(N_HALVES)]
        prev = (yk_halves, wc)
        xin = x1
    yk_halves, wc = prev
    return _final(xin, yk_halves, wc, params)
```

```python
import functools

import jax
import jax.numpy as jnp
from jax import lax
from jax.experimental import pallas as pl
from jax.experimental.pallas import tpu as pltpu
from jax.experimental.pallas import tpu_sc as plsc

F32 = jnp.float32
BF16 = jnp.bfloat16
I32 = jnp.int32
U32 = jnp.uint32

D = 1024
BATCH = 4
SEQ = 8192
T = BATCH * SEQ
DEPTH = 4
N_GROUPS = 4
EPG = 8
NE = N_GROUPS * EPG
TOP_K = 2
F = D // 2
CONV_A = 31
CONV_B = 3
ALPHA = (2.0 * DEPTH) ** 0.25
LN_EPS = 1e-5

LANES = 128
SUBLANES = 8
VMEM_LIMIT = 56 * 1024 * 1024

TS = 512
SR = 512
NS = SEQ // TS
N_HALVES = 2
HALF_BATCH = BATCH // N_HALVES
HALF_T = HALF_BATCH * SEQ
HALO_A = 32
HALO_B = 8
BM = 512
UNIT = 2 * BM
ROW_DMA_PRIORITY = 1
NSLOT = T * TOP_K + NE * BM
NR = 48
ADA_TN = 1536
NEG = -1e30
DP = D // 2
DT = D // LANES
assert DT == SUBLANES

SC_CORES = 2
SC_SUBCORES = 16
SC_WORKERS = SC_CORES * SC_SUBCORES
SC_CHUNK = 64


def _sigmoid(x):
    return 1.0 / (1.0 + jnp.exp(-x))


def _pack_rows(x):
    return pltpu.pack_elementwise([x[:, :DP], x[:, DP:]], packed_dtype=BF16)


def _unpack_rows(p):
    lo = pltpu.unpack_elementwise(p, index=0, packed_dtype=BF16, unpacked_dtype=F32)
    hi = pltpu.unpack_elementwise(p, index=1, packed_dtype=BF16, unpacked_dtype=F32)
    return jnp.concatenate([lo, hi], axis=1)


def _layer_norm(x, g, b):
    mu = jnp.mean(x, axis=-1, keepdims=True)
    xc = x - mu
    var = jnp.mean(xc * xc, axis=-1, keepdims=True)
    return xc * lax.rsqrt(var + LN_EPS) * g + b


def _ada_kernel(c_ref, w_ref, b_ref, o_ref):
    c = c_ref[...]
    ca = (c * _sigmoid(c)).astype(BF16)
    w = w_ref[0].astype(BF16)
    o_ref[0] = jnp.dot(ca, w, preferred_element_type=F32) + b_ref[0]


def _ada_mods(c, ada_w, ada_b):
    out = pl.pallas_call(
        _ada_kernel,
        grid=(DEPTH, 6 * D // ADA_TN),
        in_specs=[
            pl.BlockSpec((BATCH, D), lambda i, j: (0, 0)),
            pl.BlockSpec((1, D, ADA_TN), lambda i, j: (i, 0, j)),
            pl.BlockSpec((1, 1, ADA_TN), lambda i, j: (i, 0, j)),
        ],
        out_specs=pl.BlockSpec((1, BATCH, ADA_TN), lambda i, j: (i, 0, j)),
        out_shape=jax.ShapeDtypeStruct((DEPTH, BATCH, 6 * D), F32),
        compiler_params=pltpu.CompilerParams(
            dimension_semantics=("arbitrary", "arbitrary"), vmem_limit_bytes=VMEM_LIMIT),
        name="ada_mods",
    )(c, ada_w, ada_b.reshape(DEPTH, 1, 6 * D))
    return out.reshape(DEPTH, BATCH, 6, D)


def _combine(x1, y0p, y1p, wc, g_f, ln_g, ln_b):
    w0 = jnp.tile(wc[:, :LANES], (1, D // LANES))
    w1 = jnp.tile(wc[:, LANES:], (1, D // LANES))
    y = w0 * _unpack_rows(y0p) + w1 * _unpack_rows(y1p)
    return _layer_norm(ALPHA * x1 + (1.0 + g_f) * y, ln_g, ln_b)


def _route(h2, wr_ref, rb_ref, tri_ref, cnt_ref, ri_ref, wc_ref, cnto_ref):
    lt = lax.dot_general(wr_ref[...], h2, (((1,), (1,)), ((), ())),
                         preferred_element_type=F32) + rb_ref[...]
    iota8 = lax.broadcasted_iota(I32, (SUBLANES, TS), 0).astype(F32)
    gl = lt[0:SUBLANES]
    gmax = jnp.max(gl, axis=0, keepdims=True)
    gidx = jnp.min(jnp.where(gl == gmax, iota8, float(SUBLANES)), axis=0, keepdims=True)
    gw = 1.0 / jnp.sum(jnp.exp(gl - gmax), axis=0, keepdims=True)
    el = lt[SUBLANES:2 * SUBLANES]
    for g in range(1, N_GROUPS):
        el = jnp.where(gidx == float(g), lt[SUBLANES * (g + 1):SUBLANES * (g + 2)], el)
    m1 = jnp.max(el, axis=0, keepdims=True)
    i1 = jnp.min(jnp.where(el == m1, iota8, float(SUBLANES)), axis=0, keepdims=True)
    el2 = jnp.where(iota8 == i1, -jnp.inf, el)
    m2 = jnp.max(el2, axis=0, keepdims=True)
    i2 = jnp.min(jnp.where(el2 == m2, iota8, float(SUBLANES)), axis=0, keepdims=True)
    r = jnp.exp(m2 - m1)
    w_a = gw / (1.0 + r)
    w_b = gw * r / (1.0 + r)
    e1 = gidx * float(EPG) + i1
    e2 = gidx * float(EPG) + i2

    iota_e = lax.broadcasted_iota(I32, (NE, TS), 0).astype(F32)
    oh1 = iota_e == e1
    oh2 = iota_e == e2
    oh = jnp.concatenate([jnp.where(oh1, 1.0, 0.0), jnp.where(oh2, 1.0, 0.0)], axis=0)
    before = jnp.dot(oh.astype(BF16), tri_ref[...], preferred_element_type=F32)
    tot = jnp.sum(oh, axis=1, keepdims=True)
    cnt = cnt_ref[...]
    base = jnp.tile(cnt, (1, TS // LANES))
    tot1 = tot[:NE]
    tot2 = tot[NE:]
    rank1 = jnp.sum(jnp.where(oh1, base + before[:NE], 0.0), axis=0, keepdims=True)
    rank2 = jnp.sum(jnp.where(oh2, base + tot1 + before[NE:], 0.0), axis=0, keepdims=True)
    new_cnt = cnt + tot1 + tot2
    cnt_ref[...] = new_cnt
    cnto_ref[...] = new_cnt.astype(I32)

    ri_ref[0:1, :] = e1.astype(I32)
    ri_ref[1:2, :] = e2.astype(I32)
    ri_ref[2:3, :] = rank1.astype(I32)
    ri_ref[3:4, :] = rank2.astype(I32)
    ri_ref[4:8, :] = jnp.zeros((4, TS), I32)
    wc_ref[0, :, :LANES] = jnp.broadcast_to(w_a, (LANES, TS)).T
    wc_ref[0, :, LANES:] = jnp.broadcast_to(w_b, (LANES, TS)).T


def _conv_taps(uext_ref, u, w_dw, halo, width, r0):
    uext_ref[halo + r0:halo + r0 + SR, :] = u
    acc = None
    for k in range(width):
        off = r0 + halo - (width - 1) + k
        term = w_dw[k:k + 1, :] * uext_ref[off:off + SR, :]
        acc = term if acc is None else acc + term
    return acc


def _conv_time_major(tm_ref, o2_ref, u, wk_ref, bias, r0, width, halo):
    for j in range(DT):
        tm_ref[pl.ds((halo + r0) * DT + j, SR, stride=DT), :] = u[:, j * LANES:(j + 1) * LANES]
    acc = None
    for k in range(width):
        off = (r0 + halo - (width - 1) + k) * DT
        term = tm_ref[off:off + SR * DT, :].reshape(SR, DT, LANES) * wk_ref[k]
        acc = term if acc is None else acc + term
    if bias is not None:
        acc = acc + bias
    o2_ref[r0 * DT:(r0 + SR) * DT, :] = acc.reshape(SR * DT, LANES)
    return jnp.concatenate([o2_ref[pl.ds(r0 * DT + j, SR, stride=DT), :] for j in range(DT)], axis=1)


def _mixer_kernel(*refs, kind, has_prev, carry):
    it = iter(refs)
    xin_ref = next(it)
    if has_prev:
        y0_ref, y1_ref, wcin_ref, pmods_ref, pln_ref = (next(it) for _ in range(5))
    if carry:
        for _ in range(4):
            next(it)
        cnt_in_ref = next(it)
    mods_ref, mln_ref, win_ref = next(it), next(it), next(it)
    if kind == "a":
        bin_ref, wdw_ref, bdw_ref, vec_ref = next(it), next(it), next(it), next(it)
    else:
        wdw_ref = next(it)
    wout_ref, wr_ref, rb_ref, tri_ref = (next(it) for _ in range(4))
    x1_ref, h2_ref, ri_ref, wc_ref, cnto_ref = (next(it) for _ in range(5))
    if kind == "a":
        uext_ref, o2_ref, h2b_ref, cnt_ref = (next(it) for _ in range(4))
    else:
        uext_ref, h2b_ref, cnt_ref = (next(it) for _ in range(3))
    win_s, wout_s = next(it), next(it)
    mln_ref, win_ref, wdw_ref, wout_ref, wr_ref, rb_ref = (
        r.at[0] for r in (mln_ref, win_ref, wdw_ref, wout_ref, wr_ref, rb_ref))
    if has_prev:
        pln_ref = pln_ref.at[0]
    if kind == "a":
        bin_ref, bdw_ref, vec_ref = (r.at[0] for r in (bin_ref, bdw_ref, vec_ref))

    first = (pl.program_id(0) == 0) & (pl.program_id(1) == 0)

    @pl.when(first)
    def _():
        if carry:
            cnt_ref[...] = cnt_in_ref[...].astype(F32)
        else:
            cnt_ref[...] = jnp.zeros((NE, LANES), F32)
        win_s[...] = win_ref[...]
        wout_s[...] = wout_ref[...]

    halo_rows = HALO_A * DT if kind == "a" else HALO_B

    @pl.when(pl.program_id(1) == 0)
    def _():
        uext_ref[0:halo_rows, :] = jnp.zeros((halo_rows, uext_ref.shape[1]), F32)

    m = mods_ref[0, 0]
    mln = mln_ref[...]
    for i in range(TS // SR):
        r0 = i * SR
        rows = slice(r0, r0 + SR)
        x = xin_ref[0, rows, :]
        if has_prev:
            pln = pln_ref[...]
            x = _combine(x, y0_ref[0, rows, :], y1_ref[0, rows, :], wcin_ref[0, rows, :],
                         pmods_ref[0, 0][5:6, :], pln[0:1, :], pln[1:2, :])

        h = (x * (1.0 + m[1:2, :]) + m[0:1, :]).astype(BF16)
        cols = [jnp.dot(h, win_s[:, c * D:(c + 1) * D], preferred_element_type=F32)
                for c in range(win_s.shape[1] // D)]
        if kind == "a":
            b_in = bin_ref[...]
            vec = vec_ref[...]
            u = (cols[0] + b_in[:, :D]) * _sigmoid(cols[1] + b_in[:, D:])
            u = _conv_time_major(uext_ref, o2_ref, u, wdw_ref, bdw_ref[...], r0, CONV_A, HALO_A)
            u = _layer_norm(u, vec[0:1, :], vec[1:2, :])
            u = u * _sigmoid(u)
            y = jnp.dot(u.astype(BF16), wout_s[...], preferred_element_type=F32) + vec[2:3, :]
        else:
            gb = cols[0]
            q = cols[1] * cols[2]
            u = _conv_taps(uext_ref, q, wdw_ref[...], HALO_B, CONV_B, r0)
            y = jnp.dot((gb * u).astype(BF16), wout_s[...], preferred_element_type=F32)

        x1 = _layer_norm(ALPHA * x + (1.0 + m[2:3, :]) * y, mln[0:1, :], mln[1:2, :])
        x1_ref[0, rows, :] = x1
        h2 = x1 * (1.0 + m[4:5, :]) + m[3:4, :]
        h2_ref[0, rows, :] = _pack_rows(h2)
        h2b_ref[rows, :] = h2.astype(BF16)

    uext_ref[0:halo_rows, :] = uext_ref[uext_ref.shape[0] - halo_rows:uext_ref.shape[0], :]
    _route(h2b_ref[...], wr_ref, rb_ref, tri_ref, cnt_ref, ri_ref, wc_ref, cnto_ref)


def _tile_spec(width, b0=0):
    return pl.BlockSpec((1, TS, width), lambda b, s: (b + b0, s, 0))


def _yk_spec(k):
    return pl.BlockSpec((1, TS, DP), lambda b, s: (k, b * NS + s, 0))


def _const_spec(shape):
    nd = len(shape)
    return pl.BlockSpec(shape, lambda b, s: (0,) * nd)


def _mods_spec(layer, b0=0):
    return pl.BlockSpec((1, 1, 6, D), lambda b, s: (layer, b + b0, 0, 0))


def _layer_spec(arr, idx):
    tail = arr.shape[1:]
    return pl.BlockSpec((1,) + tail, lambda b, s: (idx,) + (0,) * len(tail))


def _any_spec():
    return pl.BlockSpec(memory_space=pl.ANY)


def _mixer_call(kind, layer, xin, prev, params, name, b0, n_batch, carried):
    mods, mix_ln, ffn_ln, wr, rb, tri = (params[k] for k in ("mods", "mix_ln", "ffn_ln", "wr", "rb", "tri"))
    weights = params[kind]
    j = layer // 2
    has_prev = prev is not None
    carry = carried is not None
    args = [xin]
    specs = [_tile_spec(D, b0)]
    if has_prev:
        yk, wcin = prev
        args += [yk, yk, wcin, mods, ffn_ln]
        specs += [_yk_spec(0), _yk_spec(1), _tile_spec(2 * LANES, b0), _mods_spec(layer - 1, b0),
                  _layer_spec(ffn_ln, layer - 1)]
    aliases = {}
    if carry:
        for out_idx, arr in enumerate(carried):
            aliases[len(args)] = out_idx
            args.append(arr)
            specs.append(_any_spec() if out_idx < 4 else _const_spec((NE, LANES)))
    args += [mods, mix_ln]
    specs += [_mods_spec(layer, b0), _layer_spec(mix_ln, layer)]
    for w in weights:
        args.append(w)
        specs.append(_layer_spec(w, j))
    args += [wr, rb, tri]
    specs += [_layer_spec(wr, layer), _layer_spec(rb, layer), _const_spec(tri.shape)]
    if kind == "a":
        conv_scratch = [pltpu.VMEM(((TS + HALO_A) * DT, LANES), F32), pltpu.VMEM((TS * DT, LANES), F32)]
    else:
        conv_scratch = [pltpu.VMEM((TS + HALO_B, D), F32)]
    out_shape = (
        jax.ShapeDtypeStruct((BATCH, SEQ, D), F32),
        jax.ShapeDtypeStruct((BATCH, SEQ, DP), U32),
        jax.ShapeDtypeStruct((SUBLANES, T), I32),
        jax.ShapeDtypeStruct((BATCH, SEQ, 2 * LANES), F32),
        jax.ShapeDtypeStruct((NE, LANES), I32),
    )
    out_specs = (
        _tile_spec(D, b0), _tile_spec(DP, b0),
        pl.BlockSpec((SUBLANES, TS), lambda b, s: (0, (b + b0) * NS + s)),
        _tile_spec(2 * LANES, b0),
        pl.BlockSpec((NE, LANES), lambda b, s: (0, 0)),
    )
    return pl.pallas_call(
        functools.partial(_mixer_kernel, kind=kind, has_prev=has_prev, carry=carry),
        grid=(n_batch, NS),
        in_specs=specs,
        out_specs=out_specs,
        out_shape=out_shape,
        input_output_aliases=aliases,
        scratch_shapes=conv_scratch + [pltpu.VMEM((TS, D), BF16), pltpu.VMEM((NE, LANES), F32),
                                       pltpu.VMEM(weights[0].shape[1:], BF16),
                                       pltpu.VMEM(weights[-1].shape[1:], BF16)],
        compiler_params=pltpu.CompilerParams(
            dimension_semantics=("arbitrary", "arbitrary"), vmem_limit_bytes=VMEM_LIMIT),
        name=name,
    )(*args)


def _mixer(kind, layer, xin, prev, params):
    name = f"mixer_{kind}{layer}"
    if prev is None:
        return _mixer_call(kind, layer, xin, None, params, name, 0, BATCH, None)
    yk_halves, wcin = prev
    out = None
    for hh in range(N_HALVES):
        out = _mixer_call(kind, layer, xin, (yk_halves[hh], wcin), params, f"{name}_h{hh}",
                          hh * HALF_BATCH, HALF_BATCH, out)
    return out


def _dest_kernel(ps_ref, ri_ref, o_ref):
    ri = ri_ref[...]
    e = ri[0:2, :]
    start = jnp.zeros_like(e)
    for k in range(NE):
        start = jnp.where(e == k, ps_ref[k], start)
    o_ref[0:2, :] = start + ri[2:4, :]
    o_ref[2:8, :] = jnp.zeros((6, ri.shape[1]), I32)


def _dest_slots(pad_starts, ri):
    tn = 4096
    return pl.pallas_call(
        _dest_kernel,
        grid_spec=pltpu.PrefetchScalarGridSpec(
            num_scalar_prefetch=1,
            grid=(T // tn,),
            in_specs=[pl.BlockSpec((SUBLANES, tn), lambda i, ps: (0, i))],
            out_specs=pl.BlockSpec((SUBLANES, tn), lambda i, ps: (0, i)),
        ),
        out_shape=jax.ShapeDtypeStruct((SUBLANES, T), I32),
        compiler_params=pltpu.CompilerParams(dimension_semantics=("arbitrary",)),
        name="dest_slots",
    )(pad_starts, ri)


def _expert_rows(x_packed, wgu_s, wd_s):
    x = _unpack_rows(x_packed).astype(BF16)
    gu = jnp.dot(x, wgu_s[...], preferred_element_type=F32)
    g = gu[:, :F]
    hid = (g * _sigmoid(g) * gu[:, F:]).astype(BF16)
    return _pack_rows(jnp.dot(hid, wd_s[...], preferred_element_type=F32))


def _expert_kernel(st_ref, nb_ref, xs_hbm, wg_ref, wu_ref, wd_ref, yb_hbm,
                   wgu_s, wd_s, xbuf, obuf, xtail, otail, pend, xsem, osem, tsem):
    e = pl.program_id(0)
    start = st_ref[e]
    n_units = nb_ref[e] // 2
    has_tail = nb_ref[e] % 2 == 1

    def x_copy(first_row, u, slot):
        return pltpu.make_async_copy(xs_hbm.at[pl.ds(pl.multiple_of(first_row + u * UNIT, BM), UNIT)],
                                     xbuf.at[slot], xsem.at[slot])

    def aligned(row):
        return row if isinstance(row, int) else pl.multiple_of(row, BM)

    def o_copy(first_row, u, slot):
        return pltpu.make_async_copy(obuf.at[slot], yb_hbm.at[pl.ds(aligned(first_row + u * UNIT), UNIT)],
                                     osem.at[slot])

    def xt_copy(first_row, units_before):
        return pltpu.make_async_copy(xs_hbm.at[pl.ds(pl.multiple_of(first_row + units_before * UNIT, BM), BM)],
                                     xtail, tsem.at[0])

    def ot_copy(first_row, units_before):
        return pltpu.make_async_copy(otail, yb_hbm.at[pl.ds(aligned(first_row + units_before * UNIT), BM)],
                                     tsem.at[1])

    def fetch_first(first_row, units, tail):
        @pl.when(units > 0)
        def _():
            x_copy(first_row, 0, 0).start(priority=ROW_DMA_PRIORITY)

        @pl.when(tail)
        def _():
            xt_copy(first_row, units).start(priority=ROW_DMA_PRIORITY)

    def release(b):
        @pl.when(pend[b] == 1)
        def _():
            (ot_copy(0, 0) if b == 2 else o_copy(0, 0, b)).wait()
            pend[b] = 0

    @pl.when(e == 0)
    def _():
        for b in range(3):
            pend[b] = 0
        fetch_first(start, n_units, has_tail)

    @pl.when(nb_ref[e] > 0)
    def _():
        wgu_s[:, :F] = wg_ref[0, 0].astype(BF16)
        wgu_s[:, F:] = wu_ref[0, 0].astype(BF16)
        wd_s[...] = wd_ref[0, 0].astype(BF16)

    @pl.loop(0, n_units, step=2)
    def _(u0):
        for slot in range(2):
            u = u0 + slot

            @pl.when(u < n_units)
            def _():
                x_copy(start, u, slot).wait()

                @pl.when(u + 1 < n_units)
                def _():
                    x_copy(start, u + 1, 1 - slot).start(priority=ROW_DMA_PRIORITY)

                release(slot)
                obuf[slot] = _expert_rows(xbuf[slot], wgu_s, wd_s)
                o_copy(start, u, slot).start(priority=ROW_DMA_PRIORITY)
                pend[slot] = 1

    @pl.when(has_tail)
    def _():
        xt_copy(start, n_units).wait()
        release(2)
        otail[...] = _expert_rows(xtail[...], wgu_s, wd_s)
        ot_copy(start, n_units).start(priority=ROW_DMA_PRIORITY)
        pend[2] = 1

    nxt = jnp.minimum(e + 1, NE - 1)

    @pl.when(e + 1 < NE)
    def _():
        fetch_first(st_ref[nxt], nb_ref[nxt] // 2, nb_ref[nxt] % 2 == 1)

    @pl.when(e == NE - 1)
    def _():
        for b in range(3):
            release(b)


def _experts(layer, starts, n_blocks, xs, w_gate, w_up, w_down, name):
    def w_map(e, st, nb):
        return (layer, e, 0, 0)

    return pl.pallas_call(
        _expert_kernel,
        grid_spec=pltpu.PrefetchScalarGridSpec(
            num_scalar_prefetch=2,
            grid=(NE,),
            in_specs=[
                pl.BlockSpec(memory_space=pl.ANY),
                pl.BlockSpec((1, 1, D, F), w_map),
                pl.BlockSpec((1, 1, D, F), w_map),
                pl.BlockSpec((1, 1, F, D), w_map),
            ],
            out_specs=pl.BlockSpec(memory_space=pl.ANY),
            scratch_shapes=[
                pltpu.VMEM((D, 2 * F), BF16), pltpu.VMEM((F, D), BF16),
                pltpu.VMEM((2, UNIT, DP), U32), pltpu.VMEM((2, UNIT, DP), U32),
                pltpu.VMEM((BM, DP), U32), pltpu.VMEM((BM, DP), U32),
                pltpu.SMEM((3,), I32),
                pltpu.SemaphoreType.DMA((2,)), pltpu.SemaphoreType.DMA((2,)), pltpu.SemaphoreType.DMA((2,)),
            ],
        ),
        out_shape=jax.ShapeDtypeStruct((NSLOT, DP), U32),
        compiler_params=pltpu.CompilerParams(
            dimension_semantics=("arbitrary",), vmem_limit_bytes=VMEM_LIMIT),
        name=name,
    )(starts, n_blocks, xs, w_gate, w_up, w_down)


def _sc_worker_id():
    return lax.axis_index("s") * SC_CORES + lax.axis_index("c")


def _sc_mesh():
    return plsc.VectorSubcoreMesh(core_axis_name="c", subcore_axis_name="s")


def _sc_scratch(n_index_rows):
    return [
        pltpu.VMEM((n_index_rows, SC_CHUNK), I32),
        pltpu.VMEM((2, SC_CHUNK, DP), U32),
        pltpu.SemaphoreType.DMA((2,)),
        pltpu.SemaphoreType.DMA((2,)),
    ]


def _dispatch_rows(h2p, dest):
    per_w = T // SC_WORKERS
    n_chunks = per_w // SC_CHUNK

    @functools.partial(
        pl.kernel, mesh=_sc_mesh(),
        out_type=jax.ShapeDtypeStruct((NSLOT, DP), U32),
        scratch_types=_sc_scratch(TOP_K * n_chunks),
        name="dispatch_rows",
    )
    def k(h2_hbm, dest_hbm, out_hbm, dest_v, rows_v, rsem, wsem):
        wid = _sc_worker_id()
        for kk in range(TOP_K):
            pltpu.sync_copy(dest_hbm.at[kk, pl.ds(wid * n_chunks, n_chunks)],
                            dest_v.at[pl.ds(kk * n_chunks, n_chunks)])
        base = wid * per_w

        def read(c, slot):
            return pltpu.make_async_copy(h2_hbm.at[pl.ds(base + c * SC_CHUNK, SC_CHUNK)],
                                         rows_v.at[slot], rsem.at[slot])

        def write(c, kk, slot):
            return pltpu.make_async_copy(rows_v.at[slot], out_hbm.at[dest_v.at[kk * n_chunks + c]],
                                         wsem.at[slot])

        read(0, 0).start()

        @pl.loop(0, n_chunks, step=2)
        def _(c):
            for b in range(2):
                cc = c + b
                read(cc, b).wait()

                @pl.when(cc + 1 < n_chunks)
                def _():
                    @pl.when(cc >= 1)
                    def _():
                        for kk in range(TOP_K):
                            write(cc - 1, kk, 1 - b).wait()
                    read(cc + 1, 1 - b).start()

                for kk in range(TOP_K):
                    write(cc, kk, b).start()

        for slot, cc in ((0, n_chunks - 2), (1, n_chunks - 1)):
            for kk in range(TOP_K):
                write(cc, kk, slot).wait()

    return k(h2p, dest.reshape(TOP_K, T // SC_CHUNK, SC_CHUNK))


def _return_rows(yb, dest):
    m = dest.size
    per_w = m // SC_WORKERS
    n_chunks = per_w // SC_CHUNK

    @functools.partial(
        pl.kernel, mesh=_sc_mesh(),
        out_type=jax.ShapeDtypeStruct((m, DP), U32),
        scratch_types=_sc_scratch(n_chunks),
        name="return_rows",
    )
    def k(yb_hbm, dest_hbm, out_hbm, idx_v, rows_v, gsem, wsem):
        wid = _sc_worker_id()
        pltpu.sync_copy(dest_hbm.at[pl.ds(wid * n_chunks, n_chunks)], idx_v)
        base = wid * per_w

        def gather(c, slot):
            return pltpu.make_async_copy(yb_hbm.at[idx_v.at[c]], rows_v.at[slot], gsem.at[slot])

        def write(c, slot):
            return pltpu.make_async_copy(rows_v.at[slot], out_hbm.at[pl.ds(base + c * SC_CHUNK, SC_CHUNK)],
                                         wsem.at[slot])

        gather(0, 0).start()

        @pl.loop(0, n_chunks, step=2)
        def _(c):
            for b in range(2):
                cc = c + b
                gather(cc, b).wait()

                @pl.when(cc + 1 < n_chunks)
                def _():
                    @pl.when(cc >= 1)
                    def _():
                        write(cc - 1, 1 - b).wait()
                    gather(cc + 1, 1 - b).start()

                write(cc, b).start()

        write(n_chunks - 2, 0).wait()
        write(n_chunks - 1, 1).wait()

    return k(yb, dest.reshape(m // SC_CHUNK, SC_CHUNK))


def _final_kernel(x1_ref, y0_ref, y1_ref, wc_ref, mods_ref, ln_ref, *rest):
    o_ref = rest[-1]
    ln = ln_ref[0]
    o_ref[0] = _combine(x1_ref[0], y0_ref[0], y1_ref[0], wc_ref[0], mods_ref[0, 0][5:6, :],
                        ln[0:1, :], ln[1:2, :])


def _final(x1, yk_halves, wc, params):
    mods, ffn_ln = params["mods"], params["ffn_ln"]
    out = None
    for hh in range(N_HALVES):
        b0 = hh * HALF_BATCH
        args = [x1, yk_halves[hh], yk_halves[hh], wc, mods, ffn_ln]
        specs = [_tile_spec(D, b0), _yk_spec(0), _yk_spec(1), _tile_spec(2 * LANES, b0),
                 _mods_spec(DEPTH - 1, b0), _layer_spec(ffn_ln, DEPTH - 1)]
        aliases = {}
        if out is not None:
            aliases[len(args)] = 0
            args.append(out)
            specs.append(_any_spec())
        out = pl.pallas_call(
            _final_kernel,
            grid=(HALF_BATCH, NS),
            in_specs=specs,
            out_specs=_tile_spec(D, b0),
            out_shape=jax.ShapeDtypeStruct((BATCH, SEQ, D), F32),
            input_output_aliases=aliases,
            compiler_params=pltpu.CompilerParams(
                dimension_semantics=("arbitrary", "arbitrary"), vmem_limit_bytes=VMEM_LIMIT),
            name=f"final_combine_h{hh}",
        )(*args)
    return out


def _plan(counts):
    n_blocks = (counts + BM - 1) // BM
    padded = n_blocks * BM
    pad_starts = (jnp.cumsum(padded) - padded).astype(I32)
    return pad_starts, n_blocks.astype(I32)


def _router_params(w_group, b_group, w_expert, b_expert):
    wr = jnp.zeros((DEPTH, NR, D), F32)
    wr = wr.at[:, 0:N_GROUPS].set(jnp.swapaxes(w_group, 1, 2))
    wr = wr.at[:, SUBLANES:SUBLANES + NE].set(jnp.swapaxes(w_expert, 1, 2))
    rb = jnp.full((DEPTH, NR), NEG, F32)
    rb = rb.at[:, 0:N_GROUPS].set(b_group).at[:, SUBLANES:SUBLANES + NE].set(b_expert)
    rb = rb.at[:, SUBLANES + NE:].set(0.0)
    return wr.astype(BF16), rb.reshape(DEPTH, NR, 1)


def kernel(x, c, ada_w, ada_b, a_w_in, a_b_in, a_w_dw, a_b_dw, a_ln_g, a_ln_b, a_w_out, a_b_out,
           b_w_in, b_w_dw, b_w_out, mix_ln_g, mix_ln_b, ffn_ln_g, ffn_ln_b,
           r_w_group, r_b_group, r_w_expert, r_b_expert, e_w_gate, e_w_up, e_w_down):
    n_a = a_w_in.shape[0]
    wr, rb = _router_params(r_w_group, r_b_group, r_w_expert, r_b_expert)
    params = {
        "mods": _ada_mods(c, ada_w, ada_b),
        "mix_ln": jnp.stack([mix_ln_g, mix_ln_b], axis=1),
        "ffn_ln": jnp.stack([ffn_ln_g, ffn_ln_b], axis=1),
        "wr": wr, "rb": rb,
        "tri": (jnp.arange(TS)[:, None] < jnp.arange(TS)[None, :]).astype(BF16),
        "a": [a_w_in.astype(BF16), a_b_in.reshape(n_a, 1, 2 * D), a_w_dw.reshape(n_a, CONV_A, DT, LANES),
              a_b_dw.reshape(n_a, DT, LANES), jnp.stack([a_ln_g, a_ln_b, a_b_out], axis=1),
              a_w_out.astype(BF16)],
        "b": [b_w_in.astype(BF16), b_w_dw, b_w_out.astype(BF16)],
    }
    prev = None
    xin = x
    for i in range(DEPTH):
        kind = "a" if i % 2 == 0 else "b"
        x1, h2, ri, wc, counts = _mixer(kind, i, xin, prev, params)
        pad_starts, n_blocks = _plan(counts[:, 0])
        dest = _dest_slots(pad_starts, ri)[0:TOP_K]
        xs = _dispatch_rows(h2.reshape(T, DP), dest)
        yb = _experts(i, pad_starts, n_blocks, xs, e_w_gate, e_w_up, e_w_down, name=f"experts{i}")
        yk_halves = [_return_rows(yb, dest[:, hh * HALF_T:(hh + 1) * HALF_T]).reshape(TOP_K, HALF_T, DP)
                     for hh in range(N_HALVES)]
        prev = (yk_halves, wc)
        xin = x1
    yk_halves, wc = prev
    return _final(xin, yk_halves, wc, params)
```

```python
import functools

import jax
import jax.numpy as jnp
from jax import lax
from jax.experimental import pallas as pl
from jax.experimental.pallas import tpu as pltpu
from jax.experimental.pallas import tpu_sc as plsc

F32 = jnp.float32
BF16 = jnp.bfloat16
I32 = jnp.int32
U32 = jnp.uint32

D = 1024
BATCH = 4
SEQ = 8192
T = BATCH * SEQ
DEPTH = 4
N_GROUPS = 4
EPG = 8
NE = N_GROUPS * EPG
TOP_K = 2
F = D // 2
CONV_A = 31
CONV_B = 3
ALPHA = (2.0 * DEPTH) ** 0.25
LN_EPS = 1e-5

LANES = 128
SUBLANES = 8
VMEM_LIMIT = 56 * 1024 * 1024

TS = 512
SR = 512
NS = SEQ // TS
N_HALVES = 2
HALF_BATCH = BATCH // N_HALVES
HALF_T = HALF_BATCH * SEQ
HALO_A = 32
HALO_B = 8
BM = 256
BLOCKS_PER_UNIT = 4
UNIT = BLOCKS_PER_UNIT * BM
TAILS = (2 * BM, BM)
ROW_DMA_PRIORITY = 1
NSLOT = T * TOP_K + NE * BM
NR = 48
ADA_TN = 1536
NEG = -1e30
DP = D // 2
DT = D // LANES
assert DT == SUBLANES

SC_CORES = 2
SC_SUBCORES = 16
SC_WORKERS = SC_CORES * SC_SUBCORES
SC_CHUNK = 64


def _sigmoid(x):
    return 1.0 / (1.0 + jnp.exp(-x))


def _pack_rows(x):
    return pltpu.pack_elementwise([x[:, :DP], x[:, DP:]], packed_dtype=BF16)


def _unpack_rows(p):
    lo = pltpu.unpack_elementwise(p, index=0, packed_dtype=BF16, unpacked_dtype=F32)
    hi = pltpu.unpack_elementwise(p, index=1, packed_dtype=BF16, unpacked_dtype=F32)
    return jnp.concatenate([lo, hi], axis=1)


def _layer_norm(x, g, b):
    mu = jnp.mean(x, axis=-1, keepdims=True)
    xc = x - mu
    var = jnp.mean(xc * xc, axis=-1, keepdims=True)
    return xc * lax.rsqrt(var + LN_EPS) * g + b


def _ada_kernel(c_ref, w_ref, b_ref, o_ref):
    c = c_ref[...]
    ca = (c * _sigmoid(c)).astype(BF16)
    w = w_ref[0].astype(BF16)
    o_ref[0] = jnp.dot(ca, w, preferred_element_type=F32) + b_ref[0]


def _ada_mods(c, ada_w, ada_b):
    out = pl.pallas_call(
        _ada_kernel,
        grid=(DEPTH, 6 * D // ADA_TN),
        in_specs=[
            pl.BlockSpec((BATCH, D), lambda i, j: (0, 0)),
            pl.BlockSpec((1, D, ADA_TN), lambda i, j: (i, 0, j)),
            pl.BlockSpec((1, 1, ADA_TN), lambda i, j: (i, 0, j)),
        ],
        out_specs=pl.BlockSpec((1, BATCH, ADA_TN), lambda i, j: (i, 0, j)),
        out_shape=jax.ShapeDtypeStruct((DEPTH, BATCH, 6 * D), F32),
        compiler_params=pltpu.CompilerParams(
            dimension_semantics=("arbitrary", "arbitrary"), vmem_limit_bytes=VMEM_LIMIT),
        name="ada_mods",
    )(c, ada_w, ada_b.reshape(DEPTH, 1, 6 * D))
    return out.reshape(DEPTH, BATCH, 6, D)


def _combine(x1, y0p, y1p, wc, g_f, ln_g, ln_b):
    w0 = jnp.tile(wc[:, :LANES], (1, D // LANES))
    w1 = jnp.tile(wc[:, LANES:], (1, D // LANES))
    y = w0 * _unpack_rows(y0p) + w1 * _unpack_rows(y1p)
    return _layer_norm(ALPHA * x1 + (1.0 + g_f) * y, ln_g, ln_b)


def _route(h2, wr_ref, rb_ref, tri_ref, cnt_ref, ri_ref, wc_ref, cnto_ref):
    lt = lax.dot_general(wr_ref[...], h2, (((1,), (1,)), ((), ())),
                         preferred_element_type=F32) + rb_ref[...]
    iota8 = lax.broadcasted_iota(I32, (SUBLANES, TS), 0).astype(F32)
    gl = lt[0:SUBLANES]
    gmax = jnp.max(gl, axis=0, keepdims=True)
    gidx = jnp.min(jnp.where(gl == gmax, iota8, float(SUBLANES)), axis=0, keepdims=True)
    gw = 1.0 / jnp.sum(jnp.exp(gl - gmax), axis=0, keepdims=True)
    el = lt[SUBLANES:2 * SUBLANES]
    for g in range(1, N_GROUPS):
        el = jnp.where(gidx == float(g), lt[SUBLANES * (g + 1):SUBLANES * (g + 2)], el)
    m1 = jnp.max(el, axis=0, keepdims=True)
    i1 = jnp.min(jnp.where(el == m1, iota8, float(SUBLANES)), axis=0, keepdims=True)
    el2 = jnp.where(iota8 == i1, -jnp.inf, el)
    m2 = jnp.max(el2, axis=0, keepdims=True)
    i2 = jnp.min(jnp.where(el2 == m2, iota8, float(SUBLANES)), axis=0, keepdims=True)
    r = jnp.exp(m2 - m1)
    w_a = gw / (1.0 + r)
    w_b = gw * r / (1.0 + r)
    e1 = gidx * float(EPG) + i1
    e2 = gidx * float(EPG) + i2

    iota_e = lax.broadcasted_iota(I32, (NE, TS), 0).astype(F32)
    oh1 = iota_e == e1
    oh2 = iota_e == e2
    oh = jnp.concatenate([jnp.where(oh1, 1.0, 0.0), jnp.where(oh2, 1.0, 0.0)], axis=0)
    before = jnp.dot(oh.astype(BF16), tri_ref[...], preferred_element_type=F32)
    tot = jnp.sum(oh, axis=1, keepdims=True)
    cnt = cnt_ref[...]
    base = jnp.tile(cnt, (1, TS // LANES))
    tot1 = tot[:NE]
    tot2 = tot[NE:]
    rank1 = jnp.sum(jnp.where(oh1, base + before[:NE], 0.0), axis=0, keepdims=True)
    rank2 = jnp.sum(jnp.where(oh2, base + tot1 + before[NE:], 0.0), axis=0, keepdims=True)
    new_cnt = cnt + tot1 + tot2
    cnt_ref[...] = new_cnt
    cnto_ref[...] = new_cnt.astype(I32)

    ri_ref[0:1, :] = e1.astype(I32)
    ri_ref[1:2, :] = e2.astype(I32)
    ri_ref[2:3, :] = rank1.astype(I32)
    ri_ref[3:4, :] = rank2.astype(I32)
    ri_ref[4:8, :] = jnp.zeros((4, TS), I32)
    wc_ref[0, :, :LANES] = jnp.broadcast_to(w_a, (LANES, TS)).T
    wc_ref[0, :, LANES:] = jnp.broadcast_to(w_b, (LANES, TS)).T


def _conv_taps(uext_ref, u, w_dw, halo, width, r0):
    uext_ref[halo + r0:halo + r0 + SR, :] = u
    acc = None
    for k in range(width):
        off = r0 + halo - (width - 1) + k
        term = w_dw[k:k + 1, :] * uext_ref[off:off + SR, :]
        acc = term if acc is None else acc + term
    return acc


def _conv_time_major(tm_ref, o2_ref, u, wk_ref, bias, r0, width, halo):
    for j in range(DT):
        tm_ref[pl.ds((halo + r0) * DT + j, SR, stride=DT), :] = u[:, j * LANES:(j + 1) * LANES]
    acc = None
    for k in range(width):
        off = (r0 + halo - (width - 1) + k) * DT
        term = tm_ref[off:off + SR * DT, :].reshape(SR, DT, LANES) * wk_ref[k]
        acc = term if acc is None else acc + term
    if bias is not None:
        acc = acc + bias
    o2_ref[r0 * DT:(r0 + SR) * DT, :] = acc.reshape(SR * DT, LANES)
    return jnp.concatenate([o2_ref[pl.ds(r0 * DT + j, SR, stride=DT), :] for j in range(DT)], axis=1)


def _mixer_kernel(*refs, kind, has_prev, carry):
    it = iter(refs)
    xin_ref = next(it)
    if has_prev:
        y0_ref, y1_ref, wcin_ref, pmods_ref, pln_ref = (next(it) for _ in range(5))
    if carry:
        for _ in range(4):
            next(it)
        cnt_in_ref = next(it)
    mods_ref, mln_ref, win_ref = next(it), next(it), next(it)
    if kind == "a":
        bin_ref, wdw_ref, bdw_ref, vec_ref = next(it), next(it), next(it), next(it)
    else:
        wdw_ref = next(it)
    wout_ref, wr_ref, rb_ref, tri_ref = (next(it) for _ in range(4))
    x1_ref, h2_ref, ri_ref, wc_ref, cnto_ref = (next(it) for _ in range(5))
    if kind == "a":
        uext_ref, o2_ref, h2b_ref, cnt_ref = (next(it) for _ in range(4))
    else:
        uext_ref, h2b_ref, cnt_ref = (next(it) for _ in range(3))
    win_s, wout_s = next(it), next(it)
    mln_ref, win_ref, wdw_ref, wout_ref, wr_ref, rb_ref = (
        r.at[0] for r in (mln_ref, win_ref, wdw_ref, wout_ref, wr_ref, rb_ref))
    if has_prev:
        pln_ref = pln_ref.at[0]
    if kind == "a":
        bin_ref, bdw_ref, vec_ref = (r.at[0] for r in (bin_ref, bdw_ref, vec_ref))

    first = (pl.program_id(0) == 0) & (pl.program_id(1) == 0)

    @pl.when(first)
    def _():
        if carry:
            cnt_ref[...] = cnt_in_ref[...].astype(F32)
        else:
            cnt_ref[...] = jnp.zeros((NE, LANES), F32)
        win_s[...] = win_ref[...]
        wout_s[...] = wout_ref[...]

    halo_rows = HALO_A * DT if kind == "a" else HALO_B

    @pl.when(pl.program_id(1) == 0)
    def _():
        uext_ref[0:halo_rows, :] = jnp.zeros((halo_rows, uext_ref.shape[1]), F32)

    m = mods_ref[0, 0]
    mln = mln_ref[...]
    for i in range(TS // SR):
        r0 = i * SR
        rows = slice(r0, r0 + SR)
        x = xin_ref[0, rows, :]
        if has_prev:
            pln = pln_ref[...]
            x = _combine(x, y0_ref[0, rows, :], y1_ref[0, rows, :], wcin_ref[0, rows, :],
                         pmods_ref[0, 0][5:6, :], pln[0:1, :], pln[1:2, :])

        h = (x * (1.0 + m[1:2, :]) + m[0:1, :]).astype(BF16)
        cols = [jnp.dot(h, win_s[:, c * D:(c + 1) * D], preferred_element_type=F32)
                for c in range(win_s.shape[1] // D)]
        if kind == "a":
            b_in = bin_ref[...]
            vec = vec_ref[...]
            u = (cols[0] + b_in[:, :D]) * _sigmoid(cols[1] + b_in[:, D:])
            u = _conv_time_major(uext_ref, o2_ref, u, wdw_ref, bdw_ref[...], r0, CONV_A, HALO_A)
            u = _layer_norm(u, vec[0:1, :], vec[1:2, :])
            u = u * _sigmoid(u)
            y = jnp.dot(u.astype(BF16), wout_s[...], preferred_element_type=F32) + vec[2:3, :]
        else:
            gb = cols[0]
            q = cols[1] * cols[2]
            u = _conv_taps(uext_ref, q, wdw_ref[...], HALO_B, CONV_B, r0)
            y = jnp.dot((gb * u).astype(BF16), wout_s[...], preferred_element_type=F32)

        x1 = _layer_norm(ALPHA * x + (1.0 + m[2:3, :]) * y, mln[0:1, :], mln[1:2, :])
        x1_ref[0, rows, :] = x1
        h2 = x1 * (1.0 + m[4:5, :]) + m[3:4, :]
        h2_ref[0, rows, :] = _pack_rows(h2)
        h2b_ref[rows, :] = h2.astype(BF16)

    uext_ref[0:halo_rows, :] = uext_ref[uext_ref.shape[0] - halo_rows:uext_ref.shape[0], :]
    _route(h2b_ref[...], wr_ref, rb_ref, tri_ref, cnt_ref, ri_ref, wc_ref, cnto_ref)


def _tile_spec(width, b0=0):
    return pl.BlockSpec((1, TS, width), lambda b, s: (b + b0, s, 0))


def _yk_spec(k):
    return pl.BlockSpec((1, TS, DP), lambda b, s: (k, b * NS + s, 0))


def _const_spec(shape):
    nd = len(shape)
    return pl.BlockSpec(shape, lambda b, s: (0,) * nd)


def _mods_spec(layer, b0=0):
    return pl.BlockSpec((1, 1, 6, D), lambda b, s: (layer, b + b0, 0, 0))


def _layer_spec(arr, idx):
    tail = arr.shape[1:]
    return pl.BlockSpec((1,) + tail, lambda b, s: (idx,) + (0,) * len(tail))


def _any_spec():
    return pl.BlockSpec(memory_space=pl.ANY)


def _mixer_call(kind, layer, xin, prev, params, name, b0, n_batch, carried):
    mods, mix_ln, ffn_ln, wr, rb, tri = (params[k] for k in ("mods", "mix_ln", "ffn_ln", "wr", "rb", "tri"))
    weights = params[kind]
    j = layer // 2
    has_prev = prev is not None
    carry = carried is not None
    args = [xin]
    specs = [_tile_spec(D, b0)]
    if has_prev:
        yk, wcin = prev
        args += [yk, yk, wcin, mods, ffn_ln]
        specs += [_yk_spec(0), _yk_spec(1), _tile_spec(2 * LANES, b0), _mods_spec(layer - 1, b0),
                  _layer_spec(ffn_ln, layer - 1)]
    aliases = {}
    if carry:
        for out_idx, arr in enumerate(carried):
            aliases[len(args)] = out_idx
            args.append(arr)
            specs.append(_any_spec() if out_idx < 4 else _const_spec((NE, LANES)))
    args += [mods, mix_ln]
    specs += [_mods_spec(layer, b0), _layer_spec(mix_ln, layer)]
    for w in weights:
        args.append(w)
        specs.append(_layer_spec(w, j))
    args += [wr, rb, tri]
    specs += [_layer_spec(wr, layer), _layer_spec(rb, layer), _const_spec(tri.shape)]
    if kind == "a":
        conv_scratch = [pltpu.VMEM(((TS + HALO_A) * DT, LANES), F32), pltpu.VMEM((TS * DT, LANES), F32)]
    else:
        conv_scratch = [pltpu.VMEM((TS + HALO_B, D), F32)]
    out_shape = (
        jax.ShapeDtypeStruct((BATCH, SEQ, D), F32),
        jax.ShapeDtypeStruct((BATCH, SEQ, DP), U32),
        jax.ShapeDtypeStruct((SUBLANES, T), I32),
        jax.ShapeDtypeStruct((BATCH, SEQ, 2 * LANES), F32),
        jax.ShapeDtypeStruct((NE, LANES), I32),
    )
    out_specs = (
        _tile_spec(D, b0), _tile_spec(DP, b0),
        pl.BlockSpec((SUBLANES, TS), lambda b, s: (0, (b + b0) * NS + s)),
        _tile_spec(2 * LANES, b0),
        pl.BlockSpec((NE, LANES), lambda b, s: (0, 0)),
    )
    return pl.pallas_call(
        functools.partial(_mixer_kernel, kind=kind, has_prev=has_prev, carry=carry),
        grid=(n_batch, NS),
        in_specs=specs,
        out_specs=out_specs,
        out_shape=out_shape,
        input_output_aliases=aliases,
        scratch_shapes=conv_scratch + [pltpu.VMEM((TS, D), BF16), pltpu.VMEM((NE, LANES), F32),
                                       pltpu.VMEM(weights[0].shape[1:], BF16),
                                       pltpu.VMEM(weights[-1].shape[1:], BF16)],
        compiler_params=pltpu.CompilerParams(
            dimension_semantics=("arbitrary", "arbitrary"), vmem_limit_bytes=VMEM_LIMIT),
        name=name,
    )(*args)


def _mixer(kind, layer, xin, prev, params):
    name = f"mixer_{kind}{layer}"
    if prev is None:
        return _mixer_call(kind, layer, xin, None, params, name, 0, BATCH, None)
    yk_halves, wcin = prev
    out = None
    for hh in range(N_HALVES):
        out = _mixer_call(kind, layer, xin, (yk_halves[hh], wcin), params, f"{name}_h{hh}",
                          hh * HALF_BATCH, HALF_BATCH, out)
    return out


def _dest_kernel(ps_ref, ri_ref, o_ref):
    ri = ri_ref[...]
    e = ri[0:2, :]
    start = jnp.zeros_like(e)
    for k in range(NE):
        start = jnp.where(e == k, ps_ref[k], start)
    o_ref[0:2, :] = start + ri[2:4, :]
    o_ref[2:8, :] = jnp.zeros((6, ri.shape[1]), I32)


def _dest_slots(pad_starts, ri):
    tn = 4096
    return pl.pallas_call(
        _dest_kernel,
        grid_spec=pltpu.PrefetchScalarGridSpec(
            num_scalar_prefetch=1,
            grid=(T // tn,),
            in_specs=[pl.BlockSpec((SUBLANES, tn), lambda i, ps: (0, i))],
            out_specs=pl.BlockSpec((SUBLANES, tn), lambda i, ps: (0, i)),
        ),
        out_shape=jax.ShapeDtypeStruct((SUBLANES, T), I32),
        compiler_params=pltpu.CompilerParams(dimension_semantics=("arbitrary",)),
        name="dest_slots",
    )(pad_starts, ri)


def _expert_rows(x_packed, wgu_s, wd_s):
    x = _unpack_rows(x_packed).astype(BF16)
    gu = jnp.dot(x, wgu_s[...], preferred_element_type=F32)
    g = gu[:, :F]
    hid = (g * _sigmoid(g) * gu[:, F:]).astype(BF16)
    return _pack_rows(jnp.dot(hid, wd_s[...], preferred_element_type=F32))


def _expert_kernel(st_ref, nb_ref, xs_hbm, wg_ref, wu_ref, wd_ref, yb_hbm,
                   wgu_s, wd_s, xbuf, obuf, xt0, ot0, xt1, ot1, xsem, osem, tsem):
    e = pl.program_id(0)
    start = st_ref[e]
    xtails, otails = (xt0, xt1), (ot0, ot1)

    def split(n_blocks):
        units = n_blocks // BLOCKS_PER_UNIT
        rem = n_blocks % BLOCKS_PER_UNIT
        on = (rem >= 2, rem % 2 == 1)
        off0 = units * UNIT
        off1 = off0 + jnp.where(on[0], TAILS[0], 0)
        return units, on, (off0, off1)

    n_units, tail_on, tail_off = split(nb_ref[e])

    def x_copy(first_row, u, slot):
        return pltpu.make_async_copy(xs_hbm.at[pl.ds(pl.multiple_of(first_row + u * UNIT, BM), UNIT)],
                                     xbuf.at[slot], xsem.at[slot])

    def o_copy(u, slot):
        return pltpu.make_async_copy(obuf.at[slot],
                                     yb_hbm.at[pl.ds(pl.multiple_of(start + u * UNIT, BM), UNIT)], osem.at[slot])

    def xt_copy(k, first_row, off):
        return pltpu.make_async_copy(xs_hbm.at[pl.ds(pl.multiple_of(first_row + off, BM), TAILS[k])],
                                     xtails[k], tsem.at[2 * k])

    def ot_copy(k):
        return pltpu.make_async_copy(otails[k], yb_hbm.at[pl.ds(pl.multiple_of(start + tail_off[k], BM), TAILS[k])],
                                     tsem.at[2 * k + 1])

    def fetch_first(first_row, n_blocks):
        units, on, off = split(n_blocks)

        @pl.when(units > 0)
        def _():
            x_copy(first_row, 0, 0).start(priority=ROW_DMA_PRIORITY)

        for k in range(len(TAILS)):
            @pl.when(on[k])
            def _():
                xt_copy(k, first_row, off[k]).start(priority=ROW_DMA_PRIORITY)

    @pl.when(e == 0)
    def _():
        fetch_first(start, nb_ref[e])

    @pl.when(nb_ref[e] > 0)
    def _():
        wgu_s[:, :F] = wg_ref[0, 0].astype(BF16)
        wgu_s[:, F:] = wu_ref[0, 0].astype(BF16)
        wd_s[...] = wd_ref[0, 0].astype(BF16)

    @pl.loop(0, n_units, step=2)
    def _(u0):
        for slot in range(2):
            u = u0 + slot

            @pl.when(u < n_units)
            def _():
                x_copy(start, u, slot).wait()

                @pl.when(u + 1 < n_units)
                def _():
                    x_copy(start, u + 1, 1 - slot).start(priority=ROW_DMA_PRIORITY)

                @pl.when(u >= 2)
                def _():
                    o_copy(u - 2, slot).wait()

                obuf[slot] = _expert_rows(xbuf[slot], wgu_s, wd_s)
                o_copy(u, slot).start(priority=ROW_DMA_PRIORITY)

    for k in range(len(TAILS)):
        @pl.when(tail_on[k])
        def _():
            xt_copy(k, start, tail_off[k]).wait()
            otails[k][...] = _expert_rows(xtails[k][...], wgu_s, wd_s)
            ot_copy(k).start(priority=ROW_DMA_PRIORITY)

    nxt = jnp.minimum(e + 1, NE - 1)

    @pl.when(e + 1 < NE)
    def _():
        fetch_first(st_ref[nxt], nb_ref[nxt])

    for back in (1, 2):
        @pl.when(n_units >= back)
        def _():
            last = n_units - back
            o_copy(last, last % 2).wait()

    for k in range(len(TAILS)):
        @pl.when(tail_on[k])
        def _():
            ot_copy(k).wait()


def _experts(layer, starts, n_blocks, xs, w_gate, w_up, w_down, name):
    def w_map(e, st, nb):
        return (layer, e, 0, 0)

    return pl.pallas_call(
        _expert_kernel,
        grid_spec=pltpu.PrefetchScalarGridSpec(
            num_scalar_prefetch=2,
            grid=(NE,),
            in_specs=[
                pl.BlockSpec(memory_space=pl.ANY),
                pl.BlockSpec((1, 1, D, F), w_map),
                pl.BlockSpec((1, 1, D, F), w_map),
                pl.BlockSpec((1, 1, F, D), w_map),
            ],
            out_specs=pl.BlockSpec(memory_space=pl.ANY),
            scratch_shapes=[
                pltpu.VMEM((D, 2 * F), BF16), pltpu.VMEM((F, D), BF16),
                pltpu.VMEM((2, UNIT, DP), U32), pltpu.VMEM((2, UNIT, DP), U32),
                pltpu.VMEM((TAILS[0], DP), U32), pltpu.VMEM((TAILS[0], DP), U32),
                pltpu.VMEM((TAILS[1], DP), U32), pltpu.VMEM((TAILS[1], DP), U32),
                pltpu.SemaphoreType.DMA((2,)), pltpu.SemaphoreType.DMA((2,)),
                pltpu.SemaphoreType.DMA((2 * len(TAILS),)),
            ],
        ),
        out_shape=jax.ShapeDtypeStruct((NSLOT, DP), U32),
        compiler_params=pltpu.CompilerParams(
            dimension_semantics=("arbitrary",), vmem_limit_bytes=VMEM_LIMIT),
        name=name,
    )(starts, n_blocks, xs, w_gate, w_up, w_down)


def _sc_worker_id():
    return lax.axis_index("s") * SC_CORES + lax.axis_index("c")


def _sc_mesh():
    return plsc.VectorSubcoreMesh(core_axis_name="c", subcore_axis_name="s")


def _sc_scratch(n_index_rows):
    return [
        pltpu.VMEM((n_index_rows, SC_CHUNK), I32),
        pltpu.VMEM((2, SC_CHUNK, DP), U32),
        pltpu.SemaphoreType.DMA((2,)),
        pltpu.SemaphoreType.DMA((2,)),
    ]


def _dispatch_rows(h2p, dest):
    per_w = T // SC_WORKERS
    n_chunks = per_w // SC_CHUNK

    @functools.partial(
        pl.kernel, mesh=_sc_mesh(),
        out_type=jax.ShapeDtypeStruct((NSLOT, DP), U32),
        scratch_types=_sc_scratch(TOP_K * n_chunks),
        name="dispatch_rows",
    )
    def k(h2_hbm, dest_hbm, out_hbm, dest_v, rows_v, rsem, wsem):
        wid = _sc_worker_id()
        for kk in range(TOP_K):
            pltpu.sync_copy(dest_hbm.at[kk, pl.ds(wid * n_chunks, n_chunks)],
                            dest_v.at[pl.ds(kk * n_chunks, n_chunks)])
        base = wid * per_w

        def read(c, slot):
            return pltpu.make_async_copy(h2_hbm.at[pl.ds(base + c * SC_CHUNK, SC_CHUNK)],
                                         rows_v.at[slot], rsem.at[slot])

        def write(c, kk, slot):
            return pltpu.make_async_copy(rows_v.at[slot], out_hbm.at[dest_v.at[kk * n_chunks + c]],
                                         wsem.at[slot])

        read(0, 0).start()

        @pl.loop(0, n_chunks, step=2)
        def _(c):
            for b in range(2):
                cc = c + b
                read(cc, b).wait()

                @pl.when(cc + 1 < n_chunks)
                def _():
                    @pl.when(cc >= 1)
                    def _():
                        for kk in range(TOP_K):
                            write(cc - 1, kk, 1 - b).wait()
                    read(cc + 1, 1 - b).start()

                for kk in range(TOP_K):
                    write(cc, kk, b).start()

        for slot, cc in ((0, n_chunks - 2), (1, n_chunks - 1)):
            for kk in range(TOP_K):
                write(cc, kk, slot).wait()

    return k(h2p, dest.reshape(TOP_K, T // SC_CHUNK, SC_CHUNK))


def _return_rows(yb, dest):
    m = dest.size
    per_w = m // SC_WORKERS
    n_chunks = per_w // SC_CHUNK

    @functools.partial(
        pl.kernel, mesh=_sc_mesh(),
        out_type=jax.ShapeDtypeStruct((m, DP), U32),
        scratch_types=_sc_scratch(n_chunks),
        name="return_rows",
    )
    def k(yb_hbm, dest_hbm, out_hbm, idx_v, rows_v, gsem, wsem):
        wid = _sc_worker_id()
        pltpu.sync_copy(dest_hbm.at[pl.ds(wid * n_chunks, n_chunks)], idx_v)
        base = wid * per_w

        def gather(c, slot):
            return pltpu.make_async_copy(yb_hbm.at[idx_v.at[c]], rows_v.at[slot], gsem.at[slot])

        def write(c, slot):
            return pltpu.make_async_copy(rows_v.at[slot], out_hbm.at[pl.ds(base + c * SC_CHUNK, SC_CHUNK)],
                                         wsem.at[slot])

        gather(0, 0).start()

        @pl.loop(0, n_chunks, step=2)
        def _(c):
            for b in range(2):
                cc = c + b
                gather(cc, b).wait()

                @pl.when(cc + 1 < n_chunks)
                def _():
                    @pl.when(cc >= 1)
                    def _():
                        write(cc - 1, 1 - b).wait()
                    gather(cc + 1, 1 - b).start()

                write(cc, b).start()

        write(n_chunks - 2, 0).wait()
        write(n_chunks - 1, 1).wait()

    return k(yb, dest.reshape(m // SC_CHUNK, SC_CHUNK))


def _final_kernel(x1_ref, y0_ref, y1_ref, wc_ref, mods_ref, ln_ref, *rest):
    o_ref = rest[-1]
    ln = ln_ref[0]
    o_ref[0] = _combine(x1_ref[0], y0_ref[0], y1_ref[0], wc_ref[0], mods_ref[0, 0][5:6, :],
                        ln[0:1, :], ln[1:2, :])


def _final(x1, yk_halves, wc, params):
    mods, ffn_ln = params["mods"], params["ffn_ln"]
    out = None
    for hh in range(N_HALVES):
        b0 = hh * HALF_BATCH
        args = [x1, yk_halves[hh], yk_halves[hh], wc, mods, ffn_ln]
        specs = [_tile_spec(D, b0), _yk_spec(0), _yk_spec(1), _tile_spec(2 * LANES, b0),
                 _mods_spec(DEPTH - 1, b0), _layer_spec(ffn_ln, DEPTH - 1)]
        aliases = {}
        if out is not None:
            aliases[len(args)] = 0
            args.append(out)
            specs.append(_any_spec())
        out = pl.pallas_call(
            _final_kernel,
            grid=(HALF_BATCH, NS),
            in_specs=specs,
            out_specs=_tile_spec(D, b0),
            out_shape=jax.ShapeDtypeStruct((BATCH, SEQ, D), F32),
            input_output_aliases=aliases,
            compiler_params=pltpu.CompilerParams(
                dimension_semantics=("arbitrary", "arbitrary"), vmem_limit_bytes=VMEM_LIMIT),
            name=f"final_combine_h{hh}",
        )(*args)
    return out


def _plan(counts):
    n_blocks = (counts + BM - 1) // BM
    padded = n_blocks * BM
    pad_starts = (jnp.cumsum(padded) - padded).astype(I32)
    return pad_starts, n_blocks.astype(I32)


def _router_params(w_group, b_group, w_expert, b_expert):
    wr = jnp.zeros((DEPTH, NR, D), F32)
    wr = wr.at[:, 0:N_GROUPS].set(jnp.swapaxes(w_group, 1, 2))
    wr = wr.at[:, SUBLANES:SUBLANES + NE].set(jnp.swapaxes(w_expert, 1, 2))
    rb = jnp.full((DEPTH, NR), NEG, F32)
    rb = rb.at[:, 0:N_GROUPS].set(b_group).at[:, SUBLANES:SUBLANES + NE].set(b_expert)
    rb = rb.at[:, SUBLANES + NE:].set(0.0)
    return wr.astype(BF16), rb.reshape(DEPTH, NR, 1)


def kernel(x, c, ada_w, ada_b, a_w_in, a_b_in, a_w_dw, a_b_dw, a_ln_g, a_ln_b, a_w_out, a_b_out,
           b_w_in, b_w_dw, b_w_out, mix_ln_g, mix_ln_b, ffn_ln_g, ffn_ln_b,
           r_w_group, r_b_group, r_w_expert, r_b_expert, e_w_gate, e_w_up, e_w_down):
    n_a = a_w_in.shape[0]
    wr, rb = _router_params(r_w_group, r_b_group, r_w_expert, r_b_expert)
    params = {
        "mods": _ada_mods(c, ada_w, ada_b),
        "mix_ln": jnp.stack([mix_ln_g, mix_ln_b], axis=1),
        "ffn_ln": jnp.stack([ffn_ln_g, ffn_ln_b], axis=1),
        "wr": wr, "rb": rb,
        "tri": (jnp.arange(TS)[:, None] < jnp.arange(TS)[None, :]).astype(BF16),
        "a": [a_w_in.astype(BF16), a_b_in.reshape(n_a, 1, 2 * D), a_w_dw.reshape(n_a, CONV_A, DT, LANES),
              a_b_dw.reshape(n_a, DT, LANES), jnp.stack([a_ln_g, a_ln_b, a_b_out], axis=1),
              a_w_out.astype(BF16)],
        "b": [b_w_in.astype(BF16), b_w_dw, b_w_out.astype(BF16)],
    }
    prev = None
    xin = x
    for i in range(DEPTH):
        kind = "a" if i % 2 == 0 else "b"
        x1, h2, ri, wc, counts = _mixer(kind, i, xin, prev, params)
        pad_starts, n_blocks = _plan(counts[:, 0])
        dest = _dest_slots(pad_starts, ri)[0:TOP_K]
        xs = _dispatch_rows(h2.reshape(T, DP), dest)
        yb = _experts(i, pad_starts, n_blocks, xs, e_w_gate, e_w_up, e_w_down, name=f"experts{i}")
        yk_halves = [_return_rows(yb, dest[:, hh * HALF_T:(hh + 1) * HALF_T]).reshape(TOP_K, HALF_T, DP)
                     for hh in range(N_HALVES)]
        prev = (yk_halves, wc)
        xin = x1
    yk_halves, wc = prev
    return _final(xin, yk_halves, wc, params)
```

```python
import functools

import jax
import jax.numpy as jnp
from jax import lax
from jax.experimental import pallas as pl
from jax.experimental.pallas import tpu as pltpu
from jax.experimental.pallas import tpu_sc as plsc

F32 = jnp.float32
BF16 = jnp.bfloat16
I32 = jnp.int32
U32 = jnp.uint32

D = 1024
BATCH = 4
SEQ = 8192
T = BATCH * SEQ
DEPTH = 4
N_GROUPS = 4
EPG = 8
NE = N_GROUPS * EPG
TOP_K = 2
F = D // 2
CONV_A = 31
CONV_B = 3
ALPHA = (2.0 * DEPTH) ** 0.25
LN_EPS = 1e-5

LANES = 128
SUBLANES = 8
VMEM_LIMIT = 56 * 1024 * 1024

TS = 512
SR = 512
NS = SEQ // TS
N_HALVES = 2
HALF_BATCH = BATCH // N_HALVES
HALF_T = HALF_BATCH * SEQ
HALO_A = 32
HALO_B = 8
BM = 256
BLOCKS_PER_UNIT = 8
UNIT = BLOCKS_PER_UNIT * BM
TAILS = (4 * BM, 2 * BM, BM)
ROW_DMA_PRIORITY = 1
NSLOT = T * TOP_K + NE * BM
NR = 48
ADA_TN = 1536
NEG = -1e30
DP = D // 2
DT = D // LANES
assert DT == SUBLANES

SC_CORES = 2
SC_SUBCORES = 16
SC_WORKERS = SC_CORES * SC_SUBCORES
SC_CHUNK = 64


def _sigmoid(x):
    return 1.0 / (1.0 + jnp.exp(-x))


def _pack_rows(x):
    return pltpu.pack_elementwise([x[:, :DP], x[:, DP:]], packed_dtype=BF16)


def _unpack_rows(p):
    lo = pltpu.unpack_elementwise(p, index=0, packed_dtype=BF16, unpacked_dtype=F32)
    hi = pltpu.unpack_elementwise(p, index=1, packed_dtype=BF16, unpacked_dtype=F32)
    return jnp.concatenate([lo, hi], axis=1)


def _layer_norm(x, g, b):
    mu = jnp.mean(x, axis=-1, keepdims=True)
    xc = x - mu
    var = jnp.mean(xc * xc, axis=-1, keepdims=True)
    return xc * lax.rsqrt(var + LN_EPS) * g + b


def _ada_kernel(c_ref, w_ref, b_ref, o_ref):
    c = c_ref[...]
    ca = (c * _sigmoid(c)).astype(BF16)
    w = w_ref[0].astype(BF16)
    o_ref[0] = jnp.dot(ca, w, preferred_element_type=F32) + b_ref[0]


def _ada_mods(c, ada_w, ada_b):
    out = pl.pallas_call(
        _ada_kernel,
        grid=(DEPTH, 6 * D // ADA_TN),
        in_specs=[
            pl.BlockSpec((BATCH, D), lambda i, j: (0, 0)),
            pl.BlockSpec((1, D, ADA_TN), lambda i, j: (i, 0, j)),
            pl.BlockSpec((1, 1, ADA_TN), lambda i, j: (i, 0, j)),
        ],
        out_specs=pl.BlockSpec((1, BATCH, ADA_TN), lambda i, j: (i, 0, j)),
        out_shape=jax.ShapeDtypeStruct((DEPTH, BATCH, 6 * D), F32),
        compiler_params=pltpu.CompilerParams(
            dimension_semantics=("arbitrary", "arbitrary"), vmem_limit_bytes=VMEM_LIMIT),
        name="ada_mods",
    )(c, ada_w, ada_b.reshape(DEPTH, 1, 6 * D))
    return out.reshape(DEPTH, BATCH, 6, D)


def _combine(x1, y0p, y1p, wc, g_f, ln_g, ln_b):
    w0 = jnp.tile(wc[:, :LANES], (1, D // LANES))
    w1 = jnp.tile(wc[:, LANES:], (1, D // LANES))
    y = w0 * _unpack_rows(y0p) + w1 * _unpack_rows(y1p)
    return _layer_norm(ALPHA * x1 + (1.0 + g_f) * y, ln_g, ln_b)


def _route(h2, wr_ref, rb_ref, tri_ref, cnt_ref, ri_ref, wc_ref, cnto_ref):
    lt = lax.dot_general(wr_ref[...], h2, (((1,), (1,)), ((), ())),
                         preferred_element_type=F32) + rb_ref[...]
    iota8 = lax.broadcasted_iota(I32, (SUBLANES, TS), 0).astype(F32)
    gl = lt[0:SUBLANES]
    gmax = jnp.max(gl, axis=0, keepdims=True)
    gidx = jnp.min(jnp.where(gl == gmax, iota8, float(SUBLANES)), axis=0, keepdims=True)
    gw = 1.0 / jnp.sum(jnp.exp(gl - gmax), axis=0, keepdims=True)
    el = lt[SUBLANES:2 * SUBLANES]
    for g in range(1, N_GROUPS):
        el = jnp.where(gidx == float(g), lt[SUBLANES * (g + 1):SUBLANES * (g + 2)], el)
    m1 = jnp.max(el, axis=0, keepdims=True)
    i1 = jnp.min(jnp.where(el == m1, iota8, float(SUBLANES)), axis=0, keepdims=True)
    el2 = jnp.where(iota8 == i1, -jnp.inf, el)
    m2 = jnp.max(el2, axis=0, keepdims=True)
    i2 = jnp.min(jnp.where(el2 == m2, iota8, float(SUBLANES)), axis=0, keepdims=True)
    r = jnp.exp(m2 - m1)
    w_a = gw / (1.0 + r)
    w_b = gw * r / (1.0 + r)
    e1 = gidx * float(EPG) + i1
    e2 = gidx * float(EPG) + i2

    iota_e = lax.broadcasted_iota(I32, (NE, TS), 0).astype(F32)
    oh1 = iota_e == e1
    oh2 = iota_e == e2
    oh = jnp.concatenate([jnp.where(oh1, 1.0, 0.0), jnp.where(oh2, 1.0, 0.0)], axis=0)
    before = jnp.dot(oh.astype(BF16), tri_ref[...], preferred_element_type=F32)
    tot = jnp.sum(oh, axis=1, keepdims=True)
    cnt = cnt_ref[...]
    base = jnp.tile(cnt, (1, TS // LANES))
    tot1 = tot[:NE]
    tot2 = tot[NE:]
    rank1 = jnp.sum(jnp.where(oh1, base + before[:NE], 0.0), axis=0, keepdims=True)
    rank2 = jnp.sum(jnp.where(oh2, base + tot1 + before[NE:], 0.0), axis=0, keepdims=True)
    new_cnt = cnt + tot1 + tot2
    cnt_ref[...] = new_cnt
    cnto_ref[...] = new_cnt.astype(I32)

    ri_ref[0:1, :] = e1.astype(I32)
    ri_ref[1:2, :] = e2.astype(I32)
    ri_ref[2:3, :] = rank1.astype(I32)
    ri_ref[3:4, :] = rank2.astype(I32)
    ri_ref[4:8, :] = jnp.zeros((4, TS), I32)
    wc_ref[0, :, :LANES] = jnp.broadcast_to(w_a, (LANES, TS)).T
    wc_ref[0, :, LANES:] = jnp.broadcast_to(w_b, (LANES, TS)).T


def _conv_taps(uext_ref, u, w_dw, halo, width, r0):
    uext_ref[halo + r0:halo + r0 + SR, :] = u
    acc = None
    for k in range(width):
        off = r0 + halo - (width - 1) + k
        term = w_dw[k:k + 1, :] * uext_ref[off:off + SR, :]
        acc = term if acc is None else acc + term
    return acc


def _conv_time_major(tm_ref, o2_ref, u, wk_ref, bias, r0, width, halo):
    for j in range(DT):
        tm_ref[pl.ds((halo + r0) * DT + j, SR, stride=DT), :] = u[:, j * LANES:(j + 1) * LANES]
    acc = None
    for k in range(width):
        off = (r0 + halo - (width - 1) + k) * DT
        term = tm_ref[off:off + SR * DT, :].reshape(SR, DT, LANES) * wk_ref[k]
        acc = term if acc is None else acc + term
    if bias is not None:
        acc = acc + bias
    o2_ref[r0 * DT:(r0 + SR) * DT, :] = acc.reshape(SR * DT, LANES)
    return jnp.concatenate([o2_ref[pl.ds(r0 * DT + j, SR, stride=DT), :] for j in range(DT)], axis=1)


def _mixer_kernel(*refs, kind, has_prev, carry):
    it = iter(refs)
    xin_ref = next(it)
    if has_prev:
        y0_ref, y1_ref, wcin_ref, pmods_ref, pln_ref = (next(it) for _ in range(5))
    if carry:
        for _ in range(4):
            next(it)
        cnt_in_ref = next(it)
    mods_ref, mln_ref, win_ref = next(it), next(it), next(it)
    if kind == "a":
        bin_ref, wdw_ref, bdw_ref, vec_ref = next(it), next(it), next(it), next(it)
    else:
        wdw_ref = next(it)
    wout_ref, wr_ref, rb_ref, tri_ref = (next(it) for _ in range(4))
    x1_ref, h2_ref, ri_ref, wc_ref, cnto_ref = (next(it) for _ in range(5))
    if kind == "a":
        uext_ref, o2_ref, h2b_ref, cnt_ref = (next(it) for _ in range(4))
    else:
        uext_ref, h2b_ref, cnt_ref = (next(it) for _ in range(3))
    win_s, wout_s = next(it), next(it)
    mln_ref, win_ref, wdw_ref, wout_ref, wr_ref, rb_ref = (
        r.at[0] for r in (mln_ref, win_ref, wdw_ref, wout_ref, wr_ref, rb_ref))
    if has_prev:
        pln_ref = pln_ref.at[0]
    if kind == "a":
        bin_ref, bdw_ref, vec_ref = (r.at[0] for r in (bin_ref, bdw_ref, vec_ref))

    first = (pl.program_id(0) == 0) & (pl.program_id(1) == 0)

    @pl.when(first)
    def _():
        if carry:
            cnt_ref[...] = cnt_in_ref[...].astype(F32)
        else:
            cnt_ref[...] = jnp.zeros((NE, LANES), F32)
        win_s[...] = win_ref[...]
        wout_s[...] = wout_ref[...]

    halo_rows = HALO_A * DT if kind == "a" else HALO_B

    @pl.when(pl.program_id(1) == 0)
    def _():
        uext_ref[0:halo_rows, :] = jnp.zeros((halo_rows, uext_ref.shape[1]), F32)

    m = mods_ref[0, 0]
    mln = mln_ref[...]
    for i in range(TS // SR):
        r0 = i * SR
        rows = slice(r0, r0 + SR)
        x = xin_ref[0, rows, :]
        if has_prev:
            pln = pln_ref[...]
            x = _combine(x, y0_ref[0, rows, :], y1_ref[0, rows, :], wcin_ref[0, rows, :],
                         pmods_ref[0, 0][5:6, :], pln[0:1, :], pln[1:2, :])

        h = (x * (1.0 + m[1:2, :]) + m[0:1, :]).astype(BF16)
        cols = [jnp.dot(h, win_s[:, c * D:(c + 1) * D], preferred_element_type=F32)
                for c in range(win_s.shape[1] // D)]
        if kind == "a":
            b_in = bin_ref[...]
            vec = vec_ref[...]
            u = (cols[0] + b_in[:, :D]) * _sigmoid(cols[1] + b_in[:, D:])
            u = _conv_time_major(uext_ref, o2_ref, u, wdw_ref, bdw_ref[...], r0, CONV_A, HALO_A)
            u = _layer_norm(u, vec[0:1, :], vec[1:2, :])
            u = u * _sigmoid(u)
            y = jnp.dot(u.astype(BF16), wout_s[...], preferred_element_type=F32) + vec[2:3, :]
        else:
            gb = cols[0]
            q = cols[1] * cols[2]
            u = _conv_taps(uext_ref, q, wdw_ref[...], HALO_B, CONV_B, r0)
            y = jnp.dot((gb * u).astype(BF16), wout_s[...], preferred_element_type=F32)

        x1 = _layer_norm(ALPHA * x + (1.0 + m[2:3, :]) * y, mln[0:1, :], mln[1:2, :])
        x1_ref[0, rows, :] = x1
        h2 = x1 * (1.0 + m[4:5, :]) + m[3:4, :]
        h2_ref[0, rows, :] = _pack_rows(h2)
        h2b_ref[rows, :] = h2.astype(BF16)

    uext_ref[0:halo_rows, :] = uext_ref[uext_ref.shape[0] - halo_rows:uext_ref.shape[0], :]
    _route(h2b_ref[...], wr_ref, rb_ref, tri_ref, cnt_ref, ri_ref, wc_ref, cnto_ref)


def _tile_spec(width, b0=0):
    return pl.BlockSpec((1, TS, width), lambda b, s: (b + b0, s, 0))


def _yk_spec(k):
    return pl.BlockSpec((1, TS, DP), lambda b, s: (k, b * NS + s, 0))


def _const_spec(shape):
    nd = len(shape)
    return pl.BlockSpec(shape, lambda b, s: (0,) * nd)


def _mods_spec(layer, b0=0):
    return pl.BlockSpec((1, 1, 6, D), lambda b, s: (layer, b + b0, 0, 0))


def _layer_spec(arr, idx):
    tail = arr.shape[1:]
    return pl.BlockSpec((1,) + tail, lambda b, s: (idx,) + (0,) * len(tail))


def _any_spec():
    return pl.BlockSpec(memory_space=pl.ANY)


def _mixer_call(kind, layer, xin, prev, params, name, b0, n_batch, carried):
    mods, mix_ln, ffn_ln, wr, rb, tri = (params[k] for k in ("mods", "mix_ln", "ffn_ln", "wr", "rb", "tri"))
    weights = params[kind]
    j = layer // 2
    has_prev = prev is not None
    carry = carried is not None
    args = [xin]
    specs = [_tile_spec(D, b0)]
    if has_prev:
        yk, wcin = prev
        args += [yk, yk, wcin, mods, ffn_ln]
        specs += [_yk_spec(0), _yk_spec(1), _tile_spec(2 * LANES, b0), _mods_spec(layer - 1, b0),
                  _layer_spec(ffn_ln, layer - 1)]
    aliases = {}
    if carry:
        for out_idx, arr in enumerate(carried):
            aliases[len(args)] = out_idx
            args.append(arr)
            specs.append(_any_spec() if out_idx < 4 else _const_spec((NE, LANES)))
    args += [mods, mix_ln]
    specs += [_mods_spec(layer, b0), _layer_spec(mix_ln, layer)]
    for w in weights:
        args.append(w)
        specs.append(_layer_spec(w, j))
    args += [wr, rb, tri]
    specs += [_layer_spec(wr, layer), _layer_spec(rb, layer), _const_spec(tri.shape)]
    if kind == "a":
        conv_scratch = [pltpu.VMEM(((TS + HALO_A) * DT, LANES), F32), pltpu.VMEM((TS * DT, LANES), F32)]
    else:
        conv_scratch = [pltpu.VMEM((TS + HALO_B, D), F32)]
    out_shape = (
        jax.ShapeDtypeStruct((BATCH, SEQ, D), F32),
        jax.ShapeDtypeStruct((BATCH, SEQ, DP), U32),
        jax.ShapeDtypeStruct((SUBLANES, T), I32),
        jax.ShapeDtypeStruct((BATCH, SEQ, 2 * LANES), F32),
        jax.ShapeDtypeStruct((NE, LANES), I32),
    )
    out_specs = (
        _tile_spec(D, b0), _tile_spec(DP, b0),
        pl.BlockSpec((SUBLANES, TS), lambda b, s: (0, (b + b0) * NS + s)),
        _tile_spec(2 * LANES, b0),
        pl.BlockSpec((NE, LANES), lambda b, s: (0, 0)),
    )
    return pl.pallas_call(
        functools.partial(_mixer_kernel, kind=kind, has_prev=has_prev, carry=carry),
        grid=(n_batch, NS),
        in_specs=specs,
        out_specs=out_specs,
        out_shape=out_shape,
        input_output_aliases=aliases,
        scratch_shapes=conv_scratch + [pltpu.VMEM((TS, D), BF16), pltpu.VMEM((NE, LANES), F32),
                                       pltpu.VMEM(weights[0].shape[1:], BF16),
                                       pltpu.VMEM(weights[-1].shape[1:], BF16)],
        compiler_params=pltpu.CompilerParams(
            dimension_semantics=("arbitrary", "arbitrary"), vmem_limit_bytes=VMEM_LIMIT),
        name=name,
    )(*args)


def _mixer(kind, layer, xin, prev, params):
    name = f"mixer_{kind}{layer}"
    if prev is None:
        return _mixer_call(kind, layer, xin, None, params, name, 0, BATCH, None)
    yk_halves, wcin = prev
    out = None
    for hh in range(N_HALVES):
        out = _mixer_call(kind, layer, xin, (yk_halves[hh], wcin), params, f"{name}_h{hh}",
                          hh * HALF_BATCH, HALF_BATCH, out)
    return out


def _dest_kernel(ps_ref, ri_ref, o_ref):
    ri = ri_ref[...]
    e = ri[0:2, :]
    start = jnp.zeros_like(e)
    for k in range(NE):
        start = jnp.where(e == k, ps_ref[k], start)
    o_ref[0:2, :] = start + ri[2:4, :]
    o_ref[2:8, :] = jnp.zeros((6, ri.shape[1]), I32)


def _dest_slots(pad_starts, ri):
    tn = 4096
    return pl.pallas_call(
        _dest_kernel,
        grid_spec=pltpu.PrefetchScalarGridSpec(
            num_scalar_prefetch=1,
            grid=(T // tn,),
            in_specs=[pl.BlockSpec((SUBLANES, tn), lambda i, ps: (0, i))],
            out_specs=pl.BlockSpec((SUBLANES, tn), lambda i, ps: (0, i)),
        ),
        out_shape=jax.ShapeDtypeStruct((SUBLANES, T), I32),
        compiler_params=pltpu.CompilerParams(dimension_semantics=("arbitrary",)),
        name="dest_slots",
    )(pad_starts, ri)


def _expert_rows(x_packed, wgu_s, wd_s):
    x = _unpack_rows(x_packed).astype(BF16)
    gu = jnp.dot(x, wgu_s[...], preferred_element_type=F32)
    g = gu[:, :F]
    hid = (g * _sigmoid(g) * gu[:, F:]).astype(BF16)
    return _pack_rows(jnp.dot(hid, wd_s[...], preferred_element_type=F32))


def _expert_kernel(st_ref, nb_ref, xs_hbm, wg_ref, wu_ref, wd_ref, yb_hbm,
                   wgu_s, wd_s, xbuf, obuf, *rest):
    e = pl.program_id(0)
    start = st_ref[e]
    xtails, otails = rest[0:2 * len(TAILS):2], rest[1:2 * len(TAILS):2]
    xsem, osem, tsem = rest[2 * len(TAILS):]

    def split(n_blocks):
        units = n_blocks // BLOCKS_PER_UNIT
        rem = n_blocks % BLOCKS_PER_UNIT
        on, off = [], []
        row = units * UNIT
        for rows in TAILS:
            blocks = rows // BM
            present = (rem // blocks) % 2 == 1
            on.append(present)
            off.append(row)
            row = row + jnp.where(present, rows, 0)
        return units, on, off

    n_units, tail_on, tail_off = split(nb_ref[e])

    def x_copy(first_row, u, slot):
        return pltpu.make_async_copy(xs_hbm.at[pl.ds(pl.multiple_of(first_row + u * UNIT, BM), UNIT)],
                                     xbuf.at[slot], xsem.at[slot])

    def o_copy(u, slot):
        return pltpu.make_async_copy(obuf.at[slot],
                                     yb_hbm.at[pl.ds(pl.multiple_of(start + u * UNIT, BM), UNIT)], osem.at[slot])

    def xt_copy(k, first_row, off):
        return pltpu.make_async_copy(xs_hbm.at[pl.ds(pl.multiple_of(first_row + off, BM), TAILS[k])],
                                     xtails[k], tsem.at[2 * k])

    def ot_copy(k):
        return pltpu.make_async_copy(otails[k], yb_hbm.at[pl.ds(pl.multiple_of(start + tail_off[k], BM), TAILS[k])],
                                     tsem.at[2 * k + 1])

    def fetch_first(first_row, n_blocks):
        units, on, off = split(n_blocks)

        @pl.when(units > 0)
        def _():
            x_copy(first_row, 0, 0).start(priority=ROW_DMA_PRIORITY)

        for k in range(len(TAILS)):
            @pl.when(on[k])
            def _():
                xt_copy(k, first_row, off[k]).start(priority=ROW_DMA_PRIORITY)

    @pl.when(e == 0)
    def _():
        fetch_first(start, nb_ref[e])

    @pl.when(nb_ref[e] > 0)
    def _():
        wgu_s[:, :F] = wg_ref[0, 0].astype(BF16)
        wgu_s[:, F:] = wu_ref[0, 0].astype(BF16)
        wd_s[...] = wd_ref[0, 0].astype(BF16)

    @pl.loop(0, n_units, step=2)
    def _(u0):
        for slot in range(2):
            u = u0 + slot

            @pl.when(u < n_units)
            def _():
                x_copy(start, u, slot).wait()

                @pl.when(u + 1 < n_units)
                def _():
                    x_copy(start, u + 1, 1 - slot).start(priority=ROW_DMA_PRIORITY)

                @pl.when(u >= 2)
                def _():
                    o_copy(u - 2, slot).wait()

                obuf[slot] = _expert_rows(xbuf[slot], wgu_s, wd_s)
                o_copy(u, slot).start(priority=ROW_DMA_PRIORITY)

    for k in range(len(TAILS)):
        @pl.when(tail_on[k])
        def _():
            xt_copy(k, start, tail_off[k]).wait()
            otails[k][...] = _expert_rows(xtails[k][...], wgu_s, wd_s)
            ot_copy(k).start(priority=ROW_DMA_PRIORITY)

    nxt = jnp.minimum(e + 1, NE - 1)

    @pl.when(e + 1 < NE)
    def _():
        fetch_first(st_ref[nxt], nb_ref[nxt])

    for back in (1, 2):
        @pl.when(n_units >= back)
        def _():
            last = n_units - back
            o_copy(last, last % 2).wait()

    for k in range(len(TAILS)):
        @pl.when(tail_on[k])
        def _():
            ot_copy(k).wait()


def _experts(layer, starts, n_blocks, xs, w_gate, w_up, w_down, name):
    def w_map(e, st, nb):
        return (layer, e, 0, 0)

    return pl.pallas_call(
        _expert_kernel,
        grid_spec=pltpu.PrefetchScalarGridSpec(
            num_scalar_prefetch=2,
            grid=(NE,),
            in_specs=[
                pl.BlockSpec(memory_space=pl.ANY),
                pl.BlockSpec((1, 1, D, F), w_map),
                pl.BlockSpec((1, 1, D, F), w_map),
                pl.BlockSpec((1, 1, F, D), w_map),
            ],
            out_specs=pl.BlockSpec(memory_space=pl.ANY),
            scratch_shapes=[
                pltpu.VMEM((D, 2 * F), BF16), pltpu.VMEM((F, D), BF16),
                pltpu.VMEM((2, UNIT, DP), U32), pltpu.VMEM((2, UNIT, DP), U32),
                *[pltpu.VMEM((rows, DP), U32) for rows in TAILS for _ in range(2)],
                pltpu.SemaphoreType.DMA((2,)), pltpu.SemaphoreType.DMA((2,)),
                pltpu.SemaphoreType.DMA((2 * len(TAILS),)),
            ],
        ),
        out_shape=jax.ShapeDtypeStruct((NSLOT, DP), U32),
        compiler_params=pltpu.CompilerParams(
            dimension_semantics=("arbitrary",), vmem_limit_bytes=VMEM_LIMIT),
        name=name,
    )(starts, n_blocks, xs, w_gate, w_up, w_down)


def _sc_worker_id():
    return lax.axis_index("s") * SC_CORES + lax.axis_index("c")


def _sc_mesh():
    return plsc.VectorSubcoreMesh(core_axis_name="c", subcore_axis_name="s")


def _sc_scratch(n_index_rows):
    return [
        pltpu.VMEM((n_index_rows, SC_CHUNK), I32),
        pltpu.VMEM((2, SC_CHUNK, DP), U32),
        pltpu.SemaphoreType.DMA((2,)),
        pltpu.SemaphoreType.DMA((2,)),
    ]


def _dispatch_rows(h2p, dest):
    per_w = T // SC_WORKERS
    n_chunks = per_w // SC_CHUNK

    @functools.partial(
        pl.kernel, mesh=_sc_mesh(),
        out_type=jax.ShapeDtypeStruct((NSLOT, DP), U32),
        scratch_types=_sc_scratch(TOP_K * n_chunks),
        name="dispatch_rows",
    )
    def k(h2_hbm, dest_hbm, out_hbm, dest_v, rows_v, rsem, wsem):
        wid = _sc_worker_id()
        for kk in range(TOP_K):
            pltpu.sync_copy(dest_hbm.at[kk, pl.ds(wid * n_chunks, n_chunks)],
                            dest_v.at[pl.ds(kk * n_chunks, n_chunks)])
        base = wid * per_w

        def read(c, slot):
            return pltpu.make_async_copy(h2_hbm.at[pl.ds(base + c * SC_CHUNK, SC_CHUNK)],
                                         rows_v.at[slot], rsem.at[slot])

        def write(c, kk, slot):
            return pltpu.make_async_copy(rows_v.at[slot], out_hbm.at[dest_v.at[kk * n_chunks + c]],
                                         wsem.at[slot])

        read(0, 0).start()

        @pl.loop(0, n_chunks, step=2)
        def _(c):
            for b in range(2):
                cc = c + b
                read(cc, b).wait()

                @pl.when(cc + 1 < n_chunks)
                def _():
                    @pl.when(cc >= 1)
                    def _():
                        for kk in range(TOP_K):
                            write(cc - 1, kk, 1 - b).wait()
                    read(cc + 1, 1 - b).start()

                for kk in range(TOP_K):
                    write(cc, kk, b).start()

        for slot, cc in ((0, n_chunks - 2), (1, n_chunks - 1)):
            for kk in range(TOP_K):
                write(cc, kk, slot).wait()

    return k(h2p, dest.reshape(TOP_K, T // SC_CHUNK, SC_CHUNK))


def _return_rows(yb, dest):
    m = dest.size
    per_w = m // SC_WORKERS
    n_chunks = per_w // SC_CHUNK

    @functools.partial(
        pl.kernel, mesh=_sc_mesh(),
        out_type=jax.ShapeDtypeStruct((m, DP), U32),
        scratch_types=_sc_scratch(n_chunks),
        name="return_rows",
    )
    def k(yb_hbm, dest_hbm, out_hbm, idx_v, rows_v, gsem, wsem):
        wid = _sc_worker_id()
        pltpu.sync_copy(dest_hbm.at[pl.ds(wid * n_chunks, n_chunks)], idx_v)
        base = wid * per_w

        def gather(c, slot):
            return pltpu.make_async_copy(yb_hbm.at[idx_v.at[c]], rows_v.at[slot], gsem.at[slot])

        def write(c, slot):
            return pltpu.make_async_copy(rows_v.at[slot], out_hbm.at[pl.ds(base + c * SC_CHUNK, SC_CHUNK)],
                                         wsem.at[slot])

        gather(0, 0).start()

        @pl.loop(0, n_chunks, step=2)
        def _(c):
            for b in range(2):
                cc = c + b
                gather(cc, b).wait()

                @pl.when(cc + 1 < n_chunks)
                def _():
                    @pl.when(cc >= 1)
                    def _():
                        write(cc - 1, 1 - b).wait()
                    gather(cc + 1, 1 - b).start()

                write(cc, b).start()

        write(n_chunks - 2, 0).wait()
        write(n_chunks - 1, 1).wait()

    return k(yb, dest.reshape(m // SC_CHUNK, SC_CHUNK))


def _final_kernel(x1_ref, y0_ref, y1_ref, wc_ref, mods_ref, ln_ref, *rest):
    o_ref = rest[-1]
    ln = ln_ref[0]
    o_ref[0] = _combine(x1_ref[0], y0_ref[0], y1_ref[0], wc_ref[0], mods_ref[0, 0][5:6, :],
                        ln[0:1, :], ln[1:2, :])


def _final(x1, yk_halves, wc, params):
    mods, ffn_ln = params["mods"], params["ffn_ln"]
    out = None
    for hh in range(N_HALVES):
        b0 = hh * HALF_BATCH
        args = [x1, yk_halves[hh], yk_halves[hh], wc, mods, ffn_ln]
        specs = [_tile_spec(D, b0), _yk_spec(0), _yk_spec(1), _tile_spec(2 * LANES, b0),
                 _mods_spec(DEPTH - 1, b0), _layer_spec(ffn_ln, DEPTH - 1)]
        aliases = {}
        if out is not None:
            aliases[len(args)] = 0
            args.append(out)
            specs.append(_any_spec())
        out = pl.pallas_call(
            _final_kernel,
            grid=(HALF_BATCH, NS),
            in_specs=specs,
            out_specs=_tile_spec(D, b0),
            out_shape=jax.ShapeDtypeStruct((BATCH, SEQ, D), F32),
            input_output_aliases=aliases,
            compiler_params=pltpu.CompilerParams(
                dimension_semantics=("arbitrary", "arbitrary"), vmem_limit_bytes=VMEM_LIMIT),
            name=f"final_combine_h{hh}",
        )(*args)
    return out


def _plan(counts):
    n_blocks = (counts + BM - 1) // BM
    padded = n_blocks * BM
    pad_starts = (jnp.cumsum(padded) - padded).astype(I32)
    return pad_starts, n_blocks.astype(I32)


def _router_params(w_group, b_group, w_expert, b_expert):
    wr = jnp.zeros((DEPTH, NR, D), F32)
    wr = wr.at[:, 0:N_GROUPS].set(jnp.swapaxes(w_group, 1, 2))
    wr = wr.at[:, SUBLANES:SUBLANES + NE].set(jnp.swapaxes(w_expert, 1, 2))
    rb = jnp.full((DEPTH, NR), NEG, F32)
    rb = rb.at[:, 0:N_GROUPS].set(b_group).at[:, SUBLANES:SUBLANES + NE].set(b_expert)
    rb = rb.at[:, SUBLANES + NE:].set(0.0)
    return wr.astype(BF16), rb.reshape(DEPTH, NR, 1)


def kernel(x, c, ada_w, ada_b, a_w_in, a_b_in, a_w_dw, a_b_dw, a_ln_g, a_ln_b, a_w_out, a_b_out,
           b_w_in, b_w_dw, b_w_out, mix_ln_g, mix_ln_b, ffn_ln_g, ffn_ln_b,
           r_w_group, r_b_group, r_w_expert, r_b_expert, e_w_gate, e_w_up, e_w_down):
    n_a = a_w_in.shape[0]
    wr, rb = _router_params(r_w_group, r_b_group, r_w_expert, r_b_expert)
    params = {
        "mods": _ada_mods(c, ada_w, ada_b),
        "mix_ln": jnp.stack([mix_ln_g, mix_ln_b], axis=1),
        "ffn_ln": jnp.stack([ffn_ln_g, ffn_ln_b], axis=1),
        "wr": wr, "rb": rb,
        "tri": (jnp.arange(TS)[:, None] < jnp.arange(TS)[None, :]).astype(BF16),
        "a": [a_w_in.astype(BF16), a_b_in.reshape(n_a, 1, 2 * D), a_w_dw.reshape(n_a, CONV_A, DT, LANES),
              a_b_dw.reshape(n_a, DT, LANES), jnp.stack([a_ln_g, a_ln_b, a_b_out], axis=1),
              a_w_out.astype(BF16)],
        "b": [b_w_in.astype(BF16), b_w_dw, b_w_out.astype(BF16)],
    }
    prev = None
    xin = x
    for i in range(DEPTH):
        kind = "a" if i % 2 == 0 else "b"
        x1, h2, ri, wc, counts = _mixer(kind, i, xin, prev, params)
        pad_starts, n_blocks = _plan(counts[:, 0])
        dest = _dest_slots(pad_starts, ri)[0:TOP_K]
        xs = _dispatch_rows(h2.reshape(T, DP), dest)
        yb = _experts(i, pad_starts, n_blocks, xs, e_w_gate, e_w_up, e_w_down, name=f"experts{i}")
        yk_halves = [_return_rows(yb, dest[:, hh * HALF_T:(hh + 1) * HALF_T]).reshape(TOP_K, HALF_T, DP)
                     for hh in range(N_HALVES)]
        prev = (yk_halves, wc)
        xin = x1
    yk_halves, wc = prev
    return _final(xin, yk_halves, wc, params)
```

```python
import functools

import jax
import jax.numpy as jnp
from jax import lax
from jax.experimental import pallas as pl
from jax.experimental.pallas import tpu as pltpu
from jax.experimental.pallas import tpu_sc as plsc

F32 = jnp.float32
BF16 = jnp.bfloat16
I32 = jnp.int32
U32 = jnp.uint32

D = 1024
BATCH = 4
SEQ = 8192
T = BATCH * SEQ
DEPTH = 4
N_GROUPS = 4
EPG = 8
NE = N_GROUPS * EPG
TOP_K = 2
F = D // 2
CONV_A = 31
CONV_B = 3
ALPHA = (2.0 * DEPTH) ** 0.25
LN_EPS = 1e-5

LANES = 128
SUBLANES = 8
VMEM_LIMIT = 56 * 1024 * 1024

TS = 512
SR = 512
NS = SEQ // TS
PART_BATCHES = (1, 3)
PART_START = (0, 1)
assert sum(PART_BATCHES) == BATCH
HALO_A = 32
HALO_B = 8
BM = 256
BLOCKS_PER_UNIT = 4
UNIT = BLOCKS_PER_UNIT * BM
TAILS = (2 * BM, BM)
ROW_DMA_PRIORITY = 1
NSLOT = T * TOP_K + NE * BM
NR = 48
ADA_TN = 1536
NEG = -1e30
DP = D // 2
DT = D // LANES
assert DT == SUBLANES

SC_CORES = 2
SC_SUBCORES = 16
SC_WORKERS = SC_CORES * SC_SUBCORES
SC_CHUNK = 64


def _sigmoid(x):
    return 1.0 / (1.0 + jnp.exp(-x))


def _pack_rows(x):
    return pltpu.pack_elementwise([x[:, :DP], x[:, DP:]], packed_dtype=BF16)


def _unpack_rows(p):
    lo = pltpu.unpack_elementwise(p, index=0, packed_dtype=BF16, unpacked_dtype=F32)
    hi = pltpu.unpack_elementwise(p, index=1, packed_dtype=BF16, unpacked_dtype=F32)
    return jnp.concatenate([lo, hi], axis=1)


def _layer_norm(x, g, b):
    mu = jnp.mean(x, axis=-1, keepdims=True)
    xc = x - mu
    var = jnp.mean(xc * xc, axis=-1, keepdims=True)
    return xc * lax.rsqrt(var + LN_EPS) * g + b


def _ada_kernel(c_ref, w_ref, b_ref, o_ref):
    c = c_ref[...]
    ca = (c * _sigmoid(c)).astype(BF16)
    w = w_ref[0].astype(BF16)
    o_ref[0] = jnp.dot(ca, w, preferred_element_type=F32) + b_ref[0]


def _ada_mods(c, ada_w, ada_b):
    out = pl.pallas_call(
        _ada_kernel,
        grid=(DEPTH, 6 * D // ADA_TN),
        in_specs=[
            pl.BlockSpec((BATCH, D), lambda i, j: (0, 0)),
            pl.BlockSpec((1, D, ADA_TN), lambda i, j: (i, 0, j)),
            pl.BlockSpec((1, 1, ADA_TN), lambda i, j: (i, 0, j)),
        ],
        out_specs=pl.BlockSpec((1, BATCH, ADA_TN), lambda i, j: (i, 0, j)),
        out_shape=jax.ShapeDtypeStruct((DEPTH, BATCH, 6 * D), F32),
        compiler_params=pltpu.CompilerParams(
            dimension_semantics=("arbitrary", "arbitrary"), vmem_limit_bytes=VMEM_LIMIT),
        name="ada_mods",
    )(c, ada_w, ada_b.reshape(DEPTH, 1, 6 * D))
    return out.reshape(DEPTH, BATCH, 6, D)


def _combine(x1, y0p, y1p, wc, g_f, ln_g, ln_b):
    w0 = jnp.tile(wc[:, :LANES], (1, D // LANES))
    w1 = jnp.tile(wc[:, LANES:], (1, D // LANES))
    y = w0 * _unpack_rows(y0p) + w1 * _unpack_rows(y1p)
    return _layer_norm(ALPHA * x1 + (1.0 + g_f) * y, ln_g, ln_b)


def _route(h2, wr_ref, rb_ref, tri_ref, cnt_ref, ri_ref, wc_ref, cnto_ref):
    lt = lax.dot_general(wr_ref[...], h2, (((1,), (1,)), ((), ())),
                         preferred_element_type=F32) + rb_ref[...]
    iota8 = lax.broadcasted_iota(I32, (SUBLANES, TS), 0).astype(F32)
    gl = lt[0:SUBLANES]
    gmax = jnp.max(gl, axis=0, keepdims=True)
    gidx = jnp.min(jnp.where(gl == gmax, iota8, float(SUBLANES)), axis=0, keepdims=True)
    gw = 1.0 / jnp.sum(jnp.exp(gl - gmax), axis=0, keepdims=True)
    el = lt[SUBLANES:2 * SUBLANES]
    for g in range(1, N_GROUPS):
        el = jnp.where(gidx == float(g), lt[SUBLANES * (g + 1):SUBLANES * (g + 2)], el)
    m1 = jnp.max(el, axis=0, keepdims=True)
    i1 = jnp.min(jnp.where(el == m1, iota8, float(SUBLANES)), axis=0, keepdims=True)
    el2 = jnp.where(iota8 == i1, -jnp.inf, el)
    m2 = jnp.max(el2, axis=0, keepdims=True)
    i2 = jnp.min(jnp.where(el2 == m2, iota8, float(SUBLANES)), axis=0, keepdims=True)
    r = jnp.exp(m2 - m1)
    w_a = gw / (1.0 + r)
    w_b = gw * r / (1.0 + r)
    e1 = gidx * float(EPG) + i1
    e2 = gidx * float(EPG) + i2

    iota_e = lax.broadcasted_iota(I32, (NE, TS), 0).astype(F32)
    oh1 = iota_e == e1
    oh2 = iota_e == e2
    oh = jnp.concatenate([jnp.where(oh1, 1.0, 0.0), jnp.where(oh2, 1.0, 0.0)], axis=0)
    before = jnp.dot(oh.astype(BF16), tri_ref[...], preferred_element_type=F32)
    tot = jnp.sum(oh, axis=1, keepdims=True)
    cnt = cnt_ref[...]
    base = jnp.tile(cnt, (1, TS // LANES))
    tot1 = tot[:NE]
    tot2 = tot[NE:]
    rank1 = jnp.sum(jnp.where(oh1, base + before[:NE], 0.0), axis=0, keepdims=True)
    rank2 = jnp.sum(jnp.where(oh2, base + tot1 + before[NE:], 0.0), axis=0, keepdims=True)
    new_cnt = cnt + tot1 + tot2
    cnt_ref[...] = new_cnt
    cnto_ref[...] = new_cnt.astype(I32)

    ri_ref[0:1, :] = e1.astype(I32)
    ri_ref[1:2, :] = e2.astype(I32)
    ri_ref[2:3, :] = rank1.astype(I32)
    ri_ref[3:4, :] = rank2.astype(I32)
    ri_ref[4:8, :] = jnp.zeros((4, TS), I32)
    wc_ref[0, :, :LANES] = jnp.broadcast_to(w_a, (LANES, TS)).T
    wc_ref[0, :, LANES:] = jnp.broadcast_to(w_b, (LANES, TS)).T


def _conv_taps(uext_ref, u, w_dw, halo, width, r0):
    uext_ref[halo + r0:halo + r0 + SR, :] = u
    acc = None
    for k in range(width):
        off = r0 + halo - (width - 1) + k
        term = w_dw[k:k + 1, :] * uext_ref[off:off + SR, :]
        acc = term if acc is None else acc + term
    return acc


def _conv_time_major(tm_ref, o2_ref, u, wk_ref, bias, r0, width, halo):
    for j in range(DT):
        tm_ref[pl.ds((halo + r0) * DT + j, SR, stride=DT), :] = u[:, j * LANES:(j + 1) * LANES]
    acc = None
    for k in range(width):
        off = (r0 + halo - (width - 1) + k) * DT
        term = tm_ref[off:off + SR * DT, :].reshape(SR, DT, LANES) * wk_ref[k]
        acc = term if acc is None else acc + term
    if bias is not None:
        acc = acc + bias
    o2_ref[r0 * DT:(r0 + SR) * DT, :] = acc.reshape(SR * DT, LANES)
    return jnp.concatenate([o2_ref[pl.ds(r0 * DT + j, SR, stride=DT), :] for j in range(DT)], axis=1)


def _mixer_kernel(*refs, kind, has_prev, carry):
    it = iter(refs)
    xin_ref = next(it)
    if has_prev:
        y0_ref, y1_ref, wcin_ref, pmods_ref, pln_ref = (next(it) for _ in range(5))
    if carry:
        for _ in range(4):
            next(it)
        cnt_in_ref = next(it)
    mods_ref, mln_ref, win_ref = next(it), next(it), next(it)
    if kind == "a":
        bin_ref, wdw_ref, bdw_ref, vec_ref = next(it), next(it), next(it), next(it)
    else:
        wdw_ref = next(it)
    wout_ref, wr_ref, rb_ref, tri_ref = (next(it) for _ in range(4))
    x1_ref, h2_ref, ri_ref, wc_ref, cnto_ref = (next(it) for _ in range(5))
    if kind == "a":
        uext_ref, o2_ref, h2b_ref, cnt_ref = (next(it) for _ in range(4))
    else:
        uext_ref, h2b_ref, cnt_ref = (next(it) for _ in range(3))
    win_s, wout_s = next(it), next(it)
    mln_ref, win_ref, wdw_ref, wout_ref, wr_ref, rb_ref = (
        r.at[0] for r in (mln_ref, win_ref, wdw_ref, wout_ref, wr_ref, rb_ref))
    if has_prev:
        pln_ref = pln_ref.at[0]
    if kind == "a":
        bin_ref, bdw_ref, vec_ref = (r.at[0] for r in (bin_ref, bdw_ref, vec_ref))

    first = (pl.program_id(0) == 0) & (pl.program_id(1) == 0)

    @pl.when(first)
    def _():
        if carry:
            cnt_ref[...] = cnt_in_ref[...].astype(F32)
        else:
            cnt_ref[...] = jnp.zeros((NE, LANES), F32)
        win_s[...] = win_ref[...]
        wout_s[...] = wout_ref[...]

    halo_rows = HALO_A * DT if kind == "a" else HALO_B

    @pl.when(pl.program_id(1) == 0)
    def _():
        uext_ref[0:halo_rows, :] = jnp.zeros((halo_rows, uext_ref.shape[1]), F32)

    m = mods_ref[0, 0]
    mln = mln_ref[...]
    for i in range(TS // SR):
        r0 = i * SR
        rows = slice(r0, r0 + SR)
        x = xin_ref[0, rows, :]
        if has_prev:
            pln = pln_ref[...]
            x = _combine(x, y0_ref[0, rows, :], y1_ref[0, rows, :], wcin_ref[0, rows, :],
                         pmods_ref[0, 0][5:6, :], pln[0:1, :], pln[1:2, :])

        h = (x * (1.0 + m[1:2, :]) + m[0:1, :]).astype(BF16)
        cols = [jnp.dot(h, win_s[:, c * D:(c + 1) * D], preferred_element_type=F32)
                for c in range(win_s.shape[1] // D)]
        if kind == "a":
            b_in = bin_ref[...]
            vec = vec_ref[...]
            u = (cols[0] + b_in[:, :D]) * _sigmoid(cols[1] + b_in[:, D:])
            u = _conv_time_major(uext_ref, o2_ref, u, wdw_ref, bdw_ref[...], r0, CONV_A, HALO_A)
            u = _layer_norm(u, vec[0:1, :], vec[1:2, :])
            u = u * _sigmoid(u)
            y = jnp.dot(u.astype(BF16), wout_s[...], preferred_element_type=F32) + vec[2:3, :]
        else:
            gb = cols[0]
            q = cols[1] * cols[2]
            u = _conv_taps(uext_ref, q, wdw_ref[...], HALO_B, CONV_B, r0)
            y = jnp.dot((gb * u).astype(BF16), wout_s[...], preferred_element_type=F32)

        x1 = _layer_norm(ALPHA * x + (1.0 + m[2:3, :]) * y, mln[0:1, :], mln[1:2, :])
        x1_ref[0, rows, :] = x1
        h2 = x1 * (1.0 + m[4:5, :]) + m[3:4, :]
        h2_ref[0, rows, :] = _pack_rows(h2)
        h2b_ref[rows, :] = h2.astype(BF16)

    uext_ref[0:halo_rows, :] = uext_ref[uext_ref.shape[0] - halo_rows:uext_ref.shape[0], :]
    _route(h2b_ref[...], wr_ref, rb_ref, tri_ref, cnt_ref, ri_ref, wc_ref, cnto_ref)


def _tile_spec(width, b0=0):
    return pl.BlockSpec((1, TS, width), lambda b, s: (b + b0, s, 0))


def _yk_spec(k):
    return pl.BlockSpec((1, TS, DP), lambda b, s: (k, b * NS + s, 0))


def _const_spec(shape):
    nd = len(shape)
    return pl.BlockSpec(shape, lambda b, s: (0,) * nd)


def _mods_spec(layer, b0=0):
    return pl.BlockSpec((1, 1, 6, D), lambda b, s: (layer, b + b0, 0, 0))


def _layer_spec(arr, idx):
    tail = arr.shape[1:]
    return pl.BlockSpec((1,) + tail, lambda b, s: (idx,) + (0,) * len(tail))


def _any_spec():
    return pl.BlockSpec(memory_space=pl.ANY)


def _mixer_call(kind, layer, xin, prev, params, name, b0, n_batch, carried):
    mods, mix_ln, ffn_ln, wr, rb, tri = (params[k] for k in ("mods", "mix_ln", "ffn_ln", "wr", "rb", "tri"))
    weights = params[kind]
    j = layer // 2
    has_prev = prev is not None
    carry = carried is not None
    args = [xin]
    specs = [_tile_spec(D, b0)]
    if has_prev:
        yk, wcin = prev
        args += [yk, yk, wcin, mods, ffn_ln]
        specs += [_yk_spec(0), _yk_spec(1), _tile_spec(2 * LANES, b0), _mods_spec(layer - 1, b0),
                  _layer_spec(ffn_ln, layer - 1)]
    aliases = {}
    if carry:
        for out_idx, arr in enumerate(carried):
            aliases[len(args)] = out_idx
            args.append(arr)
            specs.append(_any_spec() if out_idx < 4 else _const_spec((NE, LANES)))
    args += [mods, mix_ln]
    specs += [_mods_spec(layer, b0), _layer_spec(mix_ln, layer)]
    for w in weights:
        args.append(w)
        specs.append(_layer_spec(w, j))
    args += [wr, rb, tri]
    specs += [_layer_spec(wr, layer), _layer_spec(rb, layer), _const_spec(tri.shape)]
    if kind == "a":
        conv_scratch = [pltpu.VMEM(((TS + HALO_A) * DT, LANES), F32), pltpu.VMEM((TS * DT, LANES), F32)]
    else:
        conv_scratch = [pltpu.VMEM((TS + HALO_B, D), F32)]
    out_shape = (
        jax.ShapeDtypeStruct((BATCH, SEQ, D), F32),
        jax.ShapeDtypeStruct((BATCH, SEQ, DP), U32),
        jax.ShapeDtypeStruct((SUBLANES, T), I32),
        jax.ShapeDtypeStruct((BATCH, SEQ, 2 * LANES), F32),
        jax.ShapeDtypeStruct((NE, LANES), I32),
    )
    out_specs = (
        _tile_spec(D, b0), _tile_spec(DP, b0),
        pl.BlockSpec((SUBLANES, TS), lambda b, s: (0, (b + b0) * NS + s)),
        _tile_spec(2 * LANES, b0),
        pl.BlockSpec((NE, LANES), lambda b, s: (0, 0)),
    )
    return pl.pallas_call(
        functools.partial(_mixer_kernel, kind=kind, has_prev=has_prev, carry=carry),
        grid=(n_batch, NS),
        in_specs=specs,
        out_specs=out_specs,
        out_shape=out_shape,
        input_output_aliases=aliases,
        scratch_shapes=conv_scratch + [pltpu.VMEM((TS, D), BF16), pltpu.VMEM((NE, LANES), F32),
                                       pltpu.VMEM(weights[0].shape[1:], BF16),
                                       pltpu.VMEM(weights[-1].shape[1:], BF16)],
        compiler_params=pltpu.CompilerParams(
            dimension_semantics=("arbitrary", "arbitrary"), vmem_limit_bytes=VMEM_LIMIT),
        name=name,
    )(*args)


def _mixer(kind, layer, xin, prev, params):
    name = f"mixer_{kind}{layer}"
    if prev is None:
        return _mixer_call(kind, layer, xin, None, params, name, 0, BATCH, None)
    yk_parts, wcin = prev
    out = None
    for hh, (b0, nb) in enumerate(zip(PART_START, PART_BATCHES)):
        out = _mixer_call(kind, layer, xin, (yk_parts[hh], wcin), params, f"{name}_h{hh}", b0, nb, out)
    return out


def _dest_kernel(ps_ref, ri_ref, o_ref):
    ri = ri_ref[...]
    e = ri[0:2, :]
    start = jnp.zeros_like(e)
    for k in range(NE):
        start = jnp.where(e == k, ps_ref[k], start)
    o_ref[0:2, :] = start + ri[2:4, :]
    o_ref[2:8, :] = jnp.zeros((6, ri.shape[1]), I32)


def _dest_slots(pad_starts, ri):
    tn = 4096
    return pl.pallas_call(
        _dest_kernel,
        grid_spec=pltpu.PrefetchScalarGridSpec(
            num_scalar_prefetch=1,
            grid=(T // tn,),
            in_specs=[pl.BlockSpec((SUBLANES, tn), lambda i, ps: (0, i))],
            out_specs=pl.BlockSpec((SUBLANES, tn), lambda i, ps: (0, i)),
        ),
        out_shape=jax.ShapeDtypeStruct((SUBLANES, T), I32),
        compiler_params=pltpu.CompilerParams(dimension_semantics=("arbitrary",)),
        name="dest_slots",
    )(pad_starts, ri)


def _expert_rows(x_packed, wgu_s, wd_s):
    x = _unpack_rows(x_packed).astype(BF16)
    gu = jnp.dot(x, wgu_s[...], preferred_element_type=F32)
    g = gu[:, :F]
    hid = (g * _sigmoid(g) * gu[:, F:]).astype(BF16)
    return _pack_rows(jnp.dot(hid, wd_s[...], preferred_element_type=F32))


def _expert_kernel(st_ref, nb_ref, xs_hbm, wg_ref, wu_ref, wd_ref, yb_hbm,
                   wgu_s, wd_s, xbuf, obuf, xt0, ot0, xt1, ot1, xsem, osem, tsem):
    e = pl.program_id(0)
    start = st_ref[e]
    xtails, otails = (xt0, xt1), (ot0, ot1)

    def split(n_blocks):
        units = n_blocks // BLOCKS_PER_UNIT
        rem = n_blocks % BLOCKS_PER_UNIT
        on = (rem >= 2, rem % 2 == 1)
        off0 = units * UNIT
        off1 = off0 + jnp.where(on[0], TAILS[0], 0)
        return units, on, (off0, off1)

    n_units, tail_on, tail_off = split(nb_ref[e])

    def x_copy(first_row, u, slot):
        return pltpu.make_async_copy(xs_hbm.at[pl.ds(pl.multiple_of(first_row + u * UNIT, BM), UNIT)],
                                     xbuf.at[slot], xsem.at[slot])

    def o_copy(u, slot):
        return pltpu.make_async_copy(obuf.at[slot],
                                     yb_hbm.at[pl.ds(pl.multiple_of(start + u * UNIT, BM), UNIT)], osem.at[slot])

    def xt_copy(k, first_row, off):
        return pltpu.make_async_copy(xs_hbm.at[pl.ds(pl.multiple_of(first_row + off, BM), TAILS[k])],
                                     xtails[k], tsem.at[2 * k])

    def ot_copy(k):
        return pltpu.make_async_copy(otails[k], yb_hbm.at[pl.ds(pl.multiple_of(start + tail_off[k], BM), TAILS[k])],
                                     tsem.at[2 * k + 1])

    def fetch_first(first_row, n_blocks):
        units, on, off = split(n_blocks)

        @pl.when(units > 0)
        def _():
            x_copy(first_row, 0, 0).start(priority=ROW_DMA_PRIORITY)

        for k in range(len(TAILS)):
            @pl.when(on[k])
            def _():
                xt_copy(k, first_row, off[k]).start(priority=ROW_DMA_PRIORITY)

    @pl.when(e == 0)
    def _():
        fetch_first(start, nb_ref[e])

    @pl.when(nb_ref[e] > 0)
    def _():
        wgu_s[:, :F] = wg_ref[0, 0].astype(BF16)
        wgu_s[:, F:] = wu_ref[0, 0].astype(BF16)
        wd_s[...] = wd_ref[0, 0].astype(BF16)

    @pl.loop(0, n_units, step=2)
    def _(u0):
        for slot in range(2):
            u = u0 + slot

            @pl.when(u < n_units)
            def _():
                x_copy(start, u, slot).wait()

                @pl.when(u + 1 < n_units)
                def _():
                    x_copy(start, u + 1, 1 - slot).start(priority=ROW_DMA_PRIORITY)

                @pl.when(u >= 2)
                def _():
                    o_copy(u - 2, slot).wait()

                obuf[slot] = _expert_rows(xbuf[slot], wgu_s, wd_s)
                o_copy(u, slot).start(priority=ROW_DMA_PRIORITY)

    for k in range(len(TAILS)):
        @pl.when(tail_on[k])
        def _():
            xt_copy(k, start, tail_off[k]).wait()
            otails[k][...] = _expert_rows(xtails[k][...], wgu_s, wd_s)
            ot_copy(k).start(priority=ROW_DMA_PRIORITY)

    nxt = jnp.minimum(e + 1, NE - 1)

    @pl.when(e + 1 < NE)
    def _():
        fetch_first(st_ref[nxt], nb_ref[nxt])

    for back in (1, 2):
        @pl.when(n_units >= back)
        def _():
            last = n_units - back
            o_copy(last, last % 2).wait()

    for k in range(len(TAILS)):
        @pl.when(tail_on[k])
        def _():
            ot_copy(k).wait()


def _experts(layer, starts, n_blocks, xs, w_gate, w_up, w_down, name):
    def w_map(e, st, nb):
        return (layer, e, 0, 0)

    return pl.pallas_call(
        _expert_kernel,
        grid_spec=pltpu.PrefetchScalarGridSpec(
            num_scalar_prefetch=2,
            grid=(NE,),
            in_specs=[
                pl.BlockSpec(memory_space=pl.ANY),
                pl.BlockSpec((1, 1, D, F), w_map),
                pl.BlockSpec((1, 1, D, F), w_map),
                pl.BlockSpec((1, 1, F, D), w_map),
            ],
            out_specs=pl.BlockSpec(memory_space=pl.ANY),
            scratch_shapes=[
                pltpu.VMEM((D, 2 * F), BF16), pltpu.VMEM((F, D), BF16),
                pltpu.VMEM((2, UNIT, DP), U32), pltpu.VMEM((2, UNIT, DP), U32),
                pltpu.VMEM((TAILS[0], DP), U32), pltpu.VMEM((TAILS[0], DP), U32),
                pltpu.VMEM((TAILS[1], DP), U32), pltpu.VMEM((TAILS[1], DP), U32),
                pltpu.SemaphoreType.DMA((2,)), pltpu.SemaphoreType.DMA((2,)),
                pltpu.SemaphoreType.DMA((2 * len(TAILS),)),
            ],
        ),
        out_shape=jax.ShapeDtypeStruct((NSLOT, DP), U32),
        compiler_params=pltpu.CompilerParams(
            dimension_semantics=("arbitrary",), vmem_limit_bytes=VMEM_LIMIT),
        name=name,
    )(starts, n_blocks, xs, w_gate, w_up, w_down)


def _sc_worker_id():
    return lax.axis_index("s") * SC_CORES + lax.axis_index("c")


def _sc_mesh():
    return plsc.VectorSubcoreMesh(core_axis_name="c", subcore_axis_name="s")


def _sc_scratch(n_index_rows):
    return [
        pltpu.VMEM((n_index_rows, SC_CHUNK), I32),
        pltpu.VMEM((2, SC_CHUNK, DP), U32),
        pltpu.SemaphoreType.DMA((2,)),
        pltpu.SemaphoreType.DMA((2,)),
    ]


def _dispatch_rows(h2p, dest):
    per_w = T // SC_WORKERS
    n_chunks = per_w // SC_CHUNK

    @functools.partial(
        pl.kernel, mesh=_sc_mesh(),
        out_type=jax.ShapeDtypeStruct((NSLOT, DP), U32),
        scratch_types=_sc_scratch(TOP_K * n_chunks),
        name="dispatch_rows",
    )
    def k(h2_hbm, dest_hbm, out_hbm, dest_v, rows_v, rsem, wsem):
        wid = _sc_worker_id()
        for kk in range(TOP_K):
            pltpu.sync_copy(dest_hbm.at[kk, pl.ds(wid * n_chunks, n_chunks)],
                            dest_v.at[pl.ds(kk * n_chunks, n_chunks)])
        base = wid * per_w

        def read(c, slot):
            return pltpu.make_async_copy(h2_hbm.at[pl.ds(base + c * SC_CHUNK, SC_CHUNK)],
                                         rows_v.at[slot], rsem.at[slot])

        def write(c, kk, slot):
            return pltpu.make_async_copy(rows_v.at[slot], out_hbm.at[dest_v.at[kk * n_chunks + c]],
                                         wsem.at[slot])

        read(0, 0).start()

        @pl.loop(0, n_chunks, step=2)
        def _(c):
            for b in range(2):
                cc = c + b
                read(cc, b).wait()

                @pl.when(cc + 1 < n_chunks)
                def _():
                    @pl.when(cc >= 1)
                    def _():
                        for kk in range(TOP_K):
                            write(cc - 1, kk, 1 - b).wait()
                    read(cc + 1, 1 - b).start()

                for kk in range(TOP_K):
                    write(cc, kk, b).start()

        for slot, cc in ((0, n_chunks - 2), (1, n_chunks - 1)):
            for kk in range(TOP_K):
                write(cc, kk, slot).wait()

    return k(h2p, dest.reshape(TOP_K, T // SC_CHUNK, SC_CHUNK))


def _return_rows(yb, dest):
    m = dest.size
    per_w = m // SC_WORKERS
    n_chunks = per_w // SC_CHUNK

    @functools.partial(
        pl.kernel, mesh=_sc_mesh(),
        out_type=jax.ShapeDtypeStruct((m, DP), U32),
        scratch_types=_sc_scratch(n_chunks),
        name="return_rows",
    )
    def k(yb_hbm, dest_hbm, out_hbm, idx_v, rows_v, gsem, wsem):
        wid = _sc_worker_id()
        pltpu.sync_copy(dest_hbm.at[pl.ds(wid * n_chunks, n_chunks)], idx_v)
        base = wid * per_w

        def gather(c, slot):
            return pltpu.make_async_copy(yb_hbm.at[idx_v.at[c]], rows_v.at[slot], gsem.at[slot])

        def write(c, slot):
            return pltpu.make_async_copy(rows_v.at[slot], out_hbm.at[pl.ds(base + c * SC_CHUNK, SC_CHUNK)],
                                         wsem.at[slot])

        gather(0, 0).start()

        @pl.loop(0, n_chunks, step=2)
        def _(c):
            for b in range(2):
                cc = c + b
                gather(cc, b).wait()

                @pl.when(cc + 1 < n_chunks)
                def _():
                    @pl.when(cc >= 1)
                    def _():
                        write(cc - 1, 1 - b).wait()
                    gather(cc + 1, 1 - b).start()

                write(cc, b).start()

        write(n_chunks - 2, 0).wait()
        write(n_chunks - 1, 1).wait()

    return k(yb, dest.reshape(m // SC_CHUNK, SC_CHUNK))


def _final_kernel(x1_ref, y0_ref, y1_ref, wc_ref, mods_ref, ln_ref, *rest):
    o_ref = rest[-1]
    ln = ln_ref[0]
    o_ref[0] = _combine(x1_ref[0], y0_ref[0], y1_ref[0], wc_ref[0], mods_ref[0, 0][5:6, :],
                        ln[0:1, :], ln[1:2, :])


def _final(x1, yk_parts, wc, params):
    mods, ffn_ln = params["mods"], params["ffn_ln"]
    out = None
    for hh, (b0, nb) in enumerate(zip(PART_START, PART_BATCHES)):
        args = [x1, yk_parts[hh], yk_parts[hh], wc, mods, ffn_ln]
        specs = [_tile_spec(D, b0), _yk_spec(0), _yk_spec(1), _tile_spec(2 * LANES, b0),
                 _mods_spec(DEPTH - 1, b0), _layer_spec(ffn_ln, DEPTH - 1)]
        aliases = {}
        if out is not None:
            aliases[len(args)] = 0
            args.append(out)
            specs.append(_any_spec())
        out = pl.pallas_call(
            _final_kernel,
            grid=(nb, NS),
            in_specs=specs,
            out_specs=_tile_spec(D, b0),
            out_shape=jax.ShapeDtypeStruct((BATCH, SEQ, D), F32),
            input_output_aliases=aliases,
            compiler_params=pltpu.CompilerParams(
                dimension_semantics=("arbitrary", "arbitrary"), vmem_limit_bytes=VMEM_LIMIT),
            name=f"final_combine_h{hh}",
        )(*args)
    return out


def _plan(counts):
    n_blocks = (counts + BM - 1) // BM
    padded = n_blocks * BM
    pad_starts = (jnp.cumsum(padded) - padded).astype(I32)
    return pad_starts, n_blocks.astype(I32)


def _router_params(w_group, b_group, w_expert, b_expert):
    wr = jnp.zeros((DEPTH, NR, D), F32)
    wr = wr.at[:, 0:N_GROUPS].set(jnp.swapaxes(w_group, 1, 2))
    wr = wr.at[:, SUBLANES:SUBLANES + NE].set(jnp.swapaxes(w_expert, 1, 2))
    rb = jnp.full((DEPTH, NR), NEG, F32)
    rb = rb.at[:, 0:N_GROUPS].set(b_group).at[:, SUBLANES:SUBLANES + NE].set(b_expert)
    rb = rb.at[:, SUBLANES + NE:].set(0.0)
    return wr.astype(BF16), rb.reshape(DEPTH, NR, 1)


def kernel(x, c, ada_w, ada_b, a_w_in, a_b_in, a_w_dw, a_b_dw, a_ln_g, a_ln_b, a_w_out, a_b_out,
           b_w_in, b_w_dw, b_w_out, mix_ln_g, mix_ln_b, ffn_ln_g, ffn_ln_b,
           r_w_group, r_b_group, r_w_expert, r_b_expert, e_w_gate, e_w_up, e_w_down):
    n_a = a_w_in.shape[0]
    wr, rb = _router_params(r_w_group, r_b_group, r_w_expert, r_b_expert)
    params = {
        "mods": _ada_mods(c, ada_w, ada_b),
        "mix_ln": jnp.stack([mix_ln_g, mix_ln_b], axis=1),
        "ffn_ln": jnp.stack([ffn_ln_g, ffn_ln_b], axis=1),
        "wr": wr, "rb": rb,
        "tri": (jnp.arange(TS)[:, None] < jnp.arange(TS)[None, :]).astype(BF16),
        "a": [a_w_in.astype(BF16), a_b_in.reshape(n_a, 1, 2 * D), a_w_dw.reshape(n_a, CONV_A, DT, LANES),
              a_b_dw.reshape(n_a, DT, LANES), jnp.stack([a_ln_g, a_ln_b, a_b_out], axis=1),
              a_w_out.astype(BF16)],
        "b": [b_w_in.astype(BF16), b_w_dw, b_w_out.astype(BF16)],
    }
    prev = None
    xin = x
    for i in range(DEPTH):
        kind = "a" if i % 2 == 0 else "b"
        x1, h2, ri, wc, counts = _mixer(kind, i, xin, prev, params)
        pad_starts, n_blocks = _plan(counts[:, 0])
        dest = _dest_slots(pad_starts, ri)[0:TOP_K]
        xs = _dispatch_rows(h2.reshape(T, DP), dest)
        yb = _experts(i, pad_starts, n_blocks, xs, e_w_gate, e_w_up, e_w_down, name=f"experts{i}")
        yk_parts = [_return_rows(yb, dest[:, b0 * SEQ:(b0 + nb) * SEQ]).reshape(TOP_K, nb * SEQ, DP)
                     for b0, nb in zip(PART_START, PART_BATCHES)]
        prev = (yk_parts, wc)
        xin = x1
    yk_parts, wc = prev
    return _final(xin, yk_parts, wc, params)
```

```python
import functools

import jax
import jax.numpy as jnp
from jax import lax
from jax.experimental import pallas as pl
from jax.experimental.pallas import tpu as pltpu
from jax.experimental.pallas import tpu_sc as plsc

F32 = jnp.float32
BF16 = jnp.bfloat16
I32 = jnp.int32
U32 = jnp.uint32

D = 1024
BATCH = 4
SEQ = 8192
T = BATCH * SEQ
DEPTH = 4
N_GROUPS = 4
EPG = 8
NE = N_GROUPS * EPG
TOP_K = 2
F = D // 2
CONV_A = 31
CONV_B = 3
ALPHA = (2.0 * DEPTH) ** 0.25
LN_EPS = 1e-5

LANES = 128
SUBLANES = 8
VMEM_LIMIT = 56 * 1024 * 1024

TS = 512
SR = 512
NS = SEQ // TS
PART_BATCHES = (1, 3)
PART_START = (0, 1)
assert sum(PART_BATCHES) == BATCH
HALO_A = 32
HALO_B = 8
BM = 256
BLOCKS_PER_UNIT = 4
UNIT = BLOCKS_PER_UNIT * BM
TAILS = (2 * BM, BM)
ROW_DMA_PRIORITY = 1
NSLOT = T * TOP_K + NE * BM
NR = 48
ADA_TN = 1536
NEG = -1e30
DP = D // 2
DT = D // LANES
assert DT == SUBLANES

SC_CORES = 2
SC_SUBCORES = 16
SC_WORKERS = SC_CORES * SC_SUBCORES
SC_CHUNK = 64


def _sigmoid(x):
    return 1.0 / (1.0 + jnp.exp(-x))


def _pack_rows(x):
    return pltpu.pack_elementwise([x[:, :DP], x[:, DP:]], packed_dtype=BF16)


def _unpack_rows(p):
    lo = pltpu.unpack_elementwise(p, index=0, packed_dtype=BF16, unpacked_dtype=F32)
    hi = pltpu.unpack_elementwise(p, index=1, packed_dtype=BF16, unpacked_dtype=F32)
    return jnp.concatenate([lo, hi], axis=1)


def _layer_norm(x, g, b):
    mu = jnp.mean(x, axis=-1, keepdims=True)
    xc = x - mu
    var = jnp.mean(xc * xc, axis=-1, keepdims=True)
    return xc * lax.rsqrt(var + LN_EPS) * g + b


def _ada_kernel(c_ref, w_ref, b_ref, o_ref):
    c = c_ref[...]
    ca = (c * _sigmoid(c)).astype(BF16)
    w = w_ref[0].astype(BF16)
    o_ref[0] = jnp.dot(ca, w, preferred_element_type=F32) + b_ref[0]


def _ada_mods(c, ada_w, ada_b):
    out = pl.pallas_call(
        _ada_kernel,
        grid=(DEPTH, 6 * D // ADA_TN),
        in_specs=[
            pl.BlockSpec((BATCH, D), lambda i, j: (0, 0)),
            pl.BlockSpec((1, D, ADA_TN), lambda i, j: (i, 0, j)),
            pl.BlockSpec((1, 1, ADA_TN), lambda i, j: (i, 0, j)),
        ],
        out_specs=pl.BlockSpec((1, BATCH, ADA_TN), lambda i, j: (i, 0, j)),
        out_shape=jax.ShapeDtypeStruct((DEPTH, BATCH, 6 * D), F32),
        compiler_params=pltpu.CompilerParams(
            dimension_semantics=("arbitrary", "arbitrary"), vmem_limit_bytes=VMEM_LIMIT),
        name="ada_mods",
    )(c, ada_w, ada_b.reshape(DEPTH, 1, 6 * D))
    return out.reshape(DEPTH, BATCH, 6, D)


def _combine(x1, y0p, y1p, wc, g_f, ln_g, ln_b):
    w0 = jnp.tile(wc[:, :LANES], (1, D // LANES))
    w1 = jnp.tile(wc[:, LANES:], (1, D // LANES))
    y = w0 * _unpack_rows(y0p) + w1 * _unpack_rows(y1p)
    return _layer_norm(ALPHA * x1 + (1.0 + g_f) * y, ln_g, ln_b)


def _route(h2, wr_ref, rb_ref, tri_ref, cnt_ref, ri_ref, wc_ref, cnto_ref):
    lt = lax.dot_general(wr_ref[...], h2, (((1,), (1,)), ((), ())),
                         preferred_element_type=F32) + rb_ref[...]
    iota8 = lax.broadcasted_iota(I32, (SUBLANES, TS), 0).astype(F32)
    gl = lt[0:SUBLANES]
    gmax = jnp.max(gl, axis=0, keepdims=True)
    gidx = jnp.min(jnp.where(gl == gmax, iota8, float(SUBLANES)), axis=0, keepdims=True)
    gw = 1.0 / jnp.sum(jnp.exp(gl - gmax), axis=0, keepdims=True)
    el = lt[SUBLANES:2 * SUBLANES]
    for g in range(1, N_GROUPS):
        el = jnp.where(gidx == float(g), lt[SUBLANES * (g + 1):SUBLANES * (g + 2)], el)
    m1 = jnp.max(el, axis=0, keepdims=True)
    i1 = jnp.min(jnp.where(el == m1, iota8, float(SUBLANES)), axis=0, keepdims=True)
    el2 = jnp.where(iota8 == i1, -jnp.inf, el)
    m2 = jnp.max(el2, axis=0, keepdims=True)
    i2 = jnp.min(jnp.where(el2 == m2, iota8, float(SUBLANES)), axis=0, keepdims=True)
    r = jnp.exp(m2 - m1)
    w_a = gw / (1.0 + r)
    w_b = gw * r / (1.0 + r)
    e1 = gidx * float(EPG) + i1
    e2 = gidx * float(EPG) + i2

    iota_e = lax.broadcasted_iota(I32, (NE, TS), 0).astype(F32)
    oh1 = iota_e == e1
    oh2 = iota_e == e2
    oh = jnp.concatenate([jnp.where(oh1, 1.0, 0.0), jnp.where(oh2, 1.0, 0.0)], axis=0)
    before = jnp.dot(oh.astype(BF16), tri_ref[...], preferred_element_type=F32)
    tot = jnp.sum(oh, axis=1, keepdims=True)
    cnt = cnt_ref[...]
    base = jnp.tile(cnt, (1, TS // LANES))
    tot1 = tot[:NE]
    tot2 = tot[NE:]
    rank1 = jnp.sum(jnp.where(oh1, base + before[:NE], 0.0), axis=0, keepdims=True)
    rank2 = jnp.sum(jnp.where(oh2, base + tot1 + before[NE:], 0.0), axis=0, keepdims=True)
    new_cnt = cnt + tot1 + tot2
    cnt_ref[...] = new_cnt
    cnto_ref[...] = new_cnt.astype(I32)

    ri_ref[0:1, :] = e1.astype(I32)
    ri_ref[1:2, :] = e2.astype(I32)
    ri_ref[2:3, :] = rank1.astype(I32)
    ri_ref[3:4, :] = rank2.astype(I32)
    ri_ref[4:8, :] = jnp.zeros((4, TS), I32)
    wc_ref[0, :, :LANES] = jnp.broadcast_to(w_a, (LANES, TS)).T
    wc_ref[0, :, LANES:] = jnp.broadcast_to(w_b, (LANES, TS)).T


def _conv_taps(uext_ref, u, w_dw, halo, width, r0):
    uext_ref[halo + r0:halo + r0 + SR, :] = u
    acc = None
    for k in range(width):
        off = r0 + halo - (width - 1) + k
        term = w_dw[k:k + 1, :] * uext_ref[off:off + SR, :]
        acc = term if acc is None else acc + term
    return acc


def _conv_time_major(tm_ref, o2_ref, u, wk_ref, bias, r0, width, halo):
    for j in range(DT):
        tm_ref[pl.ds((halo + r0) * DT + j, SR, stride=DT), :] = u[:, j * LANES:(j + 1) * LANES]
    acc = None
    for k in range(width):
        off = (r0 + halo - (width - 1) + k) * DT
        term = tm_ref[off:off + SR * DT, :].reshape(SR, DT, LANES) * wk_ref[k]
        acc = term if acc is None else acc + term
    if bias is not None:
        acc = acc + bias
    o2_ref[r0 * DT:(r0 + SR) * DT, :] = acc.reshape(SR * DT, LANES)
    return jnp.concatenate([o2_ref[pl.ds(r0 * DT + j, SR, stride=DT), :] for j in range(DT)], axis=1)


def _mixer_kernel(*refs, kind, has_prev, carry):
    it = iter(refs)
    xin_ref = next(it)
    if has_prev:
        y0_ref, y1_ref, wcin_ref, pmods_ref, pln_ref = (next(it) for _ in range(5))
    if carry:
        for _ in range(4):
            next(it)
        cnt_in_ref = next(it)
    mods_ref, mln_ref, win_ref = next(it), next(it), next(it)
    if kind == "a":
        bin_ref, wdw_ref, bdw_ref, vec_ref = next(it), next(it), next(it), next(it)
    else:
        wdw_ref = next(it)
    wout_ref, wr_ref, rb_ref, tri_ref = (next(it) for _ in range(4))
    x1_ref, h2_ref, ri_ref, wc_ref, cnto_ref = (next(it) for _ in range(5))
    if kind == "a":
        uext_ref, o2_ref, h2b_ref, cnt_ref = (next(it) for _ in range(4))
    else:
        uext_ref, h2b_ref, cnt_ref = (next(it) for _ in range(3))
    win_s, wout_s = next(it), next(it)
    mln_ref, win_ref, wdw_ref, wout_ref, wr_ref, rb_ref = (
        r.at[0] for r in (mln_ref, win_ref, wdw_ref, wout_ref, wr_ref, rb_ref))
    if has_prev:
        pln_ref = pln_ref.at[0]
    if kind == "a":
        bin_ref, bdw_ref, vec_ref = (r.at[0] for r in (bin_ref, bdw_ref, vec_ref))

    first = (pl.program_id(0) == 0) & (pl.program_id(1) == 0)

    @pl.when(first)
    def _():
        if carry:
            cnt_ref[...] = cnt_in_ref[...].astype(F32)
        else:
            cnt_ref[...] = jnp.zeros((NE, LANES), F32)
        win_s[...] = win_ref[...]
        wout_s[...] = wout_ref[...]

    halo_rows = HALO_A * DT if kind == "a" else HALO_B

    @pl.when(pl.program_id(1) == 0)
    def _():
        uext_ref[0:halo_rows, :] = jnp.zeros((halo_rows, uext_ref.shape[1]), F32)

    m = mods_ref[0, 0]
    mln = mln_ref[...]
    for i in range(TS // SR):
        r0 = i * SR
        rows = slice(r0, r0 + SR)
        x = xin_ref[0, rows, :]
        if has_prev:
            pln = pln_ref[...]
            x = _combine(x, y0_ref[0, rows, :], y1_ref[0, rows, :], wcin_ref[0, rows, :],
                         pmods_ref[0, 0][5:6, :], pln[0:1, :], pln[1:2, :])

        h = (x * (1.0 + m[1:2, :]) + m[0:1, :]).astype(BF16)
        cols = [jnp.dot(h, win_s[:, c * D:(c + 1) * D], preferred_element_type=F32)
                for c in range(win_s.shape[1] // D)]
        if kind == "a":
            b_in = bin_ref[...]
            vec = vec_ref[...]
            u = (cols[0] + b_in[:, :D]) * _sigmoid(cols[1] + b_in[:, D:])
            u = _conv_time_major(uext_ref, o2_ref, u, wdw_ref, bdw_ref[...], r0, CONV_A, HALO_A)
            u = _layer_norm(u, vec[0:1, :], vec[1:2, :])
            u = u * _sigmoid(u)
            y = jnp.dot(u.astype(BF16), wout_s[...], preferred_element_type=F32) + vec[2:3, :]
        else:
            gb = cols[0]
            q = cols[1] * cols[2]
            u = _conv_taps(uext_ref, q, wdw_ref[...], HALO_B, CONV_B, r0)
            y = jnp.dot((gb * u).astype(BF16), wout_s[...], preferred_element_type=F32)

        x1 = _layer_norm(ALPHA * x + (1.0 + m[2:3, :]) * y, mln[0:1, :], mln[1:2, :])
        x1_ref[0, rows, :] = x1
        h2 = x1 * (1.0 + m[4:5, :]) + m[3:4, :]
        h2_ref[0, rows, :] = _pack_rows(h2)
        h2b_ref[rows, :] = h2.astype(BF16)

    uext_ref[0:halo_rows, :] = uext_ref[uext_ref.shape[0] - halo_rows:uext_ref.shape[0], :]
    _route(h2b_ref[...], wr_ref, rb_ref, tri_ref, cnt_ref, ri_ref, wc_ref, cnto_ref)


def _tile_spec(width, b0=0):
    return pl.BlockSpec((1, TS, width), lambda b, s: (b + b0, s, 0))


def _yk_spec(k):
    return pl.BlockSpec((1, TS, DP), lambda b, s: (k, b * NS + s, 0))


def _const_spec(shape):
    nd = len(shape)
    return pl.BlockSpec(shape, lambda b, s: (0,) * nd)


def _mods_spec(layer, b0=0):
    return pl.BlockSpec((1, 1, 6, D), lambda b, s: (layer, b + b0, 0, 0))


def _layer_spec(arr, idx):
    tail = arr.shape[1:]
    return pl.BlockSpec((1,) + tail, lambda b, s: (idx,) + (0,) * len(tail))


def _any_spec():
    return pl.BlockSpec(memory_space=pl.ANY)


def _mixer_call(kind, layer, xin, prev, params, name, b0, n_batch, carried):
    mods, mix_ln, ffn_ln, wr, rb, tri = (params[k] for k in ("mods", "mix_ln", "ffn_ln", "wr", "rb", "tri"))
    weights = params[kind]
    j = layer // 2
    has_prev = prev is not None
    carry = carried is not None
    args = [xin]
    specs = [_tile_spec(D, b0)]
    if has_prev:
        yk, wcin = prev
        args += [yk, yk, wcin, mods, ffn_ln]
        specs += [_yk_spec(0), _yk_spec(1), _tile_spec(2 * LANES, b0), _mods_spec(layer - 1, b0),
                  _layer_spec(ffn_ln, layer - 1)]
    aliases = {}
    if carry:
        for out_idx, arr in enumerate(carried):
            aliases[len(args)] = out_idx
            args.append(arr)
            specs.append(_any_spec() if out_idx < 4 else _const_spec((NE, LANES)))
    args += [mods, mix_ln]
    specs += [_mods_spec(layer, b0), _layer_spec(mix_ln, layer)]
    for w in weights:
        args.append(w)
        specs.append(_layer_spec(w, j if w.shape[0] > 1 else 0))
    args += [wr, rb, tri]
    specs += [_layer_spec(wr, layer), _layer_spec(rb, layer), _const_spec(tri.shape)]
    if kind == "a":
        conv_scratch = [pltpu.VMEM(((TS + HALO_A) * DT, LANES), F32), pltpu.VMEM((TS * DT, LANES), F32)]
    else:
        conv_scratch = [pltpu.VMEM((TS + HALO_B, D), F32)]
    out_shape = (
        jax.ShapeDtypeStruct((BATCH, SEQ, D), F32),
        jax.ShapeDtypeStruct((BATCH, SEQ, DP), U32),
        jax.ShapeDtypeStruct((SUBLANES, T), I32),
        jax.ShapeDtypeStruct((BATCH, SEQ, 2 * LANES), F32),
        jax.ShapeDtypeStruct((NE, LANES), I32),
    )
    out_specs = (
        _tile_spec(D, b0), _tile_spec(DP, b0),
        pl.BlockSpec((SUBLANES, TS), lambda b, s: (0, (b + b0) * NS + s)),
        _tile_spec(2 * LANES, b0),
        pl.BlockSpec((NE, LANES), lambda b, s: (0, 0)),
    )
    return pl.pallas_call(
        functools.partial(_mixer_kernel, kind=kind, has_prev=has_prev, carry=carry),
        grid=(n_batch, NS),
        in_specs=specs,
        out_specs=out_specs,
        out_shape=out_shape,
        input_output_aliases=aliases,
        scratch_shapes=conv_scratch + [pltpu.VMEM((TS, D), BF16), pltpu.VMEM((NE, LANES), F32),
                                       pltpu.VMEM(weights[0].shape[1:], BF16),
                                       pltpu.VMEM(weights[-1].shape[1:], BF16)],
        compiler_params=pltpu.CompilerParams(
            dimension_semantics=("arbitrary", "arbitrary"), vmem_limit_bytes=VMEM_LIMIT),
        name=name,
    )(*args)


def _mixer(kind, layer, xin, prev, params):
    name = f"mixer_{kind}{layer}"
    if prev is None:
        return _mixer_call(kind, layer, xin, None, params, name, 0, BATCH, None)
    yk_parts, wcin = prev
    out = None
    for hh, (b0, nb) in enumerate(zip(PART_START, PART_BATCHES)):
        out = _mixer_call(kind, layer, xin, (yk_parts[hh], wcin), params, f"{name}_h{hh}", b0, nb, out)
    return out


def _dest_kernel(ps_ref, ri_ref, o_ref):
    ri = ri_ref[...]
    e = ri[0:2, :]
    start = jnp.zeros_like(e)
    for k in range(NE):
        start = jnp.where(e == k, ps_ref[k], start)
    o_ref[0:2, :] = start + ri[2:4, :]
    o_ref[2:8, :] = jnp.zeros((6, ri.shape[1]), I32)


def _dest_slots(pad_starts, ri):
    tn = 4096
    return pl.pallas_call(
        _dest_kernel,
        grid_spec=pltpu.PrefetchScalarGridSpec(
            num_scalar_prefetch=1,
            grid=(T // tn,),
            in_specs=[pl.BlockSpec((SUBLANES, tn), lambda i, ps: (0, i))],
            out_specs=pl.BlockSpec((SUBLANES, tn), lambda i, ps: (0, i)),
        ),
        out_shape=jax.ShapeDtypeStruct((SUBLANES, T), I32),
        compiler_params=pltpu.CompilerParams(dimension_semantics=("arbitrary",)),
        name="dest_slots",
    )(pad_starts, ri)


def _expert_rows(x_packed, wgu_s, wd_s):
    x = _unpack_rows(x_packed).astype(BF16)
    gu = jnp.dot(x, wgu_s[...], preferred_element_type=F32)
    g = gu[:, :F]
    hid = (g * _sigmoid(g) * gu[:, F:]).astype(BF16)
    return _pack_rows(jnp.dot(hid, wd_s[...], preferred_element_type=F32))


def _expert_kernel(st_ref, nb_ref, xs_hbm, wg_ref, wu_ref, wd_ref, yb_hbm,
                   wgu_s, wd_s, xbuf, obuf, xt0, ot0, xt1, ot1, xsem, osem, tsem):
    e = pl.program_id(0)
    start = st_ref[e]
    xtails, otails = (xt0, xt1), (ot0, ot1)

    def split(n_blocks):
        units = n_blocks // BLOCKS_PER_UNIT
        rem = n_blocks % BLOCKS_PER_UNIT
        on = (rem >= 2, rem % 2 == 1)
        off0 = units * UNIT
        off1 = off0 + jnp.where(on[0], TAILS[0], 0)
        return units, on, (off0, off1)

    n_units, tail_on, tail_off = split(nb_ref[e])

    def x_copy(first_row, u, slot):
        return pltpu.make_async_copy(xs_hbm.at[pl.ds(pl.multiple_of(first_row + u * UNIT, BM), UNIT)],
                                     xbuf.at[slot], xsem.at[slot])

    def o_copy(u, slot):
        return pltpu.make_async_copy(obuf.at[slot],
                                     yb_hbm.at[pl.ds(pl.multiple_of(start + u * UNIT, BM), UNIT)], osem.at[slot])

    def xt_copy(k, first_row, off):
        return pltpu.make_async_copy(xs_hbm.at[pl.ds(pl.multiple_of(first_row + off, BM), TAILS[k])],
                                     xtails[k], tsem.at[2 * k])

    def ot_copy(k):
        return pltpu.make_async_copy(otails[k], yb_hbm.at[pl.ds(pl.multiple_of(start + tail_off[k], BM), TAILS[k])],
                                     tsem.at[2 * k + 1])

    def fetch_first(first_row, n_blocks):
        units, on, off = split(n_blocks)

        @pl.when(units > 0)
        def _():
            x_copy(first_row, 0, 0).start(priority=ROW_DMA_PRIORITY)

        for k in range(len(TAILS)):
            @pl.when(on[k])
            def _():
                xt_copy(k, first_row, off[k]).start(priority=ROW_DMA_PRIORITY)

    @pl.when(e == 0)
    def _():
        fetch_first(start, nb_ref[e])

    @pl.when(nb_ref[e] > 0)
    def _():
        wgu_s[:, :F] = wg_ref[0, 0].astype(BF16)
        wgu_s[:, F:] = wu_ref[0, 0].astype(BF16)
        wd_s[...] = wd_ref[0, 0].astype(BF16)

    @pl.loop(0, n_units, step=2)
    def _(u0):
        for slot in range(2):
            u = u0 + slot

            @pl.when(u < n_units)
            def _():
                x_copy(start, u, slot).wait()

                @pl.when(u + 1 < n_units)
                def _():
                    x_copy(start, u + 1, 1 - slot).start(priority=ROW_DMA_PRIORITY)

                @pl.when(u >= 2)
                def _():
                    o_copy(u - 2, slot).wait()

                obuf[slot] = _expert_rows(xbuf[slot], wgu_s, wd_s)
                o_copy(u, slot).start(priority=ROW_DMA_PRIORITY)

    for k in range(len(TAILS)):
        @pl.when(tail_on[k])
        def _():
            xt_copy(k, start, tail_off[k]).wait()
            otails[k][...] = _expert_rows(xtails[k][...], wgu_s, wd_s)
            ot_copy(k).start(priority=ROW_DMA_PRIORITY)

    nxt = jnp.minimum(e + 1, NE - 1)

    @pl.when(e + 1 < NE)
    def _():
        fetch_first(st_ref[nxt], nb_ref[nxt])

    for back in (1, 2):
        @pl.when(n_units >= back)
        def _():
            last = n_units - back
            o_copy(last, last % 2).wait()

    for k in range(len(TAILS)):
        @pl.when(tail_on[k])
        def _():
            ot_copy(k).wait()


def _experts(layer, starts, n_blocks, xs, w_gate, w_up, w_down, name):
    def w_map(e, st, nb):
        return (layer, e, 0, 0)

    return pl.pallas_call(
        _expert_kernel,
        grid_spec=pltpu.PrefetchScalarGridSpec(
            num_scalar_prefetch=2,
            grid=(NE,),
            in_specs=[
                pl.BlockSpec(memory_space=pl.ANY),
                pl.BlockSpec((1, 1, D, F), w_map),
                pl.BlockSpec((1, 1, D, F), w_map),
                pl.BlockSpec((1, 1, F, D), w_map),
            ],
            out_specs=pl.BlockSpec(memory_space=pl.ANY),
            scratch_shapes=[
                pltpu.VMEM((D, 2 * F), BF16), pltpu.VMEM((F, D), BF16),
                pltpu.VMEM((2, UNIT, DP), U32), pltpu.VMEM((2, UNIT, DP), U32),
                pltpu.VMEM((TAILS[0], DP), U32), pltpu.VMEM((TAILS[0], DP), U32),
                pltpu.VMEM((TAILS[1], DP), U32), pltpu.VMEM((TAILS[1], DP), U32),
                pltpu.SemaphoreType.DMA((2,)), pltpu.SemaphoreType.DMA((2,)),
                pltpu.SemaphoreType.DMA((2 * len(TAILS),)),
            ],
        ),
        out_shape=jax.ShapeDtypeStruct((NSLOT, DP), U32),
        compiler_params=pltpu.CompilerParams(
            dimension_semantics=("arbitrary",), vmem_limit_bytes=VMEM_LIMIT),
        name=name,
    )(starts, n_blocks, xs, w_gate, w_up, w_down)


def _sc_worker_id():
    return lax.axis_index("s") * SC_CORES + lax.axis_index("c")


def _sc_mesh():
    return plsc.VectorSubcoreMesh(core_axis_name="c", subcore_axis_name="s")


def _sc_scratch(n_index_rows):
    return [
        pltpu.VMEM((n_index_rows, SC_CHUNK), I32),
        pltpu.VMEM((2, SC_CHUNK, DP), U32),
        pltpu.SemaphoreType.DMA((2,)),
        pltpu.SemaphoreType.DMA((2,)),
    ]


def _dispatch_rows(h2p, dest):
    per_w = T // SC_WORKERS
    n_chunks = per_w // SC_CHUNK

    @functools.partial(
        pl.kernel, mesh=_sc_mesh(),
        out_type=jax.ShapeDtypeStruct((NSLOT, DP), U32),
        scratch_types=_sc_scratch(TOP_K * n_chunks),
        name="dispatch_rows",
    )
    def k(h2_hbm, dest_hbm, out_hbm, dest_v, rows_v, rsem, wsem):
        wid = _sc_worker_id()
        for kk in range(TOP_K):
            pltpu.sync_copy(dest_hbm.at[kk, pl.ds(wid * n_chunks, n_chunks)],
                            dest_v.at[pl.ds(kk * n_chunks, n_chunks)])
        base = wid * per_w

        def read(c, slot):
            return pltpu.make_async_copy(h2_hbm.at[pl.ds(base + c * SC_CHUNK, SC_CHUNK)],
                                         rows_v.at[slot], rsem.at[slot])

        def write(c, kk, slot):
            return pltpu.make_async_copy(rows_v.at[slot], out_hbm.at[dest_v.at[kk * n_chunks + c]],
                                         wsem.at[slot])

        read(0, 0).start()

        @pl.loop(0, n_chunks, step=2)
        def _(c):
            for b in range(2):
                cc = c + b
                read(cc, b).wait()

                @pl.when(cc + 1 < n_chunks)
                def _():
                    @pl.when(cc >= 1)
                    def _():
                        for kk in range(TOP_K):
                            write(cc - 1, kk, 1 - b).wait()
                    read(cc + 1, 1 - b).start()

                for kk in range(TOP_K):
                    write(cc, kk, b).start()

        for slot, cc in ((0, n_chunks - 2), (1, n_chunks - 1)):
            for kk in range(TOP_K):
                write(cc, kk, slot).wait()

    return k(h2p, dest.reshape(TOP_K, T // SC_CHUNK, SC_CHUNK))


def _return_rows(yb, dest):
    m = dest.size
    per_w = m // SC_WORKERS
    n_chunks = per_w // SC_CHUNK

    @functools.partial(
        pl.kernel, mesh=_sc_mesh(),
        out_type=jax.ShapeDtypeStruct((m, DP), U32),
        scratch_types=_sc_scratch(n_chunks),
        name="return_rows",
    )
    def k(yb_hbm, dest_hbm, out_hbm, idx_v, rows_v, gsem, wsem):
        wid = _sc_worker_id()
        pltpu.sync_copy(dest_hbm.at[pl.ds(wid * n_chunks, n_chunks)], idx_v)
        base = wid * per_w

        def gather(c, slot):
            return pltpu.make_async_copy(yb_hbm.at[idx_v.at[c]], rows_v.at[slot], gsem.at[slot])

        def write(c, slot):
            return pltpu.make_async_copy(rows_v.at[slot], out_hbm.at[pl.ds(base + c * SC_CHUNK, SC_CHUNK)],
                                         wsem.at[slot])

        gather(0, 0).start()

        @pl.loop(0, n_chunks, step=2)
        def _(c):
            for b in range(2):
                cc = c + b
                gather(cc, b).wait()

                @pl.when(cc + 1 < n_chunks)
                def _():
                    @pl.when(cc >= 1)
                    def _():
                        write(cc - 1, 1 - b).wait()
                    gather(cc + 1, 1 - b).start()

                write(cc, b).start()

        write(n_chunks - 2, 0).wait()
        write(n_chunks - 1, 1).wait()

    return k(yb, dest.reshape(m // SC_CHUNK, SC_CHUNK))


def _final_kernel(x1_ref, y0_ref, y1_ref, wc_ref, mods_ref, ln_ref, *rest):
    o_ref = rest[-1]
    ln = ln_ref[0]
    o_ref[0] = _combine(x1_ref[0], y0_ref[0], y1_ref[0], wc_ref[0], mods_ref[0, 0][5:6, :],
                        ln[0:1, :], ln[1:2, :])


def _final(x1, yk_parts, wc, params):
    mods, ffn_ln = params["mods"], params["ffn_ln"]
    out = None
    for hh, (b0, nb) in enumerate(zip(PART_START, PART_BATCHES)):
        args = [x1, yk_parts[hh], yk_parts[hh], wc, mods, ffn_ln]
        specs = [_tile_spec(D, b0), _yk_spec(0), _yk_spec(1), _tile_spec(2 * LANES, b0),
                 _mods_spec(DEPTH - 1, b0), _layer_spec(ffn_ln, DEPTH - 1)]
        aliases = {}
        if out is not None:
            aliases[len(args)] = 0
            args.append(out)
            specs.append(_any_spec())
        out = pl.pallas_call(
            _final_kernel,
            grid=(nb, NS),
            in_specs=specs,
            out_specs=_tile_spec(D, b0),
            out_shape=jax.ShapeDtypeStruct((BATCH, SEQ, D), F32),
            input_output_aliases=aliases,
            compiler_params=pltpu.CompilerParams(
                dimension_semantics=("arbitrary", "arbitrary"), vmem_limit_bytes=VMEM_LIMIT),
            name=f"final_combine_h{hh}",
        )(*args)
    return out


def _plan(counts):
    n_blocks = (counts + BM - 1) // BM
    padded = n_blocks * BM
    pad_starts = (jnp.cumsum(padded) - padded).astype(I32)
    return pad_starts, n_blocks.astype(I32)


def _router_params(w_group, b_group, w_expert, b_expert):
    wr = jnp.zeros((DEPTH, NR, D), F32)
    wr = wr.at[:, 0:N_GROUPS].set(jnp.swapaxes(w_group, 1, 2))
    wr = wr.at[:, SUBLANES:SUBLANES + NE].set(jnp.swapaxes(w_expert, 1, 2))
    rb = jnp.full((DEPTH, NR), NEG, F32)
    rb = rb.at[:, 0:N_GROUPS].set(b_group).at[:, SUBLANES:SUBLANES + NE].set(b_expert)
    rb = rb.at[:, SUBLANES + NE:].set(0.0)
    return wr.astype(BF16), rb.reshape(DEPTH, NR, 1)


def kernel(x, c, ada_w, ada_b, a_w_in, a_b_in, a_w_dw, a_b_dw, a_ln_g, a_ln_b, a_w_out, a_b_out,
           b_w_in, b_w_dw, b_w_out, mix_ln_g, mix_ln_b, ffn_ln_g, ffn_ln_b,
           r_w_group, r_b_group, r_w_expert, r_b_expert, e_w_gate, e_w_up, e_w_down):
    n_a = a_w_in.shape[0]
    wr, rb = _router_params(r_w_group, r_b_group, r_w_expert, r_b_expert)

    def cast_layer(w, layer, after):
        wl = w[layer // 2:layer // 2 + 1]
        if after is not None:
            wl = wl + after * 0.0
        return wl.astype(BF16)

    params = {
        "mods": _ada_mods(c, ada_w, ada_b),
        "mix_ln": jnp.stack([mix_ln_g, mix_ln_b], axis=1),
        "ffn_ln": jnp.stack([ffn_ln_g, ffn_ln_b], axis=1),
        "wr": wr, "rb": rb,
        "tri": (jnp.arange(TS)[:, None] < jnp.arange(TS)[None, :]).astype(BF16),
        "a": [None, a_b_in.reshape(n_a, 1, 2 * D), a_w_dw.reshape(n_a, CONV_A, DT, LANES),
              a_b_dw.reshape(n_a, DT, LANES), jnp.stack([a_ln_g, a_ln_b, a_b_out], axis=1), None],
        "b": [None, b_w_dw, None],
    }
    w_in = {"a": a_w_in, "b": b_w_in}
    w_out = {"a": a_w_out, "b": b_w_out}
    prev = None
    xin = x
    after = None
    for i in range(DEPTH):
        kind = "a" if i % 2 == 0 else "b"
        params[kind][0] = cast_layer(w_in[kind], i, after)
        params[kind][-1] = cast_layer(w_out[kind], i, after)
        x1, h2, ri, wc, counts = _mixer(kind, i, xin, prev, params)
        after = counts[0, 0].astype(F32)
        pad_starts, n_blocks = _plan(counts[:, 0])
        dest = _dest_slots(pad_starts, ri)[0:TOP_K]
        xs = _dispatch_rows(h2.reshape(T, DP), dest)
        yb = _experts(i, pad_starts, n_blocks, xs, e_w_gate, e_w_up, e_w_down, name=f"experts{i}")
        yk_parts = [_return_rows(yb, dest[:, b0 * SEQ:(b0 + nb) * SEQ]).reshape(TOP_K, nb * SEQ, DP)
                     for b0, nb in zip(PART_START, PART_BATCHES)]
        prev = (yk_parts, wc)
        xin = x1
    yk_parts, wc = prev
    return _final(xin, yk_parts, wc, params)
```

```python
import functools

import jax
import jax.numpy as jnp
from jax import lax
from jax.experimental import pallas as pl
from jax.experimental.pallas import tpu as pltpu
from jax.experimental.pallas import tpu_sc as plsc

F32 = jnp.float32
BF16 = jnp.bfloat16
I32 = jnp.int32
U32 = jnp.uint32

D = 1024
BATCH = 4
SEQ = 8192
T = BATCH * SEQ
DEPTH = 4
N_GROUPS = 4
EPG = 8
NE = N_GROUPS * EPG
TOP_K = 2
F = D // 2
CONV_A = 31
CONV_B = 3
ALPHA = (2.0 * DEPTH) ** 0.25
LN_EPS = 1e-5

LANES = 128
SUBLANES = 8
VMEM_LIMIT = 56 * 1024 * 1024

TS = 512
SR = 512
NS = SEQ // TS
PART_BATCHES = (1, 3)
PART_START = (0, 1)
assert sum(PART_BATCHES) == BATCH
HALO_A = 32
HALO_B = 8
BM = 256
BLOCKS_PER_UNIT = 4
UNIT = BLOCKS_PER_UNIT * BM
TAILS = (2 * BM, BM)
ROW_DMA_PRIORITY = 1
NSLOT = T * TOP_K + NE * BM
NR = 48
ADA_TN = 1536
NEG = -1e30
DP = D // 2
DT = D // LANES
assert DT == SUBLANES

SC_CORES = 2
SC_SUBCORES = 16
SC_WORKERS = SC_CORES * SC_SUBCORES
SC_CHUNK = 64


def _sigmoid(x):
    return 1.0 / (1.0 + jnp.exp(-x))


def _pack_rows(x):
    return pltpu.pack_elementwise([x[:, :DP], x[:, DP:]], packed_dtype=BF16)


def _unpack_rows(p):
    lo = pltpu.unpack_elementwise(p, index=0, packed_dtype=BF16, unpacked_dtype=F32)
    hi = pltpu.unpack_elementwise(p, index=1, packed_dtype=BF16, unpacked_dtype=F32)
    return jnp.concatenate([lo, hi], axis=1)


def _layer_norm(x, g, b):
    mu = jnp.mean(x, axis=-1, keepdims=True)
    xc = x - mu
    var = jnp.mean(xc * xc, axis=-1, keepdims=True)
    return xc * lax.rsqrt(var + LN_EPS) * g + b


def _ada_kernel(c_ref, w_ref, b_ref, o_ref):
    c = c_ref[...]
    ca = (c * _sigmoid(c)).astype(BF16)
    w = w_ref[0].astype(BF16)
    o_ref[0] = jnp.dot(ca, w, preferred_element_type=F32) + b_ref[0]


def _ada_mods(layer, c, ada_w, ada_b):
    out = pl.pallas_call(
        _ada_kernel,
        grid=(1, 6 * D // ADA_TN),
        in_specs=[
            pl.BlockSpec((BATCH, D), lambda i, j: (0, 0)),
            pl.BlockSpec((1, D, ADA_TN), lambda i, j: (layer, 0, j)),
            pl.BlockSpec((1, 1, ADA_TN), lambda i, j: (layer, 0, j)),
        ],
        out_specs=pl.BlockSpec((1, BATCH, ADA_TN), lambda i, j: (0, 0, j)),
        out_shape=jax.ShapeDtypeStruct((1, BATCH, 6 * D), F32),
        compiler_params=pltpu.CompilerParams(
            dimension_semantics=("arbitrary", "arbitrary"), vmem_limit_bytes=VMEM_LIMIT),
        name=f"ada_mods{layer}",
    )(c, ada_w, ada_b.reshape(DEPTH, 1, 6 * D))
    return out.reshape(1, BATCH, 6, D)


def _combine(x1, y0p, y1p, wc, g_f, ln_g, ln_b):
    w0 = jnp.tile(wc[:, :LANES], (1, D // LANES))
    w1 = jnp.tile(wc[:, LANES:], (1, D // LANES))
    y = w0 * _unpack_rows(y0p) + w1 * _unpack_rows(y1p)
    return _layer_norm(ALPHA * x1 + (1.0 + g_f) * y, ln_g, ln_b)


def _route(h2, wr_ref, rb_ref, tri_ref, cnt_ref, ri_ref, wc_ref, cnto_ref):
    lt = lax.dot_general(wr_ref[...], h2, (((1,), (1,)), ((), ())),
                         preferred_element_type=F32) + rb_ref[...]
    iota8 = lax.broadcasted_iota(I32, (SUBLANES, TS), 0).astype(F32)
    gl = lt[0:SUBLANES]
    gmax = jnp.max(gl, axis=0, keepdims=True)
    gidx = jnp.min(jnp.where(gl == gmax, iota8, float(SUBLANES)), axis=0, keepdims=True)
    gw = 1.0 / jnp.sum(jnp.exp(gl - gmax), axis=0, keepdims=True)
    el = lt[SUBLANES:2 * SUBLANES]
    for g in range(1, N_GROUPS):
        el = jnp.where(gidx == float(g), lt[SUBLANES * (g + 1):SUBLANES * (g + 2)], el)
    m1 = jnp.max(el, axis=0, keepdims=True)
    i1 = jnp.min(jnp.where(el == m1, iota8, float(SUBLANES)), axis=0, keepdims=True)
    el2 = jnp.where(iota8 == i1, -jnp.inf, el)
    m2 = jnp.max(el2, axis=0, keepdims=True)
    i2 = jnp.min(jnp.where(el2 == m2, iota8, float(SUBLANES)), axis=0, keepdims=True)
    r = jnp.exp(m2 - m1)
    w_a = gw / (1.0 + r)
    w_b = gw * r / (1.0 + r)
    e1 = gidx * float(EPG) + i1
    e2 = gidx * float(EPG) + i2

    iota_e = lax.broadcasted_iota(I32, (NE, TS), 0).astype(F32)
    oh1 = iota_e == e1
    oh2 = iota_e == e2
    oh = jnp.concatenate([jnp.where(oh1, 1.0, 0.0), jnp.where(oh2, 1.0, 0.0)], axis=0)
    before = jnp.dot(oh.astype(BF16), tri_ref[...], preferred_element_type=F32)
    tot = jnp.sum(oh, axis=1, keepdims=True)
    cnt = cnt_ref[...]
    base = jnp.tile(cnt, (1, TS // LANES))
    tot1 = tot[:NE]
    tot2 = tot[NE:]
    rank1 = jnp.sum(jnp.where(oh1, base + before[:NE], 0.0), axis=0, keepdims=True)
    rank2 = jnp.sum(jnp.where(oh2, base + tot1 + before[NE:], 0.0), axis=0, keepdims=True)
    new_cnt = cnt + tot1 + tot2
    cnt_ref[...] = new_cnt
    cnto_ref[...] = new_cnt.astype(I32)

    ri_ref[0:1, :] = e1.astype(I32)
    ri_ref[1:2, :] = e2.astype(I32)
    ri_ref[2:3, :] = rank1.astype(I32)
    ri_ref[3:4, :] = rank2.astype(I32)
    ri_ref[4:8, :] = jnp.zeros((4, TS), I32)
    wc_ref[0, :, :LANES] = jnp.broadcast_to(w_a, (LANES, TS)).T
    wc_ref[0, :, LANES:] = jnp.broadcast_to(w_b, (LANES, TS)).T


def _conv_taps(uext_ref, u, w_dw, halo, width, r0):
    uext_ref[halo + r0:halo + r0 + SR, :] = u
    acc = None
    for k in range(width):
        off = r0 + halo - (width - 1) + k
        term = w_dw[k:k + 1, :] * uext_ref[off:off + SR, :]
        acc = term if acc is None else acc + term
    return acc


def _conv_time_major(tm_ref, o2_ref, u, wk_ref, bias, r0, width, halo):
    for j in range(DT):
        tm_ref[pl.ds((halo + r0) * DT + j, SR, stride=DT), :] = u[:, j * LANES:(j + 1) * LANES]
    acc = None
    for k in range(width):
        off = (r0 + halo - (width - 1) + k) * DT
        term = tm_ref[off:off + SR * DT, :].reshape(SR, DT, LANES) * wk_ref[k]
        acc = term if acc is None else acc + term
    if bias is not None:
        acc = acc + bias
    o2_ref[r0 * DT:(r0 + SR) * DT, :] = acc.reshape(SR * DT, LANES)
    return jnp.concatenate([o2_ref[pl.ds(r0 * DT + j, SR, stride=DT), :] for j in range(DT)], axis=1)


def _mixer_kernel(*refs, kind, has_prev, carry):
    it = iter(refs)
    xin_ref = next(it)
    if has_prev:
        y0_ref, y1_ref, wcin_ref, pmods_ref, pln_ref = (next(it) for _ in range(5))
    if carry:
        for _ in range(4):
            next(it)
        cnt_in_ref = next(it)
    mods_ref, mln_ref, win_ref = next(it), next(it), next(it)
    if kind == "a":
        bin_ref, wdw_ref, bdw_ref, vec_ref = next(it), next(it), next(it), next(it)
    else:
        wdw_ref = next(it)
    wout_ref, wr_ref, rb_ref, tri_ref = (next(it) for _ in range(4))
    x1_ref, h2_ref, ri_ref, wc_ref, cnto_ref = (next(it) for _ in range(5))
    if kind == "a":
        uext_ref, o2_ref, h2b_ref, cnt_ref = (next(it) for _ in range(4))
    else:
        uext_ref, h2b_ref, cnt_ref = (next(it) for _ in range(3))
    win_s, wout_s = next(it), next(it)
    mln_ref, win_ref, wdw_ref, wout_ref, wr_ref, rb_ref = (
        r.at[0] for r in (mln_ref, win_ref, wdw_ref, wout_ref, wr_ref, rb_ref))
    if has_prev:
        pln_ref = pln_ref.at[0]
    if kind == "a":
        bin_ref, bdw_ref, vec_ref = (r.at[0] for r in (bin_ref, bdw_ref, vec_ref))

    first = (pl.program_id(0) == 0) & (pl.program_id(1) == 0)

    @pl.when(first)
    def _():
        if carry:
            cnt_ref[...] = cnt_in_ref[...].astype(F32)
        else:
            cnt_ref[...] = jnp.zeros((NE, LANES), F32)
        win_s[...] = win_ref[...]
        wout_s[...] = wout_ref[...]

    halo_rows = HALO_A * DT if kind == "a" else HALO_B

    @pl.when(pl.program_id(1) == 0)
    def _():
        uext_ref[0:halo_rows, :] = jnp.zeros((halo_rows, uext_ref.shape[1]), F32)

    m = mods_ref[0, 0]
    mln = mln_ref[...]
    for i in range(TS // SR):
        r0 = i * SR
        rows = slice(r0, r0 + SR)
        x = xin_ref[0, rows, :]
        if has_prev:
            pln = pln_ref[...]
            x = _combine(x, y0_ref[0, rows, :], y1_ref[0, rows, :], wcin_ref[0, rows, :],
                         pmods_ref[0, 0][5:6, :], pln[0:1, :], pln[1:2, :])

        h = (x * (1.0 + m[1:2, :]) + m[0:1, :]).astype(BF16)
        cols = [jnp.dot(h, win_s[:, c * D:(c + 1) * D], preferred_element_type=F32)
                for c in range(win_s.shape[1] // D)]
        if kind == "a":
            b_in = bin_ref[...]
            vec = vec_ref[...]
            u = (cols[0] + b_in[:, :D]) * _sigmoid(cols[1] + b_in[:, D:])
            u = _conv_time_major(uext_ref, o2_ref, u, wdw_ref, bdw_ref[...], r0, CONV_A, HALO_A)
            u = _layer_norm(u, vec[0:1, :], vec[1:2, :])
            u = u * _sigmoid(u)
            y = jnp.dot(u.astype(BF16), wout_s[...], preferred_element_type=F32) + vec[2:3, :]
        else:
            gb = cols[0]
            q = cols[1] * cols[2]
            u = _conv_taps(uext_ref, q, wdw_ref[...], HALO_B, CONV_B, r0)
            y = jnp.dot((gb * u).astype(BF16), wout_s[...], preferred_element_type=F32)

        x1 = _layer_norm(ALPHA * x + (1.0 + m[2:3, :]) * y, mln[0:1, :], mln[1:2, :])
        x1_ref[0, rows, :] = x1
        h2 = x1 * (1.0 + m[4:5, :]) + m[3:4, :]
        h2_ref[0, rows, :] = _pack_rows(h2)
        h2b_ref[rows, :] = h2.astype(BF16)

    uext_ref[0:halo_rows, :] = uext_ref[uext_ref.shape[0] - halo_rows:uext_ref.shape[0], :]
    _route(h2b_ref[...], wr_ref, rb_ref, tri_ref, cnt_ref, ri_ref, wc_ref, cnto_ref)


def _tile_spec(width, b0=0):
    return pl.BlockSpec((1, TS, width), lambda b, s: (b + b0, s, 0))


def _yk_spec(k):
    return pl.BlockSpec((1, TS, DP), lambda b, s: (k, b * NS + s, 0))


def _const_spec(shape):
    nd = len(shape)
    return pl.BlockSpec(shape, lambda b, s: (0,) * nd)


def _mods_spec(b0=0):
    return pl.BlockSpec((1, 1, 6, D), lambda b, s: (0, b + b0, 0, 0))


def _layer_spec(arr, idx):
    tail = arr.shape[1:]
    return pl.BlockSpec((1,) + tail, lambda b, s: (idx,) + (0,) * len(tail))


def _any_spec():
    return pl.BlockSpec(memory_space=pl.ANY)


def _mixer_call(kind, layer, xin, prev, params, name, b0, n_batch, carried):
    mix_ln, ffn_ln, wr, rb, tri = (params[k] for k in ("mix_ln", "ffn_ln", "wr", "rb", "tri"))
    mods = params["mods"][layer]
    weights = params[kind]
    j = layer // 2
    has_prev = prev is not None
    carry = carried is not None
    args = [xin]
    specs = [_tile_spec(D, b0)]
    if has_prev:
        yk, wcin = prev
        args += [yk, yk, wcin, params["mods"][layer - 1], ffn_ln]
        specs += [_yk_spec(0), _yk_spec(1), _tile_spec(2 * LANES, b0), _mods_spec(b0),
                  _layer_spec(ffn_ln, layer - 1)]
    aliases = {}
    if carry:
        for out_idx, arr in enumerate(carried):
            aliases[len(args)] = out_idx
            args.append(arr)
            specs.append(_any_spec() if out_idx < 4 else _const_spec((NE, LANES)))
    args += [mods, mix_ln]
    specs += [_mods_spec(b0), _layer_spec(mix_ln, layer)]
    for w in weights:
        args.append(w)
        specs.append(_layer_spec(w, j if w.shape[0] > 1 else 0))
    args += [wr, rb, tri]
    specs += [_layer_spec(wr, layer), _layer_spec(rb, layer), _const_spec(tri.shape)]
    if kind == "a":
        conv_scratch = [pltpu.VMEM(((TS + HALO_A) * DT, LANES), F32), pltpu.VMEM((TS * DT, LANES), F32)]
    else:
        conv_scratch = [pltpu.VMEM((TS + HALO_B, D), F32)]
    out_shape = (
        jax.ShapeDtypeStruct((BATCH, SEQ, D), F32),
        jax.ShapeDtypeStruct((BATCH, SEQ, DP), U32),
        jax.ShapeDtypeStruct((SUBLANES, T), I32),
        jax.ShapeDtypeStruct((BATCH, SEQ, 2 * LANES), F32),
        jax.ShapeDtypeStruct((NE, LANES), I32),
    )
    out_specs = (
        _tile_spec(D, b0), _tile_spec(DP, b0),
        pl.BlockSpec((SUBLANES, TS), lambda b, s: (0, (b + b0) * NS + s)),
        _tile_spec(2 * LANES, b0),
        pl.BlockSpec((NE, LANES), lambda b, s: (0, 0)),
    )
    return pl.pallas_call(
        functools.partial(_mixer_kernel, kind=kind, has_prev=has_prev, carry=carry),
        grid=(n_batch, NS),
        in_specs=specs,
        out_specs=out_specs,
        out_shape=out_shape,
        input_output_aliases=aliases,
        scratch_shapes=conv_scratch + [pltpu.VMEM((TS, D), BF16), pltpu.VMEM((NE, LANES), F32),
                                       pltpu.VMEM(weights[0].shape[1:], BF16),
                                       pltpu.VMEM(weights[-1].shape[1:], BF16)],
        compiler_params=pltpu.CompilerParams(
            dimension_semantics=("arbitrary", "arbitrary"), vmem_limit_bytes=VMEM_LIMIT),
        name=name,
    )(*args)


def _mixer(kind, layer, xin, prev, params):
    name = f"mixer_{kind}{layer}"
    if prev is None:
        return _mixer_call(kind, layer, xin, None, params, name, 0, BATCH, None)
    yk_parts, wcin = prev
    out = None
    for hh, (b0, nb) in enumerate(zip(PART_START, PART_BATCHES)):
        out = _mixer_call(kind, layer, xin, (yk_parts[hh], wcin), params, f"{name}_h{hh}", b0, nb, out)
    return out


def _dest_kernel(ps_ref, ri_ref, o_ref):
    ri = ri_ref[...]
    e = ri[0:2, :]
    start = jnp.zeros_like(e)
    for k in range(NE):
        start = jnp.where(e == k, ps_ref[k], start)
    o_ref[0:2, :] = start + ri[2:4, :]
    o_ref[2:8, :] = jnp.zeros((6, ri.shape[1]), I32)


def _dest_slots(pad_starts, ri):
    tn = 4096
    return pl.pallas_call(
        _dest_kernel,
        grid_spec=pltpu.PrefetchScalarGridSpec(
            num_scalar_prefetch=1,
            grid=(T // tn,),
            in_specs=[pl.BlockSpec((SUBLANES, tn), lambda i, ps: (0, i))],
            out_specs=pl.BlockSpec((SUBLANES, tn), lambda i, ps: (0, i)),
        ),
        out_shape=jax.ShapeDtypeStruct((SUBLANES, T), I32),
        compiler_params=pltpu.CompilerParams(dimension_semantics=("arbitrary",)),
        name="dest_slots",
    )(pad_starts, ri)


def _expert_rows(x_packed, wgu_s, wd_s):
    x = _unpack_rows(x_packed).astype(BF16)
    gu = jnp.dot(x, wgu_s[...], preferred_element_type=F32)
    g = gu[:, :F]
    hid = (g * _sigmoid(g) * gu[:, F:]).astype(BF16)
    return _pack_rows(jnp.dot(hid, wd_s[...], preferred_element_type=F32))


def _expert_kernel(st_ref, nb_ref, xs_hbm, wg_ref, wu_ref, wd_ref, yb_hbm,
                   wgu_s, wd_s, xbuf, obuf, xt0, ot0, xt1, ot1, xsem, osem, tsem):
    e = pl.program_id(0)
    start = st_ref[e]
    xtails, otails = (xt0, xt1), (ot0, ot1)

    def split(n_blocks):
        units = n_blocks // BLOCKS_PER_UNIT
        rem = n_blocks % BLOCKS_PER_UNIT
        on = (rem >= 2, rem % 2 == 1)
        off0 = units * UNIT
        off1 = off0 + jnp.where(on[0], TAILS[0], 0)
        return units, on, (off0, off1)

    n_units, tail_on, tail_off = split(nb_ref[e])

    def x_copy(first_row, u, slot):
        return pltpu.make_async_copy(xs_hbm.at[pl.ds(pl.multiple_of(first_row + u * UNIT, BM), UNIT)],
                                     xbuf.at[slot], xsem.at[slot])

    def o_copy(u, slot):
        return pltpu.make_async_copy(obuf.at[slot],
                                     yb_hbm.at[pl.ds(pl.multiple_of(start + u * UNIT, BM), UNIT)], osem.at[slot])

    def xt_copy(k, first_row, off):
        return pltpu.make_async_copy(xs_hbm.at[pl.ds(pl.multiple_of(first_row + off, BM), TAILS[k])],
                                     xtails[k], tsem.at[2 * k])

    def ot_copy(k):
        return pltpu.make_async_copy(otails[k], yb_hbm.at[pl.ds(pl.multiple_of(start + tail_off[k], BM), TAILS[k])],
                                     tsem.at[2 * k + 1])

    def fetch_first(first_row, n_blocks):
        units, on, off = split(n_blocks)

        @pl.when(units > 0)
        def _():
            x_copy(first_row, 0, 0).start(priority=ROW_DMA_PRIORITY)

        for k in range(len(TAILS)):
            @pl.when(on[k])
            def _():
                xt_copy(k, first_row, off[k]).start(priority=ROW_DMA_PRIORITY)

    @pl.when(e == 0)
    def _():
        fetch_first(start, nb_ref[e])

    @pl.when(nb_ref[e] > 0)
    def _():
        wgu_s[:, :F] = wg_ref[0, 0].astype(BF16)
        wgu_s[:, F:] = wu_ref[0, 0].astype(BF16)
        wd_s[...] = wd_ref[0, 0].astype(BF16)

    @pl.loop(0, n_units, step=2)
    def _(u0):
        for slot in range(2):
            u = u0 + slot

            @pl.when(u < n_units)
            def _():
                x_copy(start, u, slot).wait()

                @pl.when(u + 1 < n_units)
                def _():
                    x_copy(start, u + 1, 1 - slot).start(priority=ROW_DMA_PRIORITY)

                @pl.when(u >= 2)
                def _():
                    o_copy(u - 2, slot).wait()

                obuf[slot] = _expert_rows(xbuf[slot], wgu_s, wd_s)
                o_copy(u, slot).start(priority=ROW_DMA_PRIORITY)

    for k in range(len(TAILS)):
        @pl.when(tail_on[k])
        def _():
            xt_copy(k, start, tail_off[k]).wait()
            otails[k][...] = _expert_rows(xtails[k][...], wgu_s, wd_s)
            ot_copy(k).start(priority=ROW_DMA_PRIORITY)

    nxt = jnp.minimum(e + 1, NE - 1)

    @pl.when(e + 1 < NE)
    def _():
        fetch_first(st_ref[nxt], nb_ref[nxt])

    for back in (1, 2):
        @pl.when(n_units >= back)
        def _():
            last = n_units - back
            o_copy(last, last % 2).wait()

    for k in range(len(TAILS)):
        @pl.when(tail_on[k])
        def _():
            ot_copy(k).wait()


def _experts(layer, starts, n_blocks, xs, w_gate, w_up, w_down, name):
    def w_map(e, st, nb):
        return (layer, e, 0, 0)

    return pl.pallas_call(
        _expert_kernel,
        grid_spec=pltpu.PrefetchScalarGridSpec(
            num_scalar_prefetch=2,
            grid=(NE,),
            in_specs=[
                pl.BlockSpec(memory_space=pl.ANY),
                pl.BlockSpec((1, 1, D, F), w_map),
                pl.BlockSpec((1, 1, D, F), w_map),
                pl.BlockSpec((1, 1, F, D), w_map),
            ],
            out_specs=pl.BlockSpec(memory_space=pl.ANY),
            scratch_shapes=[
                pltpu.VMEM((D, 2 * F), BF16), pltpu.VMEM((F, D), BF16),
                pltpu.VMEM((2, UNIT, DP), U32), pltpu.VMEM((2, UNIT, DP), U32),
                pltpu.VMEM((TAILS[0], DP), U32), pltpu.VMEM((TAILS[0], DP), U32),
                pltpu.VMEM((TAILS[1], DP), U32), pltpu.VMEM((TAILS[1], DP), U32),
                pltpu.SemaphoreType.DMA((2,)), pltpu.SemaphoreType.DMA((2,)),
                pltpu.SemaphoreType.DMA((2 * len(TAILS),)),
            ],
        ),
        out_shape=jax.ShapeDtypeStruct((NSLOT, DP), U32),
        compiler_params=pltpu.CompilerParams(
            dimension_semantics=("arbitrary",), vmem_limit_bytes=VMEM_LIMIT),
        name=name,
    )(starts, n_blocks, xs, w_gate, w_up, w_down)


def _sc_worker_id():
    return lax.axis_index("s") * SC_CORES + lax.axis_index("c")


def _sc_mesh():
    return plsc.VectorSubcoreMesh(core_axis_name="c", subcore_axis_name="s")


def _sc_scratch(n_index_rows):
    return [
        pltpu.VMEM((n_index_rows, SC_CHUNK), I32),
        pltpu.VMEM((2, SC_CHUNK, DP), U32),
        pltpu.SemaphoreType.DMA((2,)),
        pltpu.SemaphoreType.DMA((2,)),
    ]


def _dispatch_rows(h2p, dest):
    per_w = T // SC_WORKERS
    n_chunks = per_w // SC_CHUNK

    @functools.partial(
        pl.kernel, mesh=_sc_mesh(),
        out_type=jax.ShapeDtypeStruct((NSLOT, DP), U32),
        scratch_types=_sc_scratch(TOP_K * n_chunks),
        name="dispatch_rows",
    )
    def k(h2_hbm, dest_hbm, out_hbm, dest_v, rows_v, rsem, wsem):
        wid = _sc_worker_id()
        for kk in range(TOP_K):
            pltpu.sync_copy(dest_hbm.at[kk, pl.ds(wid * n_chunks, n_chunks)],
                            dest_v.at[pl.ds(kk * n_chunks, n_chunks)])
        base = wid * per_w

        def read(c, slot):
            return pltpu.make_async_copy(h2_hbm.at[pl.ds(base + c * SC_CHUNK, SC_CHUNK)],
                                         rows_v.at[slot], rsem.at[slot])

        def write(c, kk, slot):
            return pltpu.make_async_copy(rows_v.at[slot], out_hbm.at[dest_v.at[kk * n_chunks + c]],
                                         wsem.at[slot])

        read(0, 0).start()

        @pl.loop(0, n_chunks, step=2)
        def _(c):
            for b in range(2):
                cc = c + b
                read(cc, b).wait()

                @pl.when(cc + 1 < n_chunks)
                def _():
                    @pl.when(cc >= 1)
                    def _():
                        for kk in range(TOP_K):
                            write(cc - 1, kk, 1 - b).wait()
                    read(cc + 1, 1 - b).start()

                for kk in range(TOP_K):
                    write(cc, kk, b).start()

        for slot, cc in ((0, n_chunks - 2), (1, n_chunks - 1)):
            for kk in range(TOP_K):
                write(cc, kk, slot).wait()

    return k(h2p, dest.reshape(TOP_K, T // SC_CHUNK, SC_CHUNK))


def _return_rows(yb, dest):
    m = dest.size
    per_w = m // SC_WORKERS
    n_chunks = per_w // SC_CHUNK

    @functools.partial(
        pl.kernel, mesh=_sc_mesh(),
        out_type=jax.ShapeDtypeStruct((m, DP), U32),
        scratch_types=_sc_scratch(n_chunks),
        name="return_rows",
    )
    def k(yb_hbm, dest_hbm, out_hbm, idx_v, rows_v, gsem, wsem):
        wid = _sc_worker_id()
        pltpu.sync_copy(dest_hbm.at[pl.ds(wid * n_chunks, n_chunks)], idx_v)
        base = wid * per_w

        def gather(c, slot):
            return pltpu.make_async_copy(yb_hbm.at[idx_v.at[c]], rows_v.at[slot], gsem.at[slot])

        def write(c, slot):
            return pltpu.make_async_copy(rows_v.at[slot], out_hbm.at[pl.ds(base + c * SC_CHUNK, SC_CHUNK)],
                                         wsem.at[slot])

        gather(0, 0).start()

        @pl.loop(0, n_chunks, step=2)
        def _(c):
            for b in range(2):
                cc = c + b
                gather(cc, b).wait()

                @pl.when(cc + 1 < n_chunks)
                def _():
                    @pl.when(cc >= 1)
                    def _():
                        write(cc - 1, 1 - b).wait()
                    gather(cc + 1, 1 - b).start()

                write(cc, b).start()

        write(n_chunks - 2, 0).wait()
        write(n_chunks - 1, 1).wait()

    return k(yb, dest.reshape(m // SC_CHUNK, SC_CHUNK))


def _final_kernel(x1_ref, y0_ref, y1_ref, wc_ref, mods_ref, ln_ref, *rest):
    o_ref = rest[-1]
    ln = ln_ref[0]
    o_ref[0] = _combine(x1_ref[0], y0_ref[0], y1_ref[0], wc_ref[0], mods_ref[0, 0][5:6, :],
                        ln[0:1, :], ln[1:2, :])


def _final(x1, yk_parts, wc, params):
    mods, ffn_ln = params["mods"][DEPTH - 1], params["ffn_ln"]
    out = None
    for hh, (b0, nb) in enumerate(zip(PART_START, PART_BATCHES)):
        args = [x1, yk_parts[hh], yk_parts[hh], wc, mods, ffn_ln]
        specs = [_tile_spec(D, b0), _yk_spec(0), _yk_spec(1), _tile_spec(2 * LANES, b0),
                 _mods_spec(b0), _layer_spec(ffn_ln, DEPTH - 1)]
        aliases = {}
        if out is not None:
            aliases[len(args)] = 0
            args.append(out)
            specs.append(_any_spec())
        out = pl.pallas_call(
            _final_kernel,
            grid=(nb, NS),
            in_specs=specs,
            out_specs=_tile_spec(D, b0),
            out_shape=jax.ShapeDtypeStruct((BATCH, SEQ, D), F32),
            input_output_aliases=aliases,
            compiler_params=pltpu.CompilerParams(
                dimension_semantics=("arbitrary", "arbitrary"), vmem_limit_bytes=VMEM_LIMIT),
            name=f"final_combine_h{hh}",
        )(*args)
    return out


def _plan(counts):
    n_blocks = (counts + BM - 1) // BM
    padded = n_blocks * BM
    pad_starts = (jnp.cumsum(padded) - padded).astype(I32)
    return pad_starts, n_blocks.astype(I32)


def _router_params(w_group, b_group, w_expert, b_expert):
    wr = jnp.zeros((DEPTH, NR, D), F32)
    wr = wr.at[:, 0:N_GROUPS].set(jnp.swapaxes(w_group, 1, 2))
    wr = wr.at[:, SUBLANES:SUBLANES + NE].set(jnp.swapaxes(w_expert, 1, 2))
    rb = jnp.full((DEPTH, NR), NEG, F32)
    rb = rb.at[:, 0:N_GROUPS].set(b_group).at[:, SUBLANES:SUBLANES + NE].set(b_expert)
    rb = rb.at[:, SUBLANES + NE:].set(0.0)
    return wr.astype(BF16), rb.reshape(DEPTH, NR, 1)


def kernel(x, c, ada_w, ada_b, a_w_in, a_b_in, a_w_dw, a_b_dw, a_ln_g, a_ln_b, a_w_out, a_b_out,
           b_w_in, b_w_dw, b_w_out, mix_ln_g, mix_ln_b, ffn_ln_g, ffn_ln_b,
           r_w_group, r_b_group, r_w_expert, r_b_expert, e_w_gate, e_w_up, e_w_down):
    n_a = a_w_in.shape[0]
    wr, rb = _router_params(r_w_group, r_b_group, r_w_expert, r_b_expert)

    def cast_layer(w, layer, after):
        wl = w[layer // 2:layer // 2 + 1]
        if after is not None:
            wl = wl + after * 0.0
        return wl.astype(BF16)

    params = {
        "mods": [None] * DEPTH,
        "mix_ln": jnp.stack([mix_ln_g, mix_ln_b], axis=1),
        "ffn_ln": jnp.stack([ffn_ln_g, ffn_ln_b], axis=1),
        "wr": wr, "rb": rb,
        "tri": (jnp.arange(TS)[:, None] < jnp.arange(TS)[None, :]).astype(BF16),
        "a": [None, a_b_in.reshape(n_a, 1, 2 * D), a_w_dw.reshape(n_a, CONV_A, DT, LANES),
              a_b_dw.reshape(n_a, DT, LANES), jnp.stack([a_ln_g, a_ln_b, a_b_out], axis=1), None],
        "b": [None, b_w_dw, None],
    }
    w_in = {"a": a_w_in, "b": b_w_in}
    w_out = {"a": a_w_out, "b": b_w_out}
    prev = None
    xin = x
    after = None
    for i in range(DEPTH):
        kind = "a" if i % 2 == 0 else "b"
        params["mods"][i] = _ada_mods(i, c if after is None else c + after * 0.0, ada_w, ada_b)
        params[kind][0] = cast_layer(w_in[kind], i, after)
        params[kind][-1] = cast_layer(w_out[kind], i, after)
        x1, h2, ri, wc, counts = _mixer(kind, i, xin, prev, params)
        after = counts[0, 0].astype(F32)
        pad_starts, n_blocks = _plan(counts[:, 0])
        dest = _dest_slots(pad_starts, ri)[0:TOP_K]
        xs = _dispatch_rows(h2.reshape(T, DP), dest)
        yb = _experts(i, pad_starts, n_blocks, xs, e_w_gate, e_w_up, e_w_down, name=f"experts{i}")
        yk_parts = [_return_rows(yb, dest[:, b0 * SEQ:(b0 + nb) * SEQ]).reshape(TOP_K, nb * SEQ, DP)
                     for b0, nb in zip(PART_START, PART_BATCHES)]
        prev = (yk_parts, wc)
        xin = x1
    yk_parts, wc = prev
    return _final(xin, yk_parts, wc, params)
```

```python
import functools

import jax
import jax.numpy as jnp
from jax import lax
from jax.experimental import pallas as pl
from jax.experimental.pallas import tpu as pltpu
from jax.experimental.pallas import tpu_sc as plsc

F32 = jnp.float32
BF16 = jnp.bfloat16
I32 = jnp.int32
U32 = jnp.uint32

D = 1024
BATCH = 4
SEQ = 8192
T = BATCH * SEQ
DEPTH = 4
N_GROUPS = 4
EPG = 8
NE = N_GROUPS * EPG
TOP_K = 2
F = D // 2
CONV_A = 31
CONV_B = 3
ALPHA = (2.0 * DEPTH) ** 0.25
LN_EPS = 1e-5

LANES = 128
SUBLANES = 8
VMEM_LIMIT = 56 * 1024 * 1024

TS = 512
SR = 512
NS = SEQ // TS
PART_BATCHES = (2, 2)
PART_START = (0, 2)
assert sum(PART_BATCHES) == BATCH
HALO_A = 32
HALO_B = 8
BM = 256
BLOCKS_PER_UNIT = 4
UNIT = BLOCKS_PER_UNIT * BM
TAILS = (2 * BM, BM)
ROW_DMA_PRIORITY = 1
PAD_SLOTS = NE * BM
NR = 48
ADA_TN = 1536
NEG = -1e30
DP = D // 2
DT = D // LANES
assert DT == SUBLANES

SC_CORES = 2
SC_SUBCORES = 16
SC_WORKERS = SC_CORES * SC_SUBCORES
SC_CHUNK = 64


def _sigmoid(x):
    return 1.0 / (1.0 + jnp.exp(-x))


def _pack_rows(x):
    return pltpu.pack_elementwise([x[:, :DP], x[:, DP:]], packed_dtype=BF16)


def _unpack_rows(p):
    lo = pltpu.unpack_elementwise(p, index=0, packed_dtype=BF16, unpacked_dtype=F32)
    hi = pltpu.unpack_elementwise(p, index=1, packed_dtype=BF16, unpacked_dtype=F32)
    return jnp.concatenate([lo, hi], axis=1)


def _layer_norm(x, g, b):
    mu = jnp.mean(x, axis=-1, keepdims=True)
    xc = x - mu
    var = jnp.mean(xc * xc, axis=-1, keepdims=True)
    return xc * lax.rsqrt(var + LN_EPS) * g + b


def _ada_kernel(c_ref, w_ref, b_ref, o_ref):
    c = c_ref[...]
    ca = (c * _sigmoid(c)).astype(BF16)
    w = w_ref[0].astype(BF16)
    o_ref[0] = jnp.dot(ca, w, preferred_element_type=F32) + b_ref[0]


def _ada_mods(c, ada_w, ada_b):
    out = pl.pallas_call(
        _ada_kernel,
        grid=(DEPTH, 6 * D // ADA_TN),
        in_specs=[
            pl.BlockSpec((BATCH, D), lambda i, j: (0, 0)),
            pl.BlockSpec((1, D, ADA_TN), lambda i, j: (i, 0, j)),
            pl.BlockSpec((1, 1, ADA_TN), lambda i, j: (i, 0, j)),
        ],
        out_specs=pl.BlockSpec((1, BATCH, ADA_TN), lambda i, j: (i, 0, j)),
        out_shape=jax.ShapeDtypeStruct((DEPTH, BATCH, 6 * D), F32),
        compiler_params=pltpu.CompilerParams(
            dimension_semantics=("arbitrary", "arbitrary"), vmem_limit_bytes=VMEM_LIMIT),
        name="ada_mods",
    )(c, ada_w, ada_b.reshape(DEPTH, 1, 6 * D))
    return out.reshape(DEPTH, BATCH, 6, D)


def _combine(x1, y0p, y1p, wc, g_f, ln_g, ln_b):
    w0 = jnp.tile(wc[:, :LANES], (1, D // LANES))
    w1 = jnp.tile(wc[:, LANES:], (1, D // LANES))
    y = w0 * _unpack_rows(y0p) + w1 * _unpack_rows(y1p)
    return _layer_norm(ALPHA * x1 + (1.0 + g_f) * y, ln_g, ln_b)


def _route(h2, wr_ref, rb_ref, tri_ref, cnt_ref, ri_ref, wc_ref, cnto_ref):
    lt = lax.dot_general(wr_ref[...], h2, (((1,), (1,)), ((), ())),
                         preferred_element_type=F32) + rb_ref[...]
    iota8 = lax.broadcasted_iota(I32, (SUBLANES, TS), 0).astype(F32)
    gl = lt[0:SUBLANES]
    gmax = jnp.max(gl, axis=0, keepdims=True)
    gidx = jnp.min(jnp.where(gl == gmax, iota8, float(SUBLANES)), axis=0, keepdims=True)
    gw = 1.0 / jnp.sum(jnp.exp(gl - gmax), axis=0, keepdims=True)
    el = lt[SUBLANES:2 * SUBLANES]
    for g in range(1, N_GROUPS):
        el = jnp.where(gidx == float(g), lt[SUBLANES * (g + 1):SUBLANES * (g + 2)], el)
    m1 = jnp.max(el, axis=0, keepdims=True)
    i1 = jnp.min(jnp.where(el == m1, iota8, float(SUBLANES)), axis=0, keepdims=True)
    el2 = jnp.where(iota8 == i1, -jnp.inf, el)
    m2 = jnp.max(el2, axis=0, keepdims=True)
    i2 = jnp.min(jnp.where(el2 == m2, iota8, float(SUBLANES)), axis=0, keepdims=True)
    r = jnp.exp(m2 - m1)
    w_a = gw / (1.0 + r)
    w_b = gw * r / (1.0 + r)
    e1 = gidx * float(EPG) + i1
    e2 = gidx * float(EPG) + i2

    iota_e = lax.broadcasted_iota(I32, (NE, TS), 0).astype(F32)
    oh1 = iota_e == e1
    oh2 = iota_e == e2
    oh = jnp.concatenate([jnp.where(oh1, 1.0, 0.0), jnp.where(oh2, 1.0, 0.0)], axis=0)
    before = jnp.dot(oh.astype(BF16), tri_ref[...], preferred_element_type=F32)
    tot = jnp.sum(oh, axis=1, keepdims=True)
    cnt = cnt_ref[...]
    base = jnp.tile(cnt, (1, TS // LANES))
    tot1 = tot[:NE]
    tot2 = tot[NE:]
    rank1 = jnp.sum(jnp.where(oh1, base + before[:NE], 0.0), axis=0, keepdims=True)
    rank2 = jnp.sum(jnp.where(oh2, base + tot1 + before[NE:], 0.0), axis=0, keepdims=True)
    new_cnt = cnt + tot1 + tot2
    cnt_ref[...] = new_cnt
    cnto_ref[...] = new_cnt.astype(I32)

    ri_ref[0:1, :] = e1.astype(I32)
    ri_ref[1:2, :] = e2.astype(I32)
    ri_ref[2:3, :] = rank1.astype(I32)
    ri_ref[3:4, :] = rank2.astype(I32)
    ri_ref[4:8, :] = jnp.zeros((4, TS), I32)
    wc_ref[0, :, :LANES] = jnp.broadcast_to(w_a, (LANES, TS)).T
    wc_ref[0, :, LANES:] = jnp.broadcast_to(w_b, (LANES, TS)).T


def _conv_taps(uext_ref, u, w_dw, halo, width, r0):
    uext_ref[halo + r0:halo + r0 + SR, :] = u
    acc = None
    for k in range(width):
        off = r0 + halo - (width - 1) + k
        term = w_dw[k:k + 1, :] * uext_ref[off:off + SR, :]
        acc = term if acc is None else acc + term
    return acc


def _conv_time_major(tm_ref, o2_ref, u, wk_ref, bias, r0, width, halo):
    for j in range(DT):
        tm_ref[pl.ds((halo + r0) * DT + j, SR, stride=DT), :] = u[:, j * LANES:(j + 1) * LANES]
    acc = None
    for k in range(width):
        off = (r0 + halo - (width - 1) + k) * DT
        term = tm_ref[off:off + SR * DT, :].reshape(SR, DT, LANES) * wk_ref[k]
        acc = term if acc is None else acc + term
    if bias is not None:
        acc = acc + bias
    o2_ref[r0 * DT:(r0 + SR) * DT, :] = acc.reshape(SR * DT, LANES)
    return jnp.concatenate([o2_ref[pl.ds(r0 * DT + j, SR, stride=DT), :] for j in range(DT)], axis=1)


def _mixer_kernel(*refs, kind, has_prev, carry):
    it = iter(refs)
    xin_ref = next(it)
    if has_prev:
        y0_ref, y1_ref, wcin_ref, pmods_ref, pln_ref = (next(it) for _ in range(5))
    if carry:
        for _ in range(4):
            next(it)
        cnt_in_ref = next(it)
    mods_ref, mln_ref, win_ref = next(it), next(it), next(it)
    if kind == "a":
        bin_ref, wdw_ref, bdw_ref, vec_ref = next(it), next(it), next(it), next(it)
    else:
        wdw_ref = next(it)
    wout_ref, wr_ref, rb_ref, tri_ref = (next(it) for _ in range(4))
    x1_ref, h2_ref, ri_ref, wc_ref, cnto_ref = (next(it) for _ in range(5))
    if kind == "a":
        uext_ref, o2_ref, h2b_ref, cnt_ref = (next(it) for _ in range(4))
    else:
        uext_ref, h2b_ref, cnt_ref = (next(it) for _ in range(3))
    win_s, wout_s = next(it), next(it)
    mln_ref, win_ref, wdw_ref, wout_ref, wr_ref, rb_ref = (
        r.at[0] for r in (mln_ref, win_ref, wdw_ref, wout_ref, wr_ref, rb_ref))
    if has_prev:
        pln_ref = pln_ref.at[0]
    if kind == "a":
        bin_ref, bdw_ref, vec_ref = (r.at[0] for r in (bin_ref, bdw_ref, vec_ref))

    first = (pl.program_id(0) == 0) & (pl.program_id(1) == 0)

    @pl.when(first)
    def _():
        if carry:
            cnt_ref[...] = cnt_in_ref[...].astype(F32)
        else:
            cnt_ref[...] = jnp.zeros((NE, LANES), F32)
        win_s[...] = win_ref[...]
        wout_s[...] = wout_ref[...]

    halo_rows = HALO_A * DT if kind == "a" else HALO_B

    @pl.when(pl.program_id(1) == 0)
    def _():
        uext_ref[0:halo_rows, :] = jnp.zeros((halo_rows, uext_ref.shape[1]), F32)

    m = mods_ref[0, 0]
    mln = mln_ref[...]
    for i in range(TS // SR):
        r0 = i * SR
        rows = slice(r0, r0 + SR)
        x = xin_ref[0, rows, :]
        if has_prev:
            pln = pln_ref[...]
            x = _combine(x, y0_ref[0, rows, :], y1_ref[0, rows, :], wcin_ref[0, rows, :],
                         pmods_ref[0, 0][5:6, :], pln[0:1, :], pln[1:2, :])

        h = (x * (1.0 + m[1:2, :]) + m[0:1, :]).astype(BF16)
        cols = [jnp.dot(h, win_s[:, c * D:(c + 1) * D], preferred_element_type=F32)
                for c in range(win_s.shape[1] // D)]
        if kind == "a":
            b_in = bin_ref[...]
            vec = vec_ref[...]
            u = (cols[0] + b_in[:, :D]) * _sigmoid(cols[1] + b_in[:, D:])
            u = _conv_time_major(uext_ref, o2_ref, u, wdw_ref, bdw_ref[...], r0, CONV_A, HALO_A)
            u = _layer_norm(u, vec[0:1, :], vec[1:2, :])
            u = u * _sigmoid(u)
            y = jnp.dot(u.astype(BF16), wout_s[...], preferred_element_type=F32) + vec[2:3, :]
        else:
            gb = cols[0]
            q = cols[1] * cols[2]
            u = _conv_taps(uext_ref, q, wdw_ref[...], HALO_B, CONV_B, r0)
            y = jnp.dot((gb * u).astype(BF16), wout_s[...], preferred_element_type=F32)

        x1 = _layer_norm(ALPHA * x + (1.0 + m[2:3, :]) * y, mln[0:1, :], mln[1:2, :])
        x1_ref[0, rows, :] = x1
        h2 = x1 * (1.0 + m[4:5, :]) + m[3:4, :]
        h2_ref[0, rows, :] = _pack_rows(h2)
        h2b_ref[rows, :] = h2.astype(BF16)

    uext_ref[0:halo_rows, :] = uext_ref[uext_ref.shape[0] - halo_rows:uext_ref.shape[0], :]
    _route(h2b_ref[...], wr_ref, rb_ref, tri_ref, cnt_ref, ri_ref, wc_ref, cnto_ref)


def _tile_spec(width, b0=0):
    return pl.BlockSpec((1, TS, width), lambda b, s: (b + b0, s, 0))


def _yk_spec(k):
    return pl.BlockSpec((1, TS, DP), lambda b, s: (k, b * NS + s, 0))


def _const_spec(shape):
    nd = len(shape)
    return pl.BlockSpec(shape, lambda b, s: (0,) * nd)


def _mods_spec(layer, b0=0):
    return pl.BlockSpec((1, 1, 6, D), lambda b, s: (layer, b + b0, 0, 0))


def _layer_spec(arr, idx):
    tail = arr.shape[1:]
    return pl.BlockSpec((1,) + tail, lambda b, s: (idx,) + (0,) * len(tail))


def _any_spec():
    return pl.BlockSpec(memory_space=pl.ANY)


def _mixer_call(kind, layer, xin, prev, params, name, b0, n_batch, carried):
    mods, mix_ln, ffn_ln, wr, rb, tri = (params[k] for k in ("mods", "mix_ln", "ffn_ln", "wr", "rb", "tri"))
    weights = params[kind]
    j = layer // 2
    has_prev = prev is not None
    carry = carried is not None
    args = [xin]
    specs = [_tile_spec(D, b0)]
    if has_prev:
        yk, wcin = prev
        args += [yk, yk, wcin, mods, ffn_ln]
        specs += [_yk_spec(0), _yk_spec(1), _tile_spec(2 * LANES, b0), _mods_spec(layer - 1, b0),
                  _layer_spec(ffn_ln, layer - 1)]
    aliases = {}
    if carry:
        for out_idx, arr in enumerate(carried):
            aliases[len(args)] = out_idx
            args.append(arr)
            specs.append(_any_spec() if out_idx < 4 else _const_spec((NE, LANES)))
    args += [mods, mix_ln]
    specs += [_mods_spec(layer, b0), _layer_spec(mix_ln, layer)]
    for w in weights:
        args.append(w)
        specs.append(_layer_spec(w, j if w.shape[0] > 1 else 0))
    args += [wr, rb, tri]
    specs += [_layer_spec(wr, layer), _layer_spec(rb, layer), _const_spec(tri.shape)]
    if kind == "a":
        conv_scratch = [pltpu.VMEM(((TS + HALO_A) * DT, LANES), F32), pltpu.VMEM((TS * DT, LANES), F32)]
    else:
        conv_scratch = [pltpu.VMEM((TS + HALO_B, D), F32)]
    out_shape = (
        jax.ShapeDtypeStruct((BATCH, SEQ, D), F32),
        jax.ShapeDtypeStruct((BATCH, SEQ, DP), U32),
        jax.ShapeDtypeStruct((SUBLANES, T), I32),
        jax.ShapeDtypeStruct((BATCH, SEQ, 2 * LANES), F32),
        jax.ShapeDtypeStruct((NE, LANES), I32),
    )
    out_specs = (
        _tile_spec(D, b0), _tile_spec(DP, b0),
        pl.BlockSpec((SUBLANES, TS), lambda b, s: (0, (b + b0) * NS + s)),
        _tile_spec(2 * LANES, b0),
        pl.BlockSpec((NE, LANES), lambda b, s: (0, 0)),
    )
    return pl.pallas_call(
        functools.partial(_mixer_kernel, kind=kind, has_prev=has_prev, carry=carry),
        grid=(n_batch, NS),
        in_specs=specs,
        out_specs=out_specs,
        out_shape=out_shape,
        input_output_aliases=aliases,
        scratch_shapes=conv_scratch + [pltpu.VMEM((TS, D), BF16), pltpu.VMEM((NE, LANES), F32),
                                       pltpu.VMEM(weights[0].shape[1:], BF16),
                                       pltpu.VMEM(weights[-1].shape[1:], BF16)],
        compiler_params=pltpu.CompilerParams(
            dimension_semantics=("arbitrary", "arbitrary"), vmem_limit_bytes=VMEM_LIMIT),
        name=name,
    )(*args)


def _mixer(kind, layer, xin, prev, params, part):
    b0, nb = PART_START[part], PART_BATCHES[part]
    return _mixer_call(kind, layer, xin, prev, params, f"mixer_{kind}{layer}_h{part}", b0, nb, None)


def _dest_kernel(ps_ref, ri_ref, o_ref):
    ri = ri_ref[...]
    e = ri[0:2, :]
    start = jnp.zeros_like(e)
    for k in range(NE):
        start = jnp.where(e == k, ps_ref[k], start)
    o_ref[0:2, :] = start + ri[2:4, :]
    o_ref[2:8, :] = jnp.zeros((6, ri.shape[1]), I32)


def _dest_slots(pad_starts, ri, t0, n):
    tn = 4096
    first = t0 // tn
    return pl.pallas_call(
        _dest_kernel,
        grid_spec=pltpu.PrefetchScalarGridSpec(
            num_scalar_prefetch=1,
            grid=(n // tn,),
            in_specs=[pl.BlockSpec((SUBLANES, tn), lambda i, ps: (0, i + first))],
            out_specs=pl.BlockSpec((SUBLANES, tn), lambda i, ps: (0, i)),
        ),
        out_shape=jax.ShapeDtypeStruct((SUBLANES, n), I32),
        compiler_params=pltpu.CompilerParams(dimension_semantics=("arbitrary",)),
        name="dest_slots",
    )(pad_starts, ri)


def _expert_rows(x_packed, wgu_s, wd_s):
    x = _unpack_rows(x_packed).astype(BF16)
    gu = jnp.dot(x, wgu_s[...], preferred_element_type=F32)
    g = gu[:, :F]
    hid = (g * _sigmoid(g) * gu[:, F:]).astype(BF16)
    return _pack_rows(jnp.dot(hid, wd_s[...], preferred_element_type=F32))


def _expert_kernel(st_ref, nb_ref, xs_hbm, wg_ref, wu_ref, wd_ref, yb_hbm,
                   wgu_s, wd_s, xbuf, obuf, xt0, ot0, xt1, ot1, xsem, osem, tsem):
    e = pl.program_id(0)
    start = st_ref[e]
    xtails, otails = (xt0, xt1), (ot0, ot1)

    def split(n_blocks):
        units = n_blocks // BLOCKS_PER_UNIT
        rem = n_blocks % BLOCKS_PER_UNIT
        on = (rem >= 2, rem % 2 == 1)
        off0 = units * UNIT
        off1 = off0 + jnp.where(on[0], TAILS[0], 0)
        return units, on, (off0, off1)

    n_units, tail_on, tail_off = split(nb_ref[e])

    def x_copy(first_row, u, slot):
        return pltpu.make_async_copy(xs_hbm.at[pl.ds(pl.multiple_of(first_row + u * UNIT, BM), UNIT)],
                                     xbuf.at[slot], xsem.at[slot])

    def o_copy(u, slot):
        return pltpu.make_async_copy(obuf.at[slot],
                                     yb_hbm.at[pl.ds(pl.multiple_of(start + u * UNIT, BM), UNIT)], osem.at[slot])

    def xt_copy(k, first_row, off):
        return pltpu.make_async_copy(xs_hbm.at[pl.ds(pl.multiple_of(first_row + off, BM), TAILS[k])],
                                     xtails[k], tsem.at[2 * k])

    def ot_copy(k):
        return pltpu.make_async_copy(otails[k], yb_hbm.at[pl.ds(pl.multiple_of(start + tail_off[k], BM), TAILS[k])],
                                     tsem.at[2 * k + 1])

    def fetch_first(first_row, n_blocks):
        units, on, off = split(n_blocks)

        @pl.when(units > 0)
        def _():
            x_copy(first_row, 0, 0).start(priority=ROW_DMA_PRIORITY)

        for k in range(len(TAILS)):
            @pl.when(on[k])
            def _():
                xt_copy(k, first_row, off[k]).start(priority=ROW_DMA_PRIORITY)

    @pl.when(e == 0)
    def _():
        fetch_first(start, nb_ref[e])

    @pl.when(nb_ref[e] > 0)
    def _():
        wgu_s[:, :F] = wg_ref[0, 0].astype(BF16)
        wgu_s[:, F:] = wu_ref[0, 0].astype(BF16)
        wd_s[...] = wd_ref[0, 0].astype(BF16)

    @pl.loop(0, n_units, step=2)
    def _(u0):
        for slot in range(2):
            u = u0 + slot

            @pl.when(u < n_units)
            def _():
                x_copy(start, u, slot).wait()

                @pl.when(u + 1 < n_units)
                def _():
                    x_copy(start, u + 1, 1 - slot).start(priority=ROW_DMA_PRIORITY)

                @pl.when(u >= 2)
                def _():
                    o_copy(u - 2, slot).wait()

                obuf[slot] = _expert_rows(xbuf[slot], wgu_s, wd_s)
                o_copy(u, slot).start(priority=ROW_DMA_PRIORITY)

    for k in range(len(TAILS)):
        @pl.when(tail_on[k])
        def _():
            xt_copy(k, start, tail_off[k]).wait()
            otails[k][...] = _expert_rows(xtails[k][...], wgu_s, wd_s)
            ot_copy(k).start(priority=ROW_DMA_PRIORITY)

    nxt = jnp.minimum(e + 1, NE - 1)

    @pl.when(e + 1 < NE)
    def _():
        fetch_first(st_ref[nxt], nb_ref[nxt])

    for back in (1, 2):
        @pl.when(n_units >= back)
        def _():
            last = n_units - back
            o_copy(last, last % 2).wait()

    for k in range(len(TAILS)):
        @pl.when(tail_on[k])
        def _():
            ot_copy(k).wait()


def _experts(layer, starts, n_blocks, xs, w_gate, w_up, w_down, name):
    def w_map(e, st, nb):
        return (layer, e, 0, 0)

    return pl.pallas_call(
        _expert_kernel,
        grid_spec=pltpu.PrefetchScalarGridSpec(
            num_scalar_prefetch=2,
            grid=(NE,),
            in_specs=[
                pl.BlockSpec(memory_space=pl.ANY),
                pl.BlockSpec((1, 1, D, F), w_map),
                pl.BlockSpec((1, 1, D, F), w_map),
                pl.BlockSpec((1, 1, F, D), w_map),
            ],
            out_specs=pl.BlockSpec(memory_space=pl.ANY),
            scratch_shapes=[
                pltpu.VMEM((D, 2 * F), BF16), pltpu.VMEM((F, D), BF16),
                pltpu.VMEM((2, UNIT, DP), U32), pltpu.VMEM((2, UNIT, DP), U32),
                pltpu.VMEM((TAILS[0], DP), U32), pltpu.VMEM((TAILS[0], DP), U32),
                pltpu.VMEM((TAILS[1], DP), U32), pltpu.VMEM((TAILS[1], DP), U32),
                pltpu.SemaphoreType.DMA((2,)), pltpu.SemaphoreType.DMA((2,)),
                pltpu.SemaphoreType.DMA((2 * len(TAILS),)),
            ],
        ),
        out_shape=jax.ShapeDtypeStruct(xs.shape, U32),
        compiler_params=pltpu.CompilerParams(
            dimension_semantics=("arbitrary",), vmem_limit_bytes=VMEM_LIMIT),
        name=name,
    )(starts, n_blocks, xs, w_gate, w_up, w_down)


def _sc_worker_id():
    return lax.axis_index("s") * SC_CORES + lax.axis_index("c")


def _sc_mesh():
    return plsc.VectorSubcoreMesh(core_axis_name="c", subcore_axis_name="s")


def _sc_scratch(n_index_rows):
    return [
        pltpu.VMEM((n_index_rows, SC_CHUNK), I32),
        pltpu.VMEM((2, SC_CHUNK, DP), U32),
        pltpu.SemaphoreType.DMA((2,)),
        pltpu.SemaphoreType.DMA((2,)),
    ]


def _dispatch_rows(h2p, dest, t0):
    n = dest.shape[1]
    per_w = n // SC_WORKERS
    n_chunks = per_w // SC_CHUNK

    @functools.partial(
        pl.kernel, mesh=_sc_mesh(),
        out_type=jax.ShapeDtypeStruct((n * TOP_K + PAD_SLOTS, DP), U32),
        scratch_types=_sc_scratch(TOP_K * n_chunks),
        name="dispatch_rows",
    )
    def k(h2_hbm, dest_hbm, out_hbm, dest_v, rows_v, rsem, wsem):
        wid = _sc_worker_id()
        for kk in range(TOP_K):
            pltpu.sync_copy(dest_hbm.at[kk, pl.ds(wid * n_chunks, n_chunks)],
                            dest_v.at[pl.ds(kk * n_chunks, n_chunks)])
        base = t0 + wid * per_w

        def read(c, slot):
            return pltpu.make_async_copy(h2_hbm.at[pl.ds(base + c * SC_CHUNK, SC_CHUNK)],
                                         rows_v.at[slot], rsem.at[slot])

        def write(c, kk, slot):
            return pltpu.make_async_copy(rows_v.at[slot], out_hbm.at[dest_v.at[kk * n_chunks + c]],
                                         wsem.at[slot])

        read(0, 0).start()

        @pl.loop(0, n_chunks, step=2)
        def _(c):
            for b in range(2):
                cc = c + b
                read(cc, b).wait()

                @pl.when(cc + 1 < n_chunks)
                def _():
                    @pl.when(cc >= 1)
                    def _():
                        for kk in range(TOP_K):
                            write(cc - 1, kk, 1 - b).wait()
                    read(cc + 1, 1 - b).start()

                for kk in range(TOP_K):
                    write(cc, kk, b).start()

        for slot, cc in ((0, n_chunks - 2), (1, n_chunks - 1)):
            for kk in range(TOP_K):
                write(cc, kk, slot).wait()

    return k(h2p, dest.reshape(TOP_K, n // SC_CHUNK, SC_CHUNK))


def _return_rows(yb, dest):
    m = dest.size
    per_w = m // SC_WORKERS
    n_chunks = per_w // SC_CHUNK

    @functools.partial(
        pl.kernel, mesh=_sc_mesh(),
        out_type=jax.ShapeDtypeStruct((m, DP), U32),
        scratch_types=_sc_scratch(n_chunks),
        name="return_rows",
    )
    def k(yb_hbm, dest_hbm, out_hbm, idx_v, rows_v, gsem, wsem):
        wid = _sc_worker_id()
        pltpu.sync_copy(dest_hbm.at[pl.ds(wid * n_chunks, n_chunks)], idx_v)
        base = wid * per_w

        def gather(c, slot):
            return pltpu.make_async_copy(yb_hbm.at[idx_v.at[c]], rows_v.at[slot], gsem.at[slot])

        def write(c, slot):
            return pltpu.make_async_copy(rows_v.at[slot], out_hbm.at[pl.ds(base + c * SC_CHUNK, SC_CHUNK)],
                                         wsem.at[slot])

        gather(0, 0).start()

        @pl.loop(0, n_chunks, step=2)
        def _(c):
            for b in range(2):
                cc = c + b
                gather(cc, b).wait()

                @pl.when(cc + 1 < n_chunks)
                def _():
                    @pl.when(cc >= 1)
                    def _():
                        write(cc - 1, 1 - b).wait()
                    gather(cc + 1, 1 - b).start()

                write(cc, b).start()

        write(n_chunks - 2, 0).wait()
        write(n_chunks - 1, 1).wait()

    return k(yb, dest.reshape(m // SC_CHUNK, SC_CHUNK))


def _final_kernel(x1_ref, y0_ref, y1_ref, wc_ref, mods_ref, ln_ref, *rest):
    o_ref = rest[-1]
    ln = ln_ref[0]
    o_ref[0] = _combine(x1_ref[0], y0_ref[0], y1_ref[0], wc_ref[0], mods_ref[0, 0][5:6, :],
                        ln[0:1, :], ln[1:2, :])


def _final(streams, params):
    mods, ffn_ln = params["mods"], params["ffn_ln"]
    out = None
    for hh, (b0, nb) in enumerate(zip(PART_START, PART_BATCHES)):
        x1, yk, wc = streams[hh]
        args = [x1, yk, yk, wc, mods, ffn_ln]
        specs = [_tile_spec(D, b0), _yk_spec(0), _yk_spec(1), _tile_spec(2 * LANES, b0),
                 _mods_spec(DEPTH - 1, b0), _layer_spec(ffn_ln, DEPTH - 1)]
        aliases = {}
        if out is not None:
            aliases[len(args)] = 0
            args.append(out)
            specs.append(_any_spec())
        out = pl.pallas_call(
            _final_kernel,
            grid=(nb, NS),
            in_specs=specs,
            out_specs=_tile_spec(D, b0),
            out_shape=jax.ShapeDtypeStruct((BATCH, SEQ, D), F32),
            input_output_aliases=aliases,
            compiler_params=pltpu.CompilerParams(
                dimension_semantics=("arbitrary", "arbitrary"), vmem_limit_bytes=VMEM_LIMIT),
            name=f"final_combine_h{hh}",
        )(*args)
    return out


def _plan(counts):
    n_blocks = (counts + BM - 1) // BM
    padded = n_blocks * BM
    pad_starts = (jnp.cumsum(padded) - padded).astype(I32)
    return pad_starts, n_blocks.astype(I32)


def _router_params(w_group, b_group, w_expert, b_expert):
    wr = jnp.zeros((DEPTH, NR, D), F32)
    wr = wr.at[:, 0:N_GROUPS].set(jnp.swapaxes(w_group, 1, 2))
    wr = wr.at[:, SUBLANES:SUBLANES + NE].set(jnp.swapaxes(w_expert, 1, 2))
    rb = jnp.full((DEPTH, NR), NEG, F32)
    rb = rb.at[:, 0:N_GROUPS].set(b_group).at[:, SUBLANES:SUBLANES + NE].set(b_expert)
    rb = rb.at[:, SUBLANES + NE:].set(0.0)
    return wr.astype(BF16), rb.reshape(DEPTH, NR, 1)


def kernel(x, c, ada_w, ada_b, a_w_in, a_b_in, a_w_dw, a_b_dw, a_ln_g, a_ln_b, a_w_out, a_b_out,
           b_w_in, b_w_dw, b_w_out, mix_ln_g, mix_ln_b, ffn_ln_g, ffn_ln_b,
           r_w_group, r_b_group, r_w_expert, r_b_expert, e_w_gate, e_w_up, e_w_down):
    n_a = a_w_in.shape[0]
    wr, rb = _router_params(r_w_group, r_b_group, r_w_expert, r_b_expert)

    def cast_layer(w, layer, after):
        wl = w[layer // 2:layer // 2 + 1]
        if after is not None:
            wl = wl + after * 0.0
        return wl.astype(BF16)

    params = {
        "mods": _ada_mods(c, ada_w, ada_b),
        "mix_ln": jnp.stack([mix_ln_g, mix_ln_b], axis=1),
        "ffn_ln": jnp.stack([ffn_ln_g, ffn_ln_b], axis=1),
        "wr": wr, "rb": rb,
        "tri": (jnp.arange(TS)[:, None] < jnp.arange(TS)[None, :]).astype(BF16),
        "a": [None, a_b_in.reshape(n_a, 1, 2 * D), a_w_dw.reshape(n_a, CONV_A, DT, LANES),
              a_b_dw.reshape(n_a, DT, LANES), jnp.stack([a_ln_g, a_ln_b, a_b_out], axis=1), None],
        "b": [None, b_w_dw, None],
    }
    w_in = {"a": a_w_in, "b": b_w_in}
    w_out = {"a": a_w_out, "b": b_w_out}
    n_parts = len(PART_BATCHES)
    xin = [x] * n_parts
    prev = [None] * n_parts
    after = None
    for i in range(DEPTH):
        kind = "a" if i % 2 == 0 else "b"
        params[kind][0] = cast_layer(w_in[kind], i, after)
        params[kind][-1] = cast_layer(w_out[kind], i, after)
        for p, (b0, nb) in enumerate(zip(PART_START, PART_BATCHES)):
            x1, h2, ri, wc, counts = _mixer(kind, i, xin[p], prev[p], params, p)
            after = counts[0, 0].astype(F32)
            pad_starts, n_blocks = _plan(counts[:, 0])
            dest = _dest_slots(pad_starts, ri, b0 * SEQ, nb * SEQ)[0:TOP_K]
            xs = _dispatch_rows(h2.reshape(T, DP), dest, b0 * SEQ)
            yb = _experts(i, pad_starts, n_blocks, xs, e_w_gate, e_w_up, e_w_down, name=f"experts{i}_h{p}")
            yk = _return_rows(yb, dest).reshape(TOP_K, nb * SEQ, DP)
            prev[p] = (yk, wc)
            xin[p] = x1
    return _final([(xin[p],) + prev[p] for p in range(n_parts)], params)
```

```python
import functools

import jax
import jax.numpy as jnp
from jax import lax
from jax.experimental import pallas as pl
from jax.experimental.pallas import tpu as pltpu
from jax.experimental.pallas import tpu_sc as plsc

F32 = jnp.float32
BF16 = jnp.bfloat16
I32 = jnp.int32
U32 = jnp.uint32

D = 1024
BATCH = 4
SEQ = 8192
T = BATCH * SEQ
DEPTH = 4
N_GROUPS = 4
EPG = 8
NE = N_GROUPS * EPG
TOP_K = 2
F = D // 2
CONV_A = 31
CONV_B = 3
ALPHA = (2.0 * DEPTH) ** 0.25
LN_EPS = 1e-5

LANES = 128
SUBLANES = 8
VMEM_LIMIT = 56 * 1024 * 1024

TS = 512
SR = 512
NS = SEQ // TS
PART_BATCHES = (1, 3)
PART_START = (0, 1)
assert sum(PART_BATCHES) == BATCH
HALO_A = 32
HALO_B = 8
BM = 256
BM_SHIFT = BM.bit_length() - 1
assert BM == 1 << BM_SHIFT
BLOCKS_PER_UNIT = 4
UNIT = BLOCKS_PER_UNIT * BM
TAILS = (2 * BM, BM)
ROW_DMA_PRIORITY = 1
NSLOT = T * TOP_K + NE * BM
NR = 48
ADA_TN = 1536
NEG = -1e30
DP = D // 2
DT = D // LANES
assert DT == SUBLANES

SC_CORES = 2
SC_SUBCORES = 16
SC_WORKERS = SC_CORES * SC_SUBCORES
SC_CHUNK = 64


def _sigmoid(x):
    return 1.0 / (1.0 + jnp.exp(-x))


def _pack_rows(x):
    return pltpu.pack_elementwise([x[:, :DP], x[:, DP:]], packed_dtype=BF16)


def _unpack_rows(p):
    lo = pltpu.unpack_elementwise(p, index=0, packed_dtype=BF16, unpacked_dtype=F32)
    hi = pltpu.unpack_elementwise(p, index=1, packed_dtype=BF16, unpacked_dtype=F32)
    return jnp.concatenate([lo, hi], axis=1)


def _layer_norm(x, g, b):
    mu = jnp.mean(x, axis=-1, keepdims=True)
    xc = x - mu
    var = jnp.mean(xc * xc, axis=-1, keepdims=True)
    return xc * lax.rsqrt(var + LN_EPS) * g + b


def _ada_kernel(c_ref, w_ref, b_ref, o_ref):
    c = c_ref[...]
    ca = (c * _sigmoid(c)).astype(BF16)
    w = w_ref[0].astype(BF16)
    o_ref[0] = jnp.dot(ca, w, preferred_element_type=F32) + b_ref[0]


def _ada_mods(c, ada_w, ada_b):
    out = pl.pallas_call(
        _ada_kernel,
        grid=(DEPTH, 6 * D // ADA_TN),
        in_specs=[
            pl.BlockSpec((BATCH, D), lambda i, j: (0, 0)),
            pl.BlockSpec((1, D, ADA_TN), lambda i, j: (i, 0, j)),
            pl.BlockSpec((1, 1, ADA_TN), lambda i, j: (i, 0, j)),
        ],
        out_specs=pl.BlockSpec((1, BATCH, ADA_TN), lambda i, j: (i, 0, j)),
        out_shape=jax.ShapeDtypeStruct((DEPTH, BATCH, 6 * D), F32),
        compiler_params=pltpu.CompilerParams(
            dimension_semantics=("arbitrary", "arbitrary"), vmem_limit_bytes=VMEM_LIMIT),
        name="ada_mods",
    )(c, ada_w, ada_b.reshape(DEPTH, 1, 6 * D))
    return out.reshape(DEPTH, BATCH, 6, D)


def _combine(x1, y0p, y1p, wc, g_f, ln_g, ln_b):
    w0 = jnp.tile(wc[:, :LANES], (1, D // LANES))
    w1 = jnp.tile(wc[:, LANES:], (1, D // LANES))
    y = w0 * _unpack_rows(y0p) + w1 * _unpack_rows(y1p)
    return _layer_norm(ALPHA * x1 + (1.0 + g_f) * y, ln_g, ln_b)


def _route(h2, wr_ref, rb_ref, tri_ref, cnt_ref, ri_ref, wc_ref, cnto_ref):
    lt = lax.dot_general(wr_ref[...], h2, (((1,), (1,)), ((), ())),
                         preferred_element_type=F32) + rb_ref[...]
    iota8 = lax.broadcasted_iota(I32, (SUBLANES, TS), 0).astype(F32)
    gl = lt[0:SUBLANES]
    gmax = jnp.max(gl, axis=0, keepdims=True)
    gidx = jnp.min(jnp.where(gl == gmax, iota8, float(SUBLANES)), axis=0, keepdims=True)
    gw = 1.0 / jnp.sum(jnp.exp(gl - gmax), axis=0, keepdims=True)
    el = lt[SUBLANES:2 * SUBLANES]
    for g in range(1, N_GROUPS):
        el = jnp.where(gidx == float(g), lt[SUBLANES * (g + 1):SUBLANES * (g + 2)], el)
    m1 = jnp.max(el, axis=0, keepdims=True)
    i1 = jnp.min(jnp.where(el == m1, iota8, float(SUBLANES)), axis=0, keepdims=True)
    el2 = jnp.where(iota8 == i1, -jnp.inf, el)
    m2 = jnp.max(el2, axis=0, keepdims=True)
    i2 = jnp.min(jnp.where(el2 == m2, iota8, float(SUBLANES)), axis=0, keepdims=True)
    r = jnp.exp(m2 - m1)
    w_a = gw / (1.0 + r)
    w_b = gw * r / (1.0 + r)
    e1 = gidx * float(EPG) + i1
    e2 = gidx * float(EPG) + i2

    iota_e = lax.broadcasted_iota(I32, (NE, TS), 0).astype(F32)
    oh1 = iota_e == e1
    oh2 = iota_e == e2
    oh = jnp.concatenate([jnp.where(oh1, 1.0, 0.0), jnp.where(oh2, 1.0, 0.0)], axis=0)
    before = jnp.dot(oh.astype(BF16), tri_ref[...], preferred_element_type=F32)
    tot = jnp.sum(oh, axis=1, keepdims=True)
    cnt = cnt_ref[...]
    base = jnp.tile(cnt, (1, TS // LANES))
    tot1 = tot[:NE]
    tot2 = tot[NE:]
    rank1 = jnp.sum(jnp.where(oh1, base + before[:NE], 0.0), axis=0, keepdims=True)
    rank2 = jnp.sum(jnp.where(oh2, base + tot1 + before[NE:], 0.0), axis=0, keepdims=True)
    new_cnt = cnt + tot1 + tot2
    cnt_ref[...] = new_cnt
    cnto_ref[...] = new_cnt.astype(I32)

    ri_ref[0:1, :] = e1.astype(I32)
    ri_ref[1:2, :] = e2.astype(I32)
    ri_ref[2:3, :] = rank1.astype(I32)
    ri_ref[3:4, :] = rank2.astype(I32)
    ri_ref[4:8, :] = jnp.zeros((4, TS), I32)
    wc_ref[0, :, :LANES] = jnp.broadcast_to(w_a, (LANES, TS)).T
    wc_ref[0, :, LANES:] = jnp.broadcast_to(w_b, (LANES, TS)).T


def _conv_taps(uext_ref, u, w_dw, halo, width, r0):
    uext_ref[halo + r0:halo + r0 + SR, :] = u
    acc = None
    for k in range(width):
        off = r0 + halo - (width - 1) + k
        term = w_dw[k:k + 1, :] * uext_ref[off:off + SR, :]
        acc = term if acc is None else acc + term
    return acc


def _conv_time_major(tm_ref, o2_ref, u, wk_ref, bias, r0, width, halo):
    for j in range(DT):
        tm_ref[pl.ds((halo + r0) * DT + j, SR, stride=DT), :] = u[:, j * LANES:(j + 1) * LANES]
    acc = None
    for k in range(width):
        off = (r0 + halo - (width - 1) + k) * DT
        term = tm_ref[off:off + SR * DT, :].reshape(SR, DT, LANES) * wk_ref[k]
        acc = term if acc is None else acc + term
    if bias is not None:
        acc = acc + bias
    o2_ref[r0 * DT:(r0 + SR) * DT, :] = acc.reshape(SR * DT, LANES)
    return jnp.concatenate([o2_ref[pl.ds(r0 * DT + j, SR, stride=DT), :] for j in range(DT)], axis=1)


def _mixer_kernel(*refs, kind, has_prev, carry):
    it = iter(refs)
    xin_ref = next(it)
    if has_prev:
        y0_ref, y1_ref, wcin_ref, pmods_ref, pln_ref = (next(it) for _ in range(5))
    if carry:
        for _ in range(4):
            next(it)
        cnt_in_ref = next(it)
    mods_ref, mln_ref, win_ref = next(it), next(it), next(it)
    if kind == "a":
        bin_ref, wdw_ref, bdw_ref, vec_ref = next(it), next(it), next(it), next(it)
    else:
        wdw_ref = next(it)
    wout_ref, wr_ref, rb_ref, tri_ref = (next(it) for _ in range(4))
    x1_ref, h2_ref, ri_ref, wc_ref, cnto_ref = (next(it) for _ in range(5))
    if kind == "a":
        uext_ref, o2_ref, h2b_ref, cnt_ref = (next(it) for _ in range(4))
    else:
        uext_ref, h2b_ref, cnt_ref = (next(it) for _ in range(3))
    win_s, wout_s = next(it), next(it)
    mln_ref, win_ref, wdw_ref, wout_ref, wr_ref, rb_ref = (
        r.at[0] for r in (mln_ref, win_ref, wdw_ref, wout_ref, wr_ref, rb_ref))
    if has_prev:
        pln_ref = pln_ref.at[0]
    if kind == "a":
        bin_ref, bdw_ref, vec_ref = (r.at[0] for r in (bin_ref, bdw_ref, vec_ref))

    first = (pl.program_id(0) == 0) & (pl.program_id(1) == 0)

    @pl.when(first)
    def _():
        if carry:
            cnt_ref[...] = cnt_in_ref[...].astype(F32)
        else:
            cnt_ref[...] = jnp.zeros((NE, LANES), F32)
        win_s[...] = win_ref[...]
        wout_s[...] = wout_ref[...]

    halo_rows = HALO_A * DT if kind == "a" else HALO_B

    @pl.when(pl.program_id(1) == 0)
    def _():
        uext_ref[0:halo_rows, :] = jnp.zeros((halo_rows, uext_ref.shape[1]), F32)

    m = mods_ref[0, 0]
    mln = mln_ref[...]
    for i in range(TS // SR):
        r0 = i * SR
        rows = slice(r0, r0 + SR)
        x = xin_ref[0, rows, :]
        if has_prev:
            pln = pln_ref[...]
            x = _combine(x, y0_ref[0, rows, :], y1_ref[0, rows, :], wcin_ref[0, rows, :],
                         pmods_ref[0, 0][5:6, :], pln[0:1, :], pln[1:2, :])

        h = (x * (1.0 + m[1:2, :]) + m[0:1, :]).astype(BF16)
        cols = [jnp.dot(h, win_s[:, c * D:(c + 1) * D], preferred_element_type=F32)
                for c in range(win_s.shape[1] // D)]
        if kind == "a":
            b_in = bin_ref[...]
            vec = vec_ref[...]
            u = (cols[0] + b_in[:, :D]) * _sigmoid(cols[1] + b_in[:, D:])
            u = _conv_time_major(uext_ref, o2_ref, u, wdw_ref, bdw_ref[...], r0, CONV_A, HALO_A)
            u = _layer_norm(u, vec[0:1, :], vec[1:2, :])
            u = u * _sigmoid(u)
            y = jnp.dot(u.astype(BF16), wout_s[...], preferred_element_type=F32) + vec[2:3, :]
        else:
            gb = cols[0]
            q = cols[1] * cols[2]
            u = _conv_taps(uext_ref, q, wdw_ref[...], HALO_B, CONV_B, r0)
            y = jnp.dot((gb * u).astype(BF16), wout_s[...], preferred_element_type=F32)

        x1 = _layer_norm(ALPHA * x + (1.0 + m[2:3, :]) * y, mln[0:1, :], mln[1:2, :])
        x1_ref[0, rows, :] = x1
        h2 = x1 * (1.0 + m[4:5, :]) + m[3:4, :]
        h2_ref[0, rows, :] = _pack_rows(h2)
        h2b_ref[rows, :] = h2.astype(BF16)

    uext_ref[0:halo_rows, :] = uext_ref[uext_ref.shape[0] - halo_rows:uext_ref.shape[0], :]
    _route(h2b_ref[...], wr_ref, rb_ref, tri_ref, cnt_ref, ri_ref, wc_ref, cnto_ref)


def _tile_spec(width, b0=0):
    return pl.BlockSpec((1, TS, width), lambda b, s: (b + b0, s, 0))


def _yk_spec(k):
    return pl.BlockSpec((1, TS, DP), lambda b, s: (k, b * NS + s, 0))


def _const_spec(shape):
    nd = len(shape)
    return pl.BlockSpec(shape, lambda b, s: (0,) * nd)


def _mods_spec(layer, b0=0):
    return pl.BlockSpec((1, 1, 6, D), lambda b, s: (layer, b + b0, 0, 0))


def _layer_spec(arr, idx):
    tail = arr.shape[1:]
    return pl.BlockSpec((1,) + tail, lambda b, s: (idx,) + (0,) * len(tail))


def _any_spec():
    return pl.BlockSpec(memory_space=pl.ANY)


def _mixer_call(kind, layer, xin, prev, params, name, b0, n_batch, carried):
    mods, mix_ln, ffn_ln, wr, rb, tri = (params[k] for k in ("mods", "mix_ln", "ffn_ln", "wr", "rb", "tri"))
    weights = params[kind]
    j = layer // 2
    has_prev = prev is not None
    carry = carried is not None
    args = [xin]
    specs = [_tile_spec(D, b0)]
    if has_prev:
        yk, wcin = prev
        args += [yk, yk, wcin, mods, ffn_ln]
        specs += [_yk_spec(0), _yk_spec(1), _tile_spec(2 * LANES, b0), _mods_spec(layer - 1, b0),
                  _layer_spec(ffn_ln, layer - 1)]
    aliases = {}
    if carry:
        for out_idx, arr in enumerate(carried):
            aliases[len(args)] = out_idx
            args.append(arr)
            specs.append(_any_spec() if out_idx < 4 else _const_spec((NE, LANES)))
    args += [mods, mix_ln]
    specs += [_mods_spec(layer, b0), _layer_spec(mix_ln, layer)]
    for w in weights:
        args.append(w)
        specs.append(_layer_spec(w, j if w.shape[0] > 1 else 0))
    args += [wr, rb, tri]
    specs += [_layer_spec(wr, layer), _layer_spec(rb, layer), _const_spec(tri.shape)]
    if kind == "a":
        conv_scratch = [pltpu.VMEM(((TS + HALO_A) * DT, LANES), F32), pltpu.VMEM((TS * DT, LANES), F32)]
    else:
        conv_scratch = [pltpu.VMEM((TS + HALO_B, D), F32)]
    out_shape = (
        jax.ShapeDtypeStruct((BATCH, SEQ, D), F32),
        jax.ShapeDtypeStruct((BATCH, SEQ, DP), U32),
        jax.ShapeDtypeStruct((SUBLANES, T), I32),
        jax.ShapeDtypeStruct((BATCH, SEQ, 2 * LANES), F32),
        jax.ShapeDtypeStruct((NE, LANES), I32),
    )
    out_specs = (
        _tile_spec(D, b0), _tile_spec(DP, b0),
        pl.BlockSpec((SUBLANES, TS), lambda b, s: (0, (b + b0) * NS + s)),
        _tile_spec(2 * LANES, b0),
        pl.BlockSpec((NE, LANES), lambda b, s: (0, 0)),
    )
    return pl.pallas_call(
        functools.partial(_mixer_kernel, kind=kind, has_prev=has_prev, carry=carry),
        grid=(n_batch, NS),
        in_specs=specs,
        out_specs=out_specs,
        out_shape=out_shape,
        input_output_aliases=aliases,
        scratch_shapes=conv_scratch + [pltpu.VMEM((TS, D), BF16), pltpu.VMEM((NE, LANES), F32),
                                       pltpu.VMEM(weights[0].shape[1:], BF16),
                                       pltpu.VMEM(weights[-1].shape[1:], BF16)],
        compiler_params=pltpu.CompilerParams(
            dimension_semantics=("arbitrary", "arbitrary"), vmem_limit_bytes=VMEM_LIMIT),
        name=name,
    )(*args)


def _mixer(kind, layer, xin, prev, params):
    name = f"mixer_{kind}{layer}"
    if prev is None:
        return _mixer_call(kind, layer, xin, None, params, name, 0, BATCH, None)
    yk_parts, wcin = prev
    out = None
    for hh, (b0, nb) in enumerate(zip(PART_START, PART_BATCHES)):
        out = _mixer_call(kind, layer, xin, (yk_parts[hh], wcin), params, f"{name}_h{hh}", b0, nb, out)
    return out


def _dest_kernel(cnt_ref, ri_ref, o_ref, st_ref, nb_ref):
    first_row = jnp.int32(0)
    starts = []
    for k in range(NE):
        blocks = lax.shift_right_logical(cnt_ref[k, 0] + (BM - 1), BM_SHIFT)
        starts.append(first_row)
        st_ref[k] = first_row
        nb_ref[k] = blocks
        first_row = first_row + blocks * BM
    ri = ri_ref[...]
    e = ri[0:2, :]
    start = jnp.zeros_like(e)
    for k in range(NE):
        start = jnp.where(e == k, starts[k], start)
    o_ref[0:2, :] = start + ri[2:4, :]
    o_ref[2:8, :] = jnp.zeros((6, ri.shape[1]), I32)


def _dest_slots(counts, ri):
    tn = 4096
    smem = pl.BlockSpec(memory_space=pltpu.SMEM)
    return pl.pallas_call(
        _dest_kernel,
        grid_spec=pltpu.PrefetchScalarGridSpec(
            num_scalar_prefetch=1,
            grid=(T // tn,),
            in_specs=[pl.BlockSpec((SUBLANES, tn), lambda i, cnt: (0, i))],
            out_specs=(pl.BlockSpec((SUBLANES, tn), lambda i, cnt: (0, i)), smem, smem),
        ),
        out_shape=(jax.ShapeDtypeStruct((SUBLANES, T), I32),
                   jax.ShapeDtypeStruct((NE,), I32), jax.ShapeDtypeStruct((NE,), I32)),
        compiler_params=pltpu.CompilerParams(dimension_semantics=("arbitrary",)),
        name="dest_slots",
    )(counts, ri)


def _expert_rows(x_packed, wgu_s, wd_s):
    x = _unpack_rows(x_packed).astype(BF16)
    gu = jnp.dot(x, wgu_s[...], preferred_element_type=F32)
    g = gu[:, :F]
    hid = (g * _sigmoid(g) * gu[:, F:]).astype(BF16)
    return _pack_rows(jnp.dot(hid, wd_s[...], preferred_element_type=F32))


def _expert_kernel(st_ref, nb_ref, xs_hbm, wg_ref, wu_ref, wd_ref, yb_hbm,
                   wgu_s, wd_s, xbuf, obuf, xt0, ot0, xt1, ot1, xsem, osem, tsem):
    e = pl.program_id(0)
    start = st_ref[e]
    xtails, otails = (xt0, xt1), (ot0, ot1)

    def split(n_blocks):
        units = n_blocks // BLOCKS_PER_UNIT
        rem = n_blocks % BLOCKS_PER_UNIT
        on = (rem >= 2, rem % 2 == 1)
        off0 = units * UNIT
        off1 = off0 + jnp.where(on[0], TAILS[0], 0)
        return units, on, (off0, off1)

    n_units, tail_on, tail_off = split(nb_ref[e])

    def x_copy(first_row, u, slot):
        return pltpu.make_async_copy(xs_hbm.at[pl.ds(pl.multiple_of(first_row + u * UNIT, BM), UNIT)],
                                     xbuf.at[slot], xsem.at[slot])

    def o_copy(u, slot):
        return pltpu.make_async_copy(obuf.at[slot],
                                     yb_hbm.at[pl.ds(pl.multiple_of(start + u * UNIT, BM), UNIT)], osem.at[slot])

    def xt_copy(k, first_row, off):
        return pltpu.make_async_copy(xs_hbm.at[pl.ds(pl.multiple_of(first_row + off, BM), TAILS[k])],
                                     xtails[k], tsem.at[2 * k])

    def ot_copy(k):
        return pltpu.make_async_copy(otails[k], yb_hbm.at[pl.ds(pl.multiple_of(start + tail_off[k], BM), TAILS[k])],
                                     tsem.at[2 * k + 1])

    def fetch_first(first_row, n_blocks):
        units, on, off = split(n_blocks)

        @pl.when(units > 0)
        def _():
            x_copy(first_row, 0, 0).start(priority=ROW_DMA_PRIORITY)

        for k in range(len(TAILS)):
            @pl.when(on[k])
            def _():
                xt_copy(k, first_row, off[k]).start(priority=ROW_DMA_PRIORITY)

    @pl.when(e == 0)
    def _():
        fetch_first(start, nb_ref[e])

    @pl.when(nb_ref[e] > 0)
    def _():
        wgu_s[:, :F] = wg_ref[0, 0].astype(BF16)
        wgu_s[:, F:] = wu_ref[0, 0].astype(BF16)
        wd_s[...] = wd_ref[0, 0].astype(BF16)

    @pl.loop(0, n_units, step=2)
    def _(u0):
        for slot in range(2):
            u = u0 + slot

            @pl.when(u < n_units)
            def _():
                x_copy(start, u, slot).wait()

                @pl.when(u + 1 < n_units)
                def _():
                    x_copy(start, u + 1, 1 - slot).start(priority=ROW_DMA_PRIORITY)

                @pl.when(u >= 2)
                def _():
                    o_copy(u - 2, slot).wait()

                obuf[slot] = _expert_rows(xbuf[slot], wgu_s, wd_s)
                o_copy(u, slot).start(priority=ROW_DMA_PRIORITY)

    for k in range(len(TAILS)):
        @pl.when(tail_on[k])
        def _():
            xt_copy(k, start, tail_off[k]).wait()
            otails[k][...] = _expert_rows(xtails[k][...], wgu_s, wd_s)
            ot_copy(k).start(priority=ROW_DMA_PRIORITY)

    nxt = jnp.minimum(e + 1, NE - 1)

    @pl.when(e + 1 < NE)
    def _():
        fetch_first(st_ref[nxt], nb_ref[nxt])

    for back in (1, 2):
        @pl.when(n_units >= back)
        def _():
            last = n_units - back
            o_copy(last, last % 2).wait()

    for k in range(len(TAILS)):
        @pl.when(tail_on[k])
        def _():
            ot_copy(k).wait()


def _experts(layer, starts, n_blocks, xs, w_gate, w_up, w_down, name):
    def w_map(e, st, nb):
        return (layer, e, 0, 0)

    return pl.pallas_call(
        _expert_kernel,
        grid_spec=pltpu.PrefetchScalarGridSpec(
            num_scalar_prefetch=2,
            grid=(NE,),
            in_specs=[
                pl.BlockSpec(memory_space=pl.ANY),
                pl.BlockSpec((1, 1, D, F), w_map),
                pl.BlockSpec((1, 1, D, F), w_map),
                pl.BlockSpec((1, 1, F, D), w_map),
            ],
            out_specs=pl.BlockSpec(memory_space=pl.ANY),
            scratch_shapes=[
                pltpu.VMEM((D, 2 * F), BF16), pltpu.VMEM((F, D), BF16),
                pltpu.VMEM((2, UNIT, DP), U32), pltpu.VMEM((2, UNIT, DP), U32),
                pltpu.VMEM((TAILS[0], DP), U32), pltpu.VMEM((TAILS[0], DP), U32),
                pltpu.VMEM((TAILS[1], DP), U32), pltpu.VMEM((TAILS[1], DP), U32),
                pltpu.SemaphoreType.DMA((2,)), pltpu.SemaphoreType.DMA((2,)),
                pltpu.SemaphoreType.DMA((2 * len(TAILS),)),
            ],
        ),
        out_shape=jax.ShapeDtypeStruct((NSLOT, DP), U32),
        compiler_params=pltpu.CompilerParams(
            dimension_semantics=("arbitrary",), vmem_limit_bytes=VMEM_LIMIT),
        name=name,
    )(starts, n_blocks, xs, w_gate, w_up, w_down)


def _sc_worker_id():
    return lax.axis_index("s") * SC_CORES + lax.axis_index("c")


def _sc_mesh():
    return plsc.VectorSubcoreMesh(core_axis_name="c", subcore_axis_name="s")


def _sc_scratch(n_index_rows):
    return [
        pltpu.VMEM((n_index_rows, SC_CHUNK), I32),
        pltpu.VMEM((2, SC_CHUNK, DP), U32),
        pltpu.SemaphoreType.DMA((2,)),
        pltpu.SemaphoreType.DMA((2,)),
    ]


def _dispatch_rows(h2p, dest):
    per_w = T // SC_WORKERS
    n_chunks = per_w // SC_CHUNK

    @functools.partial(
        pl.kernel, mesh=_sc_mesh(),
        out_type=jax.ShapeDtypeStruct((NSLOT, DP), U32),
        scratch_types=_sc_scratch(TOP_K * n_chunks),
        name="dispatch_rows",
    )
    def k(h2_hbm, dest_hbm, out_hbm, dest_v, rows_v, rsem, wsem):
        wid = _sc_worker_id()
        for kk in range(TOP_K):
            pltpu.sync_copy(dest_hbm.at[kk, pl.ds(wid * n_chunks, n_chunks)],
                            dest_v.at[pl.ds(kk * n_chunks, n_chunks)])
        base = wid * per_w

        def read(c, slot):
            return pltpu.make_async_copy(h2_hbm.at[pl.ds(base + c * SC_CHUNK, SC_CHUNK)],
                                         rows_v.at[slot], rsem.at[slot])

        def write(c, kk, slot):
            return pltpu.make_async_copy(rows_v.at[slot], out_hbm.at[dest_v.at[kk * n_chunks + c]],
                                         wsem.at[slot])

        read(0, 0).start()

        @pl.loop(0, n_chunks, step=2)
        def _(c):
            for b in range(2):
                cc = c + b
                read(cc, b).wait()

                @pl.when(cc + 1 < n_chunks)
                def _():
                    @pl.when(cc >= 1)
                    def _():
                        for kk in range(TOP_K):
                            write(cc - 1, kk, 1 - b).wait()
                    read(cc + 1, 1 - b).start()

                for kk in range(TOP_K):
                    write(cc, kk, b).start()

        for slot, cc in ((0, n_chunks - 2), (1, n_chunks - 1)):
            for kk in range(TOP_K):
                write(cc, kk, slot).wait()

    return k(h2p, dest.reshape(SUBLANES, T // SC_CHUNK, SC_CHUNK))


def _return_rows(yb, dest, t0, n):
    m = TOP_K * n
    per_w = m // SC_WORKERS
    n_chunks = per_w // SC_CHUNK
    workers_per_k = SC_WORKERS // TOP_K
    wk_shift = workers_per_k.bit_length() - 1
    assert workers_per_k == 1 << wk_shift and n % (workers_per_k * SC_CHUNK) == 0 and t0 % SC_CHUNK == 0

    @functools.partial(
        pl.kernel, mesh=_sc_mesh(),
        out_type=jax.ShapeDtypeStruct((m, DP), U32),
        scratch_types=_sc_scratch(n_chunks),
        name="return_rows",
    )
    def k(yb_hbm, dest_hbm, out_hbm, idx_v, rows_v, gsem, wsem):
        wid = _sc_worker_id()
        kk = lax.shift_right_logical(wid, wk_shift)
        first = kk * (T // SC_CHUNK) + t0 // SC_CHUNK + (wid & (workers_per_k - 1)) * n_chunks
        pltpu.sync_copy(dest_hbm.at[pl.ds(first, n_chunks)], idx_v)
        base = wid * per_w

        def gather(c, slot):
            return pltpu.make_async_copy(yb_hbm.at[idx_v.at[c]], rows_v.at[slot], gsem.at[slot])

        def write(c, slot):
            return pltpu.make_async_copy(rows_v.at[slot], out_hbm.at[pl.ds(base + c * SC_CHUNK, SC_CHUNK)],
                                         wsem.at[slot])

        gather(0, 0).start()

        @pl.loop(0, n_chunks, step=2)
        def _(c):
            for b in range(2):
                cc = c + b
                gather(cc, b).wait()

                @pl.when(cc + 1 < n_chunks)
                def _():
                    @pl.when(cc >= 1)
                    def _():
                        write(cc - 1, 1 - b).wait()
                    gather(cc + 1, 1 - b).start()

                write(cc, b).start()

        write(n_chunks - 2, 0).wait()
        write(n_chunks - 1, 1).wait()

    return k(yb, dest.reshape(SUBLANES * T // SC_CHUNK, SC_CHUNK))


def _final_kernel(x1_ref, y0_ref, y1_ref, wc_ref, mods_ref, ln_ref, *rest):
    o_ref = rest[-1]
    ln = ln_ref[0]
    o_ref[0] = _combine(x1_ref[0], y0_ref[0], y1_ref[0], wc_ref[0], mods_ref[0, 0][5:6, :],
                        ln[0:1, :], ln[1:2, :])


def _final(x1, yk_parts, wc, params):
    mods, ffn_ln = params["mods"], params["ffn_ln"]
    out = None
    for hh, (b0, nb) in enumerate(zip(PART_START, PART_BATCHES)):
        args = [x1, yk_parts[hh], yk_parts[hh], wc, mods, ffn_ln]
        specs = [_tile_spec(D, b0), _yk_spec(0), _yk_spec(1), _tile_spec(2 * LANES, b0),
                 _mods_spec(DEPTH - 1, b0), _layer_spec(ffn_ln, DEPTH - 1)]
        aliases = {}
        if out is not None:
            aliases[len(args)] = 0
            args.append(out)
            specs.append(_any_spec())
        out = pl.pallas_call(
            _final_kernel,
            grid=(nb, NS),
            in_specs=specs,
            out_specs=_tile_spec(D, b0),
            out_shape=jax.ShapeDtypeStruct((BATCH, SEQ, D), F32),
            input_output_aliases=aliases,
            compiler_params=pltpu.CompilerParams(
                dimension_semantics=("arbitrary", "arbitrary"), vmem_limit_bytes=VMEM_LIMIT),
            name=f"final_combine_h{hh}",
        )(*args)
    return out


def _router_params(w_group, b_group, w_expert, b_expert):
    wr = jnp.zeros((DEPTH, NR, D), F32)
    wr = wr.at[:, 0:N_GROUPS].set(jnp.swapaxes(w_group, 1, 2))
    wr = wr.at[:, SUBLANES:SUBLANES + NE].set(jnp.swapaxes(w_expert, 1, 2))
    rb = jnp.full((DEPTH, NR), NEG, F32)
    rb = rb.at[:, 0:N_GROUPS].set(b_group).at[:, SUBLANES:SUBLANES + NE].set(b_expert)
    rb = rb.at[:, SUBLANES + NE:].set(0.0)
    return wr.astype(BF16), rb.reshape(DEPTH, NR, 1)


def kernel(x, c, ada_w, ada_b, a_w_in, a_b_in, a_w_dw, a_b_dw, a_ln_g, a_ln_b, a_w_out, a_b_out,
           b_w_in, b_w_dw, b_w_out, mix_ln_g, mix_ln_b, ffn_ln_g, ffn_ln_b,
           r_w_group, r_b_group, r_w_expert, r_b_expert, e_w_gate, e_w_up, e_w_down):
    n_a = a_w_in.shape[0]
    wr, rb = _router_params(r_w_group, r_b_group, r_w_expert, r_b_expert)

    def cast_layer(w, layer, after):
        wl = w[layer // 2:layer // 2 + 1]
        if after is not None:
            wl = wl + after * 0.0
        return wl.astype(BF16)

    params = {
        "mods": _ada_mods(c, ada_w, ada_b),
        "mix_ln": jnp.stack([mix_ln_g, mix_ln_b], axis=1),
        "ffn_ln": jnp.stack([ffn_ln_g, ffn_ln_b], axis=1),
        "wr": wr, "rb": rb,
        "tri": (jnp.arange(TS)[:, None] < jnp.arange(TS)[None, :]).astype(BF16),
        "a": [None, a_b_in.reshape(n_a, 1, 2 * D), a_w_dw.reshape(n_a, CONV_A, DT, LANES),
              a_b_dw.reshape(n_a, DT, LANES), jnp.stack([a_ln_g, a_ln_b, a_b_out], axis=1), None],
        "b": [None, b_w_dw, None],
    }
    w_in = {"a": a_w_in, "b": b_w_in}
    w_out = {"a": a_w_out, "b": b_w_out}
    prev = None
    xin = x
    after = None
    for i in range(DEPTH):
        kind = "a" if i % 2 == 0 else "b"
        params[kind][0] = cast_layer(w_in[kind], i, after)
        params[kind][-1] = cast_layer(w_out[kind], i, after)
        x1, h2, ri, wc, counts = _mixer(kind, i, xin, prev, params)
        after = counts[0, 0].astype(F32)
        dest, pad_starts, n_blocks = _dest_slots(counts, ri)
        xs = _dispatch_rows(h2.reshape(T, DP), dest)
        yb = _experts(i, pad_starts, n_blocks, xs, e_w_gate, e_w_up, e_w_down, name=f"experts{i}")
        yk_parts = [_return_rows(yb, dest, b0 * SEQ, nb * SEQ).reshape(TOP_K, nb * SEQ, DP)
                     for b0, nb in zip(PART_START, PART_BATCHES)]
        prev = (yk_parts, wc)
        xin = x1
    yk_parts, wc = prev
    return _final(xin, yk_parts, wc, params)
```
